```python
import jax, jax.numpy as jnp
from jax import lax
import numpy as np

D_MODEL = 1024
BATCH = 8
SEQ = 2048
DEPTH = 2
DEC_BATCH = 128
DEC_SEQ = 8
PAST_LEN = 16384
PAGE_SIZE = 128

N_META = 16
RET_HEADS = 8
RET_DK = D_MODEL // RET_HEADS
RET_DV = D_MODEL // RET_HEADS
RET_QK = RET_HEADS * RET_DK
RET_V = RET_HEADS * RET_DV
RET_CHUNK = 128
ROPE_BASE = 10000.0
D_CONV = D_MODEL
CONV_W = 3
D_FF = ((8 * D_MODEL + 3 * 256 - 1) // (3 * 256)) * 256
EPS = 1e-6
SPLITS = (RET_QK, RET_QK, RET_V, RET_V, D_CONV, D_CONV, D_CONV, D_MODEL, D_MODEL)
N_IN = RET_QK * 2 + RET_V * 2 + D_CONV * 3 + D_MODEL * 2

kernel_name = "retention_shortconv_gated_hybrid_step"


def _rms(x, g):
    x32 = x.astype(jnp.float32)
    y = x32 * lax.rsqrt(jnp.mean(x32 * x32, axis=-1, keepdims=True) + EPS)
    return (y * g.astype(jnp.float32)).astype(x.dtype)


def _split(proj):
    outs, start = [], 0
    for w in SPLITS:
        outs.append(proj[..., start:start + w])
        start += w
    return outs


def _rotary(t, pos):
    half = t.shape[-1] // 2
    inv = jnp.power(ROPE_BASE, -jnp.arange(half, dtype=jnp.float32) / half)
    ang = pos[:, None] * inv[None, :]
    cos, sin = jnp.cos(ang), jnp.sin(ang)
    t1, t2 = t[..., :half], t[..., half:]
    return jnp.concatenate([t1 * cos - t2 * sin, t1 * sin + t2 * cos], axis=-1)


def _log_gamma():
    return jnp.log1p(-jnp.exp2(-5.0 - jnp.arange(RET_HEADS, dtype=jnp.float32)))


def _ret_chunk(q, k, v, s, log_g):
    L = q.shape[2]
    idx = jnp.arange(L, dtype=jnp.float32)
    diff = idx[:, None] - idx[None, :]
    dmask = jnp.where(diff >= 0, jnp.exp(log_g[:, None, None] * jnp.maximum(diff, 0.0)[None]), 0.0)
    scores = jnp.einsum("bhid,bhjd->bhij", q, k) * dmask[None]
    inner = jnp.einsum("bhij,bhje->bhie", scores, v)
    qdec = jnp.exp(log_g[:, None] * (idx + 1.0)[None])[None, :, :, None]
    cross = jnp.einsum("bhid,bhde->bhie", q, s) * qdec
    kdec = jnp.exp(log_g[:, None] * (L - 1.0 - idx)[None])[None, :, :, None]
    s_new = jnp.exp(log_g * L)[None, :, None, None] * s + jnp.einsum("bhjd,bhje->bhde", k * kdec, v)
    return inner + cross, s_new


def _retention(q, k, v, s0, lead):
    log_g = _log_gamma()
    b, h, L, _ = q.shape
    o_lead, s = _ret_chunk(q[:, :, :lead], k[:, :, :lead], v[:, :, :lead], s0, log_g)
    rest = L - lead
    if rest == 0:
        return o_lead, s
    nc = rest // RET_CHUNK

    def blocks(t):
        return jnp.moveaxis(t[:, :, lead:].reshape(b, h, nc, RET_CHUNK, t.shape[-1]), 2, 0)

    def step(st, qkv):
        o, st = _ret_chunk(qkv[0], qkv[1], qkv[2], st, log_g)
        return st, o

    s, o_rest = lax.scan(step, s, (blocks(q), blocks(k), blocks(v)))
    o_rest = jnp.moveaxis(o_rest, 0, 2).reshape(b, h, rest, v.shape[-1])
    return jnp.concatenate([o_lead, o_rest], axis=2), s


def _mixer(hn, pos, s0, cprev, lead, w_in, conv_w, w_ret_o, w_conv_o, w_o):
    b, L, _ = hn.shape
    proj = hn @ w_in
    q, k, v, g, bg, cg, hc, ga, gb = _split(proj)

    def heads(t, d):
        return t.reshape(b, L, RET_HEADS, d).transpose(0, 2, 1, 3).astype(jnp.float32)

    qh = _rotary(heads(q, RET_DK), pos)
    kh = _rotary(heads(k, RET_DK), pos) * (RET_DK ** -0.5)
    vh = heads(v, RET_DV)
    o, s_new = _retention(qh, kh, vh, s0.astype(jnp.float32), lead)
    mu = jnp.mean(o, axis=-1, keepdims=True)
    var = jnp.mean(jnp.square(o - mu), axis=-1, keepdims=True)
    o = ((o - mu) * lax.rsqrt(var + EPS)).transpose(0, 2, 1, 3).reshape(b, L, RET_V).astype(hn.dtype)
    ret_out = (jax.nn.silu(g) * o) @ w_ret_o

    u = cg * hc
    full = jnp.concatenate([cprev.astype(u.dtype), u], axis=1)
    y = conv_w[0] * full[:, 0:L]
    for j in range(1, CONV_W):
        y = y + conv_w[j] * full[:, j:j + L]
    conv_out = (bg * y) @ w_conv_o

    merged = jax.nn.sigmoid(ga) * ret_out + jax.nn.sigmoid(gb) * conv_out
    return merged @ w_o, s_new, full[:, -(CONV_W - 1):]


def _ffn(hn, w_gate_up, w_down):
    gu = hn @ w_gate_up
    return (jax.nn.silu(gu[..., :D_FF]) * gu[..., D_FF:]) @ w_down


def _trunk(h, pos, s_in, c_in, lead, norm_mix_g, w_in, conv_w, w_ret_o, w_conv_o, w_o,
           norm_ffn_g, w_gate_up, w_down, final_norm_g):
    s_out, c_out = [], []
    for l in range(DEPTH):
        m, s, c = _mixer(_rms(h, norm_mix_g[l]), pos, s_in[l], c_in[l], lead,
                         w_in[l], conv_w[l], w_ret_o[l], w_conv_o[l], w_o[l])
        h = h + m
        h = h + _ffn(_rms(h, norm_ffn_g[l]), w_gate_up[l], w_down[l])
        s_out.append(s)
        c_out.append(c)
    return _rms(h, final_norm_g), jnp.stack(s_out), jnp.stack(c_out)


def setup_inputs(seed: int = 0) -> dict:
    key = jax.random.key(seed)
    ks = jax.random.split(key, 16)
    f32 = jnp.float32
    n = lambda k, shape, s: jax.random.normal(k, shape, f32) * s
    return {
        "x_prompt": n(ks[0], (BATCH, SEQ, D_MODEL), 1.0),
        "x_sample": n(ks[1], (DEC_BATCH, DEC_SEQ, D_MODEL), 1.0),
        "state_ret": n(ks[2], (DEPTH, DEC_BATCH, RET_HEADS, RET_DK, RET_DV), 0.5),
        "state_conv": n(ks[3], (DEPTH, DEC_BATCH, CONV_W - 1, D_CONV), 1.0),
        "meta_tokens": n(ks[4], (N_META, D_MODEL), 1.0),
        "norm_mix_g": 1.0 + n(ks[5], (DEPTH, D_MODEL), 0.05),
        "w_in": n(ks[6], (DEPTH, D_MODEL, N_IN), D_MODEL ** -0.5),
        "conv_w": n(ks[7], (DEPTH, CONV_W, D_CONV), CONV_W ** -0.5),
        "w_ret_o": n(ks[8], (DEPTH, RET_V, D_MODEL), RET_V ** -0.5),
        "w_conv_o": n(ks[9], (DEPTH, D_CONV, D_MODEL), D_CONV ** -0.5),
        "w_o": n(ks[10], (DEPTH, D_MODEL, D_MODEL), D_MODEL ** -0.5),
        "norm_ffn_g": 1.0 + n(ks[11], (DEPTH, D_MODEL), 0.05),
        "w_gate_up": n(ks[12], (DEPTH, D_MODEL, 2 * D_FF), D_MODEL ** -0.5),
        "w_down": n(ks[13], (DEPTH, D_FF, D_MODEL), D_FF ** -0.5),
        "final_norm_g": 1.0 + n(ks[14], (D_MODEL,), 0.05),
    }


def reference(x_prompt, x_sample, state_ret, state_conv, meta_tokens, norm_mix_g, w_in, conv_w,
              w_ret_o, w_conv_o, w_o, norm_ffn_g, w_gate_up, w_down, final_norm_g):
    weights = (norm_mix_g, w_in, conv_w, w_ret_o, w_conv_o, w_o, norm_ffn_g, w_gate_up, w_down, final_norm_g)

    meta = jnp.broadcast_to(meta_tokens[None].astype(x_prompt.dtype), (BATCH, N_META, D_MODEL))
    h_p = jnp.concatenate([meta, x_prompt], axis=1)
    pos_p = jnp.arange(N_META + SEQ, dtype=jnp.float32)
    s0_p = jnp.zeros((DEPTH, BATCH, RET_HEADS, RET_DK, RET_DV), state_ret.dtype)
    c0_p = jnp.zeros((DEPTH, BATCH, CONV_W - 1, D_CONV), state_conv.dtype)
    y_p, s_p, c_p = _trunk(h_p, pos_p, s0_p, c0_p, N_META, *weights)
    y_prompt = y_p[:, N_META:]

    pos_s = PAST_LEN + jnp.arange(DEC_SEQ, dtype=jnp.float32)
    y_sample, s_s, c_s = _trunk(x_sample, pos_s, state_ret, state_conv, DEC_SEQ, *weights)

    return (y_prompt, y_sample, s_p.astype(state_ret.dtype), c_p.astype(state_conv.dtype),
            s_s.astype(state_ret.dtype), c_s.astype(state_conv.dtype))
```

```python
import functools

import jax
import jax.numpy as jnp
from jax import lax
from jax.experimental import pallas as pl
from jax.experimental.pallas import tpu as pltpu

D_MODEL = 1024
N_META = 16
HEADS = 8
DH = D_MODEL // HEADS
CHUNK = 128
ROPE_BASE = 10000.0
CONV_W = 3
D_FF = ((8 * D_MODEL + 3 * 256 - 1) // (3 * 256)) * 256
EPS = 1e-6
PAST_LEN = 16384
N_IN = 9 * D_MODEL
C_Q, C_K, C_V, C_G, C_BG, C_CG, C_HC, C_GA, C_GB = (i * D_MODEL for i in range(9))

F32 = jnp.float32
BF16 = jnp.bfloat16

VMEM_LIMIT_BYTES = 58 * 1024 * 1024

MAIN_BG = 4


def _resident(shape):
    nd = len(shape)
    return pl.BlockSpec(shape, lambda *_: (0,) * nd, pipeline_mode=pl.Buffered(1))


def _params(sem):
    return pltpu.CompilerParams(dimension_semantics=sem, vmem_limit_bytes=VMEM_LIMIT_BYTES)


def _dot(a, b):
    return jnp.dot(a, b, preferred_element_type=F32)


def _rms_f32(x, g):
    return x * lax.rsqrt(jnp.mean(x * x, axis=-1, keepdims=True) + EPS) * g


def _sigmoid(x):
    return 1.0 / (1.0 + jnp.exp(-x))


def _silu(x):
    return x * _sigmoid(x)


def _rotary(t, cos, sin):
    return t * cos + pltpu.roll(t, DH // 2, 1) * sin


def _group_norm(o):
    mu = jnp.mean(o, axis=-1, keepdims=True)
    d = o - mu
    var = jnp.mean(d * d, axis=-1, keepdims=True)
    return d * lax.rsqrt(var + EPS)


def _retention_head(q, k, kd, v, s, dmask, qdec, gl):
    scores = lax.dot_general(q, k, (((1,), (1,)), ((), ())), preferred_element_type=F32) * dmask
    inner = _dot(scores.astype(BF16), v)
    cross = _dot(q, s.astype(BF16)) * qdec
    s_new = gl * s + lax.dot_general(kd, v, (((0,), (0,)), ((), ())), preferred_element_type=F32)
    return inner + cross, s_new


def _short_conv(u, tail, cw, rows):
    r1 = pltpu.roll(u, 1, 0)
    r2 = pltpu.roll(u, 2, 0)
    t0, t1 = tail[0:1, :], tail[1:2, :]
    sh1 = jnp.where(rows == 0, t1, r1)
    sh2 = jnp.where(rows == 0, t0, jnp.where(rows == 1, t1, r2))
    return cw[0:1, :] * sh2 + cw[1:2, :] * sh1 + cw[2:3, :] * u


def _log_gamma():
    return jnp.log1p(-jnp.exp2(-5.0 - jnp.arange(HEADS, dtype=F32)))


def _rope_tables(pos):
    half = DH // 2
    inv = jnp.power(ROPE_BASE, -jnp.arange(half, dtype=F32) / half)
    ang = pos[:, None] * inv[None, :]
    cos, sin = jnp.cos(ang), jnp.sin(ang)
    cosf = jnp.concatenate([cos, cos], axis=-1)
    sinf = jnp.concatenate([-sin, sin], axis=-1)
    scale = DH ** -0.5
    return cosf, sinf, cosf * scale, sinf * scale


def _decay_tables(L):
    log_g = _log_gamma()
    idx = jnp.arange(L, dtype=F32)
    diff = idx[:, None] - idx[None, :]
    dmask = jnp.where(diff >= 0, jnp.exp(log_g[:, None, None] * jnp.maximum(diff, 0.0)[None]), 0.0)
    qdec = jnp.exp(log_g[:, None] * (idx + 1.0)[None])
    kdec = jnp.exp(log_g[:, None] * (L - 1.0 - idx)[None])
    gl = jnp.exp(log_g * L)
    lanes = lambda t: jnp.repeat(t.T, DH, axis=1)
    return dmask, lanes(qdec), lanes(kdec), jnp.repeat(gl, DH)[None, :]


def _mixer_main_kernel(x_ref, s0_ref, c0_ref, g_ref, win_ref, cw_ref, wro_ref, wco_ref, wo_ref,
                       cq_ref, sq_ref, ck_ref, sk_ref, dmask_ref, qdec_ref, kdec_ref, gl_ref,
                       h_ref, s_ref, c_ref, q_s, k_s, kd_s, v_s, o_s):
    bg = x_ref.shape[0]
    m = bg * CHUNK

    @pl.when(pl.program_id(1) == 0)
    def _():
        s_ref[...] = s0_ref[...]
        c_ref[...] = c0_ref[...]

    x = x_ref[...].reshape(m, D_MODEL)
    hn = _rms_f32(x, g_ref[...]).astype(BF16)

    cq, sq, ck, sk = cq_ref[...], sq_ref[...], ck_ref[...], sk_ref[...]
    q = _dot(hn, win_ref[:, C_Q:C_Q + D_MODEL])
    k = _dot(hn, win_ref[:, C_K:C_K + D_MODEL])
    v_s[...] = _dot(hn, win_ref[:, C_V:C_V + D_MODEL]).astype(BF16)
    for b in range(bg):
        r = slice(b * CHUNK, (b + 1) * CHUNK)
        for h in range(HEADS):
            cl = slice(h * DH, (h + 1) * DH)
            q_s[r, cl] = _rotary(q[r, cl], cq, sq).astype(BF16)
            kr = _rotary(k[r, cl], ck, sk)
            k_s[r, cl] = kr.astype(BF16)
            kd_s[r, cl] = (kr * kdec_ref[:, cl]).astype(BF16)

    for b in range(bg):
        r = slice(b * CHUNK, (b + 1) * CHUNK)
        for h in range(HEADS):
            cl = slice(h * DH, (h + 1) * DH)
            o, s_new = _retention_head(q_s[r, cl], k_s[r, cl], kd_s[r, cl], v_s[r, cl], s_ref[b, h],
                                       dmask_ref[h], qdec_ref[:, cl], gl_ref[:, cl])
            s_ref[b, h] = s_new
            o_s[r, cl] = _group_norm(o)

    g = _dot(hn, win_ref[:, C_G:C_G + D_MODEL])
    ret_out = _dot((_silu(g) * o_s[...]).astype(BF16), wro_ref[...])

    bgate = _dot(hn, win_ref[:, C_BG:C_BG + D_MODEL])
    u = _dot(hn, win_ref[:, C_CG:C_CG + D_MODEL]) * _dot(hn, win_ref[:, C_HC:C_HC + D_MODEL])
    rows = lax.broadcasted_iota(jnp.int32, (CHUNK, D_MODEL), 0)
    cw = cw_ref[...]
    ys = []
    for b in range(bg):
        ub = u[b * CHUNK:(b + 1) * CHUNK]
        ys.append(_short_conv(ub, c_ref[b], cw, rows))
        c_ref[b] = ub[CHUNK - (CONV_W - 1):, :]
    y = jnp.concatenate(ys, axis=0)
    conv_out = _dot((bgate * y).astype(BF16), wco_ref[...])

    ga = _dot(hn, win_ref[:, C_GA:C_GA + D_MODEL])
    gb = _dot(hn, win_ref[:, C_GB:C_GB + D_MODEL])
    merged = _sigmoid(ga) * ret_out + _sigmoid(gb) * conv_out
    out = x + _dot(merged.astype(BF16), wo_ref[...])
    h_ref[...] = out.reshape(bg, CHUNK, D_MODEL)


def _mixer_main(x, s0, c0, g, win, cw, wro, wco, wo, rope, decay):
    nb, seq, _ = x.shape
    bg = MAIN_BG
    cq, sq, ck, sk = rope
    dmask, qdec, kdec, gl = decay
    m = bg * CHUNK
    rope_spec = pl.BlockSpec((CHUNK, DH), lambda i, c: (c, 0))
    return pl.pallas_call(
        _mixer_main_kernel,
        grid=(nb // bg, seq // CHUNK),
        in_specs=[
            pl.BlockSpec((bg, CHUNK, D_MODEL), lambda i, c: (i, c, 0)),
            pl.BlockSpec((bg, HEADS, DH, DH), lambda i, c: (i, 0, 0, 0)),
            pl.BlockSpec((bg, CONV_W - 1, D_MODEL), lambda i, c: (i, 0, 0)),
            _resident((1, D_MODEL)),
            _resident((D_MODEL, N_IN)),
            _resident((CONV_W, D_MODEL)),
            _resident((D_MODEL, D_MODEL)),
            _resident((D_MODEL, D_MODEL)),
            _resident((D_MODEL, D_MODEL)),
            rope_spec, rope_spec, rope_spec, rope_spec,
            _resident((HEADS, CHUNK, CHUNK)),
            _resident((CHUNK, D_MODEL)),
            _resident((CHUNK, D_MODEL)),
            _resident((1, D_MODEL)),
        ],
        out_specs=[
            pl.BlockSpec((bg, CHUNK, D_MODEL), lambda i, c: (i, c, 0)),
            pl.BlockSpec((bg, HEADS, DH, DH), lambda i, c: (i, 0, 0, 0)),
            pl.BlockSpec((bg, CONV_W - 1, D_MODEL), lambda i, c: (i, 0, 0)),
        ],
        out_shape=[
            jax.ShapeDtypeStruct(x.shape, F32),
            jax.ShapeDtypeStruct(s0.shape, F32),
            jax.ShapeDtypeStruct(c0.shape, F32),
        ],
        scratch_shapes=[pltpu.VMEM((m, D_MODEL), BF16)] * 4 + [pltpu.VMEM((m, D_MODEL), F32)],
        compiler_params=_params(("arbitrary", "arbitrary")),
        name="mixer_main",
    )(x, s0, c0, g, win, cw, wro, wco, wo, cq, sq, ck, sk, dmask, qdec, kdec, gl)


def _ffn_kernel(x_ref, g_ref, wgu_ref, wd_ref, fg_ref, o_ref, *, final):
    x = x_ref[...]
    hn = _rms_f32(x, g_ref[...]).astype(BF16)
    a = _dot(hn, wgu_ref[:, :D_FF])
    b = _dot(hn, wgu_ref[:, D_FF:])
    y = x + _dot((_silu(a) * b).astype(BF16), wd_ref[...])
    if final:
        y = _rms_f32(y, fg_ref[...])
    o_ref[...] = y


def _ffn(x, g, wgu, wd, fg, final, tm):
    rows = x.shape[0]
    row_spec = pl.BlockSpec((tm, D_MODEL), lambda i: (i, 0))
    return pl.pallas_call(
        functools.partial(_ffn_kernel, final=final),
        grid=(rows // tm,),
        in_specs=[row_spec, _resident((1, D_MODEL)), _resident((D_MODEL, 2 * D_FF)),
                  _resident((D_FF, D_MODEL)), _resident((1, D_MODEL))],
        out_specs=row_spec,
        out_shape=jax.ShapeDtypeStruct(x.shape, F32),
        compiler_params=_params(("arbitrary",)),
        name="ffn",
    )(x, g, wgu, wd, fg)


def _proj_kernel(x_ref, g_ref, win_ref, o_ref):
    hn = _rms_f32(x_ref[...], g_ref[...]).astype(BF16)
    o_ref[...] = _dot(hn, win_ref[...])


def _proj(x, g, win, tm):
    rows = x.shape[0]
    return pl.pallas_call(
        _proj_kernel,
        grid=(rows // tm,),
        in_specs=[pl.BlockSpec((tm, D_MODEL), lambda i: (i, 0)), _resident((1, D_MODEL)),
                  _resident((D_MODEL, N_IN))],
        out_specs=pl.BlockSpec((tm, N_IN), lambda i: (i, 0)),
        out_shape=jax.ShapeDtypeStruct((rows, N_IN), F32),
        compiler_params=_params(("arbitrary",)),
        name="proj_short",
    )(x, g, win)


def _ret_short_kernel(p_ref, s_ref, c_ref, cw_ref, cq_ref, sq_ref, ck_ref, sk_ref,
                      dmask_ref, qdec_ref, kdec_ref, gl_ref,
                      gated_ref, bgy_ref, sn_ref, cn_ref, *, L):
    sb = c_ref.shape[0]
    cq, sq, ck, sk = cq_ref[...], sq_ref[...], ck_ref[...], sk_ref[...]
    rows = lax.broadcasted_iota(jnp.int32, (L, D_MODEL), 0)
    cw = cw_ref[...]
    for b in range(sb):
        r = slice(b * L, (b + 1) * L)
        for h in range(HEADS):
            cl = slice(h * DH, (h + 1) * DH)
            q = _rotary(p_ref[r, C_Q + h * DH:C_Q + (h + 1) * DH], cq, sq)
            k = _rotary(p_ref[r, C_K + h * DH:C_K + (h + 1) * DH], ck, sk)
            v = p_ref[r, C_V + h * DH:C_V + (h + 1) * DH]
            kd = k * kdec_ref[:, cl]
            o, s_new = _retention_head(q.astype(BF16), k.astype(BF16), kd.astype(BF16), v.astype(BF16),
                                       s_ref[0, b, h], dmask_ref[h], qdec_ref[:, cl], gl_ref[:, cl])
            sn_ref[b, h] = s_new
            g = p_ref[r, C_G + h * DH:C_G + (h + 1) * DH]
            gated_ref[r, cl] = _silu(g) * _group_norm(o)
        u = p_ref[r, C_CG:C_CG + D_MODEL] * p_ref[r, C_HC:C_HC + D_MODEL]
        y = _short_conv(u, c_ref[b], cw, rows)
        bgy_ref[r, :] = p_ref[r, C_BG:C_BG + D_MODEL] * y
        cn_ref[b] = u[L - (CONV_W - 1):, :]


def _ret_short(proj, state, layer, cprev, cw, rope, decay, L, sb):
    nseq = cprev.shape[0]
    cq, sq, ck, sk = rope
    dmask, qdec, kdec, gl = decay
    rows = nseq * L
    return pl.pallas_call(
        functools.partial(_ret_short_kernel, L=L),
        grid=(nseq // sb,),
        in_specs=[
            pl.BlockSpec((sb * L, C_GA), lambda i: (i, 0)),
            pl.BlockSpec((1, sb, HEADS, DH, DH), lambda i: (layer, i, 0, 0, 0)),
            pl.BlockSpec((sb, CONV_W - 1, D_MODEL), lambda i: (i, 0, 0)),
            _resident((CONV_W, D_MODEL)),
            _resident((L, DH)), _resident((L, DH)), _resident((L, DH)), _resident((L, DH)),
            _resident((HEADS, L, L)),
            _resident((L, D_MODEL)),
            _resident((L, D_MODEL)),
            _resident((1, D_MODEL)),
        ],
        out_specs=[
            pl.BlockSpec((sb * L, D_MODEL), lambda i: (i, 0)),
            pl.BlockSpec((sb * L, D_MODEL), lambda i: (i, 0)),
            pl.BlockSpec((sb, HEADS, DH, DH), lambda i: (i, 0, 0, 0)),
            pl.BlockSpec((sb, CONV_W - 1, D_MODEL), lambda i: (i, 0, 0)),
        ],
        out_shape=[
            jax.ShapeDtypeStruct((rows, D_MODEL), F32),
            jax.ShapeDtypeStruct((rows, D_MODEL), F32),
            jax.ShapeDtypeStruct((nseq, HEADS, DH, DH), F32),
            jax.ShapeDtypeStruct((nseq, CONV_W - 1, D_MODEL), F32),
        ],
        compiler_params=_params(("arbitrary",)),
        name="ret_short",
    )(proj, state, cprev, cw, cq, sq, ck, sk, dmask, qdec, kdec, gl)


def _out_short_kernel(x_ref, gated_ref, bgy_ref, ga_ref, gb_ref, wro_ref, wco_ref, wo_ref, o_ref):
    ret_out = _dot(gated_ref[...].astype(BF16), wro_ref[...])
    conv_out = _dot(bgy_ref[...].astype(BF16), wco_ref[...])
    merged = _sigmoid(ga_ref[...]) * ret_out + _sigmoid(gb_ref[...]) * conv_out
    o_ref[...] = x_ref[...] + _dot(merged.astype(BF16), wo_ref[...])


def _out_short(x, gated, bgy, proj, wro, wco, wo, tm):
    rows = x.shape[0]
    row_spec = pl.BlockSpec((tm, D_MODEL), lambda i: (i, 0))
    w_spec = _resident((D_MODEL, D_MODEL))
    return pl.pallas_call(
        _out_short_kernel,
        grid=(rows // tm,),
        in_specs=[row_spec, row_spec, row_spec,
                  pl.BlockSpec((tm, D_MODEL), lambda i: (i, C_GA // D_MODEL)),
                  pl.BlockSpec((tm, D_MODEL), lambda i: (i, C_GB // D_MODEL)),
                  w_spec, w_spec, w_spec],
        out_specs=row_spec,
        out_shape=jax.ShapeDtypeStruct(x.shape, F32),
        compiler_params=_params(("arbitrary",)),
        name="out_short",
    )(x, gated, bgy, proj, proj, wro, wco, wo)


def _tile_rows(t, n):
    return jnp.tile(t, (n, 1))


def kernel(x_prompt, x_sample, state_ret, state_conv, meta_tokens, norm_mix_g, w_in, conv_w, w_ret_o,
           w_conv_o, w_o, norm_ffn_g, w_gate_up, w_down, final_norm_g):
    depth = w_in.shape[0]
    nb, seq, _ = x_prompt.shape
    ns, ls, _ = x_sample.shape

    win_b, wro_b, wco_b, wo_b = (w.astype(BF16) for w in (w_in, w_ret_o, w_conv_o, w_o))
    wgu_b, wd_b = w_gate_up.astype(BF16), w_down.astype(BF16)
    gm = norm_mix_g.reshape(depth, 1, D_MODEL)
    gf = norm_ffn_g.reshape(depth, 1, D_MODEL)
    fg = final_norm_g.reshape(1, D_MODEL)

    rope_meta = _rope_tables(jnp.arange(N_META, dtype=F32))
    rope_main = _rope_tables(N_META + jnp.arange(seq, dtype=F32))
    rope_samp = _rope_tables(PAST_LEN + jnp.arange(ls, dtype=F32))
    dec_meta, dec_main, dec_samp = _decay_tables(N_META), _decay_tables(CHUNK), _decay_tables(ls)

    h_meta = jnp.broadcast_to(meta_tokens[None].astype(F32), (nb, N_META, D_MODEL)).reshape(nb * N_META, D_MODEL)
    h_main = x_prompt
    h_samp = x_sample.reshape(ns * ls, D_MODEL)
    zero_s = jnp.zeros((depth, nb, HEADS, DH, DH), F32)
    zero_c = jnp.zeros((nb, CONV_W - 1, D_MODEL), F32)

    sb_samp, sb_meta = 8, nb
    s_p, c_p, s_s, c_s = [], [], [], []
    for l in range(depth):
        last = l == depth - 1
        pm = _proj(h_meta, gm[l], win_b[l], nb * N_META)
        gated, bgy, s_m, c_m = _ret_short(pm, zero_s, l, zero_c, conv_w[l], rope_meta, dec_meta, N_META, sb_meta)
        if not last:
            h_meta = _out_short(h_meta, gated, bgy, pm, wro_b[l], wco_b[l], wo_b[l], nb * N_META)
            h_meta = _ffn(h_meta, gf[l], wgu_b[l], wd_b[l], fg, False, nb * N_META)

        h_main, s_l, c_l = _mixer_main(h_main, s_m, c_m, gm[l], win_b[l], conv_w[l], wro_b[l], wco_b[l],
                                       wo_b[l], rope_main, dec_main)
        h_main = _ffn(h_main.reshape(nb * seq, D_MODEL), gf[l], wgu_b[l], wd_b[l], fg, last, 512)
        h_main = h_main.reshape(nb, seq, D_MODEL)
        s_p.append(s_l)
        c_p.append(c_l)

        ps = _proj(h_samp, gm[l], win_b[l], 256)
        gated, bgy, s_l, c_l = _ret_short(ps, state_ret, l, state_conv[l], conv_w[l], rope_samp, dec_samp, ls, sb_samp)
        h_samp = _out_short(h_samp, gated, bgy, ps, wro_b[l], wco_b[l], wo_b[l], 256)
        h_samp = _ffn(h_samp, gf[l], wgu_b[l], wd_b[l], fg, last, 256)
        s_s.append(s_l)
        c_s.append(c_l)

    return (h_main, h_samp.reshape(ns, ls, D_MODEL), jnp.stack(s_p), jnp.stack(c_p),
            jnp.stack(s_s), jnp.stack(c_s))
```

```python
import functools

import numpy as np
import jax
import jax.numpy as jnp
from jax import lax
from jax.experimental import pallas as pl
from jax.experimental.pallas import tpu as pltpu

D_MODEL = 1024
N_META = 16
HEADS = 8
DH = D_MODEL // HEADS
CHUNK = 128
ROPE_BASE = 10000.0
CONV_W = 3
D_FF = ((8 * D_MODEL + 3 * 256 - 1) // (3 * 256)) * 256
EPS = 1e-6
PAST_LEN = 16384
N_IN = 9 * D_MODEL
C_Q, C_K, C_V, C_G, C_BG, C_CG, C_HC, C_GA, C_GB = (i * D_MODEL for i in range(9))

F32 = jnp.float32
BF16 = jnp.bfloat16

VMEM_LIMIT_BYTES = 58 * 1024 * 1024

MAIN_BG = 4


def _resident(shape):
    nd = len(shape)
    return pl.BlockSpec(shape, lambda *_: (0,) * nd, pipeline_mode=pl.Buffered(1))


def _resident_layer(shape, layer):
    nd = len(shape)
    return pl.BlockSpec((None,) + tuple(shape), lambda *_: (layer,) + (0,) * nd, pipeline_mode=pl.Buffered(1))


def _params(sem):
    return pltpu.CompilerParams(dimension_semantics=sem, vmem_limit_bytes=VMEM_LIMIT_BYTES)


def _dot(a, b):
    return jnp.dot(a, b, preferred_element_type=F32)


def _rms_f32(x, g):
    return x * lax.rsqrt(jnp.mean(x * x, axis=-1, keepdims=True) + EPS) * g


def _sigmoid(x):
    return 1.0 / (1.0 + jnp.exp(-x))


def _silu(x):
    return x * _sigmoid(x)


def _rotary(t, cos, sin):
    return t * cos + pltpu.roll(t, DH // 2, 1) * sin


def _group_norm(o):
    mu = jnp.mean(o, axis=-1, keepdims=True)
    d = o - mu
    var = jnp.mean(d * d, axis=-1, keepdims=True)
    return d * lax.rsqrt(var + EPS)


def _retention_head(q, k, kd, v, s, dmask, qdec, gl):
    scores = lax.dot_general(q, k, (((1,), (1,)), ((), ())), preferred_element_type=F32) * dmask
    inner = _dot(scores.astype(BF16), v)
    cross = _dot(q, s.astype(BF16)) * qdec
    s_new = gl * s + lax.dot_general(kd, v, (((0,), (0,)), ((), ())), preferred_element_type=F32)
    return inner + cross, s_new


def _short_conv(u, tail, cw, rows):
    r1 = pltpu.roll(u, 1, 0)
    r2 = pltpu.roll(u, 2, 0)
    t0, t1 = tail[0:1, :], tail[1:2, :]
    sh1 = jnp.where(rows == 0, t1, r1)
    sh2 = jnp.where(rows == 0, t0, jnp.where(rows == 1, t1, r2))
    return cw[0:1, :] * sh2 + cw[1:2, :] * sh1 + cw[2:3, :] * u


def _log_gamma():
    return np.log1p(-np.exp2(-5.0 - np.arange(HEADS, dtype=np.float64)))


def _const(t):
    return jnp.asarray(np.asarray(t, dtype=np.float32))


def _rope_tables(pos):
    half = DH // 2
    inv = np.power(ROPE_BASE, -np.arange(half, dtype=np.float64) / half)
    ang = np.asarray(pos, dtype=np.float64)[:, None] * inv[None, :]
    cos, sin = np.cos(ang), np.sin(ang)
    cosf = np.concatenate([cos, cos], axis=-1)
    sinf = np.concatenate([-sin, sin], axis=-1)
    scale = DH ** -0.5
    return tuple(_const(t) for t in (cosf, sinf, cosf * scale, sinf * scale))


def _decay_tables(L):
    log_g = _log_gamma()
    idx = np.arange(L, dtype=np.float64)
    diff = idx[:, None] - idx[None, :]
    dmask = np.where(diff >= 0, np.exp(log_g[:, None, None] * np.maximum(diff, 0.0)[None]), 0.0)
    qdec = np.exp(log_g[:, None] * (idx + 1.0)[None])
    kdec = np.exp(log_g[:, None] * (L - 1.0 - idx)[None])
    gl = np.exp(log_g * L)
    lanes = lambda t: np.repeat(t.T, DH, axis=1)
    return tuple(_const(t) for t in (dmask, lanes(qdec), lanes(kdec), np.repeat(gl, DH)[None, :]))


def _mixer_main_kernel(x_ref, s0_ref, c0_ref, g_ref, win_ref, cw_ref, wro_ref, wco_ref, wo_ref,
                       cq_ref, sq_ref, ck_ref, sk_ref, dmask_ref, qdec_ref, kdec_ref, gl_ref,
                       h_ref, s_ref, c_ref, q_s, k_s, kd_s, v_s, o_s):
    bg = x_ref.shape[0]
    m = bg * CHUNK

    @pl.when(pl.program_id(1) == 0)
    def _():
        s_ref[...] = s0_ref[...]
        c_ref[...] = c0_ref[...]

    x = x_ref[...].reshape(m, D_MODEL)
    hn = _rms_f32(x, g_ref[...]).astype(BF16)

    cq, sq, ck, sk = cq_ref[...], sq_ref[...], ck_ref[...], sk_ref[...]
    q = _dot(hn, win_ref[:, C_Q:C_Q + D_MODEL])
    k = _dot(hn, win_ref[:, C_K:C_K + D_MODEL])
    v_s[...] = _dot(hn, win_ref[:, C_V:C_V + D_MODEL]).astype(BF16)
    for b in range(bg):
        r = slice(b * CHUNK, (b + 1) * CHUNK)
        for h in range(HEADS):
            cl = slice(h * DH, (h + 1) * DH)
            q_s[r, cl] = _rotary(q[r, cl], cq, sq).astype(BF16)
            kr = _rotary(k[r, cl], ck, sk)
            k_s[r, cl] = kr.astype(BF16)
            kd_s[r, cl] = (kr * kdec_ref[:, cl]).astype(BF16)

    for b in range(bg):
        r = slice(b * CHUNK, (b + 1) * CHUNK)
        for h in range(HEADS):
            cl = slice(h * DH, (h + 1) * DH)
            o, s_new = _retention_head(q_s[r, cl], k_s[r, cl], kd_s[r, cl], v_s[r, cl], s_ref[b, h],
                                       dmask_ref[h], qdec_ref[:, cl], gl_ref[:, cl])
            s_ref[b, h] = s_new
            o_s[r, cl] = _group_norm(o)

    g = _dot(hn, win_ref[:, C_G:C_G + D_MODEL])
    ret_out = _dot((_silu(g) * o_s[...]).astype(BF16), wro_ref[...])

    bgate = _dot(hn, win_ref[:, C_BG:C_BG + D_MODEL])
    u = _dot(hn, win_ref[:, C_CG:C_CG + D_MODEL]) * _dot(hn, win_ref[:, C_HC:C_HC + D_MODEL])
    rows = lax.broadcasted_iota(jnp.int32, (CHUNK, D_MODEL), 0)
    cw = cw_ref[...]
    ys = []
    for b in range(bg):
        ub = u[b * CHUNK:(b + 1) * CHUNK]
        ys.append(_short_conv(ub, c_ref[b], cw, rows))
        c_ref[b] = ub[CHUNK - (CONV_W - 1):, :]
    y = jnp.concatenate(ys, axis=0)
    conv_out = _dot((bgate * y).astype(BF16), wco_ref[...])

    ga = _dot(hn, win_ref[:, C_GA:C_GA + D_MODEL])
    gb = _dot(hn, win_ref[:, C_GB:C_GB + D_MODEL])
    merged = _sigmoid(ga) * ret_out + _sigmoid(gb) * conv_out
    out = x + _dot(merged.astype(BF16), wo_ref[...])
    h_ref[...] = out.reshape(bg, CHUNK, D_MODEL)


def _mixer_main(x, s0, c0, layer, g, win, cw, wro, wco, wo, rope, decay):
    nb, seq, _ = x.shape
    bg = MAIN_BG
    cq, sq, ck, sk = rope
    dmask, qdec, kdec, gl = decay
    m = bg * CHUNK
    rope_spec = pl.BlockSpec((CHUNK, DH), lambda i, c: (c, 0))
    return pl.pallas_call(
        _mixer_main_kernel,
        grid=(nb // bg, seq // CHUNK),
        in_specs=[
            pl.BlockSpec((bg, CHUNK, D_MODEL), lambda i, c: (i, c, 0)),
            pl.BlockSpec((None, bg, HEADS, DH, DH), lambda i, c: (layer, i, 0, 0, 0)),
            pl.BlockSpec((bg, CONV_W - 1, D_MODEL), lambda i, c: (i, 0, 0)),
            _resident_layer((1, D_MODEL), layer),
            _resident_layer((D_MODEL, N_IN), layer),
            _resident_layer((CONV_W, D_MODEL), layer),
            _resident_layer((D_MODEL, D_MODEL), layer),
            _resident_layer((D_MODEL, D_MODEL), layer),
            _resident_layer((D_MODEL, D_MODEL), layer),
            rope_spec, rope_spec, rope_spec, rope_spec,
            _resident((HEADS, CHUNK, CHUNK)),
            _resident((CHUNK, D_MODEL)),
            _resident((CHUNK, D_MODEL)),
            _resident((1, D_MODEL)),
        ],
        out_specs=[
            pl.BlockSpec((bg, CHUNK, D_MODEL), lambda i, c: (i, c, 0)),
            pl.BlockSpec((bg, HEADS, DH, DH), lambda i, c: (i, 0, 0, 0)),
            pl.BlockSpec((bg, CONV_W - 1, D_MODEL), lambda i, c: (i, 0, 0)),
        ],
        out_shape=[
            jax.ShapeDtypeStruct(x.shape, F32),
            jax.ShapeDtypeStruct(s0.shape[1:], F32),
            jax.ShapeDtypeStruct(c0.shape, F32),
        ],
        scratch_shapes=[pltpu.VMEM((m, D_MODEL), BF16)] * 4 + [pltpu.VMEM((m, D_MODEL), F32)],
        compiler_params=_params(("arbitrary", "arbitrary")),
        name="mixer_main",
    )(x, s0, c0, g, win, cw, wro, wco, wo, cq, sq, ck, sk, dmask, qdec, kdec, gl)


def _ffn_kernel(x_ref, g_ref, wgu_ref, wd_ref, fg_ref, o_ref, *, final):
    x = x_ref[...]
    hn = _rms_f32(x, g_ref[...]).astype(BF16)
    a = _dot(hn, wgu_ref[:, :D_FF])
    b = _dot(hn, wgu_ref[:, D_FF:])
    y = x + _dot((_silu(a) * b).astype(BF16), wd_ref[...])
    if final:
        y = _rms_f32(y, fg_ref[...])
    o_ref[...] = y


def _ffn(x, layer, g, wgu, wd, fg, final, tm):
    rows = x.shape[0]
    row_spec = pl.BlockSpec((tm, D_MODEL), lambda i: (i, 0))
    return pl.pallas_call(
        functools.partial(_ffn_kernel, final=final),
        grid=(rows // tm,),
        in_specs=[row_spec, _resident_layer((1, D_MODEL), layer), _resident_layer((D_MODEL, 2 * D_FF), layer),
                  _resident_layer((D_FF, D_MODEL), layer), _resident((1, D_MODEL))],
        out_specs=row_spec,
        out_shape=jax.ShapeDtypeStruct(x.shape, F32),
        compiler_params=_params(("arbitrary",)),
        name="ffn",
    )(x, g, wgu, wd, fg)


def _proj_kernel(x_ref, g_ref, win_ref, o_ref):
    hn = _rms_f32(x_ref[...], g_ref[...]).astype(BF16)
    o_ref[...] = _dot(hn, win_ref[...])


def _proj(x, layer, g, win, tm):
    rows = x.shape[0]
    return pl.pallas_call(
        _proj_kernel,
        grid=(rows // tm,),
        in_specs=[pl.BlockSpec((tm, D_MODEL), lambda i: (i, 0)), _resident_layer((1, D_MODEL), layer),
                  _resident_layer((D_MODEL, N_IN), layer)],
        out_specs=pl.BlockSpec((tm, N_IN), lambda i: (i, 0)),
        out_shape=jax.ShapeDtypeStruct((rows, N_IN), F32),
        compiler_params=_params(("arbitrary",)),
        name="proj_short",
    )(x, g, win)


def _ret_short_kernel(p_ref, s_ref, c_ref, cw_ref, cq_ref, sq_ref, ck_ref, sk_ref,
                      dmask_ref, qdec_ref, kdec_ref, gl_ref, *rest, L):
    gated_ref, bgy_ref, sn_ref, cn_ref = rest[-4:]
    sb = c_ref.shape[0]
    cq, sq, ck, sk = cq_ref[...], sq_ref[...], ck_ref[...], sk_ref[...]
    rows = lax.broadcasted_iota(jnp.int32, (L, D_MODEL), 0)
    cw = cw_ref[...]
    for b in range(sb):
        r = slice(b * L, (b + 1) * L)
        for h in range(HEADS):
            cl = slice(h * DH, (h + 1) * DH)
            q = _rotary(p_ref[r, C_Q + h * DH:C_Q + (h + 1) * DH], cq, sq)
            k = _rotary(p_ref[r, C_K + h * DH:C_K + (h + 1) * DH], ck, sk)
            v = p_ref[r, C_V + h * DH:C_V + (h + 1) * DH]
            kd = k * kdec_ref[:, cl]
            o, s_new = _retention_head(q.astype(BF16), k.astype(BF16), kd.astype(BF16), v.astype(BF16),
                                       s_ref[b, h], dmask_ref[h], qdec_ref[:, cl], gl_ref[:, cl])
            sn_ref[b, h] = s_new
            g = p_ref[r, C_G + h * DH:C_G + (h + 1) * DH]
            gated_ref[r, cl] = _silu(g) * _group_norm(o)
        u = p_ref[r, C_CG:C_CG + D_MODEL] * p_ref[r, C_HC:C_HC + D_MODEL]
        y = _short_conv(u, c_ref[b], cw, rows)
        bgy_ref[r, :] = p_ref[r, C_BG:C_BG + D_MODEL] * y
        cn_ref[b] = u[L - (CONV_W - 1):, :]


def _ret_short(proj, state, layer, cprev, cw, rope, decay, L, sb, s_out_prev):
    nseq = cprev.shape[0]
    cq, sq, ck, sk = rope
    dmask, qdec, kdec, gl = decay
    rows = nseq * L
    state_spec = pl.BlockSpec((None, sb, HEADS, DH, DH), lambda i: (layer, i, 0, 0, 0))
    in_specs = [
        pl.BlockSpec((sb * L, C_GA), lambda i: (i, 0)),
        state_spec,
        pl.BlockSpec((sb, CONV_W - 1, D_MODEL), lambda i: (i, 0, 0)),
        _resident_layer((CONV_W, D_MODEL), layer),
        _resident((L, DH)), _resident((L, DH)), _resident((L, DH)), _resident((L, DH)),
        _resident((HEADS, L, L)),
        _resident((L, D_MODEL)),
        _resident((L, D_MODEL)),
        _resident((1, D_MODEL)),
    ]
    args = [proj, state, cprev, cw, cq, sq, ck, sk, dmask, qdec, kdec, gl]
    aliases = {}
    if s_out_prev is not None:
        aliases = {len(args): 2}
        in_specs.append(pl.BlockSpec(memory_space=pl.ANY))
        args.append(s_out_prev)
    return pl.pallas_call(
        functools.partial(_ret_short_kernel, L=L),
        grid=(nseq // sb,),
        in_specs=in_specs,
        out_specs=[
            pl.BlockSpec((sb * L, D_MODEL), lambda i: (i, 0)),
            pl.BlockSpec((sb * L, D_MODEL), lambda i: (i, 0)),
            state_spec,
            pl.BlockSpec((sb, CONV_W - 1, D_MODEL), lambda i: (i, 0, 0)),
        ],
        out_shape=[
            jax.ShapeDtypeStruct((rows, D_MODEL), F32),
            jax.ShapeDtypeStruct((rows, D_MODEL), F32),
            jax.ShapeDtypeStruct(state.shape, F32),
            jax.ShapeDtypeStruct((nseq, CONV_W - 1, D_MODEL), F32),
        ],
        input_output_aliases=aliases,
        compiler_params=_params(("arbitrary",)),
        name="ret_short",
    )(*args)


def _out_short_kernel(x_ref, gated_ref, bgy_ref, ga_ref, gb_ref, wro_ref, wco_ref, wo_ref, o_ref):
    ret_out = _dot(gated_ref[...].astype(BF16), wro_ref[...])
    conv_out = _dot(bgy_ref[...].astype(BF16), wco_ref[...])
    merged = _sigmoid(ga_ref[...]) * ret_out + _sigmoid(gb_ref[...]) * conv_out
    o_ref[...] = x_ref[...] + _dot(merged.astype(BF16), wo_ref[...])


def _out_short(x, gated, bgy, proj, layer, wro, wco, wo, tm):
    rows = x.shape[0]
    row_spec = pl.BlockSpec((tm, D_MODEL), lambda i: (i, 0))
    w_spec = _resident_layer((D_MODEL, D_MODEL), layer)
    return pl.pallas_call(
        _out_short_kernel,
        grid=(rows // tm,),
        in_specs=[row_spec, row_spec, row_spec,
                  pl.BlockSpec((tm, D_MODEL), lambda i: (i, C_GA // D_MODEL)),
                  pl.BlockSpec((tm, D_MODEL), lambda i: (i, C_GB // D_MODEL)),
                  w_spec, w_spec, w_spec],
        out_specs=row_spec,
        out_shape=jax.ShapeDtypeStruct(x.shape, F32),
        compiler_params=_params(("arbitrary",)),
        name="out_short",
    )(x, gated, bgy, proj, proj, wro, wco, wo)


def _tile_rows(t, n):
    return jnp.tile(t, (n, 1))


def kernel(x_prompt, x_sample, state_ret, state_conv, meta_tokens, norm_mix_g, w_in, conv_w, w_ret_o,
           w_conv_o, w_o, norm_ffn_g, w_gate_up, w_down, final_norm_g):
    depth = w_in.shape[0]
    nb, seq, _ = x_prompt.shape
    ns, ls, _ = x_sample.shape

    win_b, wro_b, wco_b, wo_b = (w.astype(BF16) for w in (w_in, w_ret_o, w_conv_o, w_o))
    wgu_b, wd_b = w_gate_up.astype(BF16), w_down.astype(BF16)
    gm = norm_mix_g.reshape(depth, 1, D_MODEL)
    gf = norm_ffn_g.reshape(depth, 1, D_MODEL)
    fg = final_norm_g.reshape(1, D_MODEL)

    rope_meta = _rope_tables(np.arange(N_META))
    rope_main = _rope_tables(N_META + np.arange(seq))
    rope_samp = _rope_tables(PAST_LEN + np.arange(ls))
    dec_meta, dec_main, dec_samp = _decay_tables(N_META), _decay_tables(CHUNK), _decay_tables(ls)

    h_meta = jnp.broadcast_to(meta_tokens[None].astype(F32), (nb, N_META, D_MODEL)).reshape(nb * N_META, D_MODEL)
    h_main = x_prompt
    h_samp = x_sample.reshape(ns * ls, D_MODEL)
    zero_s = jnp.zeros((depth, nb, HEADS, DH, DH), F32)
    zero_c = jnp.zeros((nb, CONV_W - 1, D_MODEL), F32)

    sb_samp, sb_meta = 8, nb
    s_p, c_p, c_s = [], [], []
    s_meta = s_samp = None
    for l in range(depth):
        last = l == depth - 1
        pm = _proj(h_meta, l, gm, win_b, nb * N_META)
        gated, bgy, s_meta, c_m = _ret_short(pm, zero_s, l, zero_c, conv_w, rope_meta, dec_meta, N_META, sb_meta,
                                             s_meta)
        if not last:
            h_meta = _out_short(h_meta, gated, bgy, pm, l, wro_b, wco_b, wo_b, nb * N_META)
            h_meta = _ffn(h_meta, l, gf, wgu_b, wd_b, fg, False, nb * N_META)

        h_main, s_l, c_l = _mixer_main(h_main, s_meta, c_m, l, gm, win_b, conv_w, wro_b, wco_b, wo_b,
                                       rope_main, dec_main)
        h_main = _ffn(h_main.reshape(nb * seq, D_MODEL), l, gf, wgu_b, wd_b, fg, last, 512)
        h_main = h_main.reshape(nb, seq, D_MODEL)
        s_p.append(s_l)
        c_p.append(c_l)

        ps = _proj(h_samp, l, gm, win_b, 256)
        gated, bgy, s_samp, c_l = _ret_short(ps, state_ret, l, state_conv[l], conv_w, rope_samp, dec_samp, ls,
                                             sb_samp, s_samp)
        h_samp = _out_short(h_samp, gated, bgy, ps, l, wro_b, wco_b, wo_b, 256)
        h_samp = _ffn(h_samp, l, gf, wgu_b, wd_b, fg, last, 256)
        c_s.append(c_l)

    return (h_main, h_samp.reshape(ns, ls, D_MODEL), jnp.stack(s_p), jnp.stack(c_p), s_samp, jnp.stack(c_s))
```

```python
import functools

import numpy as np
import jax
import jax.numpy as jnp
from jax import lax
from jax.experimental import pallas as pl
from jax.experimental.pallas import tpu as pltpu

D_MODEL = 1024
N_META = 16
HEADS = 8
DH = D_MODEL // HEADS
CHUNK = 128
ROPE_BASE = 10000.0
CONV_W = 3
D_FF = ((8 * D_MODEL + 3 * 256 - 1) // (3 * 256)) * 256
EPS = 1e-6
PAST_LEN = 16384
N_IN = 9 * D_MODEL
C_Q, C_K, C_V, C_G, C_BG, C_CG, C_HC, C_GA, C_GB = (i * D_MODEL for i in range(9))

F32 = jnp.float32
BF16 = jnp.bfloat16

VMEM_LIMIT_BYTES = 58 * 1024 * 1024

MAIN_BG = 4


def _resident(shape):
    nd = len(shape)
    return pl.BlockSpec(shape, lambda *_: (0,) * nd, pipeline_mode=pl.Buffered(1))


def _resident_layer(shape, layer):
    nd = len(shape)
    return pl.BlockSpec((None,) + tuple(shape), lambda *_: (layer,) + (0,) * nd, pipeline_mode=pl.Buffered(1))


def _params(sem):
    return pltpu.CompilerParams(dimension_semantics=sem, vmem_limit_bytes=VMEM_LIMIT_BYTES)


def _dot(a, b):
    return jnp.dot(a, b, preferred_element_type=F32)


def _rms_f32(x, g):
    return x * lax.rsqrt(jnp.mean(x * x, axis=-1, keepdims=True) + EPS) * g


def _sigmoid(x):
    return 1.0 / (1.0 + jnp.exp(-x))


def _silu(x):
    return x * _sigmoid(x)


def _rotary(t, cos, sin):
    return t * cos + pltpu.roll(t, DH // 2, 1) * sin


def _group_norm(o):
    mu = jnp.mean(o, axis=-1, keepdims=True)
    d = o - mu
    var = jnp.mean(d * d, axis=-1, keepdims=True)
    return d * lax.rsqrt(var + EPS)


def _retention_head(q, k, kd, v, s, dmask, qdec, gl):
    scores = lax.dot_general(q, k, (((1,), (1,)), ((), ())), preferred_element_type=F32) * dmask
    inner = _dot(scores.astype(BF16), v)
    cross = _dot(q, s.astype(BF16)) * qdec
    s_new = gl * s + lax.dot_general(kd, v, (((0,), (0,)), ((), ())), preferred_element_type=F32)
    return inner + cross, s_new


def _retention_head_paired(q, kt, kdt, v, s, dmask, qdec, gl):
    L = q.shape[0]
    sc = _dot(q, jnp.concatenate([kt, s.astype(BF16)], axis=1))
    scores = (sc[:, :L] * dmask).astype(BF16)
    iu = _dot(jnp.concatenate([scores, kdt], axis=0), v)
    return iu[:L] + sc[:, L:] * qdec, gl * s + iu[L:]


def _short_conv(u, tail, cw, rows):
    r1 = pltpu.roll(u, 1, 0)
    r2 = pltpu.roll(u, 2, 0)
    t0, t1 = tail[0:1, :], tail[1:2, :]
    sh1 = jnp.where(rows == 0, t1, r1)
    sh2 = jnp.where(rows == 0, t0, jnp.where(rows == 1, t1, r2))
    return cw[0:1, :] * sh2 + cw[1:2, :] * sh1 + cw[2:3, :] * u


def _log_gamma():
    return np.log1p(-np.exp2(-5.0 - np.arange(HEADS, dtype=np.float64)))


def _const(t):
    return jnp.asarray(np.asarray(t, dtype=np.float32))


def _rope_tables(pos):
    half = DH // 2
    inv = np.power(ROPE_BASE, -np.arange(half, dtype=np.float64) / half)
    ang = np.asarray(pos, dtype=np.float64)[:, None] * inv[None, :]
    cos, sin = np.cos(ang), np.sin(ang)
    cosf = np.concatenate([cos, cos], axis=-1)
    sinf = np.concatenate([-sin, sin], axis=-1)
    scale = DH ** -0.5
    return tuple(_const(t) for t in (cosf, sinf, cosf * scale, sinf * scale))


def _decay_tables(L):
    log_g = _log_gamma()
    idx = np.arange(L, dtype=np.float64)
    diff = idx[:, None] - idx[None, :]
    dmask = np.where(diff >= 0, np.exp(log_g[:, None, None] * np.maximum(diff, 0.0)[None]), 0.0)
    qdec = np.exp(log_g[:, None] * (idx + 1.0)[None])
    kdec = np.exp(log_g[:, None] * (L - 1.0 - idx)[None])
    gl = np.exp(log_g * L)
    lanes = lambda t: np.repeat(t.T, DH, axis=1)
    return tuple(_const(t) for t in (dmask, lanes(qdec), lanes(kdec), np.repeat(gl, DH)[None, :], kdec[:, None, :]))


def _mixer_main_kernel(x_ref, s0_ref, c0_ref, g_ref, win_ref, cw_ref, wro_ref, wco_ref, wo_ref,
                       cq_ref, sq_ref, ck_ref, sk_ref, dmask_ref, qdec_ref, kdec_ref, gl_ref,
                       h_ref, s_ref, c_ref, q_s, kt_s, kdt_s, v_s, o_s):
    bg = x_ref.shape[0]
    m = bg * CHUNK

    @pl.when(pl.program_id(1) == 0)
    def _():
        s_ref[...] = s0_ref[...]
        c_ref[...] = c0_ref[...]

    x = x_ref[...].reshape(m, D_MODEL)
    hn = _rms_f32(x, g_ref[...]).astype(BF16)

    cq, sq, ck, sk = cq_ref[...], sq_ref[...], ck_ref[...], sk_ref[...]
    q = _dot(hn, win_ref[:, C_Q:C_Q + D_MODEL])
    k = _dot(hn, win_ref[:, C_K:C_K + D_MODEL])
    v_s[...] = _dot(hn, win_ref[:, C_V:C_V + D_MODEL]).astype(BF16)
    for b in range(bg):
        r = slice(b * CHUNK, (b + 1) * CHUNK)
        for h in range(HEADS):
            cl = slice(h * DH, (h + 1) * DH)
            q_s[r, cl] = _rotary(q[r, cl], cq, sq).astype(BF16)
            krt = _rotary(k[r, cl], ck, sk).T
            kt_s[b * HEADS + h] = krt.astype(BF16)
            kdt_s[b * HEADS + h] = (krt * kdec_ref[h]).astype(BF16)

    for b in range(bg):
        r = slice(b * CHUNK, (b + 1) * CHUNK)
        for h in range(HEADS):
            cl = slice(h * DH, (h + 1) * DH)
            o, s_new = _retention_head_paired(q_s[r, cl], kt_s[b * HEADS + h], kdt_s[b * HEADS + h], v_s[r, cl],
                                              s_ref[b, h], dmask_ref[h], qdec_ref[:, cl], gl_ref[:, cl])
            s_ref[b, h] = s_new
            o_s[r, cl] = _group_norm(o)

    g = _dot(hn, win_ref[:, C_G:C_G + D_MODEL])
    ret_out = _dot((_silu(g) * o_s[...]).astype(BF16), wro_ref[...])

    bgate = _dot(hn, win_ref[:, C_BG:C_BG + D_MODEL])
    u = _dot(hn, win_ref[:, C_CG:C_CG + D_MODEL]) * _dot(hn, win_ref[:, C_HC:C_HC + D_MODEL])
    rows = lax.broadcasted_iota(jnp.int32, (CHUNK, D_MODEL), 0)
    cw = cw_ref[...]
    ys = []
    for b in range(bg):
        ub = u[b * CHUNK:(b + 1) * CHUNK]
        ys.append(_short_conv(ub, c_ref[b], cw, rows))
        c_ref[b] = ub[CHUNK - (CONV_W - 1):, :]
    y = jnp.concatenate(ys, axis=0)
    conv_out = _dot((bgate * y).astype(BF16), wco_ref[...])

    ga = _dot(hn, win_ref[:, C_GA:C_GA + D_MODEL])
    gb = _dot(hn, win_ref[:, C_GB:C_GB + D_MODEL])
    merged = _sigmoid(ga) * ret_out + _sigmoid(gb) * conv_out
    out = x + _dot(merged.astype(BF16), wo_ref[...])
    h_ref[...] = out.reshape(bg, CHUNK, D_MODEL)


def _mixer_main(x, s0, c0, layer, g, win, cw, wro, wco, wo, rope, decay):
    nb, seq, _ = x.shape
    bg = MAIN_BG
    cq, sq, ck, sk = rope
    dmask, qdec, _, gl, kdec = decay
    m = bg * CHUNK
    rope_spec = pl.BlockSpec((CHUNK, DH), lambda i, c: (c, 0))
    return pl.pallas_call(
        _mixer_main_kernel,
        grid=(nb // bg, seq // CHUNK),
        in_specs=[
            pl.BlockSpec((bg, CHUNK, D_MODEL), lambda i, c: (i, c, 0)),
            pl.BlockSpec((None, bg, HEADS, DH, DH), lambda i, c: (layer, i, 0, 0, 0)),
            pl.BlockSpec((bg, CONV_W - 1, D_MODEL), lambda i, c: (i, 0, 0)),
            _resident_layer((1, D_MODEL), layer),
            _resident_layer((D_MODEL, N_IN), layer),
            _resident_layer((CONV_W, D_MODEL), layer),
            _resident_layer((D_MODEL, D_MODEL), layer),
            _resident_layer((D_MODEL, D_MODEL), layer),
            _resident_layer((D_MODEL, D_MODEL), layer),
            rope_spec, rope_spec, rope_spec, rope_spec,
            _resident((HEADS, CHUNK, CHUNK)),
            _resident((CHUNK, D_MODEL)),
            _resident((HEADS, 1, CHUNK)),
            _resident((1, D_MODEL)),
        ],
        out_specs=[
            pl.BlockSpec((bg, CHUNK, D_MODEL), lambda i, c: (i, c, 0)),
            pl.BlockSpec((bg, HEADS, DH, DH), lambda i, c: (i, 0, 0, 0)),
            pl.BlockSpec((bg, CONV_W - 1, D_MODEL), lambda i, c: (i, 0, 0)),
        ],
        out_shape=[
            jax.ShapeDtypeStruct(x.shape, F32),
            jax.ShapeDtypeStruct(s0.shape[1:], F32),
            jax.ShapeDtypeStruct(c0.shape, F32),
        ],
        scratch_shapes=[
            pltpu.VMEM((m, D_MODEL), BF16),
            pltpu.VMEM((bg * HEADS, DH, CHUNK), BF16),
            pltpu.VMEM((bg * HEADS, DH, CHUNK), BF16),
            pltpu.VMEM((m, D_MODEL), BF16),
            pltpu.VMEM((m, D_MODEL), F32),
        ],
        compiler_params=_params(("arbitrary", "arbitrary")),
        name="mixer_main",
    )(x, s0, c0, g, win, cw, wro, wco, wo, cq, sq, ck, sk, dmask, qdec, kdec, gl)


def _ffn_kernel(x_ref, g_ref, wgu_ref, wd_ref, fg_ref, o_ref, *, final):
    x = x_ref[...]
    hn = _rms_f32(x, g_ref[...]).astype(BF16)
    a = _dot(hn, wgu_ref[:, :D_FF])
    b = _dot(hn, wgu_ref[:, D_FF:])
    y = x + _dot((_silu(a) * b).astype(BF16), wd_ref[...])
    if final:
        y = _rms_f32(y, fg_ref[...])
    o_ref[...] = y


def _ffn(x, layer, g, wgu, wd, fg, final, tm):
    rows = x.shape[0]
    row_spec = pl.BlockSpec((tm, D_MODEL), lambda i: (i, 0))
    return pl.pallas_call(
        functools.partial(_ffn_kernel, final=final),
        grid=(rows // tm,),
        in_specs=[row_spec, _resident_layer((1, D_MODEL), layer), _resident_layer((D_MODEL, 2 * D_FF), layer),
                  _resident_layer((D_FF, D_MODEL), layer), _resident((1, D_MODEL))],
        out_specs=row_spec,
        out_shape=jax.ShapeDtypeStruct(x.shape, F32),
        compiler_params=_params(("arbitrary",)),
        name="ffn",
    )(x, g, wgu, wd, fg)


def _proj_kernel(x_ref, g_ref, win_ref, o_ref):
    hn = _rms_f32(x_ref[...], g_ref[...]).astype(BF16)
    o_ref[...] = _dot(hn, win_ref[...])


def _proj(x, layer, g, win, tm):
    rows = x.shape[0]
    return pl.pallas_call(
        _proj_kernel,
        grid=(rows // tm,),
        in_specs=[pl.BlockSpec((tm, D_MODEL), lambda i: (i, 0)), _resident_layer((1, D_MODEL), layer),
                  _resident_layer((D_MODEL, N_IN), layer)],
        out_specs=pl.BlockSpec((tm, N_IN), lambda i: (i, 0)),
        out_shape=jax.ShapeDtypeStruct((rows, N_IN), F32),
        compiler_params=_params(("arbitrary",)),
        name="proj_short",
    )(x, g, win)


def _ret_short_kernel(p_ref, s_ref, c_ref, cw_ref, cq_ref, sq_ref, ck_ref, sk_ref,
                      dmask_ref, qdec_ref, kdec_ref, gl_ref, *rest, L):
    gated_ref, bgy_ref, sn_ref, cn_ref = rest[-4:]
    sb = c_ref.shape[0]
    cq, sq, ck, sk = cq_ref[...], sq_ref[...], ck_ref[...], sk_ref[...]
    rows = lax.broadcasted_iota(jnp.int32, (L, D_MODEL), 0)
    cw = cw_ref[...]
    for b in range(sb):
        r = slice(b * L, (b + 1) * L)
        for h in range(HEADS):
            cl = slice(h * DH, (h + 1) * DH)
            q = _rotary(p_ref[r, C_Q + h * DH:C_Q + (h + 1) * DH], cq, sq)
            k = _rotary(p_ref[r, C_K + h * DH:C_K + (h + 1) * DH], ck, sk)
            v = p_ref[r, C_V + h * DH:C_V + (h + 1) * DH]
            kd = k * kdec_ref[:, cl]
            o, s_new = _retention_head(q.astype(BF16), k.astype(BF16), kd.astype(BF16), v.astype(BF16),
                                       s_ref[b, h], dmask_ref[h], qdec_ref[:, cl], gl_ref[:, cl])
            sn_ref[b, h] = s_new
            g = p_ref[r, C_G + h * DH:C_G + (h + 1) * DH]
            gated_ref[r, cl] = _silu(g) * _group_norm(o)
        u = p_ref[r, C_CG:C_CG + D_MODEL] * p_ref[r, C_HC:C_HC + D_MODEL]
        y = _short_conv(u, c_ref[b], cw, rows)
        bgy_ref[r, :] = p_ref[r, C_BG:C_BG + D_MODEL] * y
        cn_ref[b] = u[L - (CONV_W - 1):, :]


def _ret_short(proj, state, layer, cprev, cw, rope, decay, L, sb, s_out_prev):
    nseq = cprev.shape[0]
    cq, sq, ck, sk = rope
    dmask, qdec, kdec, gl, _ = decay
    rows = nseq * L
    state_spec = pl.BlockSpec((None, sb, HEADS, DH, DH), lambda i: (layer, i, 0, 0, 0))
    in_specs = [
        pl.BlockSpec((sb * L, C_GA), lambda i: (i, 0)),
        state_spec,
        pl.BlockSpec((sb, CONV_W - 1, D_MODEL), lambda i: (i, 0, 0)),
        _resident_layer((CONV_W, D_MODEL), layer),
        _resident((L, DH)), _resident((L, DH)), _resident((L, DH)), _resident((L, DH)),
        _resident((HEADS, L, L)),
        _resident((L, D_MODEL)),
        _resident((L, D_MODEL)),
        _resident((1, D_MODEL)),
    ]
    args = [proj, state, cprev, cw, cq, sq, ck, sk, dmask, qdec, kdec, gl]
    aliases = {}
    if s_out_prev is not None:
        aliases = {len(args): 2}
        in_specs.append(pl.BlockSpec(memory_space=pl.ANY))
        args.append(s_out_prev)
    return pl.pallas_call(
        functools.partial(_ret_short_kernel, L=L),
        grid=(nseq // sb,),
        in_specs=in_specs,
        out_specs=[
            pl.BlockSpec((sb * L, D_MODEL), lambda i: (i, 0)),
            pl.BlockSpec((sb * L, D_MODEL), lambda i: (i, 0)),
            state_spec,
            pl.BlockSpec((sb, CONV_W - 1, D_MODEL), lambda i: (i, 0, 0)),
        ],
        out_shape=[
            jax.ShapeDtypeStruct((rows, D_MODEL), F32),
            jax.ShapeDtypeStruct((rows, D_MODEL), F32),
            jax.ShapeDtypeStruct(state.shape, F32),
            jax.ShapeDtypeStruct((nseq, CONV_W - 1, D_MODEL), F32),
        ],
        input_output_aliases=aliases,
        compiler_params=_params(("arbitrary",)),
        name="ret_short",
    )(*args)


def _out_short_kernel(x_ref, gated_ref, bgy_ref, ga_ref, gb_ref, wro_ref, wco_ref, wo_ref, o_ref):
    ret_out = _dot(gated_ref[...].astype(BF16), wro_ref[...])
    conv_out = _dot(bgy_ref[...].astype(BF16), wco_ref[...])
    merged = _sigmoid(ga_ref[...]) * ret_out + _sigmoid(gb_ref[...]) * conv_out
    o_ref[...] = x_ref[...] + _dot(merged.astype(BF16), wo_ref[...])


def _out_short(x, gated, bgy, proj, layer, wro, wco, wo, tm):
    rows = x.shape[0]
    row_spec = pl.BlockSpec((tm, D_MODEL), lambda i: (i, 0))
    w_spec = _resident_layer((D_MODEL, D_MODEL), layer)
    return pl.pallas_call(
        _out_short_kernel,
        grid=(rows // tm,),
        in_specs=[row_spec, row_spec, row_spec,
                  pl.BlockSpec((tm, D_MODEL), lambda i: (i, C_GA // D_MODEL)),
                  pl.BlockSpec((tm, D_MODEL), lambda i: (i, C_GB // D_MODEL)),
                  w_spec, w_spec, w_spec],
        out_specs=row_spec,
        out_shape=jax.ShapeDtypeStruct(x.shape, F32),
        compiler_params=_params(("arbitrary",)),
        name="out_short",
    )(x, gated, bgy, proj, proj, wro, wco, wo)


def _tile_rows(t, n):
    return jnp.tile(t, (n, 1))


def kernel(x_prompt, x_sample, state_ret, state_conv, meta_tokens, norm_mix_g, w_in, conv_w, w_ret_o,
           w_conv_o, w_o, norm_ffn_g, w_gate_up, w_down, final_norm_g):
    depth = w_in.shape[0]
    nb, seq, _ = x_prompt.shape
    ns, ls, _ = x_sample.shape

    win_b, wro_b, wco_b, wo_b = (w.astype(BF16) for w in (w_in, w_ret_o, w_conv_o, w_o))
    wgu_b, wd_b = w_gate_up.astype(BF16), w_down.astype(BF16)
    gm = norm_mix_g.reshape(depth, 1, D_MODEL)
    gf = norm_ffn_g.reshape(depth, 1, D_MODEL)
    fg = final_norm_g.reshape(1, D_MODEL)

    rope_meta = _rope_tables(np.arange(N_META))
    rope_main = _rope_tables(N_META + np.arange(seq))
    rope_samp = _rope_tables(PAST_LEN + np.arange(ls))
    dec_meta, dec_main, dec_samp = _decay_tables(N_META), _decay_tables(CHUNK), _decay_tables(ls)

    h_meta = jnp.broadcast_to(meta_tokens[None].astype(F32), (nb, N_META, D_MODEL)).reshape(nb * N_META, D_MODEL)
    h_main = x_prompt
    h_samp = x_sample.reshape(ns * ls, D_MODEL)
    zero_s = jnp.zeros((depth, nb, HEADS, DH, DH), F32)
    zero_c = jnp.zeros((nb, CONV_W - 1, D_MODEL), F32)

    sb_samp, sb_meta = 8, nb
    s_p, c_p, c_s = [], [], []
    s_meta = s_samp = None
    for l in range(depth):
        last = l == depth - 1
        pm = _proj(h_meta, l, gm, win_b, nb * N_META)
        gated, bgy, s_meta, c_m = _ret_short(pm, zero_s, l, zero_c, conv_w, rope_meta, dec_meta, N_META, sb_meta,
                                             s_meta)
        if not last:
            h_meta = _out_short(h_meta, gated, bgy, pm, l, wro_b, wco_b, wo_b, nb * N_META)
            h_meta = _ffn(h_meta, l, gf, wgu_b, wd_b, fg, False, nb * N_META)

        h_main, s_l, c_l = _mixer_main(h_main, s_meta, c_m, l, gm, win_b, conv_w, wro_b, wco_b, wo_b,
                                       rope_main, dec_main)
        h_main = _ffn(h_main.reshape(nb * seq, D_MODEL), l, gf, wgu_b, wd_b, fg, last, 512)
        h_main = h_main.reshape(nb, seq, D_MODEL)
        s_p.append(s_l)
        c_p.append(c_l)

        ps = _proj(h_samp, l, gm, win_b, 256)
        gated, bgy, s_samp, c_l = _ret_short(ps, state_ret, l, state_conv[l], conv_w, rope_samp, dec_samp, ls,
                                             sb_samp, s_samp)
        h_samp = _out_short(h_samp, gated, bgy, ps, l, wro_b, wco_b, wo_b, 256)
        h_samp = _ffn(h_samp, l, gf, wgu_b, wd_b, fg, last, 256)
        c_s.append(c_l)

    return (h_main, h_samp.reshape(ns, ls, D_MODEL), jnp.stack(s_p), jnp.stack(c_p), s_samp, jnp.stack(c_s))
```

```python
import functools

import numpy as np
import jax
import jax.numpy as jnp
from jax import lax
from jax.experimental import pallas as pl
from jax.experimental.pallas import tpu as pltpu

D_MODEL = 1024
N_META = 16
HEADS = 8
DH = D_MODEL // HEADS
CHUNK = 128
ROPE_BASE = 10000.0
CONV_W = 3
D_FF = ((8 * D_MODEL + 3 * 256 - 1) // (3 * 256)) * 256
EPS = 1e-6
PAST_LEN = 16384
N_IN = 9 * D_MODEL
C_Q, C_K, C_V, C_G, C_BG, C_CG, C_HC, C_GA, C_GB = (i * D_MODEL for i in range(9))

F32 = jnp.float32
BF16 = jnp.bfloat16

VMEM_LIMIT_BYTES = 58 * 1024 * 1024

MAIN_BG = 4
MAIN_FFN_TM = 512
SAMPLE_SB = 16
UPDATE_SB = 8


def _resident(shape):
    nd = len(shape)
    return pl.BlockSpec(shape, lambda *_: (0,) * nd, pipeline_mode=pl.Buffered(1))


def _resident_layer(shape, layer):
    nd = len(shape)
    return pl.BlockSpec((None,) + tuple(shape), lambda *_: (layer,) + (0,) * nd, pipeline_mode=pl.Buffered(1))


def _params(sem):
    return pltpu.CompilerParams(dimension_semantics=sem, vmem_limit_bytes=VMEM_LIMIT_BYTES)


def _dot(a, b):
    return jnp.dot(a, b, preferred_element_type=F32)


def _dot_nt(a, b):
    return lax.dot_general(a, b, (((1,), (1,)), ((), ())), preferred_element_type=F32)


def _dot_tn(a, b):
    return lax.dot_general(a, b, (((0,), (0,)), ((), ())), preferred_element_type=F32)


def _rms_f32(x, g):
    return x * lax.rsqrt(jnp.mean(x * x, axis=-1, keepdims=True) + EPS) * g


def _sigmoid(x):
    return 1.0 / (1.0 + jnp.exp(-x))


def _silu(x):
    return x * _sigmoid(x)


def _rotary(t, cos, sin):
    return t * cos + pltpu.roll(t, DH // 2, 1) * sin


def _group_norm(o):
    mu = jnp.mean(o, axis=-1, keepdims=True)
    d = o - mu
    var = jnp.mean(d * d, axis=-1, keepdims=True)
    return d * lax.rsqrt(var + EPS)


def _retention_head_paired(q, kt, kdt, v, s, dmask, qdec, gl):
    L = q.shape[0]
    sc = _dot(q, jnp.concatenate([kt, s.astype(BF16)], axis=1))
    scores = (sc[:, :L] * dmask).astype(BF16)
    iu = _dot(jnp.concatenate([scores, kdt], axis=0), v)
    return iu[:L] + sc[:, L:] * qdec, gl * s + iu[L:]


def _short_conv(u, tail, cw, rows):
    r1 = pltpu.roll(u, 1, 0)
    r2 = pltpu.roll(u, 2, 0)
    t0, t1 = tail[0:1, :], tail[1:2, :]
    sh1 = jnp.where(rows == 0, t1, r1)
    sh2 = jnp.where(rows == 0, t0, jnp.where(rows == 1, t1, r2))
    return cw[0:1, :] * sh2 + cw[1:2, :] * sh1 + cw[2:3, :] * u


def _out_proj(x, gated, bgy, ga, gb, wro_ref, wco_ref, wo_ref):
    ret_out = _dot(gated.astype(BF16), wro_ref[...])
    conv_out = _dot(bgy.astype(BF16), wco_ref[...])
    merged = _sigmoid(ga) * ret_out + _sigmoid(gb) * conv_out
    return x + _dot(merged.astype(BF16), wo_ref[...])


def _ffn_block(x, g_ref, wgu_ref, wd_ref):
    hn = _rms_f32(x, g_ref[...]).astype(BF16)
    a = _dot(hn, wgu_ref[:, :D_FF])
    b = _dot(hn, wgu_ref[:, D_FF:])
    return x + _dot((_silu(a) * b).astype(BF16), wd_ref[...])


def _log_gamma():
    return np.log1p(-np.exp2(-5.0 - np.arange(HEADS, dtype=np.float64)))


def _const(t):
    return jnp.asarray(np.asarray(t, dtype=np.float32))


def _rope_tables(pos):
    half = DH // 2
    inv = np.power(ROPE_BASE, -np.arange(half, dtype=np.float64) / half)
    ang = np.asarray(pos, dtype=np.float64)[:, None] * inv[None, :]
    cos, sin = np.cos(ang), np.sin(ang)
    cosf = np.concatenate([cos, cos], axis=-1)
    sinf = np.concatenate([-sin, sin], axis=-1)
    scale = DH ** -0.5
    return tuple(_const(t) for t in (cosf, sinf, cosf * scale, sinf * scale))


def _decay_tables(L):
    log_g = _log_gamma()
    idx = np.arange(L, dtype=np.float64)
    diff = idx[:, None] - idx[None, :]
    dmask = np.where(diff >= 0, np.exp(log_g[:, None, None] * np.maximum(diff, 0.0)[None]), 0.0)
    qdec = np.exp(log_g[:, None] * (idx + 1.0)[None])
    kdec = np.exp(log_g[:, None] * (L - 1.0 - idx)[None])
    gl = np.exp(log_g * L)
    lanes = lambda t: np.repeat(t.T, DH, axis=1)
    return tuple(_const(t) for t in (dmask, lanes(qdec), lanes(kdec), np.repeat(gl, DH)[None, :], kdec[:, None, :]))


def _mixer_main_kernel(x_ref, s0_ref, c0_ref, g_ref, win_ref, cw_ref, wro_ref, wco_ref, wo_ref,
                       cq_ref, sq_ref, ck_ref, sk_ref, dmask_ref, qdec_ref, kdec_ref, gl_ref,
                       h_ref, s_ref, c_ref, q_s, kt_s, kdt_s, v_s, o_s):
    bg = x_ref.shape[0]
    m = bg * CHUNK

    @pl.when(pl.program_id(1) == 0)
    def _():
        for b in range(bg):
            s_ref[b] = s0_ref[...]
            c_ref[b] = c0_ref[...]

    x = x_ref[...].reshape(m, D_MODEL)
    hn = _rms_f32(x, g_ref[...]).astype(BF16)

    cq, sq, ck, sk = cq_ref[...], sq_ref[...], ck_ref[...], sk_ref[...]
    q = _dot(hn, win_ref[:, C_Q:C_Q + D_MODEL])
    k = _dot(hn, win_ref[:, C_K:C_K + D_MODEL])
    v_s[...] = _dot(hn, win_ref[:, C_V:C_V + D_MODEL]).astype(BF16)
    for b in range(bg):
        r = slice(b * CHUNK, (b + 1) * CHUNK)
        for h in range(HEADS):
            cl = slice(h * DH, (h + 1) * DH)
            q_s[r, cl] = _rotary(q[r, cl], cq, sq).astype(BF16)
            krt = _rotary(k[r, cl], ck, sk).T
            kt_s[b * HEADS + h] = krt.astype(BF16)
            kdt_s[b * HEADS + h] = (krt * kdec_ref[h]).astype(BF16)

    for b in range(bg):
        r = slice(b * CHUNK, (b + 1) * CHUNK)
        for h in range(HEADS):
            cl = slice(h * DH, (h + 1) * DH)
            o, s_new = _retention_head_paired(q_s[r, cl], kt_s[b * HEADS + h], kdt_s[b * HEADS + h], v_s[r, cl],
                                              s_ref[b, h], dmask_ref[h], qdec_ref[:, cl], gl_ref[:, cl])
            s_ref[b, h] = s_new
            o_s[r, cl] = _group_norm(o)

    g = _dot(hn, win_ref[:, C_G:C_G + D_MODEL])
    gated = _silu(g) * o_s[...]

    bgate = _dot(hn, win_ref[:, C_BG:C_BG + D_MODEL])
    u = _dot(hn, win_ref[:, C_CG:C_CG + D_MODEL]) * _dot(hn, win_ref[:, C_HC:C_HC + D_MODEL])
    rows = lax.broadcasted_iota(jnp.int32, (CHUNK, D_MODEL), 0)
    cw = cw_ref[...]
    ys = []
    for b in range(bg):
        ub = u[b * CHUNK:(b + 1) * CHUNK]
        ys.append(_short_conv(ub, c_ref[b], cw, rows))
        c_ref[b] = ub[CHUNK - (CONV_W - 1):, :]
    bgy = bgate * jnp.concatenate(ys, axis=0)

    ga = _dot(hn, win_ref[:, C_GA:C_GA + D_MODEL])
    gb = _dot(hn, win_ref[:, C_GB:C_GB + D_MODEL])
    out = _out_proj(x, gated, bgy, ga, gb, wro_ref, wco_ref, wo_ref)
    h_ref[...] = out.reshape(bg, CHUNK, D_MODEL)


def _mixer_main(x, s0, c0, layer, g, win, cw, wro, wco, wo, rope, decay):
    nb, seq, _ = x.shape
    bg = MAIN_BG
    cq, sq, ck, sk = rope
    dmask, qdec, _, gl, kdec = decay
    m = bg * CHUNK
    rope_spec = pl.BlockSpec((CHUNK, DH), lambda i, c: (c, 0))
    return pl.pallas_call(
        _mixer_main_kernel,
        grid=(nb // bg, seq // CHUNK),
        in_specs=[
            pl.BlockSpec((bg, CHUNK, D_MODEL), lambda i, c: (i, c, 0)),
            _resident_layer((HEADS, DH, DH), 0),
            _resident_layer((CONV_W - 1, D_MODEL), 0),
            _resident_layer((1, D_MODEL), layer),
            _resident_layer((D_MODEL, N_IN), layer),
            _resident_layer((CONV_W, D_MODEL), layer),
            _resident_layer((D_MODEL, D_MODEL), layer),
            _resident_layer((D_MODEL, D_MODEL), layer),
            _resident_layer((D_MODEL, D_MODEL), layer),
            rope_spec, rope_spec, rope_spec, rope_spec,
            _resident((HEADS, CHUNK, CHUNK)),
            _resident((CHUNK, D_MODEL)),
            _resident((HEADS, 1, CHUNK)),
            _resident((1, D_MODEL)),
        ],
        out_specs=[
            pl.BlockSpec((bg, CHUNK, D_MODEL), lambda i, c: (i, c, 0)),
            pl.BlockSpec((bg, HEADS, DH, DH), lambda i, c: (i, 0, 0, 0)),
            pl.BlockSpec((bg, CONV_W - 1, D_MODEL), lambda i, c: (i, 0, 0)),
        ],
        out_shape=[
            jax.ShapeDtypeStruct(x.shape, F32),
            jax.ShapeDtypeStruct((nb, HEADS, DH, DH), F32),
            jax.ShapeDtypeStruct((nb, CONV_W - 1, D_MODEL), F32),
        ],
        scratch_shapes=[
            pltpu.VMEM((m, D_MODEL), BF16),
            pltpu.VMEM((bg * HEADS, DH, CHUNK), BF16),
            pltpu.VMEM((bg * HEADS, DH, CHUNK), BF16),
            pltpu.VMEM((m, D_MODEL), BF16),
            pltpu.VMEM((m, D_MODEL), F32),
        ],
        compiler_params=_params(("arbitrary", "arbitrary")),
        name="mixer_main",
    )(x, s0, c0, g, win, cw, wro, wco, wo, cq, sq, ck, sk, dmask, qdec, kdec, gl)


def _ffn_kernel(x_ref, g_ref, wgu_ref, wd_ref, fg_ref, o_ref, *, final):
    y = _ffn_block(x_ref[...], g_ref, wgu_ref, wd_ref)
    if final:
        y = _rms_f32(y, fg_ref[...])
    o_ref[...] = y


def _ffn(x, layer, g, wgu, wd, fg, final, tm):
    rows = x.shape[0]
    row_spec = pl.BlockSpec((tm, D_MODEL), lambda i: (i, 0))
    return pl.pallas_call(
        functools.partial(_ffn_kernel, final=final),
        grid=(rows // tm,),
        in_specs=[row_spec, _resident_layer((1, D_MODEL), layer), _resident_layer((D_MODEL, 2 * D_FF), layer),
                  _resident_layer((D_FF, D_MODEL), layer), _resident((1, D_MODEL))],
        out_specs=row_spec,
        out_shape=jax.ShapeDtypeStruct(x.shape, F32),
        compiler_params=_params(("arbitrary",)),
        name="ffn",
    )(x, g, wgu, wd, fg)


def _proj_kernel(x_ref, g_ref, win_ref, o_ref):
    hn = _rms_f32(x_ref[...], g_ref[...]).astype(BF16)
    o_ref[...] = _dot(hn, win_ref[...])


def _proj(x, layer, g, win, tm):
    rows = x.shape[0]
    return pl.pallas_call(
        _proj_kernel,
        grid=(rows // tm,),
        in_specs=[pl.BlockSpec((tm, D_MODEL), lambda i: (i, 0)), _resident_layer((1, D_MODEL), layer),
                  _resident_layer((D_MODEL, N_IN), layer)],
        out_specs=pl.BlockSpec((tm, N_IN), lambda i: (i, 0)),
        out_shape=jax.ShapeDtypeStruct((rows, N_IN), F32),
        compiler_params=_params(("arbitrary",)),
        name="proj_short",
    )(x, g, win)


def _ret_short_kernel(p_ref, s_ref, c_ref, cw_ref, cq_ref, sq_ref, ck_ref, sk_ref,
                      dmask_ref, qdec_ref, kdec_ref, gl_ref, gated_ref, bgy_ref, cn_ref, a_ref, b_ref, *, L, update):
    sb = c_ref.shape[0]
    cq, sq, ck, sk = cq_ref[...], sq_ref[...], ck_ref[...], sk_ref[...]
    rows = lax.broadcasted_iota(jnp.int32, (L, D_MODEL), 0)
    cw = cw_ref[...]
    if update:
        b_ref[...] = jnp.zeros_like(b_ref)
    for b in range(sb):
        r = slice(b * L, (b + 1) * L)
        staged = []
        for h in range(HEADS):
            cl = slice(h * DH, (h + 1) * DH)
            q = _rotary(p_ref[r, C_Q + h * DH:C_Q + (h + 1) * DH], cq, sq).astype(BF16)
            k = _rotary(p_ref[r, C_K + h * DH:C_K + (h + 1) * DH], ck, sk)
            v = p_ref[r, C_V + h * DH:C_V + (h + 1) * DH]
            kd = k * kdec_ref[:, cl]
            s = s_ref[b, h]
            scores = (_dot_nt(q, k.astype(BF16)) * dmask_ref[h]).astype(BF16)
            cross = _dot(q, s.astype(BF16)) * qdec_ref[:, cl]
            if update:
                a_ref[b, h] = gl_ref[:, cl] * s + _dot_tn(kd.astype(BF16), v.astype(BF16))
            else:
                a_ref[r, cl] = kd
                b_ref[r, cl] = v
            staged.append((scores, cross, v.astype(BF16)))
        for h, (scores, cross, v) in enumerate(staged):
            cl = slice(h * DH, (h + 1) * DH)
            o = _dot(scores, v) + cross
            g = p_ref[r, C_G + h * DH:C_G + (h + 1) * DH]
            gated_ref[r, cl] = _silu(g) * _group_norm(o)
        u = p_ref[r, C_CG:C_CG + D_MODEL] * p_ref[r, C_HC:C_HC + D_MODEL]
        y = _short_conv(u, c_ref[b], cw, rows)
        bgy_ref[r, :] = p_ref[r, C_BG:C_BG + D_MODEL] * y
        cn_ref[b] = u[L - (CONV_W - 1):, :]


def _ret_short(proj, row0, state, cprev, state_layer, layer, cw, rope, decay, nseq, L, sb, update):
    cq, sq, ck, sk = rope
    dmask, qdec, kdec, gl, _ = decay
    rows = nseq * L
    blk0 = row0 // (sb * L)
    row_spec = pl.BlockSpec((sb * L, D_MODEL), lambda i: (i, 0))
    row_shape = jax.ShapeDtypeStruct((rows, D_MODEL), F32)
    state_spec = pl.BlockSpec((sb, HEADS, DH, DH), lambda i: (i, 0, 0, 0))
    if update:
        ab_specs = [state_spec, pl.BlockSpec((8, DH), lambda i: (0, 0))]
        ab_shapes = [jax.ShapeDtypeStruct((nseq, HEADS, DH, DH), F32), jax.ShapeDtypeStruct((8, DH), F32)]
    else:
        ab_specs = [row_spec, row_spec]
        ab_shapes = [row_shape, row_shape]
    return pl.pallas_call(
        functools.partial(_ret_short_kernel, L=L, update=update),
        grid=(nseq // sb,),
        in_specs=[
            pl.BlockSpec((sb * L, C_GA), lambda i: (i + blk0, 0)),
            pl.BlockSpec((None, sb, HEADS, DH, DH), lambda i: (state_layer, i, 0, 0, 0)),
            pl.BlockSpec((None, sb, CONV_W - 1, D_MODEL), lambda i: (state_layer, i, 0, 0)),
            _resident_layer((CONV_W, D_MODEL), layer),
            _resident((L, DH)), _resident((L, DH)), _resident((L, DH)), _resident((L, DH)),
            _resident((HEADS, L, L)),
            _resident((L, D_MODEL)),
            _resident((L, D_MODEL)),
            _resident((1, D_MODEL)),
        ],
        out_specs=[row_spec, row_spec, pl.BlockSpec((sb, CONV_W - 1, D_MODEL), lambda i: (i, 0, 0))] + ab_specs,
        out_shape=[row_shape, row_shape, jax.ShapeDtypeStruct((nseq, CONV_W - 1, D_MODEL), F32)] + ab_shapes,
        compiler_params=_params(("arbitrary",)),
        name="ret_short",
    )(proj, state, cprev, cw, cq, sq, ck, sk, dmask, qdec, kdec, gl)


def _out_ffn_kernel(x_ref, gated_ref, bgy_ref, ga_ref, gb_ref, wro_ref, wco_ref, wo_ref,
                    g_ref, wgu_ref, wd_ref, fg_ref, o_ref, *, final):
    h = _out_proj(x_ref[...], gated_ref[...], bgy_ref[...], ga_ref[...], gb_ref[...], wro_ref, wco_ref, wo_ref)
    y = _ffn_block(h, g_ref, wgu_ref, wd_ref)
    if final:
        y = _rms_f32(y, fg_ref[...])
    o_ref[...] = y


def _out_ffn(x, gated, bgy, proj, layer, wro, wco, wo, g, wgu, wd, fg, final, rows, tm):
    row_spec = pl.BlockSpec((tm, D_MODEL), lambda i: (i, 0))
    w_spec = _resident_layer((D_MODEL, D_MODEL), layer)
    return pl.pallas_call(
        functools.partial(_out_ffn_kernel, final=final),
        grid=(rows // tm,),
        in_specs=[row_spec, row_spec, row_spec,
                  pl.BlockSpec((tm, D_MODEL), lambda i: (i, C_GA // D_MODEL)),
                  pl.BlockSpec((tm, D_MODEL), lambda i: (i, C_GB // D_MODEL)),
                  w_spec, w_spec, w_spec,
                  _resident_layer((1, D_MODEL), layer), _resident_layer((D_MODEL, 2 * D_FF), layer),
                  _resident_layer((D_FF, D_MODEL), layer), _resident((1, D_MODEL))],
        out_specs=row_spec,
        out_shape=jax.ShapeDtypeStruct((rows, D_MODEL), F32),
        compiler_params=_params(("arbitrary",)),
        name="out_ffn_short",
    )(x, gated, bgy, proj, proj, wro, wco, wo, g, wgu, wd, fg)


def _state_update_kernel(s_ref, gl_ref, *refs, L):
    depth, sb = s_ref.shape[:2]
    kd_refs, v_refs, o_ref = refs[:depth], refs[depth:2 * depth], refs[2 * depth]
    for l in range(depth):
        for b in range(sb):
            r = slice(b * L, (b + 1) * L)
            for h in range(HEADS):
                cl = slice(h * DH, (h + 1) * DH)
                upd = _dot_tn(kd_refs[l][r, cl].astype(BF16), v_refs[l][r, cl].astype(BF16))
                o_ref[l, b, h] = gl_ref[:, cl] * s_ref[l, b, h] + upd


def _state_update(state, gl, kds, vs, L, sb):
    depth, nseq = state.shape[:2]
    state_spec = pl.BlockSpec((depth, sb, HEADS, DH, DH), lambda i: (0, i, 0, 0, 0))
    row_spec = pl.BlockSpec((sb * L, D_MODEL), lambda i: (i, 0))
    return pl.pallas_call(
        functools.partial(_state_update_kernel, L=L),
        grid=(nseq // sb,),
        in_specs=[state_spec, _resident((1, D_MODEL))] + [row_spec] * (2 * depth),
        out_specs=state_spec,
        out_shape=jax.ShapeDtypeStruct(state.shape, F32),
        compiler_params=_params(("arbitrary",)),
        name="state_update",
    )(state, gl, *kds, *vs)


def kernel(x_prompt, x_sample, state_ret, state_conv, meta_tokens, norm_mix_g, w_in, conv_w, w_ret_o,
           w_conv_o, w_o, norm_ffn_g, w_gate_up, w_down, final_norm_g):
    depth = w_in.shape[0]
    nb, seq, _ = x_prompt.shape
    ns, ls, _ = x_sample.shape
    n_samp = ns * ls
    n_short = n_samp + N_META
    short_tm = n_short // 5
    assert short_tm * 5 == n_short and short_tm % 8 == 0 and n_samp % N_META == 0

    win_b, wro_b, wco_b, wo_b = (w.astype(BF16) for w in (w_in, w_ret_o, w_conv_o, w_o))
    wgu_b, wd_b = w_gate_up.astype(BF16), w_down.astype(BF16)
    gm = norm_mix_g.reshape(depth, 1, D_MODEL)
    gf = norm_ffn_g.reshape(depth, 1, D_MODEL)
    fg = final_norm_g.reshape(1, D_MODEL)

    rope_meta = _rope_tables(np.arange(N_META))
    rope_main = _rope_tables(N_META + np.arange(seq))
    rope_samp = _rope_tables(PAST_LEN + np.arange(ls))
    dec_meta, dec_main, dec_samp = _decay_tables(N_META), _decay_tables(CHUNK), _decay_tables(ls)

    h_main = x_prompt
    h_short = jnp.concatenate([x_sample.reshape(n_samp, D_MODEL), meta_tokens.astype(F32)], axis=0)
    zero_s = jnp.zeros((1, 1, HEADS, DH, DH), F32)
    zero_c = jnp.zeros((1, 1, CONV_W - 1, D_MODEL), F32)

    s_p, c_p, c_s, kds, vs = [], [], [], [], []
    y_samp = None
    for l in range(depth):
        last = l == depth - 1
        proj = _proj(h_short, l, gm, win_b, short_tm)
        gated_m, bgy_m, c_m, s_m, _ = _ret_short(proj, n_samp, zero_s, zero_c, 0, l, conv_w, rope_meta, dec_meta,
                                                 1, N_META, 1, True)
        gated_s, bgy_s, c_l, kd, v = _ret_short(proj, 0, state_ret, state_conv, l, l, conv_w, rope_samp, dec_samp,
                                                ns, ls, SAMPLE_SB, False)
        c_s.append(c_l)
        kds.append(kd)
        vs.append(v)

        h_main, s_l, c_l = _mixer_main(h_main, s_m, c_m, l, gm, win_b, conv_w, wro_b, wco_b, wo_b,
                                       rope_main, dec_main)
        h_main = _ffn(h_main.reshape(nb * seq, D_MODEL), l, gf, wgu_b, wd_b, fg, last, MAIN_FFN_TM)
        h_main = h_main.reshape(nb, seq, D_MODEL)
        s_p.append(s_l)
        c_p.append(c_l)

        if last:
            y_samp = _out_ffn(h_short, gated_s, bgy_s, proj, l, wro_b, wco_b, wo_b, gf, wgu_b, wd_b, fg, True,
                              n_samp, 256)
        else:
            gated = jnp.concatenate([gated_s, gated_m], axis=0)
            bgy = jnp.concatenate([bgy_s, bgy_m], axis=0)
            h_short = _out_ffn(h_short, gated, bgy, proj, l, wro_b, wco_b, wo_b, gf, wgu_b, wd_b, fg, False,
                               n_short, short_tm)

    s_s = _state_update(state_ret, dec_samp[3], kds, vs, ls, UPDATE_SB)
    return (h_main, y_samp.reshape(ns, ls, D_MODEL), jnp.stack(s_p), jnp.stack(c_p), s_s, jnp.stack(c_s))
```

```python
import functools

import numpy as np
import jax
import jax.numpy as jnp
from jax import lax
from jax.experimental import pallas as pl
from jax.experimental.pallas import tpu as pltpu

D_MODEL = 1024
N_META = 16
HEADS = 8
DH = D_MODEL // HEADS
CHUNK = 128
ROPE_BASE = 10000.0
CONV_W = 3
D_FF = ((8 * D_MODEL + 3 * 256 - 1) // (3 * 256)) * 256
EPS = 1e-6
PAST_LEN = 16384
N_IN = 9 * D_MODEL
C_Q, C_K, C_V, C_G, C_BG, C_CG, C_HC, C_GA, C_GB = (i * D_MODEL for i in range(9))

F32 = jnp.float32
BF16 = jnp.bfloat16

VMEM_LIMIT_BYTES = 58 * 1024 * 1024

MAIN_BG = 4
MAIN_FFN_TM = 512


def _resident(shape):
    nd = len(shape)
    return pl.BlockSpec(shape, lambda *_: (0,) * nd, pipeline_mode=pl.Buffered(1))


def _resident_layer(shape, layer):
    nd = len(shape)
    return pl.BlockSpec((None,) + tuple(shape), lambda *_: (layer,) + (0,) * nd, pipeline_mode=pl.Buffered(1))


def _params(sem):
    return pltpu.CompilerParams(dimension_semantics=sem, vmem_limit_bytes=VMEM_LIMIT_BYTES)


def _dot(a, b):
    return jnp.dot(a, b, preferred_element_type=F32)


def _dot_nt(a, b):
    return lax.dot_general(a, b, (((1,), (1,)), ((), ())), preferred_element_type=F32)


def _dot_tn(a, b):
    return lax.dot_general(a, b, (((0,), (0,)), ((), ())), preferred_element_type=F32)


def _rms_f32(x, g):
    return x * lax.rsqrt(jnp.mean(x * x, axis=-1, keepdims=True) + EPS) * g


def _sigmoid(x):
    return 1.0 / (1.0 + jnp.exp(-x))


def _silu(x):
    return x * _sigmoid(x)


def _rotary(t, cos, sin):
    return t * cos + pltpu.roll(t, DH // 2, 1) * sin


def _group_norm(o):
    mu = jnp.mean(o, axis=-1, keepdims=True)
    d = o - mu
    var = jnp.mean(d * d, axis=-1, keepdims=True)
    return d * lax.rsqrt(var + EPS)


def _retention_head_paired(q, kt, kdt, v, s, dmask, qdec, gl):
    L = q.shape[0]
    sc = _dot(q, jnp.concatenate([kt, s.astype(BF16)], axis=1))
    scores = (sc[:, :L] * dmask).astype(BF16)
    iu = _dot(jnp.concatenate([scores, kdt], axis=0), v)
    return iu[:L] + sc[:, L:] * qdec, gl * s + iu[L:]


def _short_conv(u, tail, cw, rows):
    r1 = pltpu.roll(u, 1, 0)
    r2 = pltpu.roll(u, 2, 0)
    t0, t1 = tail[0:1, :], tail[1:2, :]
    sh1 = jnp.where(rows == 0, t1, r1)
    sh2 = jnp.where(rows == 0, t0, jnp.where(rows == 1, t1, r2))
    return cw[0:1, :] * sh2 + cw[1:2, :] * sh1 + cw[2:3, :] * u


def _out_proj(x, gated, bgy, ga, gb, wro_ref, wco_ref, wo_ref):
    ret_out = _dot(gated.astype(BF16), wro_ref[...])
    conv_out = _dot(bgy.astype(BF16), wco_ref[...])
    merged = _sigmoid(ga) * ret_out + _sigmoid(gb) * conv_out
    return x + _dot(merged.astype(BF16), wo_ref[...])


def _ffn_block(x, g_ref, wgu_ref, wd_ref):
    hn = _rms_f32(x, g_ref[...]).astype(BF16)
    a = _dot(hn, wgu_ref[:, :D_FF])
    b = _dot(hn, wgu_ref[:, D_FF:])
    return x + _dot((_silu(a) * b).astype(BF16), wd_ref[...])


def _log_gamma():
    return np.log1p(-np.exp2(-5.0 - np.arange(HEADS, dtype=np.float64)))


def _const(t):
    return jnp.asarray(np.asarray(t, dtype=np.float32))


def _rope_tables(pos):
    half = DH // 2
    inv = np.power(ROPE_BASE, -np.arange(half, dtype=np.float64) / half)
    ang = np.asarray(pos, dtype=np.float64)[:, None] * inv[None, :]
    cos, sin = np.cos(ang), np.sin(ang)
    cosf = np.concatenate([cos, cos], axis=-1)
    sinf = np.concatenate([-sin, sin], axis=-1)
    scale = DH ** -0.5
    return tuple(_const(t) for t in (cosf, sinf, cosf * scale, sinf * scale))


def _decay_tables(L):
    log_g = _log_gamma()
    idx = np.arange(L, dtype=np.float64)
    diff = idx[:, None] - idx[None, :]
    dmask = np.where(diff >= 0, np.exp(log_g[:, None, None] * np.maximum(diff, 0.0)[None]), 0.0)
    qdec = np.exp(log_g[:, None] * (idx + 1.0)[None])
    kdec = np.exp(log_g[:, None] * (L - 1.0 - idx)[None])
    gl = np.exp(log_g * L)
    lanes = lambda t: np.repeat(t.T, DH, axis=1)
    return tuple(_const(t) for t in (dmask, lanes(qdec), lanes(kdec), np.repeat(gl, DH)[None, :], kdec[:, None, :]))


def _mixer_main_kernel(x_ref, s0_ref, c0_ref, g_ref, win_ref, cw_ref, wro_ref, wco_ref, wo_ref,
                       cq_ref, sq_ref, ck_ref, sk_ref, dmask_ref, qdec_ref, kdec_ref, gl_ref,
                       h_ref, s_ref, c_ref, q_s, kt_s, kdt_s, v_s, o_s):
    bg = x_ref.shape[0]
    m = bg * CHUNK

    @pl.when(pl.program_id(1) == 0)
    def _():
        for b in range(bg):
            s_ref[b] = s0_ref[...]
            c_ref[b] = c0_ref[...]

    x = x_ref[...].reshape(m, D_MODEL)
    hn = _rms_f32(x, g_ref[...]).astype(BF16)

    cq, sq, ck, sk = cq_ref[...], sq_ref[...], ck_ref[...], sk_ref[...]
    q = _dot(hn, win_ref[:, C_Q:C_Q + D_MODEL])
    k = _dot(hn, win_ref[:, C_K:C_K + D_MODEL])
    v_s[...] = _dot(hn, win_ref[:, C_V:C_V + D_MODEL]).astype(BF16)
    for b in range(bg):
        r = slice(b * CHUNK, (b + 1) * CHUNK)
        for h in range(HEADS):
            cl = slice(h * DH, (h + 1) * DH)
            q_s[r, cl] = _rotary(q[r, cl], cq, sq).astype(BF16)
            krt = _rotary(k[r, cl], ck, sk).T
            kt_s[b * HEADS + h] = krt.astype(BF16)
            kdt_s[b * HEADS + h] = (krt * kdec_ref[h]).astype(BF16)

    for b in range(bg):
        r = slice(b * CHUNK, (b + 1) * CHUNK)
        for h in range(HEADS):
            cl = slice(h * DH, (h + 1) * DH)
            o, s_new = _retention_head_paired(q_s[r, cl], kt_s[b * HEADS + h], kdt_s[b * HEADS + h], v_s[r, cl],
                                              s_ref[b, h], dmask_ref[h], qdec_ref[:, cl], gl_ref[:, cl])
            s_ref[b, h] = s_new
            o_s[r, cl] = _group_norm(o)

    g = _dot(hn, win_ref[:, C_G:C_G + D_MODEL])
    gated = _silu(g) * o_s[...]

    bgate = _dot(hn, win_ref[:, C_BG:C_BG + D_MODEL])
    u = _dot(hn, win_ref[:, C_CG:C_CG + D_MODEL]) * _dot(hn, win_ref[:, C_HC:C_HC + D_MODEL])
    rows = lax.broadcasted_iota(jnp.int32, (CHUNK, D_MODEL), 0)
    cw = cw_ref[...]
    ys = []
    for b in range(bg):
        ub = u[b * CHUNK:(b + 1) * CHUNK]
        ys.append(_short_conv(ub, c_ref[b], cw, rows))
        c_ref[b] = ub[CHUNK - (CONV_W - 1):, :]
    bgy = bgate * jnp.concatenate(ys, axis=0)

    ga = _dot(hn, win_ref[:, C_GA:C_GA + D_MODEL])
    gb = _dot(hn, win_ref[:, C_GB:C_GB + D_MODEL])
    out = _out_proj(x, gated, bgy, ga, gb, wro_ref, wco_ref, wo_ref)
    h_ref[...] = out.reshape(bg, CHUNK, D_MODEL)


def _mixer_main(x, s0, c0, layer, g, win, cw, wro, wco, wo, rope, decay):
    nb, seq, _ = x.shape
    bg = MAIN_BG
    cq, sq, ck, sk = rope
    dmask, qdec, _, gl, kdec = decay
    m = bg * CHUNK
    rope_spec = pl.BlockSpec((CHUNK, DH), lambda i, c: (c, 0))
    return pl.pallas_call(
        _mixer_main_kernel,
        grid=(nb // bg, seq // CHUNK),
        in_specs=[
            pl.BlockSpec((bg, CHUNK, D_MODEL), lambda i, c: (i, c, 0)),
            _resident_layer((HEADS, DH, DH), 0),
            _resident_layer((CONV_W - 1, D_MODEL), 0),
            _resident_layer((1, D_MODEL), layer),
            _resident_layer((D_MODEL, N_IN), layer),
            _resident_layer((CONV_W, D_MODEL), layer),
            _resident_layer((D_MODEL, D_MODEL), layer),
            _resident_layer((D_MODEL, D_MODEL), layer),
            _resident_layer((D_MODEL, D_MODEL), layer),
            rope_spec, rope_spec, rope_spec, rope_spec,
            _resident((HEADS, CHUNK, CHUNK)),
            _resident((CHUNK, D_MODEL)),
            _resident((HEADS, 1, CHUNK)),
            _resident((1, D_MODEL)),
        ],
        out_specs=[
            pl.BlockSpec((bg, CHUNK, D_MODEL), lambda i, c: (i, c, 0)),
            pl.BlockSpec((bg, HEADS, DH, DH), lambda i, c: (i, 0, 0, 0)),
            pl.BlockSpec((bg, CONV_W - 1, D_MODEL), lambda i, c: (i, 0, 0)),
        ],
        out_shape=[
            jax.ShapeDtypeStruct(x.shape, F32),
            jax.ShapeDtypeStruct((nb, HEADS, DH, DH), F32),
            jax.ShapeDtypeStruct((nb, CONV_W - 1, D_MODEL), F32),
        ],
        scratch_shapes=[
            pltpu.VMEM((m, D_MODEL), BF16),
            pltpu.VMEM((bg * HEADS, DH, CHUNK), BF16),
            pltpu.VMEM((bg * HEADS, DH, CHUNK), BF16),
            pltpu.VMEM((m, D_MODEL), BF16),
            pltpu.VMEM((m, D_MODEL), F32),
        ],
        compiler_params=_params(("arbitrary", "arbitrary")),
        name="mixer_main",
    )(x, s0, c0, g, win, cw, wro, wco, wo, cq, sq, ck, sk, dmask, qdec, kdec, gl)


def _proj_kernel(x_ref, g_ref, win_ref, o_ref):
    hn = _rms_f32(x_ref[...], g_ref[...]).astype(BF16)
    o_ref[...] = _dot(hn, win_ref[...])


def _proj(x, layer, g, win, tm):
    rows = x.shape[0]
    return pl.pallas_call(
        _proj_kernel,
        grid=(rows // tm,),
        in_specs=[pl.BlockSpec((tm, D_MODEL), lambda i: (i, 0)), _resident_layer((1, D_MODEL), layer),
                  _resident_layer((D_MODEL, N_IN), layer)],
        out_specs=pl.BlockSpec((tm, N_IN), lambda i: (i, 0)),
        out_shape=jax.ShapeDtypeStruct((rows, N_IN), F32),
        compiler_params=_params(("arbitrary",)),
        name="proj_short",
    )(x, g, win)


def _ret_short_first(p_ref, s_ref, tab_refs, sb, L, sink):
    cq_ref, sq_ref, ck_ref, sk_ref, dmask_ref, qdec_ref, kdec_ref = tab_refs
    cq, sq, ck, sk = cq_ref[...], sq_ref[...], ck_ref[...], sk_ref[...]
    staged = []
    for b in range(sb):
        r = slice(b * L, (b + 1) * L)
        for h in range(HEADS):
            cl = slice(h * DH, (h + 1) * DH)
            q = _rotary(p_ref[r, C_Q + h * DH:C_Q + (h + 1) * DH], cq, sq).astype(BF16)
            k = _rotary(p_ref[r, C_K + h * DH:C_K + (h + 1) * DH], ck, sk)
            v = p_ref[r, C_V + h * DH:C_V + (h + 1) * DH]
            s = s_ref[b, h]
            scores = (_dot_nt(q, k.astype(BF16)) * dmask_ref[h]).astype(BF16)
            cross = _dot(q, s.astype(BF16)) * qdec_ref[:, cl]
            sink(b, h, r, cl, s, k * kdec_ref[:, cl], v)
            staged.append((scores, cross, v.astype(BF16)))
    return staged


def _ret_short_second(staged, p_ref, c_ref, cw_ref, gated_ref, bgy_ref, cn_ref, sb, L):
    rows = lax.broadcasted_iota(jnp.int32, (L, D_MODEL), 0)
    cw = cw_ref[...]
    for b in range(sb):
        r = slice(b * L, (b + 1) * L)
        for h in range(HEADS):
            cl = slice(h * DH, (h + 1) * DH)
            scores, cross, v = staged[b * HEADS + h]
            o = _dot(scores, v) + cross
            g = p_ref[r, C_G + h * DH:C_G + (h + 1) * DH]
            gated_ref[r, cl] = _silu(g) * _group_norm(o)
        u = p_ref[r, C_CG:C_CG + D_MODEL] * p_ref[r, C_HC:C_HC + D_MODEL]
        y = _short_conv(u, c_ref[b], cw, rows)
        bgy_ref[r, :] = p_ref[r, C_BG:C_BG + D_MODEL] * y
        cn_ref[b] = u[L - (CONV_W - 1):, :]


def _state_sink(sn_ref, gl_ref, layer=None):
    def sink(b, h, r, cl, s, kd, v):
        s_new = gl_ref[:, cl] * s + _dot_tn(kd.astype(BF16), v.astype(BF16))
        if layer is None:
            sn_ref[b, h] = s_new
        else:
            sn_ref[layer, b, h] = s_new
    return sink


def _ret_meta_kernel(p_ref, s_ref, c_ref, cw_ref, cq_ref, sq_ref, ck_ref, sk_ref, dmask_ref, qdec_ref, kdec_ref,
                     gl_ref, gated_ref, bgy_ref, cn_ref, sn_ref, *, L):
    sb = c_ref.shape[0]
    tabs = (cq_ref, sq_ref, ck_ref, sk_ref, dmask_ref, qdec_ref, kdec_ref)
    staged = _ret_short_first(p_ref, s_ref, tabs, sb, L, _state_sink(sn_ref, gl_ref))
    _ret_short_second(staged, p_ref, c_ref, cw_ref, gated_ref, bgy_ref, cn_ref, sb, L)


def _short_table_specs(L):
    return [_resident((L, DH))] * 4 + [_resident((HEADS, L, L)), _resident((L, D_MODEL)), _resident((L, D_MODEL)),
                                       _resident((1, D_MODEL))]


def _ret_meta(proj, row0, state, cprev, layer, cw, rope, decay, nseq, L):
    dmask, qdec, kdec, gl, _ = decay
    rows = nseq * L
    row_spec = pl.BlockSpec((rows, D_MODEL), lambda i: (0, 0))
    row_shape = jax.ShapeDtypeStruct((rows, D_MODEL), F32)
    return pl.pallas_call(
        functools.partial(_ret_meta_kernel, L=L),
        grid=(1,),
        in_specs=[
            pl.BlockSpec((rows, C_GA), lambda i: (row0 // rows, 0)),
            _resident((nseq, HEADS, DH, DH)),
            _resident((nseq, CONV_W - 1, D_MODEL)),
            _resident_layer((CONV_W, D_MODEL), layer),
        ] + _short_table_specs(L),
        out_specs=[row_spec, row_spec, pl.BlockSpec((nseq, CONV_W - 1, D_MODEL), lambda i: (0, 0, 0)),
                   pl.BlockSpec((nseq, HEADS, DH, DH), lambda i: (0, 0, 0, 0))],
        out_shape=[row_shape, row_shape, jax.ShapeDtypeStruct((nseq, CONV_W - 1, D_MODEL), F32),
                   jax.ShapeDtypeStruct((nseq, HEADS, DH, DH), F32)],
        compiler_params=_params(("arbitrary",)),
        name="ret_meta",
    )(proj, state, cprev, cw, *rope, dmask, qdec, kdec, gl)


def _out_ffn_kernel(x_ref, gated_ref, bgy_ref, ga_ref, gb_ref, wro_ref, wco_ref, wo_ref,
                    g_ref, wgu_ref, wd_ref, fg_ref, o_ref, *, final):
    h = _out_proj(x_ref[...], gated_ref[...], bgy_ref[...], ga_ref[...], gb_ref[...], wro_ref, wco_ref, wo_ref)
    y = _ffn_block(h, g_ref, wgu_ref, wd_ref)
    if final:
        y = _rms_f32(y, fg_ref[...])
    o_ref[...] = y


def _out_ffn(x, gated, bgy, proj, layer, wro, wco, wo, g, wgu, wd, fg, final, rows, tm):
    row_spec = pl.BlockSpec((tm, D_MODEL), lambda i: (i, 0))
    w_spec = _resident_layer((D_MODEL, D_MODEL), layer)
    return pl.pallas_call(
        functools.partial(_out_ffn_kernel, final=final),
        grid=(rows // tm,),
        in_specs=[row_spec, row_spec, row_spec,
                  pl.BlockSpec((tm, D_MODEL), lambda i: (i, C_GA // D_MODEL)),
                  pl.BlockSpec((tm, D_MODEL), lambda i: (i, C_GB // D_MODEL)),
                  w_spec, w_spec, w_spec,
                  _resident_layer((1, D_MODEL), layer), _resident_layer((D_MODEL, 2 * D_FF), layer),
                  _resident_layer((D_FF, D_MODEL), layer), _resident((1, D_MODEL))],
        out_specs=row_spec,
        out_shape=jax.ShapeDtypeStruct((rows, D_MODEL), F32),
        compiler_params=_params(("arbitrary",)),
        name="out_ffn_short",
    )(x, gated, bgy, proj, proj, wro, wco, wo, g, wgu, wd, fg)


def _ffn_main_kernel(x_ref, g_ref, wgu_ref, wd_ref, fg_ref, p_ref, s_ref, c_ref, cw_ref,
                     cq_ref, sq_ref, ck_ref, sk_ref, dmask_ref, qdec_ref, kdec_ref, gl_ref, *rest, final, n_prev, L):
    sb = c_ref.shape[0]
    prev, outs = rest[:3 * n_prev], rest[3 * n_prev:]
    y_ref, gated_ref, bgy_ref, cn_ref = outs[:4]
    tabs = (cq_ref, sq_ref, ck_ref, sk_ref, dmask_ref, qdec_ref, kdec_ref)

    if final:
        sn_ref = outs[4]
        sink = _state_sink(sn_ref, gl_ref, n_prev)
    else:
        kd_ref, v_ref = outs[4:6]

        def sink(b, h, r, cl, s, kd, v):
            kd_ref[r, cl] = kd
            v_ref[r, cl] = v

    x = x_ref[...]
    hn = _rms_f32(x, g_ref[...]).astype(BF16)
    staged = _ret_short_first(p_ref, s_ref, tabs, sb, L, sink)
    a = _dot(hn, wgu_ref[:, :D_FF])
    if final:
        for l in range(n_prev):
            so_ref, kdo_ref, vo_ref = prev[3 * l:3 * l + 3]
            upd = _state_sink(sn_ref, gl_ref, l)
            for b in range(sb):
                r = slice(b * L, (b + 1) * L)
                for h in range(HEADS):
                    cl = slice(h * DH, (h + 1) * DH)
                    upd(b, h, r, cl, so_ref[b, h], kdo_ref[r, cl], vo_ref[r, cl])
    b = _dot(hn, wgu_ref[:, D_FF:])
    _ret_short_second(staged, p_ref, c_ref, cw_ref, gated_ref, bgy_ref, cn_ref, sb, L)
    y = x + _dot((_silu(a) * b).astype(BF16), wd_ref[...])
    if final:
        y = _rms_f32(y, fg_ref[...])
    y_ref[...] = y


def _ffn_main(x, layer, g, wgu, wd, fg, final, tm, proj, state, cconv, cw, rope, decay, L, kds, vs):
    rows = x.shape[0]
    steps = rows // tm
    depth, ns = state.shape[:2]
    sb = ns // steps
    assert sb * steps == ns and (sb * L) % 8 == 0
    n_prev = len(kds) if final else 0
    dmask, qdec, kdec, gl, _ = decay
    n_samp = ns * L
    row_spec = pl.BlockSpec((tm, D_MODEL), lambda i: (i, 0))
    srow_spec = pl.BlockSpec((sb * L, D_MODEL), lambda i: (i, 0))
    srow_shape = jax.ShapeDtypeStruct((n_samp, D_MODEL), F32)

    def state_spec(l):
        return pl.BlockSpec((None, sb, HEADS, DH, DH), lambda i: (l, i, 0, 0, 0))

    in_specs = [
        row_spec, _resident_layer((1, D_MODEL), layer), _resident_layer((D_MODEL, 2 * D_FF), layer),
        _resident_layer((D_FF, D_MODEL), layer), _resident((1, D_MODEL)),
        pl.BlockSpec((sb * L, C_GA), lambda i: (i, 0)),
        state_spec(layer),
        pl.BlockSpec((None, sb, CONV_W - 1, D_MODEL), lambda i: (layer, i, 0, 0)),
        _resident_layer((CONV_W, D_MODEL), layer),
    ] + _short_table_specs(L)
    args = [x, g, wgu, wd, fg, proj, state, cconv, cw, *rope, dmask, qdec, kdec, gl]
    for l in range(n_prev):
        in_specs += [state_spec(l), srow_spec, srow_spec]
        args += [state, kds[l], vs[l]]
    out_specs = [row_spec, srow_spec, srow_spec, pl.BlockSpec((sb, CONV_W - 1, D_MODEL), lambda i: (i, 0, 0))]
    out_shape = [jax.ShapeDtypeStruct(x.shape, F32), srow_shape, srow_shape,
                 jax.ShapeDtypeStruct((ns, CONV_W - 1, D_MODEL), F32)]
    if final:
        assert n_prev == depth - 1
        out_specs.append(pl.BlockSpec((depth, sb, HEADS, DH, DH), lambda i: (0, i, 0, 0, 0)))
        out_shape.append(jax.ShapeDtypeStruct(state.shape, F32))
    else:
        out_specs += [srow_spec, srow_spec]
        out_shape += [srow_shape, srow_shape]
    return pl.pallas_call(
        functools.partial(_ffn_main_kernel, final=final, n_prev=n_prev, L=L),
        grid=(steps,),
        in_specs=in_specs,
        out_specs=out_specs,
        out_shape=out_shape,
        compiler_params=_params(("arbitrary",)),
        name="ffn_main",
    )(*args)


def kernel(x_prompt, x_sample, state_ret, state_conv, meta_tokens, norm_mix_g, w_in, conv_w, w_ret_o,
           w_conv_o, w_o, norm_ffn_g, w_gate_up, w_down, final_norm_g):
    depth = w_in.shape[0]
    nb, seq, _ = x_prompt.shape
    ns, ls, _ = x_sample.shape
    n_samp = ns * ls
    n_short = n_samp + N_META
    short_tm = n_short // 5
    assert short_tm * 5 == n_short and short_tm % 8 == 0 and n_samp % N_META == 0

    win_b, wro_b, wco_b, wo_b = (w.astype(BF16) for w in (w_in, w_ret_o, w_conv_o, w_o))
    wgu_b, wd_b = w_gate_up.astype(BF16), w_down.astype(BF16)
    gm = norm_mix_g.reshape(depth, 1, D_MODEL)
    gf = norm_ffn_g.reshape(depth, 1, D_MODEL)
    fg = final_norm_g.reshape(1, D_MODEL)

    rope_meta = _rope_tables(np.arange(N_META))
    rope_main = _rope_tables(N_META + np.arange(seq))
    rope_samp = _rope_tables(PAST_LEN + np.arange(ls))
    dec_meta, dec_main, dec_samp = _decay_tables(N_META), _decay_tables(CHUNK), _decay_tables(ls)

    h_main = x_prompt
    h_short = jnp.concatenate([x_sample.reshape(n_samp, D_MODEL), meta_tokens.astype(F32)], axis=0)
    zero_s = jnp.zeros((1, HEADS, DH, DH), F32)
    zero_c = jnp.zeros((1, CONV_W - 1, D_MODEL), F32)

    s_p, c_p, c_s, kds, vs = [], [], [], [], []
    y_samp = s_s = None
    for l in range(depth):
        last = l == depth - 1
        proj = _proj(h_short, l, gm, win_b, short_tm)
        gated_m, bgy_m, c_m, s_m = _ret_meta(proj, n_samp, zero_s, zero_c, l, conv_w, rope_meta, dec_meta, 1, N_META)

        h_main, s_l, c_l = _mixer_main(h_main, s_m, c_m, l, gm, win_b, conv_w, wro_b, wco_b, wo_b,
                                       rope_main, dec_main)
        s_p.append(s_l)
        c_p.append(c_l)
        res = _ffn_main(h_main.reshape(nb * seq, D_MODEL), l, gf, wgu_b, wd_b, fg, last, MAIN_FFN_TM,
                        proj, state_ret, state_conv, conv_w, rope_samp, dec_samp, ls, kds, vs)
        h_main, gated_s, bgy_s, c_l = res[:4]
        h_main = h_main.reshape(nb, seq, D_MODEL)
        c_s.append(c_l)
        if last:
            s_s = res[4]
        else:
            kds.append(res[4])
            vs.append(res[5])

        if last:
            y_samp = _out_ffn(h_short, gated_s, bgy_s, proj, l, wro_b, wco_b, wo_b, gf, wgu_b, wd_b, fg, True,
                              n_samp, 256)
        else:
            gated = jnp.concatenate([gated_s, gated_m], axis=0)
            bgy = jnp.concatenate([bgy_s, bgy_m], axis=0)
            h_short = _out_ffn(h_short, gated, bgy, proj, l, wro_b, wco_b, wo_b, gf, wgu_b, wd_b, fg, False,
                               n_short, short_tm)

    return (h_main, y_samp.reshape(ns, ls, D_MODEL), jnp.stack(s_p), jnp.stack(c_p), s_s, jnp.stack(c_s))
```

```python
import functools

import numpy as np
import jax
import jax.numpy as jnp
from jax import lax
from jax.experimental import pallas as pl
from jax.experimental.pallas import tpu as pltpu

D_MODEL = 1024
N_META = 16
HEADS = 8
DH = D_MODEL // HEADS
CHUNK = 128
ROPE_BASE = 10000.0
CONV_W = 3
D_FF = ((8 * D_MODEL + 3 * 256 - 1) // (3 * 256)) * 256
EPS = 1e-6
PAST_LEN = 16384
N_IN = 9 * D_MODEL
C_Q, C_K, C_V, C_G, C_BG, C_CG, C_HC, C_GA, C_GB = (i * D_MODEL for i in range(9))

F32 = jnp.float32
BF16 = jnp.bfloat16

VMEM_LIMIT_BYTES = 58 * 1024 * 1024

MAIN_BG = 4
MAIN_FFN_TM = 512


def _resident(shape):
    nd = len(shape)
    return pl.BlockSpec(shape, lambda *_: (0,) * nd, pipeline_mode=pl.Buffered(1))


def _resident_layer(shape, layer):
    nd = len(shape)
    return pl.BlockSpec((None,) + tuple(shape), lambda *_: (layer,) + (0,) * nd, pipeline_mode=pl.Buffered(1))


def _params(sem):
    return pltpu.CompilerParams(dimension_semantics=sem, vmem_limit_bytes=VMEM_LIMIT_BYTES)


def _dot(a, b):
    return jnp.dot(a, b, preferred_element_type=F32)


def _dot_nt(a, b):
    return lax.dot_general(a, b, (((1,), (1,)), ((), ())), preferred_element_type=F32)


def _dot_tn(a, b):
    return lax.dot_general(a, b, (((0,), (0,)), ((), ())), preferred_element_type=F32)


def _rms_f32(x, g):
    return x * lax.rsqrt(jnp.mean(x * x, axis=-1, keepdims=True) + EPS) * g


def _sigmoid(x):
    return 1.0 / (1.0 + jnp.exp(-x))


def _silu(x):
    return x * _sigmoid(x)


def _rotary(t, cos, sin):
    return t * cos + pltpu.roll(t, DH // 2, 1) * sin


def _group_norm(o):
    mu = jnp.mean(o, axis=-1, keepdims=True)
    d = o - mu
    var = jnp.mean(d * d, axis=-1, keepdims=True)
    return d * lax.rsqrt(var + EPS)


def _retention_head_paired(q, kt, kdt, v, s, dmask, qdec, gl):
    L = q.shape[0]
    sc = _dot(q, jnp.concatenate([kt, s.astype(BF16)], axis=1))
    scores = (sc[:, :L] * dmask).astype(BF16)
    iu = _dot(jnp.concatenate([scores, kdt], axis=0), v)
    return iu[:L] + sc[:, L:] * qdec, gl * s + iu[L:]


def _short_conv(u, tail, cw, rows):
    r1 = pltpu.roll(u, 1, 0)
    r2 = pltpu.roll(u, 2, 0)
    t0, t1 = tail[0:1, :], tail[1:2, :]
    sh1 = jnp.where(rows == 0, t1, r1)
    sh2 = jnp.where(rows == 0, t0, jnp.where(rows == 1, t1, r2))
    return cw[0:1, :] * sh2 + cw[1:2, :] * sh1 + cw[2:3, :] * u


def _out_proj(x, gated, bgy, ga, gb, wro_ref, wco_ref, wo_ref):
    ret_out = _dot(gated.astype(BF16), wro_ref[...])
    conv_out = _dot(bgy.astype(BF16), wco_ref[...])
    merged = _sigmoid(ga) * ret_out + _sigmoid(gb) * conv_out
    return x + _dot(merged.astype(BF16), wo_ref[...])


def _ffn_block(x, g_ref, wgu_ref, wd_ref):
    hn = _rms_f32(x, g_ref[...]).astype(BF16)
    a = _dot(hn, wgu_ref[:, :D_FF])
    b = _dot(hn, wgu_ref[:, D_FF:])
    return x + _dot((_silu(a) * b).astype(BF16), wd_ref[...])


def _log_gamma():
    return np.log1p(-np.exp2(-5.0 - np.arange(HEADS, dtype=np.float64)))


def _const(t):
    return jnp.asarray(np.asarray(t, dtype=np.float32))


def _rope_tables(pos):
    half = DH // 2
    inv = np.power(ROPE_BASE, -np.arange(half, dtype=np.float64) / half)
    ang = np.asarray(pos, dtype=np.float64)[:, None] * inv[None, :]
    cos, sin = np.cos(ang), np.sin(ang)
    cosf = np.concatenate([cos, cos], axis=-1)
    sinf = np.concatenate([-sin, sin], axis=-1)
    scale = DH ** -0.5
    return tuple(_const(t) for t in (cosf, sinf, cosf * scale, sinf * scale))


def _decay_tables(L):
    log_g = _log_gamma()
    idx = np.arange(L, dtype=np.float64)
    diff = idx[:, None] - idx[None, :]
    dmask = np.where(diff >= 0, np.exp(log_g[:, None, None] * np.maximum(diff, 0.0)[None]), 0.0)
    qdec = np.exp(log_g[:, None] * (idx + 1.0)[None])
    kdec = np.exp(log_g[:, None] * (L - 1.0 - idx)[None])
    gl = np.exp(log_g * L)
    lanes = lambda t: np.repeat(t.T, DH, axis=1)
    return tuple(_const(t) for t in (dmask, lanes(qdec), lanes(kdec), np.repeat(gl, DH)[None, :], kdec[:, None, :]))


def _mixer_main_kernel(x_ref, s0_ref, c0_ref, g_ref, win_ref, cw_ref, wro_ref, wco_ref, wo_ref,
                       cq_ref, sq_ref, ck_ref, sk_ref, dmask_ref, qdec_ref, kdec_ref, gl_ref, wgu32_ref, wd32_ref,
                       h_ref, s_ref, c_ref, wgub_ref, wdb_ref, q_s, kt_s, kdt_s, v_s, o_s):
    bg = x_ref.shape[0]
    m = bg * CHUNK

    wgub_ref[...] = wgu32_ref[...].astype(BF16)
    wdb_ref[...] = wd32_ref[...].astype(BF16)

    @pl.when(pl.program_id(1) == 0)
    def _():
        for b in range(bg):
            s_ref[b] = s0_ref[...]
            c_ref[b] = c0_ref[...]

    x = x_ref[...].reshape(m, D_MODEL)
    hn = _rms_f32(x, g_ref[...]).astype(BF16)

    cq, sq, ck, sk = cq_ref[...], sq_ref[...], ck_ref[...], sk_ref[...]
    q = _dot(hn, win_ref[:, C_Q:C_Q + D_MODEL])
    k = _dot(hn, win_ref[:, C_K:C_K + D_MODEL])
    v_s[...] = _dot(hn, win_ref[:, C_V:C_V + D_MODEL]).astype(BF16)
    for b in range(bg):
        r = slice(b * CHUNK, (b + 1) * CHUNK)
        for h in range(HEADS):
            cl = slice(h * DH, (h + 1) * DH)
            q_s[r, cl] = _rotary(q[r, cl], cq, sq).astype(BF16)
            krt = _rotary(k[r, cl], ck, sk).T
            kt_s[b * HEADS + h] = krt.astype(BF16)
            kdt_s[b * HEADS + h] = (krt * kdec_ref[h]).astype(BF16)

    for b in range(bg):
        r = slice(b * CHUNK, (b + 1) * CHUNK)
        for h in range(HEADS):
            cl = slice(h * DH, (h + 1) * DH)
            o, s_new = _retention_head_paired(q_s[r, cl], kt_s[b * HEADS + h], kdt_s[b * HEADS + h], v_s[r, cl],
                                              s_ref[b, h], dmask_ref[h], qdec_ref[:, cl], gl_ref[:, cl])
            s_ref[b, h] = s_new
            o_s[r, cl] = _group_norm(o)

    g = _dot(hn, win_ref[:, C_G:C_G + D_MODEL])
    gated = _silu(g) * o_s[...]

    bgate = _dot(hn, win_ref[:, C_BG:C_BG + D_MODEL])
    u = _dot(hn, win_ref[:, C_CG:C_CG + D_MODEL]) * _dot(hn, win_ref[:, C_HC:C_HC + D_MODEL])
    rows = lax.broadcasted_iota(jnp.int32, (CHUNK, D_MODEL), 0)
    cw = cw_ref[...]
    ys = []
    for b in range(bg):
        ub = u[b * CHUNK:(b + 1) * CHUNK]
        ys.append(_short_conv(ub, c_ref[b], cw, rows))
        c_ref[b] = ub[CHUNK - (CONV_W - 1):, :]
    bgy = bgate * jnp.concatenate(ys, axis=0)

    ga = _dot(hn, win_ref[:, C_GA:C_GA + D_MODEL])
    gb = _dot(hn, win_ref[:, C_GB:C_GB + D_MODEL])
    out = _out_proj(x, gated, bgy, ga, gb, wro_ref, wco_ref, wo_ref)
    h_ref[...] = out.reshape(bg, CHUNK, D_MODEL)


def _mixer_main(x, s0, c0, layer, g, win, cw, wro, wco, wo, rope, decay, wgu32, wd32):
    nb, seq, _ = x.shape
    bg = MAIN_BG
    cq, sq, ck, sk = rope
    dmask, qdec, _, gl, kdec = decay
    m = bg * CHUNK
    n_chunks = seq // CHUNK
    steps = (nb // bg) * n_chunks
    slab = D_MODEL // steps
    assert slab * steps == D_MODEL and slab % 16 == 0
    rope_spec = pl.BlockSpec((CHUNK, DH), lambda i, c: (c, 0))
    return pl.pallas_call(
        _mixer_main_kernel,
        grid=(nb // bg, n_chunks),
        in_specs=[
            pl.BlockSpec((bg, CHUNK, D_MODEL), lambda i, c: (i, c, 0)),
            _resident_layer((HEADS, DH, DH), 0),
            _resident_layer((CONV_W - 1, D_MODEL), 0),
            _resident_layer((1, D_MODEL), layer),
            _resident_layer((D_MODEL, N_IN), 0),
            _resident_layer((CONV_W, D_MODEL), layer),
            _resident_layer((D_MODEL, D_MODEL), layer),
            _resident_layer((D_MODEL, D_MODEL), layer),
            _resident_layer((D_MODEL, D_MODEL), layer),
            rope_spec, rope_spec, rope_spec, rope_spec,
            _resident((HEADS, CHUNK, CHUNK)),
            _resident((CHUNK, D_MODEL)),
            _resident((HEADS, 1, CHUNK)),
            _resident((1, D_MODEL)),
            pl.BlockSpec((None, slab, 2 * D_FF), lambda i, c: (layer, i * n_chunks + c, 0)),
            pl.BlockSpec((None, slab, D_FF), lambda i, c: (layer, i * n_chunks + c, 0)),
        ],
        out_specs=[
            pl.BlockSpec((bg, CHUNK, D_MODEL), lambda i, c: (i, c, 0)),
            pl.BlockSpec((bg, HEADS, DH, DH), lambda i, c: (i, 0, 0, 0)),
            pl.BlockSpec((bg, CONV_W - 1, D_MODEL), lambda i, c: (i, 0, 0)),
            pl.BlockSpec((slab, 2 * D_FF), lambda i, c: (i * n_chunks + c, 0)),
            pl.BlockSpec((slab, D_FF), lambda i, c: (i * n_chunks + c, 0)),
        ],
        out_shape=[
            jax.ShapeDtypeStruct(x.shape, F32),
            jax.ShapeDtypeStruct((nb, HEADS, DH, DH), F32),
            jax.ShapeDtypeStruct((nb, CONV_W - 1, D_MODEL), F32),
            jax.ShapeDtypeStruct((D_MODEL, 2 * D_FF), BF16),
            jax.ShapeDtypeStruct((D_MODEL, D_FF), BF16),
        ],
        scratch_shapes=[
            pltpu.VMEM((m, D_MODEL), BF16),
            pltpu.VMEM((bg * HEADS, DH, CHUNK), BF16),
            pltpu.VMEM((bg * HEADS, DH, CHUNK), BF16),
            pltpu.VMEM((m, D_MODEL), BF16),
            pltpu.VMEM((m, D_MODEL), F32),
        ],
        compiler_params=_params(("arbitrary", "arbitrary")),
        name="mixer_main",
    )(x, s0, c0, g, win, cw, wro, wco, wo, cq, sq, ck, sk, dmask, qdec, kdec, gl, wgu32, wd32)


def _proj_kernel(x_ref, g_ref, w32_ref, o_ref, wb_ref, hn_s):
    @pl.when(pl.program_id(0) == 0)
    def _():
        hn_s[...] = _rms_f32(x_ref[...], g_ref[...]).astype(BF16)

    wb = w32_ref[...].astype(BF16)
    wb_ref[...] = wb
    o_ref[...] = _dot(hn_s[...], wb)


def _proj(x, layer, g, win32):
    rows = x.shape[0]
    n_col = N_IN // D_MODEL
    return pl.pallas_call(
        _proj_kernel,
        grid=(n_col,),
        in_specs=[_resident((rows, D_MODEL)), _resident_layer((1, D_MODEL), layer),
                  pl.BlockSpec((None, D_MODEL, D_MODEL), lambda j: (layer, 0, j))],
        out_specs=[pl.BlockSpec((rows, D_MODEL), lambda j: (0, j)),
                   pl.BlockSpec((None, D_MODEL, D_MODEL), lambda j: (0, 0, j))],
        out_shape=[jax.ShapeDtypeStruct((rows, N_IN), F32), jax.ShapeDtypeStruct((1, D_MODEL, N_IN), BF16)],
        scratch_shapes=[pltpu.VMEM((rows, D_MODEL), BF16)],
        compiler_params=_params(("arbitrary",)),
        name="proj_short",
    )(x, g, win32)


def _ret_short_first(p_ref, s_ref, tab_refs, sb, L, sink):
    cq_ref, sq_ref, ck_ref, sk_ref, dmask_ref, qdec_ref, kdec_ref = tab_refs
    cq, sq, ck, sk = cq_ref[...], sq_ref[...], ck_ref[...], sk_ref[...]
    staged = []
    for b in range(sb):
        r = slice(b * L, (b + 1) * L)
        for h in range(HEADS):
            cl = slice(h * DH, (h + 1) * DH)
            q = _rotary(p_ref[r, C_Q + h * DH:C_Q + (h + 1) * DH], cq, sq).astype(BF16)
            k = _rotary(p_ref[r, C_K + h * DH:C_K + (h + 1) * DH], ck, sk)
            v = p_ref[r, C_V + h * DH:C_V + (h + 1) * DH]
            s = s_ref[b, h]
            scores = (_dot_nt(q, k.astype(BF16)) * dmask_ref[h]).astype(BF16)
            cross = _dot(q, s.astype(BF16)) * qdec_ref[:, cl]
            sink(b, h, r, cl, s, k * kdec_ref[:, cl], v)
            staged.append((scores, cross, v.astype(BF16)))
    return staged


def _ret_short_second(staged, p_ref, c_ref, cw_ref, gated_ref, bgy_ref, cn_ref, sb, L):
    rows = lax.broadcasted_iota(jnp.int32, (L, D_MODEL), 0)
    cw = cw_ref[...]
    for b in range(sb):
        r = slice(b * L, (b + 1) * L)
        for h in range(HEADS):
            cl = slice(h * DH, (h + 1) * DH)
            scores, cross, v = staged[b * HEADS + h]
            o = _dot(scores, v) + cross
            g = p_ref[r, C_G + h * DH:C_G + (h + 1) * DH]
            gated_ref[r, cl] = _silu(g) * _group_norm(o)
        u = p_ref[r, C_CG:C_CG + D_MODEL] * p_ref[r, C_HC:C_HC + D_MODEL]
        y = _short_conv(u, c_ref[b], cw, rows)
        bgy_ref[r, :] = p_ref[r, C_BG:C_BG + D_MODEL] * y
        cn_ref[b] = u[L - (CONV_W - 1):, :]


def _state_sink(sn_ref, gl_ref, layer=None):
    def sink(b, h, r, cl, s, kd, v):
        s_new = gl_ref[:, cl] * s + _dot_tn(kd.astype(BF16), v.astype(BF16))
        if layer is None:
            sn_ref[b, h] = s_new
        else:
            sn_ref[layer, b, h] = s_new
    return sink


def _ret_meta_kernel(p_ref, s_ref, c_ref, cw_ref, cq_ref, sq_ref, ck_ref, sk_ref, dmask_ref, qdec_ref, kdec_ref,
                     gl_ref, gated_ref, bgy_ref, cn_ref, sn_ref, *, L):
    sb = c_ref.shape[0]
    tabs = (cq_ref, sq_ref, ck_ref, sk_ref, dmask_ref, qdec_ref, kdec_ref)
    staged = _ret_short_first(p_ref, s_ref, tabs, sb, L, _state_sink(sn_ref, gl_ref))
    _ret_short_second(staged, p_ref, c_ref, cw_ref, gated_ref, bgy_ref, cn_ref, sb, L)


def _short_table_specs(L):
    return [_resident((L, DH))] * 4 + [_resident((HEADS, L, L)), _resident((L, D_MODEL)), _resident((L, D_MODEL)),
                                       _resident((1, D_MODEL))]


def _ret_meta(proj, row0, state, cprev, layer, cw, rope, decay, nseq, L):
    dmask, qdec, kdec, gl, _ = decay
    rows = nseq * L
    row_spec = pl.BlockSpec((rows, D_MODEL), lambda i: (0, 0))
    row_shape = jax.ShapeDtypeStruct((rows, D_MODEL), F32)
    return pl.pallas_call(
        functools.partial(_ret_meta_kernel, L=L),
        grid=(1,),
        in_specs=[
            pl.BlockSpec((rows, C_GA), lambda i: (row0 // rows, 0)),
            _resident((nseq, HEADS, DH, DH)),
            _resident((nseq, CONV_W - 1, D_MODEL)),
            _resident_layer((CONV_W, D_MODEL), layer),
        ] + _short_table_specs(L),
        out_specs=[row_spec, row_spec, pl.BlockSpec((nseq, CONV_W - 1, D_MODEL), lambda i: (0, 0, 0)),
                   pl.BlockSpec((nseq, HEADS, DH, DH), lambda i: (0, 0, 0, 0))],
        out_shape=[row_shape, row_shape, jax.ShapeDtypeStruct((nseq, CONV_W - 1, D_MODEL), F32),
                   jax.ShapeDtypeStruct((nseq, HEADS, DH, DH), F32)],
        compiler_params=_params(("arbitrary",)),
        name="ret_meta",
    )(proj, state, cprev, cw, *rope, dmask, qdec, kdec, gl)


def _out_ffn_kernel(x_ref, gated_ref, bgy_ref, ga_ref, gb_ref, wro_ref, wco_ref, wo_ref,
                    g_ref, wgu_ref, wd_ref, fg_ref, o_ref, *, final):
    h = _out_proj(x_ref[...], gated_ref[...], bgy_ref[...], ga_ref[...], gb_ref[...], wro_ref, wco_ref, wo_ref)
    y = _ffn_block(h, g_ref, wgu_ref, wd_ref)
    if final:
        y = _rms_f32(y, fg_ref[...])
    o_ref[...] = y


def _out_ffn(x, gated, bgy, proj, layer, wro, wco, wo, g, wgu, wd, fg, final, rows, tm):
    row_spec = pl.BlockSpec((tm, D_MODEL), lambda i: (i, 0))
    w_spec = _resident_layer((D_MODEL, D_MODEL), layer)
    return pl.pallas_call(
        functools.partial(_out_ffn_kernel, final=final),
        grid=(rows // tm,),
        in_specs=[row_spec, row_spec, row_spec,
                  pl.BlockSpec((tm, D_MODEL), lambda i: (i, C_GA // D_MODEL)),
                  pl.BlockSpec((tm, D_MODEL), lambda i: (i, C_GB // D_MODEL)),
                  w_spec, w_spec, w_spec,
                  _resident_layer((1, D_MODEL), layer), _resident_layer((D_MODEL, 2 * D_FF), 0),
                  _resident_layer((D_FF, D_MODEL), 0), _resident((1, D_MODEL))],
        out_specs=row_spec,
        out_shape=jax.ShapeDtypeStruct((rows, D_MODEL), F32),
        compiler_params=_params(("arbitrary",)),
        name="out_ffn_short",
    )(x, gated, bgy, proj, proj, wro, wco, wo, g, wgu, wd, fg)


def _ffn_main_kernel(x_ref, g_ref, wgu_ref, wd_ref, fg_ref, p_ref, s_ref, c_ref, cw_ref,
                     cq_ref, sq_ref, ck_ref, sk_ref, dmask_ref, qdec_ref, kdec_ref, gl_ref, *rest, final, n_prev, L):
    sb = c_ref.shape[0]
    prev, outs = rest[:3 * n_prev], rest[3 * n_prev:]
    y_ref, gated_ref, bgy_ref, cn_ref = outs[:4]
    tabs = (cq_ref, sq_ref, ck_ref, sk_ref, dmask_ref, qdec_ref, kdec_ref)

    if final:
        sn_ref = outs[4]
        sink = _state_sink(sn_ref, gl_ref, n_prev)
    else:
        kd_ref, v_ref = outs[4:6]

        def sink(b, h, r, cl, s, kd, v):
            kd_ref[r, cl] = kd
            v_ref[r, cl] = v

    x = x_ref[...]
    hn = _rms_f32(x, g_ref[...]).astype(BF16)
    staged = _ret_short_first(p_ref, s_ref, tabs, sb, L, sink)
    a = _dot(hn, wgu_ref[:, :D_FF])
    if final:
        for l in range(n_prev):
            so_ref, kdo_ref, vo_ref = prev[3 * l:3 * l + 3]
            upd = _state_sink(sn_ref, gl_ref, l)
            for b in range(sb):
                r = slice(b * L, (b + 1) * L)
                for h in range(HEADS):
                    cl = slice(h * DH, (h + 1) * DH)
                    upd(b, h, r, cl, so_ref[b, h], kdo_ref[r, cl], vo_ref[r, cl])
    b = _dot(hn, wgu_ref[:, D_FF:])
    _ret_short_second(staged, p_ref, c_ref, cw_ref, gated_ref, bgy_ref, cn_ref, sb, L)
    y = x + _dot((_silu(a) * b).astype(BF16), wd_ref[...])
    if final:
        y = _rms_f32(y, fg_ref[...])
    y_ref[...] = y


def _ffn_main(x, layer, g, wgu, wd, fg, final, tm, proj, state, cconv, cw, rope, decay, L, kds, vs):
    rows = x.shape[0]
    steps = rows // tm
    depth, ns = state.shape[:2]
    sb = ns // steps
    assert sb * steps == ns and (sb * L) % 8 == 0
    n_prev = len(kds) if final else 0
    dmask, qdec, kdec, gl, _ = decay
    n_samp = ns * L
    row_spec = pl.BlockSpec((tm, D_MODEL), lambda i: (i, 0))
    srow_spec = pl.BlockSpec((sb * L, D_MODEL), lambda i: (i, 0))
    srow_shape = jax.ShapeDtypeStruct((n_samp, D_MODEL), F32)

    def state_spec(l):
        return pl.BlockSpec((None, sb, HEADS, DH, DH), lambda i: (l, i, 0, 0, 0))

    in_specs = [
        row_spec, _resident_layer((1, D_MODEL), layer), _resident_layer((D_MODEL, 2 * D_FF), 0),
        _resident_layer((D_FF, D_MODEL), 0), _resident((1, D_MODEL)),
        pl.BlockSpec((sb * L, C_GA), lambda i: (i, 0)),
        state_spec(layer),
        pl.BlockSpec((None, sb, CONV_W - 1, D_MODEL), lambda i: (layer, i, 0, 0)),
        _resident_layer((CONV_W, D_MODEL), layer),
    ] + _short_table_specs(L)
    args = [x, g, wgu, wd, fg, proj, state, cconv, cw, *rope, dmask, qdec, kdec, gl]
    for l in range(n_prev):
        in_specs += [state_spec(l), srow_spec, srow_spec]
        args += [state, kds[l], vs[l]]
    out_specs = [row_spec, srow_spec, srow_spec, pl.BlockSpec((sb, CONV_W - 1, D_MODEL), lambda i: (i, 0, 0))]
    out_shape = [jax.ShapeDtypeStruct(x.shape, F32), srow_shape, srow_shape,
                 jax.ShapeDtypeStruct((ns, CONV_W - 1, D_MODEL), F32)]
    if final:
        assert n_prev == depth - 1
        out_specs.append(pl.BlockSpec((depth, sb, HEADS, DH, DH), lambda i: (0, i, 0, 0, 0)))
        out_shape.append(jax.ShapeDtypeStruct(state.shape, F32))
    else:
        out_specs += [srow_spec, srow_spec]
        out_shape += [srow_shape, srow_shape]
    return pl.pallas_call(
        functools.partial(_ffn_main_kernel, final=final, n_prev=n_prev, L=L),
        grid=(steps,),
        in_specs=in_specs,
        out_specs=out_specs,
        out_shape=out_shape,
        compiler_params=_params(("arbitrary",)),
        name="ffn_main",
    )(*args)


def kernel(x_prompt, x_sample, state_ret, state_conv, meta_tokens, norm_mix_g, w_in, conv_w, w_ret_o,
           w_conv_o, w_o, norm_ffn_g, w_gate_up, w_down, final_norm_g):
    depth = w_in.shape[0]
    nb, seq, _ = x_prompt.shape
    ns, ls, _ = x_sample.shape
    n_samp = ns * ls
    n_short = n_samp + N_META
    short_tm = n_short // 5
    assert short_tm * 5 == n_short and short_tm % 8 == 0 and n_samp % N_META == 0

    wro_b, wco_b, wo_b = (w.astype(BF16) for w in (w_ret_o, w_conv_o, w_o))
    wd32 = w_down.reshape(depth, D_MODEL, D_FF)
    gm = norm_mix_g.reshape(depth, 1, D_MODEL)
    gf = norm_ffn_g.reshape(depth, 1, D_MODEL)
    fg = final_norm_g.reshape(1, D_MODEL)

    rope_meta = _rope_tables(np.arange(N_META))
    rope_main = _rope_tables(N_META + np.arange(seq))
    rope_samp = _rope_tables(PAST_LEN + np.arange(ls))
    dec_meta, dec_main, dec_samp = _decay_tables(N_META), _decay_tables(CHUNK), _decay_tables(ls)

    h_main = x_prompt
    h_short = jnp.concatenate([x_sample.reshape(n_samp, D_MODEL), meta_tokens.astype(F32)], axis=0)
    zero_s = jnp.zeros((1, HEADS, DH, DH), F32)
    zero_c = jnp.zeros((1, CONV_W - 1, D_MODEL), F32)

    s_p, c_p, c_s, kds, vs = [], [], [], [], []
    y_samp = s_s = None
    for l in range(depth):
        last = l == depth - 1
        proj, win_b = _proj(h_short, l, gm, w_in)
        gated_m, bgy_m, c_m, s_m = _ret_meta(proj, n_samp, zero_s, zero_c, l, conv_w, rope_meta, dec_meta, 1, N_META)

        h_main, s_l, c_l, wgu_b, wd_b = _mixer_main(h_main, s_m, c_m, l, gm, win_b, conv_w, wro_b, wco_b, wo_b,
                                                    rope_main, dec_main, w_gate_up, wd32)
        wgu_b = wgu_b.reshape(1, D_MODEL, 2 * D_FF)
        wd_b = wd_b.reshape(1, D_FF, D_MODEL)
        s_p.append(s_l)
        c_p.append(c_l)
        res = _ffn_main(h_main.reshape(nb * seq, D_MODEL), l, gf, wgu_b, wd_b, fg, last, MAIN_FFN_TM,
                        proj, state_ret, state_conv, conv_w, rope_samp, dec_samp, ls, kds, vs)
        h_main, gated_s, bgy_s, c_l = res[:4]
        h_main = h_main.reshape(nb, seq, D_MODEL)
        c_s.append(c_l)
        if last:
            s_s = res[4]
        else:
            kds.append(res[4])
            vs.append(res[5])

        if last:
            y_samp = _out_ffn(h_short, gated_s, bgy_s, proj, l, wro_b, wco_b, wo_b, gf, wgu_b, wd_b, fg, True,
                              n_samp, 256)
        else:
            gated = jnp.concatenate([gated_s, gated_m], axis=0)
            bgy = jnp.concatenate([bgy_s, bgy_m], axis=0)
            h_short = _out_ffn(h_short, gated, bgy, proj, l, wro_b, wco_b, wo_b, gf, wgu_b, wd_b, fg, False,
                               n_short, short_tm)

    return (h_main, y_samp.reshape(ns, ls, D_MODEL), jnp.stack(s_p), jnp.stack(c_p), s_s, jnp.stack(c_s))
```

```python
import functools

import numpy as np
import jax
import jax.numpy as jnp
from jax import lax
from jax.experimental import pallas as pl
from jax.experimental.pallas import tpu as pltpu

D_MODEL = 1024
N_META = 16
HEADS = 8
DH = D_MODEL // HEADS
CHUNK = 128
ROPE_BASE = 10000.0
CONV_W = 3
D_FF = ((8 * D_MODEL + 3 * 256 - 1) // (3 * 256)) * 256
EPS = 1e-6
PAST_LEN = 16384
N_IN = 9 * D_MODEL
C_Q, C_K, C_V, C_G, C_BG, C_CG, C_HC, C_GA, C_GB = (i * D_MODEL for i in range(9))

F32 = jnp.float32
BF16 = jnp.bfloat16

VMEM_LIMIT_BYTES = 58 * 1024 * 1024

MAIN_BG = 4
MAIN_FFN_TM = 512


def _resident(shape):
    nd = len(shape)
    return pl.BlockSpec(shape, lambda *_: (0,) * nd, pipeline_mode=pl.Buffered(1))


def _resident_layer(shape, layer):
    nd = len(shape)
    return pl.BlockSpec((None,) + tuple(shape), lambda *_: (layer,) + (0,) * nd, pipeline_mode=pl.Buffered(1))


def _params(sem):
    return pltpu.CompilerParams(dimension_semantics=sem, vmem_limit_bytes=VMEM_LIMIT_BYTES)


def _dot(a, b):
    return jnp.dot(a, b, preferred_element_type=F32)


def _dot_nt(a, b):
    return lax.dot_general(a, b, (((1,), (1,)), ((), ())), preferred_element_type=F32)


def _dot_tn(a, b):
    return lax.dot_general(a, b, (((0,), (0,)), ((), ())), preferred_element_type=F32)


def _rms_f32(x, g):
    return x * lax.rsqrt(jnp.mean(x * x, axis=-1, keepdims=True) + EPS) * g


def _sigmoid(x):
    return 1.0 / (1.0 + jnp.exp(-x))


def _silu(x):
    return x * _sigmoid(x)


def _rotary(t, cos, sin):
    return t * cos + pltpu.roll(t, DH // 2, 1) * sin


def _group_norm(o):
    mu = jnp.mean(o, axis=-1, keepdims=True)
    d = o - mu
    var = jnp.mean(d * d, axis=-1, keepdims=True)
    return d * lax.rsqrt(var + EPS)


def _retention_head_paired(q, kt, kdt, v, s, dmask, qdec, gl):
    L = q.shape[0]
    sc = _dot(q, jnp.concatenate([kt, s.astype(BF16)], axis=1))
    scores = (sc[:, :L] * dmask).astype(BF16)
    iu = _dot(jnp.concatenate([scores, kdt], axis=0), v)
    return iu[:L] + sc[:, L:] * qdec, gl * s + iu[L:]


def _short_conv(u, tail, cw, rows):
    r1 = pltpu.roll(u, 1, 0)
    r2 = pltpu.roll(u, 2, 0)
    t0, t1 = tail[0:1, :], tail[1:2, :]
    sh1 = jnp.where(rows == 0, t1, r1)
    sh2 = jnp.where(rows == 0, t0, jnp.where(rows == 1, t1, r2))
    return cw[0:1, :] * sh2 + cw[1:2, :] * sh1 + cw[2:3, :] * u


def _out_proj(x, gated, bgy, ga, gb, wro_ref, wco_ref, wo_ref):
    ret_out = _dot(gated.astype(BF16), wro_ref[...])
    conv_out = _dot(bgy.astype(BF16), wco_ref[...])
    merged = _sigmoid(ga) * ret_out + _sigmoid(gb) * conv_out
    return x + _dot(merged.astype(BF16), wo_ref[...])


def _ffn_block(x, g_ref, wgu_ref, wd_ref):
    hn = _rms_f32(x, g_ref[...]).astype(BF16)
    a = _dot(hn, wgu_ref[:, :D_FF])
    b = _dot(hn, wgu_ref[:, D_FF:])
    return x + _dot((_silu(a) * b).astype(BF16), wd_ref[...])


def _log_gamma():
    return np.log1p(-np.exp2(-5.0 - np.arange(HEADS, dtype=np.float64)))


def _const(t):
    return jnp.asarray(np.asarray(t, dtype=np.float32))


def _rope_tables(pos):
    half = DH // 2
    inv = np.power(ROPE_BASE, -np.arange(half, dtype=np.float64) / half)
    ang = np.asarray(pos, dtype=np.float64)[:, None] * inv[None, :]
    cos, sin = np.cos(ang), np.sin(ang)
    cosf = np.concatenate([cos, cos], axis=-1)
    sinf = np.concatenate([-sin, sin], axis=-1)
    scale = DH ** -0.5
    return tuple(_const(t) for t in (cosf, sinf, cosf * scale, sinf * scale))


def _decay_tables(L):
    log_g = _log_gamma()
    idx = np.arange(L, dtype=np.float64)
    diff = idx[:, None] - idx[None, :]
    dmask = np.where(diff >= 0, np.exp(log_g[:, None, None] * np.maximum(diff, 0.0)[None]), 0.0)
    qdec = np.exp(log_g[:, None] * (idx + 1.0)[None])
    kdec = np.exp(log_g[:, None] * (L - 1.0 - idx)[None])
    gl = np.exp(log_g * L)
    lanes = lambda t: np.repeat(t.T, DH, axis=1)
    return tuple(_const(t) for t in (dmask, lanes(qdec), lanes(kdec), np.repeat(gl, DH)[None, :], kdec[:, None, :]))


def _mixer_main_kernel(x_ref, s0_ref, c0_ref, g_ref, win_ref, cw_ref, wro_ref, wco_ref, wo_ref,
                       cq_ref, sq_ref, ck_ref, sk_ref, dmask_ref, qdec_ref, kdec_ref, gl_ref, wgu32_ref, wd32_ref,
                       h_ref, s_ref, c_ref, wgub_ref, wdb_ref, q_s, kt_s, kdt_s, v_s, o_s):
    bg = x_ref.shape[0]
    m = bg * CHUNK

    wgub_ref[...] = wgu32_ref[...].astype(BF16)
    step = pl.program_id(0) * pl.num_programs(1) + pl.program_id(1)

    @pl.when(step < D_FF // wd32_ref.shape[0])
    def _():
        wdb_ref[...] = wd32_ref[...].astype(BF16)

    @pl.when(pl.program_id(1) == 0)
    def _():
        for b in range(bg):
            s_ref[b] = s0_ref[...]
            c_ref[b] = c0_ref[...]

    x = x_ref[...].reshape(m, D_MODEL)
    hn = _rms_f32(x, g_ref[...]).astype(BF16)

    cq, sq, ck, sk = cq_ref[...], sq_ref[...], ck_ref[...], sk_ref[...]
    q = _dot(hn, win_ref[:, C_Q:C_Q + D_MODEL])
    k = _dot(hn, win_ref[:, C_K:C_K + D_MODEL])
    v_s[...] = _dot(hn, win_ref[:, C_V:C_V + D_MODEL]).astype(BF16)
    for b in range(bg):
        r = slice(b * CHUNK, (b + 1) * CHUNK)
        for h in range(HEADS):
            cl = slice(h * DH, (h + 1) * DH)
            q_s[r, cl] = _rotary(q[r, cl], cq, sq).astype(BF16)
            krt = _rotary(k[r, cl], ck, sk).T
            kt_s[b * HEADS + h] = krt.astype(BF16)
            kdt_s[b * HEADS + h] = (krt * kdec_ref[h]).astype(BF16)

    for b in range(bg):
        r = slice(b * CHUNK, (b + 1) * CHUNK)
        for h in range(HEADS):
            cl = slice(h * DH, (h + 1) * DH)
            o, s_new = _retention_head_paired(q_s[r, cl], kt_s[b * HEADS + h], kdt_s[b * HEADS + h], v_s[r, cl],
                                              s_ref[b, h], dmask_ref[h], qdec_ref[:, cl], gl_ref[:, cl])
            s_ref[b, h] = s_new
            o_s[r, cl] = _group_norm(o)

    g = _dot(hn, win_ref[:, C_G:C_G + D_MODEL])
    gated = _silu(g) * o_s[...]

    bgate = _dot(hn, win_ref[:, C_BG:C_BG + D_MODEL])
    u = _dot(hn, win_ref[:, C_CG:C_CG + D_MODEL]) * _dot(hn, win_ref[:, C_HC:C_HC + D_MODEL])
    rows = lax.broadcasted_iota(jnp.int32, (CHUNK, D_MODEL), 0)
    cw = cw_ref[...]
    ys = []
    for b in range(bg):
        ub = u[b * CHUNK:(b + 1) * CHUNK]
        ys.append(_short_conv(ub, c_ref[b], cw, rows))
        c_ref[b] = ub[CHUNK - (CONV_W - 1):, :]
    bgy = bgate * jnp.concatenate(ys, axis=0)

    ga = _dot(hn, win_ref[:, C_GA:C_GA + D_MODEL])
    gb = _dot(hn, win_ref[:, C_GB:C_GB + D_MODEL])
    out = _out_proj(x, gated, bgy, ga, gb, wro_ref, wco_ref, wo_ref)
    h_ref[...] = out.reshape(bg, CHUNK, D_MODEL)


def _mixer_main(x, s0, c0, layer, g, win, cw, wro, wco, wo, rope, decay, wgu32, wd32):
    nb, seq, _ = x.shape
    bg = MAIN_BG
    cq, sq, ck, sk = rope
    dmask, qdec, _, gl, kdec = decay
    m = bg * CHUNK
    n_chunks = seq // CHUNK
    steps = (nb // bg) * n_chunks
    slab = D_MODEL // steps
    wd_slab = 16 * (-(-D_FF // (16 * steps)))
    while D_FF % wd_slab:
        wd_slab += 16
    wd_last = D_FF // wd_slab - 1
    assert slab * steps == D_MODEL and slab % 16 == 0 and wd_last < steps

    def wd_block(i, c):
        return jnp.minimum(i * n_chunks + c, wd_last)

    rope_spec = pl.BlockSpec((CHUNK, DH), lambda i, c: (c, 0))
    return pl.pallas_call(
        _mixer_main_kernel,
        grid=(nb // bg, n_chunks),
        in_specs=[
            pl.BlockSpec((bg, CHUNK, D_MODEL), lambda i, c: (i, c, 0)),
            _resident_layer((HEADS, DH, DH), 0),
            _resident_layer((CONV_W - 1, D_MODEL), 0),
            _resident_layer((1, D_MODEL), layer),
            _resident_layer((D_MODEL, N_IN), 0),
            _resident_layer((CONV_W, D_MODEL), layer),
            _resident_layer((D_MODEL, D_MODEL), layer),
            _resident_layer((D_MODEL, D_MODEL), layer),
            _resident_layer((D_MODEL, D_MODEL), layer),
            rope_spec, rope_spec, rope_spec, rope_spec,
            _resident((HEADS, CHUNK, CHUNK)),
            _resident((CHUNK, D_MODEL)),
            _resident((HEADS, 1, CHUNK)),
            _resident((1, D_MODEL)),
            pl.BlockSpec((None, slab, 2 * D_FF), lambda i, c: (layer, i * n_chunks + c, 0)),
            pl.BlockSpec((None, wd_slab, D_MODEL), lambda i, c: (layer, wd_block(i, c), 0)),
        ],
        out_specs=[
            pl.BlockSpec((bg, CHUNK, D_MODEL), lambda i, c: (i, c, 0)),
            pl.BlockSpec((bg, HEADS, DH, DH), lambda i, c: (i, 0, 0, 0)),
            pl.BlockSpec((bg, CONV_W - 1, D_MODEL), lambda i, c: (i, 0, 0)),
            pl.BlockSpec((None, slab, 2 * D_FF), lambda i, c: (0, i * n_chunks + c, 0)),
            pl.BlockSpec((None, wd_slab, D_MODEL), lambda i, c: (0, wd_block(i, c), 0)),
        ],
        out_shape=[
            jax.ShapeDtypeStruct(x.shape, F32),
            jax.ShapeDtypeStruct((nb, HEADS, DH, DH), F32),
            jax.ShapeDtypeStruct((nb, CONV_W - 1, D_MODEL), F32),
            jax.ShapeDtypeStruct((1, D_MODEL, 2 * D_FF), BF16),
            jax.ShapeDtypeStruct((1, D_FF, D_MODEL), BF16),
        ],
        scratch_shapes=[
            pltpu.VMEM((m, D_MODEL), BF16),
            pltpu.VMEM((bg * HEADS, DH, CHUNK), BF16),
            pltpu.VMEM((bg * HEADS, DH, CHUNK), BF16),
            pltpu.VMEM((m, D_MODEL), BF16),
            pltpu.VMEM((m, D_MODEL), F32),
        ],
        compiler_params=_params(("arbitrary", "arbitrary")),
        name="mixer_main",
    )(x, s0, c0, g, win, cw, wro, wco, wo, cq, sq, ck, sk, dmask, qdec, kdec, gl, wgu32, wd32)


def _proj_kernel(x_ref, g_ref, w32_ref, o_ref, wb_ref, hn_s):
    @pl.when(pl.program_id(0) == 0)
    def _():
        hn_s[...] = _rms_f32(x_ref[...], g_ref[...]).astype(BF16)

    wb = w32_ref[...].astype(BF16)
    wb_ref[...] = wb
    o_ref[...] = _dot(hn_s[...], wb)


def _proj(x, layer, g, win32):
    rows = x.shape[0]
    n_col = N_IN // D_MODEL
    return pl.pallas_call(
        _proj_kernel,
        grid=(n_col,),
        in_specs=[_resident((rows, D_MODEL)), _resident_layer((1, D_MODEL), layer),
                  pl.BlockSpec((None, D_MODEL, D_MODEL), lambda j: (layer, 0, j))],
        out_specs=[pl.BlockSpec((rows, D_MODEL), lambda j: (0, j)),
                   pl.BlockSpec((None, D_MODEL, D_MODEL), lambda j: (0, 0, j))],
        out_shape=[jax.ShapeDtypeStruct((rows, N_IN), F32), jax.ShapeDtypeStruct((1, D_MODEL, N_IN), BF16)],
        scratch_shapes=[pltpu.VMEM((rows, D_MODEL), BF16)],
        compiler_params=_params(("arbitrary",)),
        name="proj_short",
    )(x, g, win32)


def _ret_short_first(p_ref, s_ref, tab_refs, sb, L, sink):
    cq_ref, sq_ref, ck_ref, sk_ref, dmask_ref, qdec_ref, kdec_ref = tab_refs
    cq, sq, ck, sk = cq_ref[...], sq_ref[...], ck_ref[...], sk_ref[...]
    staged = []
    for b in range(sb):
        r = slice(b * L, (b + 1) * L)
        for h in range(HEADS):
            cl = slice(h * DH, (h + 1) * DH)
            q = _rotary(p_ref[r, C_Q + h * DH:C_Q + (h + 1) * DH], cq, sq).astype(BF16)
            k = _rotary(p_ref[r, C_K + h * DH:C_K + (h + 1) * DH], ck, sk)
            v = p_ref[r, C_V + h * DH:C_V + (h + 1) * DH]
            s = s_ref[b, h]
            scores = (_dot_nt(q, k.astype(BF16)) * dmask_ref[h]).astype(BF16)
            cross = _dot(q, s.astype(BF16)) * qdec_ref[:, cl]
            sink(b, h, r, cl, s, k * kdec_ref[:, cl], v)
            staged.append((scores, cross, v.astype(BF16)))
    return staged


def _ret_short_second(staged, p_ref, c_ref, cw_ref, gated_ref, bgy_ref, cn_ref, sb, L):
    rows = lax.broadcasted_iota(jnp.int32, (L, D_MODEL), 0)
    cw = cw_ref[...]
    for b in range(sb):
        r = slice(b * L, (b + 1) * L)
        for h in range(HEADS):
            cl = slice(h * DH, (h + 1) * DH)
            scores, cross, v = staged[b * HEADS + h]
            o = _dot(scores, v) + cross
            g = p_ref[r, C_G + h * DH:C_G + (h + 1) * DH]
            gated_ref[r, cl] = _silu(g) * _group_norm(o)
        u = p_ref[r, C_CG:C_CG + D_MODEL] * p_ref[r, C_HC:C_HC + D_MODEL]
        y = _short_conv(u, c_ref[b], cw, rows)
        bgy_ref[r, :] = p_ref[r, C_BG:C_BG + D_MODEL] * y
        cn_ref[b] = u[L - (CONV_W - 1):, :]


def _state_sink(sn_ref, gl_ref, layer=None):
    def sink(b, h, r, cl, s, kd, v):
        s_new = gl_ref[:, cl] * s + _dot_tn(kd.astype(BF16), v.astype(BF16))
        if layer is None:
            sn_ref[b, h] = s_new
        else:
            sn_ref[layer, b, h] = s_new
    return sink


def _ret_meta_kernel(p_ref, s_ref, c_ref, cw_ref, cq_ref, sq_ref, ck_ref, sk_ref, dmask_ref, qdec_ref, kdec_ref,
                     gl_ref, gated_ref, bgy_ref, cn_ref, sn_ref, *, L):
    sb = c_ref.shape[0]
    tabs = (cq_ref, sq_ref, ck_ref, sk_ref, dmask_ref, qdec_ref, kdec_ref)
    staged = _ret_short_first(p_ref, s_ref, tabs, sb, L, _state_sink(sn_ref, gl_ref))
    _ret_short_second(staged, p_ref, c_ref, cw_ref, gated_ref, bgy_ref, cn_ref, sb, L)


def _short_table_specs(L):
    return [_resident((L, DH))] * 4 + [_resident((HEADS, L, L)), _resident((L, D_MODEL)), _resident((L, D_MODEL)),
                                       _resident((1, D_MODEL))]


def _ret_meta(proj, row0, state, cprev, layer, cw, rope, decay, nseq, L):
    dmask, qdec, kdec, gl, _ = decay
    rows = nseq * L
    row_spec = pl.BlockSpec((rows, D_MODEL), lambda i: (0, 0))
    row_shape = jax.ShapeDtypeStruct((rows, D_MODEL), F32)
    return pl.pallas_call(
        functools.partial(_ret_meta_kernel, L=L),
        grid=(1,),
        in_specs=[
            pl.BlockSpec((rows, C_GA), lambda i: (row0 // rows, 0)),
            _resident((nseq, HEADS, DH, DH)),
            _resident((nseq, CONV_W - 1, D_MODEL)),
            _resident_layer((CONV_W, D_MODEL), layer),
        ] + _short_table_specs(L),
        out_specs=[row_spec, row_spec, pl.BlockSpec((nseq, CONV_W - 1, D_MODEL), lambda i: (0, 0, 0)),
                   pl.BlockSpec((nseq, HEADS, DH, DH), lambda i: (0, 0, 0, 0))],
        out_shape=[row_shape, row_shape, jax.ShapeDtypeStruct((nseq, CONV_W - 1, D_MODEL), F32),
                   jax.ShapeDtypeStruct((nseq, HEADS, DH, DH), F32)],
        compiler_params=_params(("arbitrary",)),
        name="ret_meta",
    )(proj, state, cprev, cw, *rope, dmask, qdec, kdec, gl)


def _out_ffn_kernel(x_ref, gated_ref, bgy_ref, ga_ref, gb_ref, wro_ref, wco_ref, wo_ref,
                    g_ref, wgu_ref, wd_ref, fg_ref, o_ref, *, final):
    h = _out_proj(x_ref[...], gated_ref[...], bgy_ref[...], ga_ref[...], gb_ref[...], wro_ref, wco_ref, wo_ref)
    y = _ffn_block(h, g_ref, wgu_ref, wd_ref)
    if final:
        y = _rms_f32(y, fg_ref[...])
    o_ref[...] = y


def _out_ffn(x, gated, bgy, proj, layer, wro, wco, wo, g, wgu, wd, fg, final, rows, tm):
    row_spec = pl.BlockSpec((tm, D_MODEL), lambda i: (i, 0))
    w_spec = _resident_layer((D_MODEL, D_MODEL), layer)
    return pl.pallas_call(
        functools.partial(_out_ffn_kernel, final=final),
        grid=(rows // tm,),
        in_specs=[row_spec, row_spec, row_spec,
                  pl.BlockSpec((tm, D_MODEL), lambda i: (i, C_GA // D_MODEL)),
                  pl.BlockSpec((tm, D_MODEL), lambda i: (i, C_GB // D_MODEL)),
                  w_spec, w_spec, w_spec,
                  _resident_layer((1, D_MODEL), layer), _resident_layer((D_MODEL, 2 * D_FF), 0),
                  _resident_layer((D_FF, D_MODEL), 0), _resident((1, D_MODEL))],
        out_specs=row_spec,
        out_shape=jax.ShapeDtypeStruct((rows, D_MODEL), F32),
        compiler_params=_params(("arbitrary",)),
        name="out_ffn_short",
    )(x, gated, bgy, proj, proj, wro, wco, wo, g, wgu, wd, fg)


def _ffn_main_kernel(x_ref, g_ref, wgu_ref, wd_ref, fg_ref, p_ref, s_ref, c_ref, cw_ref,
                     cq_ref, sq_ref, ck_ref, sk_ref, dmask_ref, qdec_ref, kdec_ref, gl_ref, *rest, final, n_prev, L):
    sb = c_ref.shape[0]
    prev, outs = rest[:3 * n_prev], rest[3 * n_prev:]
    y_ref, gated_ref, bgy_ref, cn_ref = outs[:4]
    tabs = (cq_ref, sq_ref, ck_ref, sk_ref, dmask_ref, qdec_ref, kdec_ref)

    if final:
        sn_ref = outs[4]
        sink = _state_sink(sn_ref, gl_ref, n_prev)
    else:
        kd_ref, v_ref = outs[4:6]

        def sink(b, h, r, cl, s, kd, v):
            kd_ref[r, cl] = kd
            v_ref[r, cl] = v

    x = x_ref[...]
    hn = _rms_f32(x, g_ref[...]).astype(BF16)
    staged = _ret_short_first(p_ref, s_ref, tabs, sb, L, sink)
    a = _dot(hn, wgu_ref[:, :D_FF])
    if final:
        for l in range(n_prev):
            so_ref, kdo_ref, vo_ref = prev[3 * l:3 * l + 3]
            upd = _state_sink(sn_ref, gl_ref, l)
            for b in range(sb):
                r = slice(b * L, (b + 1) * L)
                for h in range(HEADS):
                    cl = slice(h * DH, (h + 1) * DH)
                    upd(b, h, r, cl, so_ref[b, h], kdo_ref[r, cl], vo_ref[r, cl])
    b = _dot(hn, wgu_ref[:, D_FF:])
    _ret_short_second(staged, p_ref, c_ref, cw_ref, gated_ref, bgy_ref, cn_ref, sb, L)
    y = x + _dot((_silu(a) * b).astype(BF16), wd_ref[...])
    if final:
        y = _rms_f32(y, fg_ref[...])
    y_ref[...] = y


def _ffn_main(x, layer, g, wgu, wd, fg, final, tm, proj, state, cconv, cw, rope, decay, L, kds, vs):
    rows = x.shape[0]
    steps = rows // tm
    depth, ns = state.shape[:2]
    sb = ns // steps
    assert sb * steps == ns and (sb * L) % 8 == 0
    n_prev = len(kds) if final else 0
    dmask, qdec, kdec, gl, _ = decay
    n_samp = ns * L
    row_spec = pl.BlockSpec((tm, D_MODEL), lambda i: (i, 0))
    srow_spec = pl.BlockSpec((sb * L, D_MODEL), lambda i: (i, 0))
    srow_shape = jax.ShapeDtypeStruct((n_samp, D_MODEL), F32)

    def state_spec(l):
        return pl.BlockSpec((None, sb, HEADS, DH, DH), lambda i: (l, i, 0, 0, 0))

    in_specs = [
        row_spec, _resident_layer((1, D_MODEL), layer), _resident_layer((D_MODEL, 2 * D_FF), 0),
        _resident_layer((D_FF, D_MODEL), 0), _resident((1, D_MODEL)),
        pl.BlockSpec((sb * L, C_GA), lambda i: (i, 0)),
        state_spec(layer),
        pl.BlockSpec((None, sb, CONV_W - 1, D_MODEL), lambda i: (layer, i, 0, 0)),
        _resident_layer((CONV_W, D_MODEL), layer),
    ] + _short_table_specs(L)
    args = [x, g, wgu, wd, fg, proj, state, cconv, cw, *rope, dmask, qdec, kdec, gl]
    for l in range(n_prev):
        in_specs += [state_spec(l), srow_spec, srow_spec]
        args += [state, kds[l], vs[l]]
    out_specs = [row_spec, srow_spec, srow_spec, pl.BlockSpec((sb, CONV_W - 1, D_MODEL), lambda i: (i, 0, 0))]
    out_shape = [jax.ShapeDtypeStruct(x.shape, F32), srow_shape, srow_shape,
                 jax.ShapeDtypeStruct((ns, CONV_W - 1, D_MODEL), F32)]
    if final:
        assert n_prev == depth - 1
        out_specs.append(pl.BlockSpec((depth, sb, HEADS, DH, DH), lambda i: (0, i, 0, 0, 0)))
        out_shape.append(jax.ShapeDtypeStruct(state.shape, F32))
    else:
        out_specs += [srow_spec, srow_spec]
        out_shape += [srow_shape, srow_shape]
    return pl.pallas_call(
        functools.partial(_ffn_main_kernel, final=final, n_prev=n_prev, L=L),
        grid=(steps,),
        in_specs=in_specs,
        out_specs=out_specs,
        out_shape=out_shape,
        compiler_params=_params(("arbitrary",)),
        name="ffn_main",
    )(*args)


def kernel(x_prompt, x_sample, state_ret, state_conv, meta_tokens, norm_mix_g, w_in, conv_w, w_ret_o,
           w_conv_o, w_o, norm_ffn_g, w_gate_up, w_down, final_norm_g):
    depth = w_in.shape[0]
    nb, seq, _ = x_prompt.shape
    ns, ls, _ = x_sample.shape
    n_samp = ns * ls
    n_short = n_samp + N_META
    short_tm = n_short // 5
    assert short_tm * 5 == n_short and short_tm % 8 == 0 and n_samp % N_META == 0

    wro_b, wco_b, wo_b = (w.astype(BF16) for w in (w_ret_o, w_conv_o, w_o))
    gm = norm_mix_g.reshape(depth, 1, D_MODEL)
    gf = norm_ffn_g.reshape(depth, 1, D_MODEL)
    fg = final_norm_g.reshape(1, D_MODEL)

    rope_meta = _rope_tables(np.arange(N_META))
    rope_main = _rope_tables(N_META + np.arange(seq))
    rope_samp = _rope_tables(PAST_LEN + np.arange(ls))
    dec_meta, dec_main, dec_samp = _decay_tables(N_META), _decay_tables(CHUNK), _decay_tables(ls)

    h_main = x_prompt
    h_short = jnp.concatenate([x_sample.reshape(n_samp, D_MODEL), meta_tokens.astype(F32)], axis=0)
    zero_s = jnp.zeros((1, HEADS, DH, DH), F32)
    zero_c = jnp.zeros((1, CONV_W - 1, D_MODEL), F32)

    s_p, c_p, c_s, kds, vs = [], [], [], [], []
    y_samp = s_s = None
    for l in range(depth):
        last = l == depth - 1
        proj, win_b = _proj(h_short, l, gm, w_in)
        gated_m, bgy_m, c_m, s_m = _ret_meta(proj, n_samp, zero_s, zero_c, l, conv_w, rope_meta, dec_meta, 1, N_META)

        h_main, s_l, c_l, wgu_b, wd_b = _mixer_main(h_main, s_m, c_m, l, gm, win_b, conv_w, wro_b, wco_b, wo_b,
                                                    rope_main, dec_main, w_gate_up, w_down)
        s_p.append(s_l)
        c_p.append(c_l)
        res = _ffn_main(h_main.reshape(nb * seq, D_MODEL), l, gf, wgu_b, wd_b, fg, last, MAIN_FFN_TM,
                        proj, state_ret, state_conv, conv_w, rope_samp, dec_samp, ls, kds, vs)
        h_main, gated_s, bgy_s, c_l = res[:4]
        h_main = h_main.reshape(nb, seq, D_MODEL)
        c_s.append(c_l)
        if last:
            s_s = res[4]
        else:
            kds.append(res[4])
            vs.append(res[5])

        if last:
            y_samp = _out_ffn(h_short, gated_s, bgy_s, proj, l, wro_b, wco_b, wo_b, gf, wgu_b, wd_b, fg, True,
                              n_samp, 256)
        else:
            gated = jnp.concatenate([gated_s, gated_m], axis=0)
            bgy = jnp.concatenate([bgy_s, bgy_m], axis=0)
            h_short = _out_ffn(h_short, gated, bgy, proj, l, wro_b, wco_b, wo_b, gf, wgu_b, wd_b, fg, False,
                               n_short, short_tm)

    return (h_main, y_samp.reshape(ns, ls, D_MODEL), jnp.stack(s_p), jnp.stack(c_p), s_s, jnp.stack(c_s))
```

```python
import functools

import numpy as np
import jax
import jax.numpy as jnp
from jax import lax
from jax.experimental import pallas as pl
from jax.experimental.pallas import tpu as pltpu

D_MODEL = 1024
N_META = 16
HEADS = 8
DH = D_MODEL // HEADS
CHUNK = 128
ROPE_BASE = 10000.0
CONV_W = 3
D_FF = ((8 * D_MODEL + 3 * 256 - 1) // (3 * 256)) * 256
EPS = 1e-6
PAST_LEN = 16384
N_IN = 9 * D_MODEL
C_Q, C_K, C_V, C_G, C_BG, C_CG, C_HC, C_GA, C_GB = (i * D_MODEL for i in range(9))

F32 = jnp.float32
BF16 = jnp.bfloat16

VMEM_LIMIT_BYTES = 58 * 1024 * 1024

MAIN_BG = 4
MAIN_FFN_TM = 512


def _resident(shape):
    nd = len(shape)
    return pl.BlockSpec(shape, lambda *_: (0,) * nd, pipeline_mode=pl.Buffered(1))


def _resident_layer(shape, layer):
    nd = len(shape)
    return pl.BlockSpec((None,) + tuple(shape), lambda *_: (layer,) + (0,) * nd, pipeline_mode=pl.Buffered(1))


def _params(sem):
    return pltpu.CompilerParams(dimension_semantics=sem, vmem_limit_bytes=VMEM_LIMIT_BYTES)


def _dot(a, b):
    return jnp.dot(a, b, preferred_element_type=F32)


def _dot_nt(a, b):
    return lax.dot_general(a, b, (((1,), (1,)), ((), ())), preferred_element_type=F32)


def _dot_tn(a, b):
    return lax.dot_general(a, b, (((0,), (0,)), ((), ())), preferred_element_type=F32)


def _rms_f32(x, g):
    return x * lax.rsqrt(jnp.mean(x * x, axis=-1, keepdims=True) + EPS) * g


def _sigmoid(x):
    return 1.0 / (1.0 + jnp.exp(-x))


def _silu(x):
    return x * _sigmoid(x)


def _rotary(t, cos, sin):
    return t * cos + pltpu.roll(t, DH // 2, 1) * sin


def _group_norm(o):
    mu = jnp.mean(o, axis=-1, keepdims=True)
    d = o - mu
    var = jnp.mean(d * d, axis=-1, keepdims=True)
    return d * lax.rsqrt(var + EPS)


def _retention_head_paired(q, kt, kdt, v, s, dmask, qdec, gl):
    L = q.shape[0]
    sc = _dot(q, jnp.concatenate([kt, s.astype(BF16)], axis=1))
    scores = (sc[:, :L] * dmask).astype(BF16)
    iu = _dot(jnp.concatenate([scores, kdt], axis=0), v)
    return iu[:L] + sc[:, L:] * qdec, gl * s + iu[L:]


def _short_conv(u, tail, cw, rows):
    r1 = pltpu.roll(u, 1, 0)
    r2 = pltpu.roll(u, 2, 0)
    t0, t1 = tail[0:1, :], tail[1:2, :]
    sh1 = jnp.where(rows == 0, t1, r1)
    sh2 = jnp.where(rows == 0, t0, jnp.where(rows == 1, t1, r2))
    return cw[0:1, :] * sh2 + cw[1:2, :] * sh1 + cw[2:3, :] * u


def _out_proj(x, gated, bgy, ga, gb, wro_ref, wco_ref, wo_ref):
    ret_out = _dot(gated.astype(BF16), wro_ref[...])
    conv_out = _dot(bgy.astype(BF16), wco_ref[...])
    merged = _sigmoid(ga) * ret_out + _sigmoid(gb) * conv_out
    return x + _dot(merged.astype(BF16), wo_ref[...])


def _ffn_block(x, g_ref, wgu_ref, wd_ref):
    hn = _rms_f32(x, g_ref[...]).astype(BF16)
    a = _dot(hn, wgu_ref[:, :D_FF])
    b = _dot(hn, wgu_ref[:, D_FF:])
    return x + _dot((_silu(a) * b).astype(BF16), wd_ref[...])


def _log_gamma():
    return np.log1p(-np.exp2(-5.0 - np.arange(HEADS, dtype=np.float64)))


def _const(t):
    return jnp.asarray(np.asarray(t, dtype=np.float32))


def _rope_tables(pos):
    half = DH // 2
    inv = np.power(ROPE_BASE, -np.arange(half, dtype=np.float64) / half)
    ang = np.asarray(pos, dtype=np.float64)[:, None] * inv[None, :]
    cos, sin = np.cos(ang), np.sin(ang)
    cosf = np.concatenate([cos, cos], axis=-1)
    sinf = np.concatenate([-sin, sin], axis=-1)
    scale = DH ** -0.5
    return tuple(_const(t) for t in (cosf, sinf, cosf * scale, sinf * scale))


def _decay_tables(L):
    log_g = _log_gamma()
    idx = np.arange(L, dtype=np.float64)
    diff = idx[:, None] - idx[None, :]
    dmask = np.where(diff >= 0, np.exp(log_g[:, None, None] * np.maximum(diff, 0.0)[None]), 0.0)
    qdec = np.exp(log_g[:, None] * (idx + 1.0)[None])
    kdec = np.exp(log_g[:, None] * (L - 1.0 - idx)[None])
    gl = np.exp(log_g * L)
    lanes = lambda t: np.repeat(t.T, DH, axis=1)
    return tuple(_const(t) for t in (dmask, lanes(qdec), lanes(kdec), np.repeat(gl, DH)[None, :], kdec[:, None, :]))


def _mixer_main_kernel(x_ref, s0_ref, c0_ref, g_ref, win_ref, cw_ref, wro_ref, wco_ref, wo_ref,
                       cq_ref, sq_ref, ck_ref, sk_ref, dmask_ref, qdec_ref, kdec_ref, gl_ref, wgu32_ref, wd32_ref,
                       h_ref, s_ref, c_ref, wgub_ref, wdb_ref, q_s, kt_s, kdt_s, v_s, o_s):
    bg = x_ref.shape[0]
    m = bg * CHUNK

    wgub_ref[...] = wgu32_ref[...].astype(BF16)
    step = pl.program_id(0) * pl.num_programs(1) + pl.program_id(1)

    @pl.when(step < D_FF // wd32_ref.shape[0])
    def _():
        wdb_ref[...] = wd32_ref[...].astype(BF16)

    @pl.when(pl.program_id(1) == 0)
    def _():
        for b in range(bg):
            s_ref[b] = s0_ref[...]
            c_ref[b] = c0_ref[...]

    x = x_ref[...].reshape(m, D_MODEL)
    hn = _rms_f32(x, g_ref[...]).astype(BF16)

    cq, sq, ck, sk = cq_ref[...], sq_ref[...], ck_ref[...], sk_ref[...]
    q = _dot(hn, win_ref[:, C_Q:C_Q + D_MODEL])
    k = _dot(hn, win_ref[:, C_K:C_K + D_MODEL])
    v_s[...] = _dot(hn, win_ref[:, C_V:C_V + D_MODEL]).astype(BF16)
    for b in range(bg):
        r = slice(b * CHUNK, (b + 1) * CHUNK)
        for h in range(HEADS):
            cl = slice(h * DH, (h + 1) * DH)
            q_s[r, cl] = _rotary(q[r, cl], cq, sq).astype(BF16)
            krt = _rotary(k[r, cl], ck, sk).T
            kt_s[b * HEADS + h] = krt.astype(BF16)
            kdt_s[b * HEADS + h] = (krt * kdec_ref[h]).astype(BF16)

    for b in range(bg):
        r = slice(b * CHUNK, (b + 1) * CHUNK)
        for h in range(HEADS):
            cl = slice(h * DH, (h + 1) * DH)
            o, s_new = _retention_head_paired(q_s[r, cl], kt_s[b * HEADS + h], kdt_s[b * HEADS + h], v_s[r, cl],
                                              s_ref[b, h], dmask_ref[h], qdec_ref[:, cl], gl_ref[:, cl])
            s_ref[b, h] = s_new
            o_s[r, cl] = _group_norm(o)

    g = _dot(hn, win_ref[:, C_G:C_G + D_MODEL])
    gated = _silu(g) * o_s[...]

    bgate = _dot(hn, win_ref[:, C_BG:C_BG + D_MODEL])
    u = _dot(hn, win_ref[:, C_CG:C_CG + D_MODEL]) * _dot(hn, win_ref[:, C_HC:C_HC + D_MODEL])
    rows = lax.broadcasted_iota(jnp.int32, (CHUNK, D_MODEL), 0)
    cw = cw_ref[...]
    ys = []
    for b in range(bg):
        ub = u[b * CHUNK:(b + 1) * CHUNK]
        ys.append(_short_conv(ub, c_ref[b], cw, rows))
        c_ref[b] = ub[CHUNK - (CONV_W - 1):, :]
    bgy = bgate * jnp.concatenate(ys, axis=0)

    ga = _dot(hn, win_ref[:, C_GA:C_GA + D_MODEL])
    gb = _dot(hn, win_ref[:, C_GB:C_GB + D_MODEL])
    out = _out_proj(x, gated, bgy, ga, gb, wro_ref, wco_ref, wo_ref)
    h_ref[...] = out.reshape(bg, CHUNK, D_MODEL)


def _mixer_main(x, s0, c0, layer, g, win, cw, wro, wco, wo, rope, decay, wgu32, wd32):
    nb, seq, _ = x.shape
    bg = MAIN_BG
    cq, sq, ck, sk = rope
    dmask, qdec, _, gl, kdec = decay
    m = bg * CHUNK
    n_chunks = seq // CHUNK
    steps = (nb // bg) * n_chunks
    slab = D_MODEL // steps
    wd_slab = 16 * (-(-D_FF // (16 * steps)))
    while D_FF % wd_slab:
        wd_slab += 16
    wd_last = D_FF // wd_slab - 1
    assert slab * steps == D_MODEL and slab % 16 == 0 and wd_last < steps

    def wd_block(i, c):
        return jnp.minimum(i * n_chunks + c, wd_last)

    rope_spec = pl.BlockSpec((CHUNK, DH), lambda i, c: (c, 0))
    return pl.pallas_call(
        _mixer_main_kernel,
        grid=(nb // bg, n_chunks),
        in_specs=[
            pl.BlockSpec((bg, CHUNK, D_MODEL), lambda i, c: (i, c, 0)),
            _resident_layer((HEADS, DH, DH), 0),
            _resident_layer((CONV_W - 1, D_MODEL), 0),
            _resident_layer((1, D_MODEL), layer),
            _resident_layer((D_MODEL, N_IN), 0),
            _resident_layer((CONV_W, D_MODEL), layer),
            _resident_layer((D_MODEL, D_MODEL), 0),
            _resident_layer((D_MODEL, D_MODEL), 0),
            _resident_layer((D_MODEL, D_MODEL), 0),
            rope_spec, rope_spec, rope_spec, rope_spec,
            _resident((HEADS, CHUNK, CHUNK)),
            _resident((CHUNK, D_MODEL)),
            _resident((HEADS, 1, CHUNK)),
            _resident((1, D_MODEL)),
            pl.BlockSpec((None, slab, 2 * D_FF), lambda i, c: (layer, i * n_chunks + c, 0)),
            pl.BlockSpec((None, wd_slab, D_MODEL), lambda i, c: (layer, wd_block(i, c), 0)),
        ],
        out_specs=[
            pl.BlockSpec((bg, CHUNK, D_MODEL), lambda i, c: (i, c, 0)),
            pl.BlockSpec((bg, HEADS, DH, DH), lambda i, c: (i, 0, 0, 0)),
            pl.BlockSpec((bg, CONV_W - 1, D_MODEL), lambda i, c: (i, 0, 0)),
            pl.BlockSpec((None, slab, 2 * D_FF), lambda i, c: (0, i * n_chunks + c, 0)),
            pl.BlockSpec((None, wd_slab, D_MODEL), lambda i, c: (0, wd_block(i, c), 0)),
        ],
        out_shape=[
            jax.ShapeDtypeStruct(x.shape, F32),
            jax.ShapeDtypeStruct((nb, HEADS, DH, DH), F32),
            jax.ShapeDtypeStruct((nb, CONV_W - 1, D_MODEL), F32),
            jax.ShapeDtypeStruct((1, D_MODEL, 2 * D_FF), BF16),
            jax.ShapeDtypeStruct((1, D_FF, D_MODEL), BF16),
        ],
        scratch_shapes=[
            pltpu.VMEM((m, D_MODEL), BF16),
            pltpu.VMEM((bg * HEADS, DH, CHUNK), BF16),
            pltpu.VMEM((bg * HEADS, DH, CHUNK), BF16),
            pltpu.VMEM((m, D_MODEL), BF16),
            pltpu.VMEM((m, D_MODEL), F32),
        ],
        compiler_params=_params(("arbitrary", "arbitrary")),
        name="mixer_main",
    )(x, s0, c0, g, win, cw, wro, wco, wo, cq, sq, ck, sk, dmask, qdec, kdec, gl, wgu32, wd32)


def _proj_kernel(x_ref, g_ref, w32_ref, sq0_ref, sq1_ref, sq2_ref, o_ref, wb_ref, sb0_ref, sb1_ref, sb2_ref, hn_s,
                 *, n_slabs):
    j = pl.program_id(0)

    @pl.when(j == 0)
    def _():
        hn_s[...] = _rms_f32(x_ref[...], g_ref[...]).astype(BF16)

    @pl.when(j < n_slabs)
    def _():
        for src, dst in ((sq0_ref, sb0_ref), (sq1_ref, sb1_ref), (sq2_ref, sb2_ref)):
            dst[...] = src[...].astype(BF16)

    wb = w32_ref[...].astype(BF16)
    wb_ref[...] = wb
    o_ref[...] = _dot(hn_s[...], wb)


def _proj(x, layer, g, win32, squares32):
    rows = x.shape[0]
    n_col = N_IN // D_MODEL
    slab = 128
    n_slabs = D_MODEL // slab
    assert n_slabs <= n_col
    slab_in = pl.BlockSpec((None, slab, D_MODEL), lambda j: (layer, jnp.minimum(j, n_slabs - 1), 0))
    slab_out = pl.BlockSpec((None, slab, D_MODEL), lambda j: (0, jnp.minimum(j, n_slabs - 1), 0))
    square_shape = jax.ShapeDtypeStruct((1, D_MODEL, D_MODEL), BF16)
    return pl.pallas_call(
        functools.partial(_proj_kernel, n_slabs=n_slabs),
        grid=(n_col,),
        in_specs=[_resident((rows, D_MODEL)), _resident_layer((1, D_MODEL), layer),
                  pl.BlockSpec((None, D_MODEL, D_MODEL), lambda j: (layer, 0, j)), slab_in, slab_in, slab_in],
        out_specs=[pl.BlockSpec((rows, D_MODEL), lambda j: (0, j)),
                   pl.BlockSpec((None, D_MODEL, D_MODEL), lambda j: (0, 0, j)), slab_out, slab_out, slab_out],
        out_shape=[jax.ShapeDtypeStruct((rows, N_IN), F32), jax.ShapeDtypeStruct((1, D_MODEL, N_IN), BF16),
                   square_shape, square_shape, square_shape],
        scratch_shapes=[pltpu.VMEM((rows, D_MODEL), BF16)],
        compiler_params=_params(("arbitrary",)),
        name="proj_short",
    )(x, g, win32, *squares32)


def _ret_short_first(p_ref, s_ref, tab_refs, sb, L, sink):
    cq_ref, sq_ref, ck_ref, sk_ref, dmask_ref, qdec_ref, kdec_ref = tab_refs
    cq, sq, ck, sk = cq_ref[...], sq_ref[...], ck_ref[...], sk_ref[...]
    staged = []
    for b in range(sb):
        r = slice(b * L, (b + 1) * L)
        for h in range(HEADS):
            cl = slice(h * DH, (h + 1) * DH)
            q = _rotary(p_ref[r, C_Q + h * DH:C_Q + (h + 1) * DH], cq, sq).astype(BF16)
            k = _rotary(p_ref[r, C_K + h * DH:C_K + (h + 1) * DH], ck, sk)
            v = p_ref[r, C_V + h * DH:C_V + (h + 1) * DH]
            s = s_ref[b, h]
            scores = (_dot_nt(q, k.astype(BF16)) * dmask_ref[h]).astype(BF16)
            cross = _dot(q, s.astype(BF16)) * qdec_ref[:, cl]
            sink(b, h, r, cl, s, k * kdec_ref[:, cl], v)
            staged.append((scores, cross, v.astype(BF16)))
    return staged


def _ret_short_second(staged, p_ref, c_ref, cw_ref, gated_ref, bgy_ref, cn_ref, sb, L):
    rows = lax.broadcasted_iota(jnp.int32, (L, D_MODEL), 0)
    cw = cw_ref[...]
    for b in range(sb):
        r = slice(b * L, (b + 1) * L)
        for h in range(HEADS):
            cl = slice(h * DH, (h + 1) * DH)
            scores, cross, v = staged[b * HEADS + h]
            o = _dot(scores, v) + cross
            g = p_ref[r, C_G + h * DH:C_G + (h + 1) * DH]
            gated_ref[r, cl] = _silu(g) * _group_norm(o)
        u = p_ref[r, C_CG:C_CG + D_MODEL] * p_ref[r, C_HC:C_HC + D_MODEL]
        y = _short_conv(u, c_ref[b], cw, rows)
        bgy_ref[r, :] = p_ref[r, C_BG:C_BG + D_MODEL] * y
        cn_ref[b] = u[L - (CONV_W - 1):, :]


def _state_sink(sn_ref, gl_ref, layer=None):
    def sink(b, h, r, cl, s, kd, v):
        s_new = gl_ref[:, cl] * s + _dot_tn(kd.astype(BF16), v.astype(BF16))
        if layer is None:
            sn_ref[b, h] = s_new
        else:
            sn_ref[layer, b, h] = s_new
    return sink


def _ret_meta_kernel(p_ref, s_ref, c_ref, cw_ref, cq_ref, sq_ref, ck_ref, sk_ref, dmask_ref, qdec_ref, kdec_ref,
                     gl_ref, gated_ref, bgy_ref, cn_ref, sn_ref, *, L):
    sb = c_ref.shape[0]
    tabs = (cq_ref, sq_ref, ck_ref, sk_ref, dmask_ref, qdec_ref, kdec_ref)
    staged = _ret_short_first(p_ref, s_ref, tabs, sb, L, _state_sink(sn_ref, gl_ref))
    _ret_short_second(staged, p_ref, c_ref, cw_ref, gated_ref, bgy_ref, cn_ref, sb, L)


def _short_table_specs(L):
    return [_resident((L, DH))] * 4 + [_resident((HEADS, L, L)), _resident((L, D_MODEL)), _resident((L, D_MODEL)),
                                       _resident((1, D_MODEL))]


def _ret_meta(proj, row0, state, cprev, layer, cw, rope, decay, nseq, L):
    dmask, qdec, kdec, gl, _ = decay
    rows = nseq * L
    row_spec = pl.BlockSpec((rows, D_MODEL), lambda i: (0, 0))
    row_shape = jax.ShapeDtypeStruct((rows, D_MODEL), F32)
    return pl.pallas_call(
        functools.partial(_ret_meta_kernel, L=L),
        grid=(1,),
        in_specs=[
            pl.BlockSpec((rows, C_GA), lambda i: (row0 // rows, 0)),
            _resident((nseq, HEADS, DH, DH)),
            _resident((nseq, CONV_W - 1, D_MODEL)),
            _resident_layer((CONV_W, D_MODEL), layer),
        ] + _short_table_specs(L),
        out_specs=[row_spec, row_spec, pl.BlockSpec((nseq, CONV_W - 1, D_MODEL), lambda i: (0, 0, 0)),
                   pl.BlockSpec((nseq, HEADS, DH, DH), lambda i: (0, 0, 0, 0))],
        out_shape=[row_shape, row_shape, jax.ShapeDtypeStruct((nseq, CONV_W - 1, D_MODEL), F32),
                   jax.ShapeDtypeStruct((nseq, HEADS, DH, DH), F32)],
        compiler_params=_params(("arbitrary",)),
        name="ret_meta",
    )(proj, state, cprev, cw, *rope, dmask, qdec, kdec, gl)


def _out_ffn_kernel(x_ref, gated_ref, bgy_ref, ga_ref, gb_ref, wro_ref, wco_ref, wo_ref,
                    g_ref, wgu_ref, wd_ref, fg_ref, o_ref, *, final):
    h = _out_proj(x_ref[...], gated_ref[...], bgy_ref[...], ga_ref[...], gb_ref[...], wro_ref, wco_ref, wo_ref)
    y = _ffn_block(h, g_ref, wgu_ref, wd_ref)
    if final:
        y = _rms_f32(y, fg_ref[...])
    o_ref[...] = y


def _out_ffn(x, gated, bgy, proj, layer, wro, wco, wo, g, wgu, wd, fg, final, rows, tm):
    row_spec = pl.BlockSpec((tm, D_MODEL), lambda i: (i, 0))
    w_spec = _resident_layer((D_MODEL, D_MODEL), 0)
    return pl.pallas_call(
        functools.partial(_out_ffn_kernel, final=final),
        grid=(rows // tm,),
        in_specs=[row_spec, row_spec, row_spec,
                  pl.BlockSpec((tm, D_MODEL), lambda i: (i, C_GA // D_MODEL)),
                  pl.BlockSpec((tm, D_MODEL), lambda i: (i, C_GB // D_MODEL)),
                  w_spec, w_spec, w_spec,
                  _resident_layer((1, D_MODEL), layer), _resident_layer((D_MODEL, 2 * D_FF), 0),
                  _resident_layer((D_FF, D_MODEL), 0), _resident((1, D_MODEL))],
        out_specs=row_spec,
        out_shape=jax.ShapeDtypeStruct((rows, D_MODEL), F32),
        compiler_params=_params(("arbitrary",)),
        name="out_ffn_short",
    )(x, gated, bgy, proj, proj, wro, wco, wo, g, wgu, wd, fg)


def _ffn_main_kernel(x_ref, g_ref, wgu_ref, wd_ref, fg_ref, p_ref, s_ref, c_ref, cw_ref,
                     cq_ref, sq_ref, ck_ref, sk_ref, dmask_ref, qdec_ref, kdec_ref, gl_ref, *rest, final, n_prev, L):
    sb = c_ref.shape[0]
    n_main = n_prev + 1 if final else 0
    prev, mains, outs = rest[:3 * n_prev], rest[3 * n_prev:3 * n_prev + n_main], rest[3 * n_prev + n_main:]
    y_ref, gated_ref, bgy_ref, cn_ref = outs[:4]
    tabs = (cq_ref, sq_ref, ck_ref, sk_ref, dmask_ref, qdec_ref, kdec_ref)

    if final:
        sn_ref, mstack_ref = outs[4:6]
        for l in range(n_main):
            mstack_ref[l] = mains[l][...]
        sink = _state_sink(sn_ref, gl_ref, n_prev)
    else:
        kd_ref, v_ref = outs[4:6]

        def sink(b, h, r, cl, s, kd, v):
            kd_ref[r, cl] = kd
            v_ref[r, cl] = v

    x = x_ref[...]
    hn = _rms_f32(x, g_ref[...]).astype(BF16)
    staged = _ret_short_first(p_ref, s_ref, tabs, sb, L, sink)
    gate = _dot(hn, wgu_ref[:, :D_FF])
    if final:
        for l in range(n_prev):
            so_ref, kdo_ref, vo_ref = prev[3 * l:3 * l + 3]
            upd = _state_sink(sn_ref, gl_ref, l)
            for b in range(sb):
                r = slice(b * L, (b + 1) * L)
                for h in range(HEADS):
                    cl = slice(h * DH, (h + 1) * DH)
                    upd(b, h, r, cl, so_ref[b, h], kdo_ref[r, cl], vo_ref[r, cl])
    up = _dot(hn, wgu_ref[:, D_FF:])
    _ret_short_second(staged, p_ref, c_ref, cw_ref, gated_ref, bgy_ref, cn_ref, sb, L)
    y = x + _dot((_silu(gate) * up).astype(BF16), wd_ref[...])
    if final:
        y = _rms_f32(y, fg_ref[...])
    y_ref[...] = y


def _ffn_main(x, layer, g, wgu, wd, fg, final, tm, proj, state, cconv, cw, rope, decay, L, kds, vs, main_states):
    rows = x.shape[0]
    steps = rows // tm
    depth, ns = state.shape[:2]
    sb = ns // steps
    assert sb * steps == ns and (sb * L) % 8 == 0
    n_prev = len(kds) if final else 0
    dmask, qdec, kdec, gl, _ = decay
    n_samp = ns * L
    row_spec = pl.BlockSpec((tm, D_MODEL), lambda i: (i, 0))
    srow_spec = pl.BlockSpec((sb * L, D_MODEL), lambda i: (i, 0))
    srow_shape = jax.ShapeDtypeStruct((n_samp, D_MODEL), F32)

    def state_spec(l):
        return pl.BlockSpec((None, sb, HEADS, DH, DH), lambda i: (l, i, 0, 0, 0))

    in_specs = [
        row_spec, _resident_layer((1, D_MODEL), layer), _resident_layer((D_MODEL, 2 * D_FF), 0),
        _resident_layer((D_FF, D_MODEL), 0), _resident((1, D_MODEL)),
        pl.BlockSpec((sb * L, C_GA), lambda i: (i, 0)),
        state_spec(layer),
        pl.BlockSpec((None, sb, CONV_W - 1, D_MODEL), lambda i: (layer, i, 0, 0)),
        _resident_layer((CONV_W, D_MODEL), layer),
    ] + _short_table_specs(L)
    args = [x, g, wgu, wd, fg, proj, state, cconv, cw, *rope, dmask, qdec, kdec, gl]
    for l in range(n_prev):
        in_specs += [state_spec(l), srow_spec, srow_spec]
        args += [state, kds[l], vs[l]]
    out_specs = [row_spec, srow_spec, srow_spec, pl.BlockSpec((sb, CONV_W - 1, D_MODEL), lambda i: (i, 0, 0))]
    out_shape = [jax.ShapeDtypeStruct(x.shape, F32), srow_shape, srow_shape,
                 jax.ShapeDtypeStruct((ns, CONV_W - 1, D_MODEL), F32)]
    if final:
        assert n_prev == depth - 1 and len(main_states) == depth
        out_specs.append(pl.BlockSpec((depth, sb, HEADS, DH, DH), lambda i: (0, i, 0, 0, 0)))
        out_shape.append(jax.ShapeDtypeStruct(state.shape, F32))
        n_mat = main_states[0].shape[0] * HEADS
        mb = n_mat // steps
        assert mb * steps == n_mat
        in_specs += [pl.BlockSpec((mb, DH, DH), lambda i: (i, 0, 0))] * depth
        args += [s.reshape(n_mat, DH, DH) for s in main_states]
        out_specs.append(pl.BlockSpec((depth, mb, DH, DH), lambda i: (0, i, 0, 0)))
        out_shape.append(jax.ShapeDtypeStruct((depth, n_mat, DH, DH), F32))
    else:
        out_specs += [srow_spec, srow_spec]
        out_shape += [srow_shape, srow_shape]
    return pl.pallas_call(
        functools.partial(_ffn_main_kernel, final=final, n_prev=n_prev, L=L),
        grid=(steps,),
        in_specs=in_specs,
        out_specs=out_specs,
        out_shape=out_shape,
        compiler_params=_params(("arbitrary",)),
        name="ffn_main",
    )(*args)


def kernel(x_prompt, x_sample, state_ret, state_conv, meta_tokens, norm_mix_g, w_in, conv_w, w_ret_o,
           w_conv_o, w_o, norm_ffn_g, w_gate_up, w_down, final_norm_g):
    depth = w_in.shape[0]
    nb, seq, _ = x_prompt.shape
    ns, ls, _ = x_sample.shape
    n_samp = ns * ls
    n_short = n_samp + N_META
    short_tm = n_short // 5
    assert short_tm * 5 == n_short and short_tm % 8 == 0 and n_samp % N_META == 0

    gm = norm_mix_g.reshape(depth, 1, D_MODEL)
    gf = norm_ffn_g.reshape(depth, 1, D_MODEL)
    fg = final_norm_g.reshape(1, D_MODEL)

    rope_meta = _rope_tables(np.arange(N_META))
    rope_main = _rope_tables(N_META + np.arange(seq))
    rope_samp = _rope_tables(PAST_LEN + np.arange(ls))
    dec_meta, dec_main, dec_samp = _decay_tables(N_META), _decay_tables(CHUNK), _decay_tables(ls)

    h_main = x_prompt
    h_short = jnp.concatenate([x_sample.reshape(n_samp, D_MODEL), meta_tokens.astype(F32)], axis=0)
    zero_s = jnp.zeros((1, HEADS, DH, DH), F32)
    zero_c = jnp.zeros((1, CONV_W - 1, D_MODEL), F32)

    s_p, c_p, c_s, kds, vs = [], [], [], [], []
    y_samp = s_s = s_p_stacked = None
    for l in range(depth):
        last = l == depth - 1
        proj, win_b, wro_b, wco_b, wo_b = _proj(h_short, l, gm, w_in, (w_ret_o, w_conv_o, w_o))
        gated_m, bgy_m, c_m, s_m = _ret_meta(proj, n_samp, zero_s, zero_c, l, conv_w, rope_meta, dec_meta, 1, N_META)

        h_main, s_l, c_l, wgu_b, wd_b = _mixer_main(h_main, s_m, c_m, l, gm, win_b, conv_w, wro_b, wco_b, wo_b,
                                                    rope_main, dec_main, w_gate_up, w_down)
        s_p.append(s_l)
        c_p.append(c_l)
        res = _ffn_main(h_main.reshape(nb * seq, D_MODEL), l, gf, wgu_b, wd_b, fg, last, MAIN_FFN_TM,
                        proj, state_ret, state_conv, conv_w, rope_samp, dec_samp, ls, kds, vs, s_p)
        h_main, gated_s, bgy_s, c_l = res[:4]
        h_main = h_main.reshape(nb, seq, D_MODEL)
        c_s.append(c_l)
        if last:
            s_s = res[4]
            s_p_stacked = res[5].reshape(depth, nb, HEADS, DH, DH)
        else:
            kds.append(res[4])
            vs.append(res[5])

        if last:
            y_samp = _out_ffn(h_short, gated_s, bgy_s, proj, l, wro_b, wco_b, wo_b, gf, wgu_b, wd_b, fg, True,
                              n_samp, 256)
        else:
            gated = jnp.concatenate([gated_s, gated_m], axis=0)
            bgy = jnp.concatenate([bgy_s, bgy_m], axis=0)
            h_short = _out_ffn(h_short, gated, bgy, proj, l, wro_b, wco_b, wo_b, gf, wgu_b, wd_b, fg, False,
                               n_short, short_tm)

    return (h_main, y_samp.reshape(ns, ls, D_MODEL), s_p_stacked, jnp.stack(c_p), s_s, jnp.stack(c_s))
```

```python
import functools

import numpy as np
import jax
import jax.numpy as jnp
from jax import lax
from jax.experimental import pallas as pl
from jax.experimental.pallas import tpu as pltpu

D_MODEL = 1024
N_META = 16
HEADS = 8
DH = D_MODEL // HEADS
CHUNK = 128
ROPE_BASE = 10000.0
CONV_W = 3
D_FF = ((8 * D_MODEL + 3 * 256 - 1) // (3 * 256)) * 256
EPS = 1e-6
PAST_LEN = 16384
N_IN = 9 * D_MODEL
C_Q, C_K, C_V, C_G, C_BG, C_CG, C_HC, C_GA, C_GB = (i * D_MODEL for i in range(9))

F32 = jnp.float32
BF16 = jnp.bfloat16

VMEM_LIMIT_BYTES = 58 * 1024 * 1024

MAIN_BG = 4
MAIN_FFN_TM = 512
FFN_PARTS = 2
assert (2 * D_FF // FFN_PARTS) % 256 == 0 and FFN_PARTS % 2 == 0


def _resident(shape):
    nd = len(shape)
    return pl.BlockSpec(shape, lambda *_: (0,) * nd, pipeline_mode=pl.Buffered(1))


def _resident_layer(shape, layer):
    nd = len(shape)
    return pl.BlockSpec((None,) + tuple(shape), lambda *_: (layer,) + (0,) * nd, pipeline_mode=pl.Buffered(1))


def _resident_col_parts(rows, width, n_parts):
    return [pl.BlockSpec((None, rows, width), functools.partial(lambda *_, q: (0, 0, q), q=q),
                         pipeline_mode=pl.Buffered(1)) for q in range(n_parts)]


def _params(sem):
    return pltpu.CompilerParams(dimension_semantics=sem, vmem_limit_bytes=VMEM_LIMIT_BYTES)


def _dot(a, b):
    return jnp.dot(a, b, preferred_element_type=F32)


def _dot_nt(a, b):
    return lax.dot_general(a, b, (((1,), (1,)), ((), ())), preferred_element_type=F32)


def _dot_tn(a, b):
    return lax.dot_general(a, b, (((0,), (0,)), ((), ())), preferred_element_type=F32)


def _rms_f32(x, g):
    return x * lax.rsqrt(jnp.mean(x * x, axis=-1, keepdims=True) + EPS) * g


def _sigmoid(x):
    return 1.0 / (1.0 + jnp.exp(-x))


def _silu(x):
    return x * _sigmoid(x)


def _rotary(t, cos, sin):
    return t * cos + pltpu.roll(t, DH // 2, 1) * sin


def _group_norm(o):
    mu = jnp.mean(o, axis=-1, keepdims=True)
    d = o - mu
    var = jnp.mean(d * d, axis=-1, keepdims=True)
    return d * lax.rsqrt(var + EPS)


def _retention_head_paired(q, kt, kdt, v, s, dmask, qdec, gl):
    L = q.shape[0]
    sc = _dot(q, jnp.concatenate([kt, s.astype(BF16)], axis=1))
    scores = (sc[:, :L] * dmask).astype(BF16)
    iu = _dot(jnp.concatenate([scores, kdt], axis=0), v)
    return iu[:L] + sc[:, L:] * qdec, gl * s + iu[L:]


def _short_conv(u, tail, cw, rows):
    r1 = pltpu.roll(u, 1, 0)
    r2 = pltpu.roll(u, 2, 0)
    t0, t1 = tail[0:1, :], tail[1:2, :]
    sh1 = jnp.where(rows == 0, t1, r1)
    sh2 = jnp.where(rows == 0, t0, jnp.where(rows == 1, t1, r2))
    return cw[0:1, :] * sh2 + cw[1:2, :] * sh1 + cw[2:3, :] * u


def _out_proj(x, gated, bgy, ga, gb, wro_ref, wco_ref, wo_ref):
    ret_out = _dot(gated.astype(BF16), wro_ref[...])
    conv_out = _dot(bgy.astype(BF16), wco_ref[...])
    merged = _sigmoid(ga) * ret_out + _sigmoid(gb) * conv_out
    return x + _dot(merged.astype(BF16), wo_ref[...])


def _swiglu(gates, ups):
    return jnp.concatenate([_silu(a) * b for a, b in zip(gates, ups)], axis=1).astype(BF16)


def _ffn_block(x, g_ref, wgu_refs, wd_ref):
    hn = _rms_f32(x, g_ref[...]).astype(BF16)
    n = len(wgu_refs) // 2
    gates = [_dot(hn, w[...]) for w in wgu_refs[:n]]
    ups = [_dot(hn, w[...]) for w in wgu_refs[n:]]
    return x + _dot(_swiglu(gates, ups), wd_ref[...])


def _log_gamma():
    return np.log1p(-np.exp2(-5.0 - np.arange(HEADS, dtype=np.float64)))


def _const(t):
    return jnp.asarray(np.asarray(t, dtype=np.float32))


def _rope_tables(pos):
    half = DH // 2
    inv = np.power(ROPE_BASE, -np.arange(half, dtype=np.float64) / half)
    ang = np.asarray(pos, dtype=np.float64)[:, None] * inv[None, :]
    cos, sin = np.cos(ang), np.sin(ang)
    cosf = np.concatenate([cos, cos], axis=-1)
    sinf = np.concatenate([-sin, sin], axis=-1)
    scale = DH ** -0.5
    return tuple(_const(t) for t in (cosf, sinf, cosf * scale, sinf * scale))


def _decay_tables(L):
    log_g = _log_gamma()
    idx = np.arange(L, dtype=np.float64)
    diff = idx[:, None] - idx[None, :]
    dmask = np.where(diff >= 0, np.exp(log_g[:, None, None] * np.maximum(diff, 0.0)[None]), 0.0)
    qdec = np.exp(log_g[:, None] * (idx + 1.0)[None])
    kdec = np.exp(log_g[:, None] * (L - 1.0 - idx)[None])
    gl = np.exp(log_g * L)
    lanes = lambda t: np.repeat(t.T, DH, axis=1)
    return tuple(_const(t) for t in (dmask, lanes(qdec), lanes(kdec), np.repeat(gl, DH)[None, :], kdec[:, None, :]))


def _mixer_main_kernel(x_ref, s0_ref, c0_ref, g_ref, *refs):
    n_groups = N_IN // D_MODEL
    wq_ref, wk_ref, wv_ref, wg_ref, wbg_ref, wcg_ref, whc_ref, wga_ref, wgb_ref = refs[:n_groups]
    (cw_ref, wro_ref, wco_ref, wo_ref, cq_ref, sq_ref, ck_ref, sk_ref, dmask_ref, qdec_ref, kdec_ref, gl_ref,
     wgu32_ref, wd32_ref, h_ref, s_ref, c_ref, wgub_ref, wdb_ref, q_s, kt_s, kdt_s, v_s, o_s) = refs[n_groups:]
    bg = x_ref.shape[0]
    m = bg * CHUNK

    wgub_ref[...] = wgu32_ref[...].astype(BF16)
    step = pl.program_id(0) * pl.num_programs(1) + pl.program_id(1)

    @pl.when(step < D_FF // wd32_ref.shape[0])
    def _():
        wdb_ref[...] = wd32_ref[...].astype(BF16)

    @pl.when(pl.program_id(1) == 0)
    def _():
        for b in range(bg):
            s_ref[b] = s0_ref[...]
            c_ref[b] = c0_ref[...]

    x = x_ref[...].reshape(m, D_MODEL)
    hn = _rms_f32(x, g_ref[...]).astype(BF16)

    cq, sq, ck, sk = cq_ref[...], sq_ref[...], ck_ref[...], sk_ref[...]
    q = _dot(hn, wq_ref[...])
    k = _dot(hn, wk_ref[...])
    v_s[...] = _dot(hn, wv_ref[...]).astype(BF16)
    for b in range(bg):
        r = slice(b * CHUNK, (b + 1) * CHUNK)
        for h in range(HEADS):
            cl = slice(h * DH, (h + 1) * DH)
            q_s[r, cl] = _rotary(q[r, cl], cq, sq).astype(BF16)
            krt = _rotary(k[r, cl], ck, sk).T
            kt_s[b * HEADS + h] = krt.astype(BF16)
            kdt_s[b * HEADS + h] = (krt * kdec_ref[h]).astype(BF16)

    for b in range(bg):
        r = slice(b * CHUNK, (b + 1) * CHUNK)
        for h in range(HEADS):
            cl = slice(h * DH, (h + 1) * DH)
            o, s_new = _retention_head_paired(q_s[r, cl], kt_s[b * HEADS + h], kdt_s[b * HEADS + h], v_s[r, cl],
                                              s_ref[b, h], dmask_ref[h], qdec_ref[:, cl], gl_ref[:, cl])
            s_ref[b, h] = s_new
            o_s[r, cl] = _group_norm(o)

    g = _dot(hn, wg_ref[...])
    gated = _silu(g) * o_s[...]

    bgate = _dot(hn, wbg_ref[...])
    u = _dot(hn, wcg_ref[...]) * _dot(hn, whc_ref[...])
    rows = lax.broadcasted_iota(jnp.int32, (CHUNK, D_MODEL), 0)
    cw = cw_ref[...]
    ys = []
    for b in range(bg):
        ub = u[b * CHUNK:(b + 1) * CHUNK]
        ys.append(_short_conv(ub, c_ref[b], cw, rows))
        c_ref[b] = ub[CHUNK - (CONV_W - 1):, :]
    bgy = bgate * jnp.concatenate(ys, axis=0)

    ga = _dot(hn, wga_ref[...])
    gb = _dot(hn, wgb_ref[...])
    out = _out_proj(x, gated, bgy, ga, gb, wro_ref, wco_ref, wo_ref)
    h_ref[...] = out.reshape(bg, CHUNK, D_MODEL)


def _mixer_main(x, s0, c0, layer, g, win, cw, wro, wco, wo, rope, decay, wgu32, wd32):
    nb, seq, _ = x.shape
    bg = MAIN_BG
    cq, sq, ck, sk = rope
    dmask, qdec, _, gl, kdec = decay
    m = bg * CHUNK
    n_chunks = seq // CHUNK
    steps = (nb // bg) * n_chunks
    slab = D_MODEL // steps
    wd_slab = 16 * (-(-D_FF // (16 * steps)))
    while D_FF % wd_slab:
        wd_slab += 16
    wd_last = D_FF // wd_slab - 1
    assert slab * steps == D_MODEL and slab % 16 == 0 and wd_last < steps

    def wd_block(i, c):
        return jnp.minimum(i * n_chunks + c, wd_last)

    rope_spec = pl.BlockSpec((CHUNK, DH), lambda i, c: (c, 0))
    return pl.pallas_call(
        _mixer_main_kernel,
        grid=(nb // bg, n_chunks),
        in_specs=[
            pl.BlockSpec((bg, CHUNK, D_MODEL), lambda i, c: (i, c, 0)),
            _resident_layer((HEADS, DH, DH), 0),
            _resident_layer((CONV_W - 1, D_MODEL), 0),
            _resident_layer((1, D_MODEL), layer),
            *_resident_col_parts(D_MODEL, D_MODEL, N_IN // D_MODEL),
            _resident_layer((CONV_W, D_MODEL), layer),
            _resident_layer((D_MODEL, D_MODEL), 0),
            _resident_layer((D_MODEL, D_MODEL), 0),
            _resident_layer((D_MODEL, D_MODEL), 0),
            rope_spec, rope_spec, rope_spec, rope_spec,
            _resident((HEADS, CHUNK, CHUNK)),
            _resident((CHUNK, D_MODEL)),
            _resident((HEADS, 1, CHUNK)),
            _resident((1, D_MODEL)),
            pl.BlockSpec((None, slab, 2 * D_FF), lambda i, c: (layer, i * n_chunks + c, 0)),
            pl.BlockSpec((None, wd_slab, D_MODEL), lambda i, c: (layer, wd_block(i, c), 0)),
        ],
        out_specs=[
            pl.BlockSpec((bg, CHUNK, D_MODEL), lambda i, c: (i, c, 0)),
            pl.BlockSpec((bg, HEADS, DH, DH), lambda i, c: (i, 0, 0, 0)),
            pl.BlockSpec((bg, CONV_W - 1, D_MODEL), lambda i, c: (i, 0, 0)),
            pl.BlockSpec((None, slab, 2 * D_FF), lambda i, c: (0, i * n_chunks + c, 0)),
            pl.BlockSpec((None, wd_slab, D_MODEL), lambda i, c: (0, wd_block(i, c), 0)),
        ],
        out_shape=[
            jax.ShapeDtypeStruct(x.shape, F32),
            jax.ShapeDtypeStruct((nb, HEADS, DH, DH), F32),
            jax.ShapeDtypeStruct((nb, CONV_W - 1, D_MODEL), F32),
            jax.ShapeDtypeStruct((1, D_MODEL, 2 * D_FF), BF16),
            jax.ShapeDtypeStruct((1, D_FF, D_MODEL), BF16),
        ],
        scratch_shapes=[
            pltpu.VMEM((m, D_MODEL), BF16),
            pltpu.VMEM((bg * HEADS, DH, CHUNK), BF16),
            pltpu.VMEM((bg * HEADS, DH, CHUNK), BF16),
            pltpu.VMEM((m, D_MODEL), BF16),
            pltpu.VMEM((m, D_MODEL), F32),
        ],
        compiler_params=_params(("arbitrary", "arbitrary")),
        name="mixer_main",
    )(x, s0, c0, g, *([win] * (N_IN // D_MODEL)), cw, wro, wco, wo, cq, sq, ck, sk, dmask, qdec, kdec, gl,
      wgu32, wd32)


def _proj_kernel(x_ref, g_ref, *refs, n_slabs, n_parts):
    w32_refs = refs[:n_parts]
    sq0_ref, sq1_ref, sq2_ref, o_ref, wb_ref, sb0_ref, sb1_ref, sb2_ref, hn_s = refs[n_parts:]
    j = pl.program_id(0)

    @pl.when(j == 0)
    def _():
        hn_s[...] = _rms_f32(x_ref[...], g_ref[...]).astype(BF16)

    @pl.when(j < n_slabs)
    def _():
        for src, dst in ((sq0_ref, sb0_ref), (sq1_ref, sb1_ref), (sq2_ref, sb2_ref)):
            dst[...] = src[...].astype(BF16)

    wb = jnp.concatenate([w[...].astype(BF16) for w in w32_refs], axis=0)
    wb_ref[...] = wb
    o_ref[...] = _dot(hn_s[...], wb)


def _proj(x, layer, g, win32, squares32):
    rows = x.shape[0]
    n_col = N_IN // D_MODEL
    slab = 128
    n_slabs = D_MODEL // slab
    assert n_slabs <= n_col
    slab_in = pl.BlockSpec((None, slab, D_MODEL), lambda j: (layer, jnp.minimum(j, n_slabs - 1), 0))
    slab_out = pl.BlockSpec((None, slab, D_MODEL), lambda j: (0, jnp.minimum(j, n_slabs - 1), 0))
    square_shape = jax.ShapeDtypeStruct((1, D_MODEL, D_MODEL), BF16)
    n_parts = 4
    part_rows = D_MODEL // n_parts
    w_parts = [pl.BlockSpec((None, part_rows, D_MODEL), functools.partial(lambda j, q: (layer, q, j), q=q))
               for q in range(n_parts)]
    return pl.pallas_call(
        functools.partial(_proj_kernel, n_slabs=n_slabs, n_parts=n_parts),
        grid=(n_col,),
        in_specs=[_resident((rows, D_MODEL)), _resident_layer((1, D_MODEL), layer)] + w_parts
        + [slab_in, slab_in, slab_in],
        out_specs=[pl.BlockSpec((rows, D_MODEL), lambda j: (0, j)),
                   pl.BlockSpec((None, D_MODEL, D_MODEL), lambda j: (0, 0, j)), slab_out, slab_out, slab_out],
        out_shape=[jax.ShapeDtypeStruct((rows, N_IN), F32), jax.ShapeDtypeStruct((1, D_MODEL, N_IN), BF16),
                   square_shape, square_shape, square_shape],
        scratch_shapes=[pltpu.VMEM((rows, D_MODEL), BF16)],
        compiler_params=_params(("arbitrary",)),
        name="proj_short",
    )(x, g, *([win32] * n_parts), *squares32)


def _ret_short_first(p_ref, s_ref, tab_refs, sb, L, sink):
    cq_ref, sq_ref, ck_ref, sk_ref, dmask_ref, qdec_ref, kdec_ref = tab_refs
    cq, sq, ck, sk = cq_ref[...], sq_ref[...], ck_ref[...], sk_ref[...]
    staged = []
    for b in range(sb):
        r = slice(b * L, (b + 1) * L)
        for h in range(HEADS):
            cl = slice(h * DH, (h + 1) * DH)
            q = _rotary(p_ref[r, C_Q + h * DH:C_Q + (h + 1) * DH], cq, sq).astype(BF16)
            k = _rotary(p_ref[r, C_K + h * DH:C_K + (h + 1) * DH], ck, sk)
            v = p_ref[r, C_V + h * DH:C_V + (h + 1) * DH]
            s = s_ref[b, h]
            scores = (_dot_nt(q, k.astype(BF16)) * dmask_ref[h]).astype(BF16)
            cross = _dot(q, s.astype(BF16)) * qdec_ref[:, cl]
            sink(b, h, r, cl, s, k * kdec_ref[:, cl], v)
            staged.append((scores, cross, v.astype(BF16)))
    return staged


def _ret_short_second(staged, p_ref, c_ref, cw_ref, gated_ref, bgy_ref, cn_ref, sb, L):
    rows = lax.broadcasted_iota(jnp.int32, (L, D_MODEL), 0)
    cw = cw_ref[...]
    for b in range(sb):
        r = slice(b * L, (b + 1) * L)
        for h in range(HEADS):
            cl = slice(h * DH, (h + 1) * DH)
            scores, cross, v = staged[b * HEADS + h]
            o = _dot(scores, v) + cross
            g = p_ref[r, C_G + h * DH:C_G + (h + 1) * DH]
            gated_ref[r, cl] = _silu(g) * _group_norm(o)
        u = p_ref[r, C_CG:C_CG + D_MODEL] * p_ref[r, C_HC:C_HC + D_MODEL]
        y = _short_conv(u, c_ref[b], cw, rows)
        bgy_ref[r, :] = p_ref[r, C_BG:C_BG + D_MODEL] * y
        cn_ref[b] = u[L - (CONV_W - 1):, :]


def _state_sink(sn_ref, gl_ref, layer=None):
    def sink(b, h, r, cl, s, kd, v):
        s_new = gl_ref[:, cl] * s + _dot_tn(kd.astype(BF16), v.astype(BF16))
        if layer is None:
            sn_ref[b, h] = s_new
        else:
            sn_ref[layer, b, h] = s_new
    return sink


def _ret_meta_kernel(p_ref, s_ref, c_ref, cw_ref, cq_ref, sq_ref, ck_ref, sk_ref, dmask_ref, qdec_ref, kdec_ref,
                     gl_ref, gated_ref, bgy_ref, cn_ref, sn_ref, *, L):
    sb = c_ref.shape[0]
    tabs = (cq_ref, sq_ref, ck_ref, sk_ref, dmask_ref, qdec_ref, kdec_ref)
    staged = _ret_short_first(p_ref, s_ref, tabs, sb, L, _state_sink(sn_ref, gl_ref))
    _ret_short_second(staged, p_ref, c_ref, cw_ref, gated_ref, bgy_ref, cn_ref, sb, L)


def _short_table_specs(L):
    return [_resident((L, DH))] * 4 + [_resident((HEADS, L, L)), _resident((L, D_MODEL)), _resident((L, D_MODEL)),
                                       _resident((1, D_MODEL))]


def _ret_meta(proj, row0, state, cprev, layer, cw, rope, decay, nseq, L):
    dmask, qdec, kdec, gl, _ = decay
    rows = nseq * L
    row_spec = pl.BlockSpec((rows, D_MODEL), lambda i: (0, 0))
    row_shape = jax.ShapeDtypeStruct((rows, D_MODEL), F32)
    return pl.pallas_call(
        functools.partial(_ret_meta_kernel, L=L),
        grid=(1,),
        in_specs=[
            pl.BlockSpec((rows, C_GA), lambda i: (row0 // rows, 0)),
            _resident((nseq, HEADS, DH, DH)),
            _resident((nseq, CONV_W - 1, D_MODEL)),
            _resident_layer((CONV_W, D_MODEL), layer),
        ] + _short_table_specs(L),
        out_specs=[row_spec, row_spec, pl.BlockSpec((nseq, CONV_W - 1, D_MODEL), lambda i: (0, 0, 0)),
                   pl.BlockSpec((nseq, HEADS, DH, DH), lambda i: (0, 0, 0, 0))],
        out_shape=[row_shape, row_shape, jax.ShapeDtypeStruct((nseq, CONV_W - 1, D_MODEL), F32),
                   jax.ShapeDtypeStruct((nseq, HEADS, DH, DH), F32)],
        compiler_params=_params(("arbitrary",)),
        name="ret_meta",
    )(proj, state, cprev, cw, *rope, dmask, qdec, kdec, gl)


def _out_ffn_kernel(x_ref, gated_ref, bgy_ref, ga_ref, gb_ref, wro_ref, wco_ref, wo_ref, g_ref, *refs, final):
    wgu_refs = refs[:FFN_PARTS]
    wd_ref, fg_ref, o_ref = refs[FFN_PARTS:]
    h = _out_proj(x_ref[...], gated_ref[...], bgy_ref[...], ga_ref[...], gb_ref[...], wro_ref, wco_ref, wo_ref)
    y = _ffn_block(h, g_ref, wgu_refs, wd_ref)
    if final:
        y = _rms_f32(y, fg_ref[...])
    o_ref[...] = y


def _out_ffn(x, gated, bgy, proj, layer, wro, wco, wo, g, wgu, wd, fg, final, rows, tm):
    row_spec = pl.BlockSpec((tm, D_MODEL), lambda i: (i, 0))
    w_spec = _resident_layer((D_MODEL, D_MODEL), 0)
    return pl.pallas_call(
        functools.partial(_out_ffn_kernel, final=final),
        grid=(rows // tm,),
        in_specs=[row_spec, row_spec, row_spec,
                  pl.BlockSpec((tm, D_MODEL), lambda i: (i, C_GA // D_MODEL)),
                  pl.BlockSpec((tm, D_MODEL), lambda i: (i, C_GB // D_MODEL)),
                  w_spec, w_spec, w_spec,
                  _resident_layer((1, D_MODEL), layer),
                  *_resident_col_parts(D_MODEL, 2 * D_FF // FFN_PARTS, FFN_PARTS),
                  _resident_layer((D_FF, D_MODEL), 0), _resident((1, D_MODEL))],
        out_specs=row_spec,
        out_shape=jax.ShapeDtypeStruct((rows, D_MODEL), F32),
        compiler_params=_params(("arbitrary",)),
        name="out_ffn_short",
    )(x, gated, bgy, proj, proj, wro, wco, wo, g, *([wgu] * FFN_PARTS), wd, fg)


def _ffn_main_kernel(x_ref, g_ref, *refs, final, n_prev, L):
    wgu_refs = refs[:FFN_PARTS]
    (wd_ref, fg_ref, p_ref, s_ref, c_ref, cw_ref, cq_ref, sq_ref, ck_ref, sk_ref, dmask_ref, qdec_ref, kdec_ref,
     gl_ref) = refs[FFN_PARTS:FFN_PARTS + 14]
    rest = refs[FFN_PARTS + 14:]
    sb = c_ref.shape[0]
    n_main = n_prev + 1 if final else 0
    prev, mains, outs = rest[:3 * n_prev], rest[3 * n_prev:3 * n_prev + n_main], rest[3 * n_prev + n_main:]
    y_ref, gated_ref, bgy_ref, cn_ref = outs[:4]
    tabs = (cq_ref, sq_ref, ck_ref, sk_ref, dmask_ref, qdec_ref, kdec_ref)

    if final:
        sn_ref, mstack_ref = outs[4:6]
        for l in range(n_main):
            mstack_ref[l] = mains[l][...]
        sink = _state_sink(sn_ref, gl_ref, n_prev)
    else:
        kd_ref, v_ref = outs[4:6]

        def sink(b, h, r, cl, s, kd, v):
            kd_ref[r, cl] = kd
            v_ref[r, cl] = v

    x = x_ref[...]
    hn = _rms_f32(x, g_ref[...]).astype(BF16)
    staged = _ret_short_first(p_ref, s_ref, tabs, sb, L, sink)
    gates = [_dot(hn, w[...]) for w in wgu_refs[:FFN_PARTS // 2]]
    if final:
        for l in range(n_prev):
            so_ref, kdo_ref, vo_ref = prev[3 * l:3 * l + 3]
            upd = _state_sink(sn_ref, gl_ref, l)
            for b in range(sb):
                r = slice(b * L, (b + 1) * L)
                for h in range(HEADS):
                    cl = slice(h * DH, (h + 1) * DH)
                    upd(b, h, r, cl, so_ref[b, h], kdo_ref[r, cl], vo_ref[r, cl])
    ups = [_dot(hn, w[...]) for w in wgu_refs[FFN_PARTS // 2:]]
    _ret_short_second(staged, p_ref, c_ref, cw_ref, gated_ref, bgy_ref, cn_ref, sb, L)
    y = x + _dot(_swiglu(gates, ups), wd_ref[...])
    if final:
        y = _rms_f32(y, fg_ref[...])
    y_ref[...] = y


def _ffn_main(x, layer, g, wgu, wd, fg, final, tm, proj, state, cconv, cw, rope, decay, L, kds, vs, main_states):
    rows = x.shape[0]
    steps = rows // tm
    depth, ns = state.shape[:2]
    sb = ns // steps
    assert sb * steps == ns and (sb * L) % 8 == 0
    n_prev = len(kds) if final else 0
    dmask, qdec, kdec, gl, _ = decay
    n_samp = ns * L
    row_spec = pl.BlockSpec((tm, D_MODEL), lambda i: (i, 0))
    srow_spec = pl.BlockSpec((sb * L, D_MODEL), lambda i: (i, 0))
    srow_shape = jax.ShapeDtypeStruct((n_samp, D_MODEL), F32)

    def state_spec(l):
        return pl.BlockSpec((None, sb, HEADS, DH, DH), lambda i: (l, i, 0, 0, 0))

    in_specs = [
        row_spec, _resident_layer((1, D_MODEL), layer),
        *_resident_col_parts(D_MODEL, 2 * D_FF // FFN_PARTS, FFN_PARTS),
        _resident_layer((D_FF, D_MODEL), 0), _resident((1, D_MODEL)),
        pl.BlockSpec((sb * L, C_GA), lambda i: (i, 0)),
        state_spec(layer),
        pl.BlockSpec((None, sb, CONV_W - 1, D_MODEL), lambda i: (layer, i, 0, 0)),
        _resident_layer((CONV_W, D_MODEL), layer),
    ] + _short_table_specs(L)
    args = [x, g, *([wgu] * FFN_PARTS), wd, fg, proj, state, cconv, cw, *rope, dmask, qdec, kdec, gl]
    for l in range(n_prev):
        in_specs += [state_spec(l), srow_spec, srow_spec]
        args += [state, kds[l], vs[l]]
    out_specs = [row_spec, srow_spec, srow_spec, pl.BlockSpec((sb, CONV_W - 1, D_MODEL), lambda i: (i, 0, 0))]
    out_shape = [jax.ShapeDtypeStruct(x.shape, F32), srow_shape, srow_shape,
                 jax.ShapeDtypeStruct((ns, CONV_W - 1, D_MODEL), F32)]
    if final:
        assert n_prev == depth - 1 and len(main_states) == depth
        out_specs.append(pl.BlockSpec((depth, sb, HEADS, DH, DH), lambda i: (0, i, 0, 0, 0)))
        out_shape.append(jax.ShapeDtypeStruct(state.shape, F32))
        n_mat = main_states[0].shape[0] * HEADS
        mb = n_mat // steps
        assert mb * steps == n_mat
        in_specs += [pl.BlockSpec((mb, DH, DH), lambda i: (i, 0, 0))] * depth
        args += [s.reshape(n_mat, DH, DH) for s in main_states]
        out_specs.append(pl.BlockSpec((depth, mb, DH, DH), lambda i: (0, i, 0, 0)))
        out_shape.append(jax.ShapeDtypeStruct((depth, n_mat, DH, DH), F32))
    else:
        out_specs += [srow_spec, srow_spec]
        out_shape += [srow_shape, srow_shape]
    return pl.pallas_call(
        functools.partial(_ffn_main_kernel, final=final, n_prev=n_prev, L=L),
        grid=(steps,),
        in_specs=in_specs,
        out_specs=out_specs,
        out_shape=out_shape,
        compiler_params=_params(("arbitrary",)),
        name="ffn_main",
    )(*args)


def kernel(x_prompt, x_sample, state_ret, state_conv, meta_tokens, norm_mix_g, w_in, conv_w, w_ret_o,
           w_conv_o, w_o, norm_ffn_g, w_gate_up, w_down, final_norm_g):
    depth = w_in.shape[0]
    nb, seq, _ = x_prompt.shape
    ns, ls, _ = x_sample.shape
    n_samp = ns * ls
    n_short = n_samp + N_META
    short_tm = n_short // 5
    assert short_tm * 5 == n_short and short_tm % 8 == 0 and n_samp % N_META == 0

    gm = norm_mix_g.reshape(depth, 1, D_MODEL)
    gf = norm_ffn_g.reshape(depth, 1, D_MODEL)
    fg = final_norm_g.reshape(1, D_MODEL)

    rope_meta = _rope_tables(np.arange(N_META))
    rope_main = _rope_tables(N_META + np.arange(seq))
    rope_samp = _rope_tables(PAST_LEN + np.arange(ls))
    dec_meta, dec_main, dec_samp = _decay_tables(N_META), _decay_tables(CHUNK), _decay_tables(ls)

    h_main = x_prompt
    h_short = jnp.concatenate([x_sample.reshape(n_samp, D_MODEL), meta_tokens.astype(F32)], axis=0)
    zero_s = jnp.zeros((1, HEADS, DH, DH), F32)
    zero_c = jnp.zeros((1, CONV_W - 1, D_MODEL), F32)

    s_p, c_p, c_s, kds, vs = [], [], [], [], []
    y_samp = s_s = s_p_stacked = None
    for l in range(depth):
        last = l == depth - 1
        proj, win_b, wro_b, wco_b, wo_b = _proj(h_short, l, gm, w_in, (w_ret_o, w_conv_o, w_o))
        gated_m, bgy_m, c_m, s_m = _ret_meta(proj, n_samp, zero_s, zero_c, l, conv_w, rope_meta, dec_meta, 1, N_META)

        h_main, s_l, c_l, wgu_b, wd_b = _mixer_main(h_main, s_m, c_m, l, gm, win_b, conv_w, wro_b, wco_b, wo_b,
                                                    rope_main, dec_main, w_gate_up, w_down)
        s_p.append(s_l)
        c_p.append(c_l)
        res = _ffn_main(h_main.reshape(nb * seq, D_MODEL), l, gf, wgu_b, wd_b, fg, last, MAIN_FFN_TM,
                        proj, state_ret, state_conv, conv_w, rope_samp, dec_samp, ls, kds, vs, s_p)
        h_main, gated_s, bgy_s, c_l = res[:4]
        h_main = h_main.reshape(nb, seq, D_MODEL)
        c_s.append(c_l)
        if last:
            s_s = res[4]
            s_p_stacked = res[5].reshape(depth, nb, HEADS, DH, DH)
        else:
            kds.append(res[4])
            vs.append(res[5])

        if last:
            y_samp = _out_ffn(h_short, gated_s, bgy_s, proj, l, wro_b, wco_b, wo_b, gf, wgu_b, wd_b, fg, True,
                              n_samp, 256)
        else:
            gated = jnp.concatenate([gated_s, gated_m], axis=0)
            bgy = jnp.concatenate([bgy_s, bgy_m], axis=0)
            h_short = _out_ffn(h_short, gated, bgy, proj, l, wro_b, wco_b, wo_b, gf, wgu_b, wd_b, fg, False,
                               n_short, short_tm)

    return (h_main, y_samp.reshape(ns, ls, D_MODEL), s_p_stacked, jnp.stack(c_p), s_s, jnp.stack(c_s))
```

```python
import functools

import numpy as np
import jax
import jax.numpy as jnp
from jax import lax
from jax.experimental import pallas as pl
from jax.experimental.pallas import tpu as pltpu

D_MODEL = 1024
N_META = 16
HEADS = 8
DH = D_MODEL // HEADS
CHUNK = 128
ROPE_BASE = 10000.0
CONV_W = 3
D_FF = ((8 * D_MODEL + 3 * 256 - 1) // (3 * 256)) * 256
EPS = 1e-6
PAST_LEN = 16384
N_IN = 9 * D_MODEL
C_Q, C_K, C_V, C_G, C_BG, C_CG, C_HC, C_GA, C_GB = (i * D_MODEL for i in range(9))

F32 = jnp.float32
BF16 = jnp.bfloat16

VMEM_LIMIT_BYTES = 58 * 1024 * 1024

MAIN_BG = 4
MAIN_FFN_TM = 512
FFN_PARTS = 2
assert (2 * D_FF // FFN_PARTS) % 256 == 0 and FFN_PARTS % 2 == 0


def _resident(shape):
    nd = len(shape)
    return pl.BlockSpec(shape, lambda *_: (0,) * nd, pipeline_mode=pl.Buffered(1))


def _resident_layer(shape, layer):
    nd = len(shape)
    return pl.BlockSpec((None,) + tuple(shape), lambda *_: (layer,) + (0,) * nd, pipeline_mode=pl.Buffered(1))


def _resident_col_parts(rows, width, n_parts):
    return [pl.BlockSpec((None, rows, width), functools.partial(lambda *_, q: (0, 0, q), q=q),
                         pipeline_mode=pl.Buffered(1)) for q in range(n_parts)]


def _params(sem):
    return pltpu.CompilerParams(dimension_semantics=sem, vmem_limit_bytes=VMEM_LIMIT_BYTES)


def _dot(a, b):
    return jnp.dot(a, b, preferred_element_type=F32)


def _dot_nt(a, b):
    return lax.dot_general(a, b, (((1,), (1,)), ((), ())), preferred_element_type=F32)


def _dot_tn(a, b):
    return lax.dot_general(a, b, (((0,), (0,)), ((), ())), preferred_element_type=F32)


def _rms_f32(x, g):
    return x * lax.rsqrt(jnp.mean(x * x, axis=-1, keepdims=True) + EPS) * g


def _sigmoid(x):
    return 1.0 / (1.0 + jnp.exp(-x))


def _silu(x):
    return x * _sigmoid(x)


def _rotary(t, cos, sin):
    return t * cos + pltpu.roll(t, DH // 2, 1) * sin


def _group_norm(o):
    mu = jnp.mean(o, axis=-1, keepdims=True)
    d = o - mu
    var = jnp.mean(d * d, axis=-1, keepdims=True)
    return d * lax.rsqrt(var + EPS)


def _retention_head_paired(q, kt, kdt, v, s, dmask, qdec, gl):
    L = q.shape[0]
    sc = _dot(q, jnp.concatenate([kt, s.astype(BF16)], axis=1))
    scores = (sc[:, :L] * dmask).astype(BF16)
    iu = _dot(jnp.concatenate([scores, kdt], axis=0), v)
    return iu[:L] + sc[:, L:] * qdec, gl * s + iu[L:]


def _short_conv(u, tail, cw, rows):
    r1 = pltpu.roll(u, 1, 0)
    r2 = pltpu.roll(u, 2, 0)
    t0, t1 = tail[0:1, :], tail[1:2, :]
    sh1 = jnp.where(rows == 0, t1, r1)
    sh2 = jnp.where(rows == 0, t0, jnp.where(rows == 1, t1, r2))
    return cw[0:1, :] * sh2 + cw[1:2, :] * sh1 + cw[2:3, :] * u


def _out_proj(x, gated, bgy, ga, gb, wro_ref, wco_ref, wo_ref):
    ret_out = _dot(gated.astype(BF16), wro_ref[...])
    conv_out = _dot(bgy.astype(BF16), wco_ref[...])
    merged = _sigmoid(ga) * ret_out + _sigmoid(gb) * conv_out
    return x + _dot(merged.astype(BF16), wo_ref[...])


def _swiglu(gates, ups):
    return jnp.concatenate([_silu(a) * b for a, b in zip(gates, ups)], axis=1).astype(BF16)


def _ffn_block(x, g_ref, wgu_refs, wd_ref):
    hn = _rms_f32(x, g_ref[...]).astype(BF16)
    n = len(wgu_refs) // 2
    gates = [_dot(hn, w[...]) for w in wgu_refs[:n]]
    ups = [_dot(hn, w[...]) for w in wgu_refs[n:]]
    return x + _dot(_swiglu(gates, ups), wd_ref[...])


def _log_gamma():
    return np.log1p(-np.exp2(-5.0 - np.arange(HEADS, dtype=np.float64)))


def _const(t):
    return jnp.asarray(np.asarray(t, dtype=np.float32))


def _rope_tables(pos):
    half = DH // 2
    inv = np.power(ROPE_BASE, -np.arange(half, dtype=np.float64) / half)
    ang = np.asarray(pos, dtype=np.float64)[:, None] * inv[None, :]
    cos, sin = np.cos(ang), np.sin(ang)
    cosf = np.concatenate([cos, cos], axis=-1)
    sinf = np.concatenate([-sin, sin], axis=-1)
    scale = DH ** -0.5
    return tuple(_const(t) for t in (cosf, sinf, cosf * scale, sinf * scale))


def _decay_tables(L):
    log_g = _log_gamma()
    idx = np.arange(L, dtype=np.float64)
    diff = idx[:, None] - idx[None, :]
    dmask = np.where(diff >= 0, np.exp(log_g[:, None, None] * np.maximum(diff, 0.0)[None]), 0.0)
    qdec = np.exp(log_g[:, None] * (idx + 1.0)[None])
    kdec = np.exp(log_g[:, None] * (L - 1.0 - idx)[None])
    gl = np.exp(log_g * L)
    lanes = lambda t: np.repeat(t.T, DH, axis=1)
    return tuple(_const(t) for t in (dmask, lanes(qdec), lanes(kdec), np.repeat(gl, DH)[None, :], kdec[:, None, :]))


def _mixer_main_kernel(x_ref, s0_ref, c0_ref, g_ref, *refs):
    n_groups = N_IN // D_MODEL
    wq_ref, wk_ref, wv_ref, wg_ref, wbg_ref, wcg_ref, whc_ref, wga_ref, wgb_ref = refs[:n_groups]
    (cw_ref, wro_ref, wco_ref, wo_ref, cq_ref, sq_ref, ck_ref, sk_ref, dmask_ref, qdec_ref, kdec_ref, gl_ref,
     wgu32_ref, wd32_ref, h_ref, s_ref, c_ref, wgub_ref, wdb_ref, q_s, kt_s, kdt_s, v_s, o_s) = refs[n_groups:]
    bg = x_ref.shape[0]
    m = bg * CHUNK

    wgub_ref[...] = wgu32_ref[...].astype(BF16)
    step = pl.program_id(0) * pl.num_programs(1) + pl.program_id(1)

    @pl.when(step < D_FF // wd32_ref.shape[0])
    def _():
        wdb_ref[...] = wd32_ref[...].astype(BF16)

    @pl.when(pl.program_id(1) == 0)
    def _():
        for b in range(bg):
            s_ref[b] = s0_ref[...]
            c_ref[b] = c0_ref[...]

    x = x_ref[...].reshape(m, D_MODEL)
    hn = _rms_f32(x, g_ref[...]).astype(BF16)

    cq, sq, ck, sk = cq_ref[...], sq_ref[...], ck_ref[...], sk_ref[...]
    q = _dot(hn, wq_ref[...])
    k = _dot(hn, wk_ref[...])
    v_s[...] = _dot(hn, wv_ref[...]).astype(BF16)
    for b in range(bg):
        r = slice(b * CHUNK, (b + 1) * CHUNK)
        for h in range(HEADS):
            cl = slice(h * DH, (h + 1) * DH)
            q_s[r, cl] = _rotary(q[r, cl], cq, sq).astype(BF16)
            krt = _rotary(k[r, cl], ck, sk).T
            kt_s[b * HEADS + h] = krt.astype(BF16)
            kdt_s[b * HEADS + h] = (krt * kdec_ref[h]).astype(BF16)

    for b in range(bg):
        r = slice(b * CHUNK, (b + 1) * CHUNK)
        for h in range(HEADS):
            cl = slice(h * DH, (h + 1) * DH)
            o, s_new = _retention_head_paired(q_s[r, cl], kt_s[b * HEADS + h], kdt_s[b * HEADS + h], v_s[r, cl],
                                              s_ref[b, h], dmask_ref[h], qdec_ref[:, cl], gl_ref[:, cl])
            s_ref[b, h] = s_new
            o_s[r, cl] = _group_norm(o)

    g = _dot(hn, wg_ref[...])
    gated = _silu(g) * o_s[...]

    bgate = _dot(hn, wbg_ref[...])
    u = _dot(hn, wcg_ref[...]) * _dot(hn, whc_ref[...])
    rows = lax.broadcasted_iota(jnp.int32, (CHUNK, D_MODEL), 0)
    cw = cw_ref[...]
    ys = []
    for b in range(bg):
        ub = u[b * CHUNK:(b + 1) * CHUNK]
        ys.append(_short_conv(ub, c_ref[b], cw, rows))
        c_ref[b] = ub[CHUNK - (CONV_W - 1):, :]
    bgy = bgate * jnp.concatenate(ys, axis=0)

    ga = _dot(hn, wga_ref[...])
    gb = _dot(hn, wgb_ref[...])
    out = _out_proj(x, gated, bgy, ga, gb, wro_ref, wco_ref, wo_ref)
    h_ref[...] = out.reshape(bg, CHUNK, D_MODEL)


def _mixer_main(x, s0, c0, layer, g, win, cw, wro, wco, wo, rope, decay, wgu32, wd32):
    nb, seq, _ = x.shape
    bg = MAIN_BG
    cq, sq, ck, sk = rope
    dmask, qdec, _, gl, kdec = decay
    m = bg * CHUNK
    n_chunks = seq // CHUNK
    steps = (nb // bg) * n_chunks
    slab = D_MODEL // steps
    wd_slab = 16 * (-(-D_FF // (16 * steps)))
    while D_FF % wd_slab:
        wd_slab += 16
    wd_last = D_FF // wd_slab - 1
    assert slab * steps == D_MODEL and slab % 16 == 0 and wd_last < steps

    def wd_block(i, c):
        return jnp.minimum(i * n_chunks + c, wd_last)

    rope_spec = pl.BlockSpec((CHUNK, DH), lambda i, c: (c, 0))
    return pl.pallas_call(
        _mixer_main_kernel,
        grid=(nb // bg, n_chunks),
        in_specs=[
            pl.BlockSpec((bg, CHUNK, D_MODEL), lambda i, c: (i, c, 0)),
            _resident_layer((HEADS, DH, DH), 0),
            _resident_layer((CONV_W - 1, D_MODEL), 0),
            _resident_layer((1, D_MODEL), layer),
            *_resident_col_parts(D_MODEL, D_MODEL, N_IN // D_MODEL),
            _resident_layer((CONV_W, D_MODEL), layer),
            _resident_layer((D_MODEL, D_MODEL), 0),
            _resident_layer((D_MODEL, D_MODEL), 0),
            _resident_layer((D_MODEL, D_MODEL), 0),
            rope_spec, rope_spec, rope_spec, rope_spec,
            _resident((HEADS, CHUNK, CHUNK)),
            _resident((CHUNK, D_MODEL)),
            _resident((HEADS, 1, CHUNK)),
            _resident((1, D_MODEL)),
            pl.BlockSpec((None, slab, 2 * D_FF), lambda i, c: (layer, i * n_chunks + c, 0)),
            pl.BlockSpec((None, wd_slab, D_MODEL), lambda i, c: (layer, wd_block(i, c), 0)),
        ],
        out_specs=[
            pl.BlockSpec((bg, CHUNK, D_MODEL), lambda i, c: (i, c, 0)),
            pl.BlockSpec((bg, HEADS, DH, DH), lambda i, c: (i, 0, 0, 0)),
            pl.BlockSpec((bg, CONV_W - 1, D_MODEL), lambda i, c: (i, 0, 0)),
            pl.BlockSpec((None, slab, 2 * D_FF), lambda i, c: (0, i * n_chunks + c, 0)),
            pl.BlockSpec((None, wd_slab, D_MODEL), lambda i, c: (0, wd_block(i, c), 0)),
        ],
        out_shape=[
            jax.ShapeDtypeStruct(x.shape, F32),
            jax.ShapeDtypeStruct((nb, HEADS, DH, DH), F32),
            jax.ShapeDtypeStruct((nb, CONV_W - 1, D_MODEL), F32),
            jax.ShapeDtypeStruct((1, D_MODEL, 2 * D_FF), BF16),
            jax.ShapeDtypeStruct((1, D_FF, D_MODEL), BF16),
        ],
        scratch_shapes=[
            pltpu.VMEM((m, D_MODEL), BF16),
            pltpu.VMEM((bg * HEADS, DH, CHUNK), BF16),
            pltpu.VMEM((bg * HEADS, DH, CHUNK), BF16),
            pltpu.VMEM((m, D_MODEL), BF16),
            pltpu.VMEM((m, D_MODEL), F32),
        ],
        compiler_params=_params(("arbitrary", "arbitrary")),
        name="mixer_main",
    )(x, s0, c0, g, *([win] * (N_IN // D_MODEL)), cw, wro, wco, wo, cq, sq, ck, sk, dmask, qdec, kdec, gl,
      wgu32, wd32)


def _round_square_slabs(step, n_slabs, sq_refs, sb_refs):
    @pl.when(step < n_slabs)
    def _():
        for src, dst in zip(sq_refs, sb_refs):
            dst[...] = src[...].astype(BF16)


def _square_slab_specs(layer, slab, n_slabs):
    slab_in = pl.BlockSpec((None, slab, D_MODEL), lambda j: (layer, jnp.minimum(j, n_slabs - 1), 0))
    slab_out = pl.BlockSpec((None, slab, D_MODEL), lambda j: (0, jnp.minimum(j, n_slabs - 1), 0))
    return [slab_in] * 3, [slab_out] * 3, [jax.ShapeDtypeStruct((1, D_MODEL, D_MODEL), BF16)] * 3


def _proj_first_kernel(xs_ref, xm_ref, g_ref, w32_ref, sq0_ref, sq1_ref, sq2_ref,
                       o_ref, wb_ref, sb0_ref, sb1_ref, sb2_ref, xcat_ref, hn_s, *, n_slabs):
    j = pl.program_id(0)
    n_samp = xs_ref.shape[0]

    @pl.when(j == 0)
    def _():
        xcat_ref[:n_samp, :] = xs_ref[...]
        xcat_ref[n_samp:, :] = xm_ref[...]
        hn_s[:n_samp, :] = _rms_f32(xs_ref[...], g_ref[...]).astype(BF16)
        hn_s[n_samp:, :] = _rms_f32(xm_ref[...], g_ref[...]).astype(BF16)

    _round_square_slabs(j, n_slabs, (sq0_ref, sq1_ref, sq2_ref), (sb0_ref, sb1_ref, sb2_ref))
    wb = w32_ref[...].astype(BF16)
    wb_ref[...] = wb
    o_ref[...] = _dot(hn_s[...], wb)


def _proj_first(xs, xm, layer, g, win32, squares32):
    rows = xs.shape[0] + xm.shape[0]
    n_col = N_IN // D_MODEL
    slab = 128
    n_slabs = D_MODEL // slab
    assert n_slabs <= n_col
    sq_in, sq_out, sq_shapes = _square_slab_specs(layer, slab, n_slabs)
    return pl.pallas_call(
        functools.partial(_proj_first_kernel, n_slabs=n_slabs),
        grid=(n_col,),
        in_specs=[_resident(xs.shape), _resident(xm.shape), _resident_layer((1, D_MODEL), layer),
                  pl.BlockSpec((None, D_MODEL, D_MODEL), lambda j: (layer, 0, j))] + sq_in,
        out_specs=[pl.BlockSpec((rows, D_MODEL), lambda j: (0, j)),
                   pl.BlockSpec((None, D_MODEL, D_MODEL), lambda j: (0, 0, j))] + sq_out
        + [pl.BlockSpec((rows, D_MODEL), lambda j: (0, 0))],
        out_shape=[jax.ShapeDtypeStruct((rows, N_IN), F32), jax.ShapeDtypeStruct((1, D_MODEL, N_IN), BF16)]
        + sq_shapes + [jax.ShapeDtypeStruct((rows, D_MODEL), F32)],
        scratch_shapes=[pltpu.VMEM((rows, D_MODEL), BF16)],
        compiler_params=_params(("arbitrary",)),
        name="proj_short_first",
    )(xs, xm, g, win32, *squares32)


def _proj_kernel(x_ref, g_ref, win_ref, sq0_ref, sq1_ref, sq2_ref, o_ref, sb0_ref, sb1_ref, sb2_ref, *, n_slabs):
    _round_square_slabs(pl.program_id(0), n_slabs, (sq0_ref, sq1_ref, sq2_ref), (sb0_ref, sb1_ref, sb2_ref))
    hn = _rms_f32(x_ref[...], g_ref[...]).astype(BF16)
    o_ref[...] = _dot(hn, win_ref[...])


def _proj(x, layer, g, win, squares32, tm):
    rows = x.shape[0]
    steps = rows // tm
    slab = 256
    n_slabs = D_MODEL // slab
    assert n_slabs <= steps and steps * tm == rows
    sq_in, sq_out, sq_shapes = _square_slab_specs(layer, slab, n_slabs)
    return pl.pallas_call(
        functools.partial(_proj_kernel, n_slabs=n_slabs),
        grid=(steps,),
        in_specs=[pl.BlockSpec((tm, D_MODEL), lambda i: (i, 0)), _resident_layer((1, D_MODEL), layer),
                  _resident_layer((D_MODEL, N_IN), 0)] + sq_in,
        out_specs=[pl.BlockSpec((tm, N_IN), lambda i: (i, 0))] + sq_out,
        out_shape=[jax.ShapeDtypeStruct((rows, N_IN), F32)] + sq_shapes,
        compiler_params=_params(("arbitrary",)),
        name="proj_short",
    )(x, g, win, *squares32)


def _ret_short_first(p_ref, s_ref, tab_refs, sb, L, sink):
    cq_ref, sq_ref, ck_ref, sk_ref, dmask_ref, qdec_ref, kdec_ref = tab_refs
    cq, sq, ck, sk = cq_ref[...], sq_ref[...], ck_ref[...], sk_ref[...]
    staged = []
    for b in range(sb):
        r = slice(b * L, (b + 1) * L)
        for h in range(HEADS):
            cl = slice(h * DH, (h + 1) * DH)
            q = _rotary(p_ref[r, C_Q + h * DH:C_Q + (h + 1) * DH], cq, sq).astype(BF16)
            k = _rotary(p_ref[r, C_K + h * DH:C_K + (h + 1) * DH], ck, sk)
            v = p_ref[r, C_V + h * DH:C_V + (h + 1) * DH]
            s = s_ref[b, h]
            scores = (_dot_nt(q, k.astype(BF16)) * dmask_ref[h]).astype(BF16)
            cross = _dot(q, s.astype(BF16)) * qdec_ref[:, cl]
            sink(b, h, r, cl, s, k * kdec_ref[:, cl], v)
            staged.append((scores, cross, v.astype(BF16)))
    return staged


def _ret_short_second(staged, p_ref, c_ref, cw_ref, gated_ref, bgy_ref, cn_ref, sb, L):
    rows = lax.broadcasted_iota(jnp.int32, (L, D_MODEL), 0)
    cw = cw_ref[...]
    for b in range(sb):
        r = slice(b * L, (b + 1) * L)
        for h in range(HEADS):
            cl = slice(h * DH, (h + 1) * DH)
            scores, cross, v = staged[b * HEADS + h]
            o = _dot(scores, v) + cross
            g = p_ref[r, C_G + h * DH:C_G + (h + 1) * DH]
            gated_ref[r, cl] = _silu(g) * _group_norm(o)
        u = p_ref[r, C_CG:C_CG + D_MODEL] * p_ref[r, C_HC:C_HC + D_MODEL]
        y = _short_conv(u, c_ref[b], cw, rows)
        bgy_ref[r, :] = p_ref[r, C_BG:C_BG + D_MODEL] * y
        cn_ref[b] = u[L - (CONV_W - 1):, :]


def _state_sink(sn_ref, gl_ref, layer=None):
    def sink(b, h, r, cl, s, kd, v):
        s_new = gl_ref[:, cl] * s + _dot_tn(kd.astype(BF16), v.astype(BF16))
        if layer is None:
            sn_ref[b, h] = s_new
        else:
            sn_ref[layer, b, h] = s_new
    return sink


def _ret_meta_kernel(p_ref, s_ref, c_ref, cw_ref, cq_ref, sq_ref, ck_ref, sk_ref, dmask_ref, qdec_ref, kdec_ref,
                     gl_ref, gated_ref, bgy_ref, cn_ref, sn_ref, *, L):
    sb = c_ref.shape[0]
    tabs = (cq_ref, sq_ref, ck_ref, sk_ref, dmask_ref, qdec_ref, kdec_ref)
    staged = _ret_short_first(p_ref, s_ref, tabs, sb, L, _state_sink(sn_ref, gl_ref))
    _ret_short_second(staged, p_ref, c_ref, cw_ref, gated_ref, bgy_ref, cn_ref, sb, L)


def _short_table_specs(L):
    return [_resident((L, DH))] * 4 + [_resident((HEADS, L, L)), _resident((L, D_MODEL)), _resident((L, D_MODEL)),
                                       _resident((1, D_MODEL))]


def _ret_meta(proj, row0, state, cprev, layer, cw, rope, decay, nseq, L):
    dmask, qdec, kdec, gl, _ = decay
    rows = nseq * L
    row_spec = pl.BlockSpec((rows, D_MODEL), lambda i: (0, 0))
    row_shape = jax.ShapeDtypeStruct((rows, D_MODEL), F32)
    return pl.pallas_call(
        functools.partial(_ret_meta_kernel, L=L),
        grid=(1,),
        in_specs=[
            pl.BlockSpec((rows, C_GA), lambda i: (row0 // rows, 0)),
            _resident((nseq, HEADS, DH, DH)),
            _resident((nseq, CONV_W - 1, D_MODEL)),
            _resident_layer((CONV_W, D_MODEL), layer),
        ] + _short_table_specs(L),
        out_specs=[row_spec, row_spec, pl.BlockSpec((nseq, CONV_W - 1, D_MODEL), lambda i: (0, 0, 0)),
                   pl.BlockSpec((nseq, HEADS, DH, DH), lambda i: (0, 0, 0, 0))],
        out_shape=[row_shape, row_shape, jax.ShapeDtypeStruct((nseq, CONV_W - 1, D_MODEL), F32),
                   jax.ShapeDtypeStruct((nseq, HEADS, DH, DH), F32)],
        compiler_params=_params(("arbitrary",)),
        name="ret_meta",
    )(proj, state, cprev, cw, *rope, dmask, qdec, kdec, gl)


def _out_ffn_kernel(x_ref, gated_ref, bgy_ref, ga_ref, gb_ref, wro_ref, wco_ref, wo_ref, g_ref, *refs, final):
    wgu_refs = refs[:FFN_PARTS]
    wd_ref, fg_ref, o_ref = refs[FFN_PARTS:]
    h = _out_proj(x_ref[...], gated_ref[...], bgy_ref[...], ga_ref[...], gb_ref[...], wro_ref, wco_ref, wo_ref)
    y = _ffn_block(h, g_ref, wgu_refs, wd_ref)
    if final:
        y = _rms_f32(y, fg_ref[...])
    o_ref[...] = y


def _out_ffn(x, gated, bgy, proj, layer, wro, wco, wo, g, wgu, wd, fg, final, rows, tm):
    row_spec = pl.BlockSpec((tm, D_MODEL), lambda i: (i, 0))
    w_spec = _resident_layer((D_MODEL, D_MODEL), 0)
    return pl.pallas_call(
        functools.partial(_out_ffn_kernel, final=final),
        grid=(rows // tm,),
        in_specs=[row_spec, row_spec, row_spec,
                  pl.BlockSpec((tm, D_MODEL), lambda i: (i, C_GA // D_MODEL)),
                  pl.BlockSpec((tm, D_MODEL), lambda i: (i, C_GB // D_MODEL)),
                  w_spec, w_spec, w_spec,
                  _resident_layer((1, D_MODEL), layer),
                  *_resident_col_parts(D_MODEL, 2 * D_FF // FFN_PARTS, FFN_PARTS),
                  _resident_layer((D_FF, D_MODEL), 0), _resident((1, D_MODEL))],
        out_specs=row_spec,
        out_shape=jax.ShapeDtypeStruct((rows, D_MODEL), F32),
        compiler_params=_params(("arbitrary",)),
        name="out_ffn_short",
    )(x, gated, bgy, proj, proj, wro, wco, wo, g, *([wgu] * FFN_PARTS), wd, fg)


def _ffn_main_kernel(x_ref, g_ref, *refs, final, n_prev, L):
    wgu_refs = refs[:FFN_PARTS]
    (wd_ref, fg_ref, p_ref, s_ref, c_ref, cw_ref, cq_ref, sq_ref, ck_ref, sk_ref, dmask_ref, qdec_ref, kdec_ref,
     gl_ref) = refs[FFN_PARTS:FFN_PARTS + 14]
    rest = refs[FFN_PARTS + 14:]
    sb = c_ref.shape[0]
    tabs = (cq_ref, sq_ref, ck_ref, sk_ref, dmask_ref, qdec_ref, kdec_ref)

    if final:
        n_main = n_prev + 1
        prev, mains, outs = rest[:3 * n_prev], rest[3 * n_prev:3 * n_prev + n_main], rest[3 * n_prev + n_main:]
        y_ref, gated_ref, bgy_ref, cn_ref, sn_ref, mstack_ref = outs
        for l in range(n_main):
            mstack_ref[l] = mains[l][...]
        sink = _state_sink(sn_ref, gl_ref, n_prev)
    else:
        wn32_ref, y_ref, gated_ref, bgy_ref, cn_ref, kd_ref, v_ref, wnb_ref = rest
        wnb_ref[...] = wn32_ref[...].astype(BF16)

        def sink(b, h, r, cl, s, kd, v):
            kd_ref[r, cl] = kd
            v_ref[r, cl] = v

    x = x_ref[...]
    hn = _rms_f32(x, g_ref[...]).astype(BF16)
    staged = _ret_short_first(p_ref, s_ref, tabs, sb, L, sink)
    gates = [_dot(hn, w[...]) for w in wgu_refs[:FFN_PARTS // 2]]
    if final:
        for l in range(n_prev):
            so_ref, kdo_ref, vo_ref = prev[3 * l:3 * l + 3]
            upd = _state_sink(sn_ref, gl_ref, l)
            for b in range(sb):
                r = slice(b * L, (b + 1) * L)
                for h in range(HEADS):
                    cl = slice(h * DH, (h + 1) * DH)
                    upd(b, h, r, cl, so_ref[b, h], kdo_ref[r, cl], vo_ref[r, cl])
    ups = [_dot(hn, w[...]) for w in wgu_refs[FFN_PARTS // 2:]]
    _ret_short_second(staged, p_ref, c_ref, cw_ref, gated_ref, bgy_ref, cn_ref, sb, L)
    y = x + _dot(_swiglu(gates, ups), wd_ref[...])
    if final:
        y = _rms_f32(y, fg_ref[...])
    y_ref[...] = y


def _ffn_main(x, layer, g, wgu, wd, fg, final, tm, proj, state, cconv, cw, rope, decay, L, kds, vs, main_states,
              win32):
    rows = x.shape[0]
    steps = rows // tm
    depth, ns = state.shape[:2]
    sb = ns // steps
    assert sb * steps == ns and (sb * L) % 8 == 0
    n_prev = len(kds) if final else 0
    dmask, qdec, kdec, gl, _ = decay
    n_samp = ns * L
    row_spec = pl.BlockSpec((tm, D_MODEL), lambda i: (i, 0))
    srow_spec = pl.BlockSpec((sb * L, D_MODEL), lambda i: (i, 0))
    srow_shape = jax.ShapeDtypeStruct((n_samp, D_MODEL), F32)

    def state_spec(l):
        return pl.BlockSpec((None, sb, HEADS, DH, DH), lambda i: (l, i, 0, 0, 0))

    in_specs = [
        row_spec, _resident_layer((1, D_MODEL), layer),
        *_resident_col_parts(D_MODEL, 2 * D_FF // FFN_PARTS, FFN_PARTS),
        _resident_layer((D_FF, D_MODEL), 0), _resident((1, D_MODEL)),
        pl.BlockSpec((sb * L, C_GA), lambda i: (i, 0)),
        state_spec(layer),
        pl.BlockSpec((None, sb, CONV_W - 1, D_MODEL), lambda i: (layer, i, 0, 0)),
        _resident_layer((CONV_W, D_MODEL), layer),
    ] + _short_table_specs(L)
    args = [x, g, *([wgu] * FFN_PARTS), wd, fg, proj, state, cconv, cw, *rope, dmask, qdec, kdec, gl]
    for l in range(n_prev):
        in_specs += [state_spec(l), srow_spec, srow_spec]
        args += [state, kds[l], vs[l]]
    out_specs = [row_spec, srow_spec, srow_spec, pl.BlockSpec((sb, CONV_W - 1, D_MODEL), lambda i: (i, 0, 0))]
    out_shape = [jax.ShapeDtypeStruct(x.shape, F32), srow_shape, srow_shape,
                 jax.ShapeDtypeStruct((ns, CONV_W - 1, D_MODEL), F32)]
    if final:
        assert n_prev == depth - 1 and len(main_states) == depth
        out_specs.append(pl.BlockSpec((depth, sb, HEADS, DH, DH), lambda i: (0, i, 0, 0, 0)))
        out_shape.append(jax.ShapeDtypeStruct(state.shape, F32))
        n_mat = main_states[0].shape[0] * HEADS
        mb = n_mat // steps
        assert mb * steps == n_mat
        in_specs += [pl.BlockSpec((mb, DH, DH), lambda i: (i, 0, 0))] * depth
        args += [s.reshape(n_mat, DH, DH) for s in main_states]
        out_specs.append(pl.BlockSpec((depth, mb, DH, DH), lambda i: (0, i, 0, 0)))
        out_shape.append(jax.ShapeDtypeStruct((depth, n_mat, DH, DH), F32))
    else:
        slab = D_MODEL // steps
        assert slab * steps == D_MODEL and slab % 16 == 0
        in_specs.append(pl.BlockSpec((None, slab, N_IN), lambda i: (layer + 1, i, 0)))
        args.append(win32)
        out_specs += [srow_spec, srow_spec, pl.BlockSpec((None, slab, N_IN), lambda i: (0, i, 0))]
        out_shape += [srow_shape, srow_shape, jax.ShapeDtypeStruct((1, D_MODEL, N_IN), BF16)]
    return pl.pallas_call(
        functools.partial(_ffn_main_kernel, final=final, n_prev=n_prev, L=L),
        grid=(steps,),
        in_specs=in_specs,
        out_specs=out_specs,
        out_shape=out_shape,
        compiler_params=_params(("arbitrary",)),
        name="ffn_main",
    )(*args)


def kernel(x_prompt, x_sample, state_ret, state_conv, meta_tokens, norm_mix_g, w_in, conv_w, w_ret_o,
           w_conv_o, w_o, norm_ffn_g, w_gate_up, w_down, final_norm_g):
    depth = w_in.shape[0]
    nb, seq, _ = x_prompt.shape
    ns, ls, _ = x_sample.shape
    n_samp = ns * ls
    n_short = n_samp + N_META
    short_tm = n_short // 5
    assert short_tm * 5 == n_short and short_tm % 8 == 0 and n_samp % N_META == 0

    gm = norm_mix_g.reshape(depth, 1, D_MODEL)
    gf = norm_ffn_g.reshape(depth, 1, D_MODEL)
    fg = final_norm_g.reshape(1, D_MODEL)

    rope_meta = _rope_tables(np.arange(N_META))
    rope_main = _rope_tables(N_META + np.arange(seq))
    rope_samp = _rope_tables(PAST_LEN + np.arange(ls))
    dec_meta, dec_main, dec_samp = _decay_tables(N_META), _decay_tables(CHUNK), _decay_tables(ls)

    h_main = x_prompt
    zero_s = jnp.zeros((1, HEADS, DH, DH), F32)
    zero_c = jnp.zeros((1, CONV_W - 1, D_MODEL), F32)
    squares32 = (w_ret_o, w_conv_o, w_o)

    s_p, c_p, c_s, kds, vs = [], [], [], [], []
    y_samp = s_s = s_p_stacked = h_short = win_b = None
    for l in range(depth):
        last = l == depth - 1
        if l == 0:
            proj, win_b, wro_b, wco_b, wo_b, h_short = _proj_first(
                x_sample.reshape(n_samp, D_MODEL), meta_tokens.astype(F32), l, gm, w_in, squares32)
        else:
            proj, wro_b, wco_b, wo_b = _proj(h_short, l, gm, win_b, squares32, short_tm)
        gated_m, bgy_m, c_m, s_m = _ret_meta(proj, n_samp, zero_s, zero_c, l, conv_w, rope_meta, dec_meta, 1, N_META)

        h_main, s_l, c_l, wgu_b, wd_b = _mixer_main(h_main, s_m, c_m, l, gm, win_b, conv_w, wro_b, wco_b, wo_b,
                                                    rope_main, dec_main, w_gate_up, w_down)
        s_p.append(s_l)
        c_p.append(c_l)
        res = _ffn_main(h_main.reshape(nb * seq, D_MODEL), l, gf, wgu_b, wd_b, fg, last, MAIN_FFN_TM,
                        proj, state_ret, state_conv, conv_w, rope_samp, dec_samp, ls, kds, vs, s_p, w_in)
        h_main, gated_s, bgy_s, c_l = res[:4]
        h_main = h_main.reshape(nb, seq, D_MODEL)
        c_s.append(c_l)
        if last:
            s_s = res[4]
            s_p_stacked = res[5].reshape(depth, nb, HEADS, DH, DH)
        else:
            kds.append(res[4])
            vs.append(res[5])
            win_b = res[6]

        if last:
            y_samp = _out_ffn(h_short, gated_s, bgy_s, proj, l, wro_b, wco_b, wo_b, gf, wgu_b, wd_b, fg, True,
                              n_samp, 256)
        else:
            gated = jnp.concatenate([gated_s, gated_m], axis=0)
            bgy = jnp.concatenate([bgy_s, bgy_m], axis=0)
            h_short = _out_ffn(h_short, gated, bgy, proj, l, wro_b, wco_b, wo_b, gf, wgu_b, wd_b, fg, False,
                               n_short, short_tm)

    return (h_main, y_samp.reshape(ns, ls, D_MODEL), s_p_stacked, jnp.stack(c_p), s_s, jnp.stack(c_s))
```

```python
import functools

import numpy as np
import jax
import jax.numpy as jnp
from jax import lax
from jax.experimental import pallas as pl
from jax.experimental.pallas import tpu as pltpu

D_MODEL = 1024
N_META = 16
HEADS = 8
DH = D_MODEL // HEADS
CHUNK = 128
ROPE_BASE = 10000.0
CONV_W = 3
D_FF = ((8 * D_MODEL + 3 * 256 - 1) // (3 * 256)) * 256
EPS = 1e-6
PAST_LEN = 16384
N_IN = 9 * D_MODEL
C_Q, C_K, C_V, C_G, C_BG, C_CG, C_HC, C_GA, C_GB = (i * D_MODEL for i in range(9))

F32 = jnp.float32
BF16 = jnp.bfloat16

VMEM_LIMIT_BYTES = 58 * 1024 * 1024

MAIN_BG = 4
MAIN_FFN_TM = 512
FFN_PARTS = 2
assert (2 * D_FF // FFN_PARTS) % 256 == 0 and FFN_PARTS % 2 == 0


def _resident(shape):
    nd = len(shape)
    return pl.BlockSpec(shape, lambda *_: (0,) * nd, pipeline_mode=pl.Buffered(1))


def _resident_layer(shape, layer):
    nd = len(shape)
    return pl.BlockSpec((None,) + tuple(shape), lambda *_: (layer,) + (0,) * nd, pipeline_mode=pl.Buffered(1))


def _resident_col_parts(rows, width, n_parts):
    return [pl.BlockSpec((None, rows, width), functools.partial(lambda *_, q: (0, 0, q), q=q),
                         pipeline_mode=pl.Buffered(1)) for q in range(n_parts)]


def _params(sem):
    return pltpu.CompilerParams(dimension_semantics=sem, vmem_limit_bytes=VMEM_LIMIT_BYTES)


def _dot(a, b):
    return jnp.dot(a, b, preferred_element_type=F32)


def _dot_nt(a, b):
    return lax.dot_general(a, b, (((1,), (1,)), ((), ())), preferred_element_type=F32)


def _dot_tn(a, b):
    return lax.dot_general(a, b, (((0,), (0,)), ((), ())), preferred_element_type=F32)


def _rms_f32(x, g):
    return x * lax.rsqrt(jnp.mean(x * x, axis=-1, keepdims=True) + EPS) * g


def _sigmoid(x):
    return 1.0 / (1.0 + jnp.exp(-x))


def _silu(x):
    return x * _sigmoid(x)


def _rotary(t, cos, sin):
    return t * cos + pltpu.roll(t, DH // 2, 1) * sin


def _group_norm(o):
    mu = jnp.mean(o, axis=-1, keepdims=True)
    d = o - mu
    var = jnp.mean(d * d, axis=-1, keepdims=True)
    return d * lax.rsqrt(var + EPS)


def _retention_head_paired(q, kt, kdt, v, s, dmask, qdec, gl):
    L = q.shape[0]
    sc = _dot(q, jnp.concatenate([kt, s.astype(BF16)], axis=1))
    scores = (sc[:, :L] * dmask).astype(BF16)
    iu = _dot(jnp.concatenate([scores, kdt], axis=0), v)
    return iu[:L] + sc[:, L:] * qdec, gl * s + iu[L:]


def _short_conv(u, tail, cw, rows):
    r1 = pltpu.roll(u, 1, 0)
    r2 = pltpu.roll(u, 2, 0)
    t0, t1 = tail[0:1, :], tail[1:2, :]
    sh1 = jnp.where(rows == 0, t1, r1)
    sh2 = jnp.where(rows == 0, t0, jnp.where(rows == 1, t1, r2))
    return cw[0:1, :] * sh2 + cw[1:2, :] * sh1 + cw[2:3, :] * u


def _out_proj(x, gated, bgy, ga, gb, wro_ref, wco_ref, wo_ref):
    ret_out = _dot(gated.astype(BF16), wro_ref[...])
    conv_out = _dot(bgy.astype(BF16), wco_ref[...])
    merged = _sigmoid(ga) * ret_out + _sigmoid(gb) * conv_out
    return x + _dot(merged.astype(BF16), wo_ref[...])


def _swiglu(gates, ups):
    return jnp.concatenate([_silu(a) * b for a, b in zip(gates, ups)], axis=1).astype(BF16)


def _ffn_block(x, g_ref, wgu_refs, wd_ref):
    hn = _rms_f32(x, g_ref[...]).astype(BF16)
    n = len(wgu_refs) // 2
    gates = [_dot(hn, w[...]) for w in wgu_refs[:n]]
    ups = [_dot(hn, w[...]) for w in wgu_refs[n:]]
    return x + _dot(_swiglu(gates, ups), wd_ref[...])


def _log_gamma():
    return np.log1p(-np.exp2(-5.0 - np.arange(HEADS, dtype=np.float64)))


def _const(t):
    return jnp.asarray(np.asarray(t, dtype=np.float32))


def _rope_tables(pos):
    half = DH // 2
    inv = np.power(ROPE_BASE, -np.arange(half, dtype=np.float64) / half)
    ang = np.asarray(pos, dtype=np.float64)[:, None] * inv[None, :]
    cos, sin = np.cos(ang), np.sin(ang)
    cosf = np.concatenate([cos, cos], axis=-1)
    sinf = np.concatenate([-sin, sin], axis=-1)
    scale = DH ** -0.5
    return tuple(_const(t) for t in (cosf, sinf, cosf * scale, sinf * scale))


def _decay_tables(L):
    log_g = _log_gamma()
    idx = np.arange(L, dtype=np.float64)
    diff = idx[:, None] - idx[None, :]
    dmask = np.where(diff >= 0, np.exp(log_g[:, None, None] * np.maximum(diff, 0.0)[None]), 0.0)
    qdec = np.exp(log_g[:, None] * (idx + 1.0)[None])
    kdec = np.exp(log_g[:, None] * (L - 1.0 - idx)[None])
    gl = np.exp(log_g * L)
    lanes = lambda t: np.repeat(t.T, DH, axis=1)
    return tuple(_const(t) for t in (dmask, lanes(qdec), lanes(kdec), np.repeat(gl, DH)[None, :], kdec[:, None, :]))


def _mixer_main_kernel(x_ref, s0_ref, c0_ref, g_ref, *refs):
    n_groups = N_IN // D_MODEL
    wq_ref, wk_ref, wv_ref, wg_ref, wbg_ref, wcg_ref, whc_ref, wga_ref, wgb_ref = refs[:n_groups]
    (cw_ref, wro_ref, wco_ref, wo_ref, cq_ref, sq_ref, ck_ref, sk_ref, dmask_ref, qdec_ref, kdec_ref, gl_ref,
     wgu32_ref, wd32_ref, h_ref, s_ref, c_ref, wgub_ref, wdb_ref, q_s, kt_s, kdt_s, v_s, o_s) = refs[n_groups:]
    bg = x_ref.shape[0]
    m = bg * CHUNK

    wgub_ref[...] = wgu32_ref[...].astype(BF16)
    step = pl.program_id(0) * pl.num_programs(1) + pl.program_id(1)

    @pl.when(step < D_FF // wd32_ref.shape[0])
    def _():
        wdb_ref[...] = wd32_ref[...].astype(BF16)

    @pl.when(pl.program_id(1) == 0)
    def _():
        for b in range(bg):
            s_ref[b] = s0_ref[...]
            c_ref[b] = c0_ref[...]

    x = x_ref[...].reshape(m, D_MODEL)
    hn = _rms_f32(x, g_ref[...]).astype(BF16)

    cq, sq, ck, sk = cq_ref[...], sq_ref[...], ck_ref[...], sk_ref[...]
    q = _dot(hn, wq_ref[...])
    k = _dot(hn, wk_ref[...])
    v_s[...] = _dot(hn, wv_ref[...]).astype(BF16)
    for b in range(bg):
        r = slice(b * CHUNK, (b + 1) * CHUNK)
        for h in range(HEADS):
            cl = slice(h * DH, (h + 1) * DH)
            q_s[r, cl] = _rotary(q[r, cl], cq, sq).astype(BF16)
            krt = _rotary(k[r, cl], ck, sk).T
            kt_s[b * HEADS + h] = krt.astype(BF16)
            kdt_s[b * HEADS + h] = (krt * kdec_ref[h]).astype(BF16)

    for b in range(bg):
        r = slice(b * CHUNK, (b + 1) * CHUNK)
        for h in range(HEADS):
            cl = slice(h * DH, (h + 1) * DH)
            o, s_new = _retention_head_paired(q_s[r, cl], kt_s[b * HEADS + h], kdt_s[b * HEADS + h], v_s[r, cl],
                                              s_ref[b, h], dmask_ref[h], qdec_ref[:, cl], gl_ref[:, cl])
            s_ref[b, h] = s_new
            o_s[r, cl] = _group_norm(o)

    g = _dot(hn, wg_ref[...])
    gated = _silu(g) * o_s[...]

    bgate = _dot(hn, wbg_ref[...])
    u = _dot(hn, wcg_ref[...]) * _dot(hn, whc_ref[...])
    rows = lax.broadcasted_iota(jnp.int32, (CHUNK, D_MODEL), 0)
    cw = cw_ref[...]
    ys = []
    for b in range(bg):
        ub = u[b * CHUNK:(b + 1) * CHUNK]
        ys.append(_short_conv(ub, c_ref[b], cw, rows))
        c_ref[b] = ub[CHUNK - (CONV_W - 1):, :]
    bgy = bgate * jnp.concatenate(ys, axis=0)

    ga = _dot(hn, wga_ref[...])
    gb = _dot(hn, wgb_ref[...])
    out = _out_proj(x, gated, bgy, ga, gb, wro_ref, wco_ref, wo_ref)
    h_ref[...] = out.reshape(bg, CHUNK, D_MODEL)


def _mixer_main(x, s0, c0, layer, g, win, cw, wro, wco, wo, rope, decay, wgu32, wd32):
    nb, seq, _ = x.shape
    bg = MAIN_BG
    cq, sq, ck, sk = rope
    dmask, qdec, _, gl, kdec = decay
    m = bg * CHUNK
    n_chunks = seq // CHUNK
    steps = (nb // bg) * n_chunks
    slab = D_MODEL // steps
    wd_slab = 16 * (-(-D_FF // (16 * steps)))
    while D_FF % wd_slab:
        wd_slab += 16
    wd_last = D_FF // wd_slab - 1
    assert slab * steps == D_MODEL and slab % 16 == 0 and wd_last < steps

    def wd_block(i, c):
        return jnp.minimum(i * n_chunks + c, wd_last)

    rope_spec = pl.BlockSpec((CHUNK, DH), lambda i, c: (c, 0))
    return pl.pallas_call(
        _mixer_main_kernel,
        grid=(nb // bg, n_chunks),
        in_specs=[
            pl.BlockSpec((bg, CHUNK, D_MODEL), lambda i, c: (i, c, 0)),
            _resident_layer((HEADS, DH, DH), 0),
            _resident_layer((CONV_W - 1, D_MODEL), 0),
            _resident_layer((1, D_MODEL), layer),
            *_resident_col_parts(D_MODEL, D_MODEL, N_IN // D_MODEL),
            _resident_layer((CONV_W, D_MODEL), layer),
            _resident_layer((D_MODEL, D_MODEL), 0),
            _resident_layer((D_MODEL, D_MODEL), 0),
            _resident_layer((D_MODEL, D_MODEL), 0),
            rope_spec, rope_spec, rope_spec, rope_spec,
            _resident((HEADS, CHUNK, CHUNK)),
            _resident((CHUNK, D_MODEL)),
            _resident((HEADS, 1, CHUNK)),
            _resident((1, D_MODEL)),
            pl.BlockSpec((None, slab, 2 * D_FF), lambda i, c: (layer, i * n_chunks + c, 0)),
            pl.BlockSpec((None, wd_slab, D_MODEL), lambda i, c: (layer, wd_block(i, c), 0)),
        ],
        out_specs=[
            pl.BlockSpec((bg, CHUNK, D_MODEL), lambda i, c: (i, c, 0)),
            pl.BlockSpec((bg, HEADS, DH, DH), lambda i, c: (i, 0, 0, 0)),
            pl.BlockSpec((bg, CONV_W - 1, D_MODEL), lambda i, c: (i, 0, 0)),
            pl.BlockSpec((None, slab, 2 * D_FF), lambda i, c: (0, i * n_chunks + c, 0)),
            pl.BlockSpec((None, wd_slab, D_MODEL), lambda i, c: (0, wd_block(i, c), 0)),
        ],
        out_shape=[
            jax.ShapeDtypeStruct(x.shape, F32),
            jax.ShapeDtypeStruct((nb, HEADS, DH, DH), F32),
            jax.ShapeDtypeStruct((nb, CONV_W - 1, D_MODEL), F32),
            jax.ShapeDtypeStruct((1, D_MODEL, 2 * D_FF), BF16),
            jax.ShapeDtypeStruct((1, D_FF, D_MODEL), BF16),
        ],
        scratch_shapes=[
            pltpu.VMEM((m, D_MODEL), BF16),
            pltpu.VMEM((bg * HEADS, DH, CHUNK), BF16),
            pltpu.VMEM((bg * HEADS, DH, CHUNK), BF16),
            pltpu.VMEM((m, D_MODEL), BF16),
            pltpu.VMEM((m, D_MODEL), F32),
        ],
        compiler_params=_params(("arbitrary", "arbitrary")),
        name="mixer_main",
    )(x, s0, c0, g, *([win] * (N_IN // D_MODEL)), cw, wro, wco, wo, cq, sq, ck, sk, dmask, qdec, kdec, gl,
      wgu32, wd32)


def _round_square_slabs(step, n_slabs, sq_refs, sb_refs):
    @pl.when(step < n_slabs)
    def _():
        for src, dst in zip(sq_refs, sb_refs):
            dst[...] = src[...].astype(BF16)


def _square_slab_specs(layer, slab, n_slabs):
    slab_in = pl.BlockSpec((None, slab, D_MODEL), lambda j: (layer, jnp.minimum(j, n_slabs - 1), 0))
    slab_out = pl.BlockSpec((None, slab, D_MODEL), lambda j: (0, jnp.minimum(j, n_slabs - 1), 0))
    return [slab_in] * 3, [slab_out] * 3, [jax.ShapeDtypeStruct((1, D_MODEL, D_MODEL), BF16)] * 3


def _proj_first_kernel(xs_ref, xm_ref, g_ref, w32_ref, sq0_ref, sq1_ref, sq2_ref,
                       o_ref, wb_ref, sb0_ref, sb1_ref, sb2_ref, xcat_ref, hn_s, *, n_slabs):
    j = pl.program_id(0)
    n_samp = xs_ref.shape[0]

    @pl.when(j == 0)
    def _():
        xcat_ref[:n_samp, :] = xs_ref[...]
        xcat_ref[n_samp:, :] = xm_ref[...]
        hn_s[:n_samp, :] = _rms_f32(xs_ref[...], g_ref[...]).astype(BF16)
        hn_s[n_samp:, :] = _rms_f32(xm_ref[...], g_ref[...]).astype(BF16)

    _round_square_slabs(j, n_slabs, (sq0_ref, sq1_ref, sq2_ref), (sb0_ref, sb1_ref, sb2_ref))
    wb = w32_ref[...].astype(BF16)
    wb_ref[...] = wb
    o_ref[...] = _dot(hn_s[...], wb)


def _proj_first(xs, xm, layer, g, win32, squares32):
    rows = xs.shape[0] + xm.shape[0]
    n_col = N_IN // D_MODEL
    slab = 128
    n_slabs = D_MODEL // slab
    assert n_slabs <= n_col
    sq_in, sq_out, sq_shapes = _square_slab_specs(layer, slab, n_slabs)
    return pl.pallas_call(
        functools.partial(_proj_first_kernel, n_slabs=n_slabs),
        grid=(n_col,),
        in_specs=[_resident(xs.shape), _resident(xm.shape), _resident_layer((1, D_MODEL), layer),
                  pl.BlockSpec((None, D_MODEL, D_MODEL), lambda j: (layer, 0, j))] + sq_in,
        out_specs=[pl.BlockSpec((rows, D_MODEL), lambda j: (0, j)),
                   pl.BlockSpec((None, D_MODEL, D_MODEL), lambda j: (0, 0, j))] + sq_out
        + [pl.BlockSpec((rows, D_MODEL), lambda j: (0, 0))],
        out_shape=[jax.ShapeDtypeStruct((rows, N_IN), F32), jax.ShapeDtypeStruct((1, D_MODEL, N_IN), BF16)]
        + sq_shapes + [jax.ShapeDtypeStruct((rows, D_MODEL), F32)],
        scratch_shapes=[pltpu.VMEM((rows, D_MODEL), BF16)],
        compiler_params=_params(("arbitrary",)),
        name="proj_short_first",
    )(xs, xm, g, win32, *squares32)


def _proj_kernel(x_ref, g_ref, win_ref, sq0_ref, sq1_ref, sq2_ref, o_ref, sb0_ref, sb1_ref, sb2_ref, *, n_slabs):
    _round_square_slabs(pl.program_id(0), n_slabs, (sq0_ref, sq1_ref, sq2_ref), (sb0_ref, sb1_ref, sb2_ref))
    hn = _rms_f32(x_ref[...], g_ref[...]).astype(BF16)
    o_ref[...] = _dot(hn, win_ref[...])


def _proj(x, layer, g, win, squares32, tm):
    rows = x.shape[0]
    steps = rows // tm
    slab = 256
    n_slabs = D_MODEL // slab
    assert n_slabs <= steps and steps * tm == rows
    sq_in, sq_out, sq_shapes = _square_slab_specs(layer, slab, n_slabs)
    return pl.pallas_call(
        functools.partial(_proj_kernel, n_slabs=n_slabs),
        grid=(steps,),
        in_specs=[pl.BlockSpec((tm, D_MODEL), lambda i: (i, 0)), _resident_layer((1, D_MODEL), layer),
                  _resident_layer((D_MODEL, N_IN), 0)] + sq_in,
        out_specs=[pl.BlockSpec((tm, N_IN), lambda i: (i, 0))] + sq_out,
        out_shape=[jax.ShapeDtypeStruct((rows, N_IN), F32)] + sq_shapes,
        compiler_params=_params(("arbitrary",)),
        name="proj_short",
    )(x, g, win, *squares32)


def _ret_short_first(p_ref, s_ref, tab_refs, sb, L, sink):
    cq_ref, sq_ref, ck_ref, sk_ref, dmask_ref, qdec_ref, kdec_ref = tab_refs
    cq, sq, ck, sk = cq_ref[...], sq_ref[...], ck_ref[...], sk_ref[...]
    staged = []
    for b in range(sb):
        r = slice(b * L, (b + 1) * L)
        for h in range(HEADS):
            cl = slice(h * DH, (h + 1) * DH)
            q = _rotary(p_ref[r, C_Q + h * DH:C_Q + (h + 1) * DH], cq, sq).astype(BF16)
            k = _rotary(p_ref[r, C_K + h * DH:C_K + (h + 1) * DH], ck, sk)
            v = p_ref[r, C_V + h * DH:C_V + (h + 1) * DH]
            s = s_ref[b, h]
            scores = (_dot_nt(q, k.astype(BF16)) * dmask_ref[h]).astype(BF16)
            cross = _dot(q, s.astype(BF16)) * qdec_ref[:, cl]
            sink(b, h, r, cl, s, k * kdec_ref[:, cl], v)
            staged.append((scores, cross, v.astype(BF16)))
    return staged


def _ret_short_second(staged, p_ref, c_ref, cw_ref, gated_ref, bgy_ref, cn_ref, sb, L):
    rows = lax.broadcasted_iota(jnp.int32, (L, D_MODEL), 0)
    cw = cw_ref[...]
    for b in range(sb):
        r = slice(b * L, (b + 1) * L)
        for h in range(HEADS):
            cl = slice(h * DH, (h + 1) * DH)
            scores, cross, v = staged[b * HEADS + h]
            o = _dot(scores, v) + cross
            g = p_ref[r, C_G + h * DH:C_G + (h + 1) * DH]
            gated_ref[r, cl] = _silu(g) * _group_norm(o)
        u = p_ref[r, C_CG:C_CG + D_MODEL] * p_ref[r, C_HC:C_HC + D_MODEL]
        y = _short_conv(u, c_ref[b], cw, rows)
        bgy_ref[r, :] = p_ref[r, C_BG:C_BG + D_MODEL] * y
        cn_ref[b] = u[L - (CONV_W - 1):, :]


def _state_sink(sn_ref, gl_ref, layer=None):
    def sink(b, h, r, cl, s, kd, v):
        s_new = gl_ref[:, cl] * s + _dot_tn(kd.astype(BF16), v.astype(BF16))
        if layer is None:
            sn_ref[b, h] = s_new
        else:
            sn_ref[layer, b, h] = s_new
    return sink


def _ret_meta_kernel(p_ref, s_ref, c_ref, cw_ref, cq_ref, sq_ref, ck_ref, sk_ref, dmask_ref, qdec_ref, kdec_ref,
                     gl_ref, gated_ref, bgy_ref, cn_ref, sn_ref, *, L):
    sb = c_ref.shape[0]
    tabs = (cq_ref, sq_ref, ck_ref, sk_ref, dmask_ref, qdec_ref, kdec_ref)
    staged = _ret_short_first(p_ref, s_ref, tabs, sb, L, _state_sink(sn_ref, gl_ref))
    _ret_short_second(staged, p_ref, c_ref, cw_ref, gated_ref, bgy_ref, cn_ref, sb, L)


def _short_table_specs(L):
    return [_resident((L, DH))] * 4 + [_resident((HEADS, L, L)), _resident((L, D_MODEL)), _resident((L, D_MODEL)),
                                       _resident((1, D_MODEL))]


def _ret_meta(proj, row0, state, cprev, layer, cw, rope, decay, nseq, L):
    dmask, qdec, kdec, gl, _ = decay
    rows = nseq * L
    row_spec = pl.BlockSpec((rows, D_MODEL), lambda i: (0, 0))
    row_shape = jax.ShapeDtypeStruct((rows, D_MODEL), F32)
    return pl.pallas_call(
        functools.partial(_ret_meta_kernel, L=L),
        grid=(1,),
        in_specs=[
            pl.BlockSpec((rows, C_GA), lambda i: (row0 // rows, 0)),
            _resident((nseq, HEADS, DH, DH)),
            _resident((nseq, CONV_W - 1, D_MODEL)),
            _resident_layer((CONV_W, D_MODEL), layer),
        ] + _short_table_specs(L),
        out_specs=[row_spec, row_spec, pl.BlockSpec((nseq, CONV_W - 1, D_MODEL), lambda i: (0, 0, 0)),
                   pl.BlockSpec((nseq, HEADS, DH, DH), lambda i: (0, 0, 0, 0))],
        out_shape=[row_shape, row_shape, jax.ShapeDtypeStruct((nseq, CONV_W - 1, D_MODEL), F32),
                   jax.ShapeDtypeStruct((nseq, HEADS, DH, DH), F32)],
        compiler_params=_params(("arbitrary",)),
        name="ret_meta",
    )(proj, state, cprev, cw, *rope, dmask, qdec, kdec, gl)


def _out_ffn_kernel(x_ref, gated_ref, bgy_ref, ga_ref, gb_ref, wro_ref, wco_ref, wo_ref, g_ref, *refs, final):
    wgu_refs = refs[:FFN_PARTS]
    wd_ref, fg_ref, o_ref = refs[FFN_PARTS:]
    h = _out_proj(x_ref[...], gated_ref[...], bgy_ref[...], ga_ref[...], gb_ref[...], wro_ref, wco_ref, wo_ref)
    y = _ffn_block(h, g_ref, wgu_refs, wd_ref)
    if final:
        y = _rms_f32(y, fg_ref[...])
    o_ref[...] = y


def _out_ffn(x, gated, bgy, proj, layer, wro, wco, wo, g, wgu, wd, fg, final, rows, tm):
    row_spec = pl.BlockSpec((tm, D_MODEL), lambda i: (i, 0))
    w_spec = _resident_layer((D_MODEL, D_MODEL), 0)
    return pl.pallas_call(
        functools.partial(_out_ffn_kernel, final=final),
        grid=(rows // tm,),
        in_specs=[row_spec, row_spec, row_spec,
                  pl.BlockSpec((tm, D_MODEL), lambda i: (i, C_GA // D_MODEL)),
                  pl.BlockSpec((tm, D_MODEL), lambda i: (i, C_GB // D_MODEL)),
                  w_spec, w_spec, w_spec,
                  _resident_layer((1, D_MODEL), layer),
                  *_resident_col_parts(D_MODEL, 2 * D_FF // FFN_PARTS, FFN_PARTS),
                  _resident_layer((D_FF, D_MODEL), 0), _resident((1, D_MODEL))],
        out_specs=row_spec,
        out_shape=jax.ShapeDtypeStruct((rows, D_MODEL), F32),
        compiler_params=_params(("arbitrary",)),
        name="out_ffn_short",
    )(x, gated, bgy, proj, proj, wro, wco, wo, g, *([wgu] * FFN_PARTS), wd, fg)


def _ffn_main_kernel(x_ref, g_ref, *refs, final, n_prev, L):
    wgu_refs = refs[:FFN_PARTS]
    (wd_ref, fg_ref, p_ref, s_ref, c_ref, cw_ref, cq_ref, sq_ref, ck_ref, sk_ref, dmask_ref, qdec_ref, kdec_ref,
     gl_ref) = refs[FFN_PARTS:FFN_PARTS + 14]
    rest = refs[FFN_PARTS + 14:]
    sb = c_ref.shape[0]
    tabs = (cq_ref, sq_ref, ck_ref, sk_ref, dmask_ref, qdec_ref, kdec_ref)

    if final:
        n_main = n_prev + 1
        prev, mains, outs = rest[:3 * n_prev], rest[3 * n_prev:3 * n_prev + n_main], rest[3 * n_prev + n_main:]
        y_ref, gated_ref, bgy_ref, cn_ref, sn_ref, mstack_ref = outs
        for l in range(n_main):
            mstack_ref[l] = mains[l][...]
        sink = _state_sink(sn_ref, gl_ref, n_prev)
    else:
        wn32_ref, y_ref, gated_ref, bgy_ref, cn_ref, kd_ref, v_ref, wnb_ref = rest
        wnb_ref[...] = wn32_ref[...].astype(BF16)

        def sink(b, h, r, cl, s, kd, v):
            kd_ref[r, cl] = kd
            v_ref[r, cl] = v

    x = x_ref[...]
    hn = _rms_f32(x, g_ref[...]).astype(BF16)
    gates = [_dot(hn, w[...]) for w in wgu_refs[:FFN_PARTS // 2]]
    if final:
        for l in range(n_prev):
            so_ref, kdo_ref, vo_ref = prev[3 * l:3 * l + 3]
            upd = _state_sink(sn_ref, gl_ref, l)
            for b in range(sb):
                r = slice(b * L, (b + 1) * L)
                for h in range(HEADS):
                    cl = slice(h * DH, (h + 1) * DH)
                    upd(b, h, r, cl, so_ref[b, h], kdo_ref[r, cl], vo_ref[r, cl])
    staged = _ret_short_first(p_ref, s_ref, tabs, sb, L, sink)
    ups = [_dot(hn, w[...]) for w in wgu_refs[FFN_PARTS // 2:]]
    _ret_short_second(staged, p_ref, c_ref, cw_ref, gated_ref, bgy_ref, cn_ref, sb, L)
    y = x + _dot(_swiglu(gates, ups), wd_ref[...])
    if final:
        y = _rms_f32(y, fg_ref[...])
    y_ref[...] = y


def _ffn_main(x, layer, g, wgu, wd, fg, final, tm, proj, state, cconv, cw, rope, decay, L, kds, vs, main_states,
              win32):
    rows = x.shape[0]
    steps = rows // tm
    depth, ns = state.shape[:2]
    sb = ns // steps
    assert sb * steps == ns and (sb * L) % 8 == 0
    n_prev = len(kds) if final else 0
    dmask, qdec, kdec, gl, _ = decay
    n_samp = ns * L
    row_spec = pl.BlockSpec((tm, D_MODEL), lambda i: (i, 0))
    srow_spec = pl.BlockSpec((sb * L, D_MODEL), lambda i: (i, 0))
    srow_shape = jax.ShapeDtypeStruct((n_samp, D_MODEL), F32)

    def state_spec(l):
        return pl.BlockSpec((None, sb, HEADS, DH, DH), lambda i: (l, i, 0, 0, 0))

    in_specs = [
        row_spec, _resident_layer((1, D_MODEL), layer),
        *_resident_col_parts(D_MODEL, 2 * D_FF // FFN_PARTS, FFN_PARTS),
        _resident_layer((D_FF, D_MODEL), 0), _resident((1, D_MODEL)),
        pl.BlockSpec((sb * L, C_GA), lambda i: (i, 0)),
        state_spec(layer),
        pl.BlockSpec((None, sb, CONV_W - 1, D_MODEL), lambda i: (layer, i, 0, 0)),
        _resident_layer((CONV_W, D_MODEL), layer),
    ] + _short_table_specs(L)
    args = [x, g, *([wgu] * FFN_PARTS), wd, fg, proj, state, cconv, cw, *rope, dmask, qdec, kdec, gl]
    for l in range(n_prev):
        in_specs += [state_spec(l), srow_spec, srow_spec]
        args += [state, kds[l], vs[l]]
    out_specs = [row_spec, srow_spec, srow_spec, pl.BlockSpec((sb, CONV_W - 1, D_MODEL), lambda i: (i, 0, 0))]
    out_shape = [jax.ShapeDtypeStruct(x.shape, F32), srow_shape, srow_shape,
                 jax.ShapeDtypeStruct((ns, CONV_W - 1, D_MODEL), F32)]
    if final:
        assert n_prev == depth - 1 and len(main_states) == depth
        out_specs.append(pl.BlockSpec((depth, sb, HEADS, DH, DH), lambda i: (0, i, 0, 0, 0)))
        out_shape.append(jax.ShapeDtypeStruct(state.shape, F32))
        n_mat = main_states[0].shape[0] * HEADS
        mb = n_mat // steps
        assert mb * steps == n_mat
        in_specs += [pl.BlockSpec((mb, DH, DH), lambda i: (i, 0, 0))] * depth
        args += [s.reshape(n_mat, DH, DH) for s in main_states]
        out_specs.append(pl.BlockSpec((depth, mb, DH, DH), lambda i: (0, i, 0, 0)))
        out_shape.append(jax.ShapeDtypeStruct((depth, n_mat, DH, DH), F32))
    else:
        slab = D_MODEL // steps
        assert slab * steps == D_MODEL and slab % 16 == 0
        in_specs.append(pl.BlockSpec((None, slab, N_IN), lambda i: (layer + 1, i, 0)))
        args.append(win32)
        out_specs += [srow_spec, srow_spec, pl.BlockSpec((None, slab, N_IN), lambda i: (0, i, 0))]
        out_shape += [srow_shape, srow_shape, jax.ShapeDtypeStruct((1, D_MODEL, N_IN), BF16)]
    return pl.pallas_call(
        functools.partial(_ffn_main_kernel, final=final, n_prev=n_prev, L=L),
        grid=(steps,),
        in_specs=in_specs,
        out_specs=out_specs,
        out_shape=out_shape,
        compiler_params=_params(("arbitrary",)),
        name="ffn_main",
    )(*args)


def kernel(x_prompt, x_sample, state_ret, state_conv, meta_tokens, norm_mix_g, w_in, conv_w, w_ret_o,
           w_conv_o, w_o, norm_ffn_g, w_gate_up, w_down, final_norm_g):
    depth = w_in.shape[0]
    nb, seq, _ = x_prompt.shape
    ns, ls, _ = x_sample.shape
    n_samp = ns * ls
    n_short = n_samp + N_META
    short_tm = n_short // 5
    assert short_tm * 5 == n_short and short_tm % 8 == 0 and n_samp % N_META == 0

    gm = norm_mix_g.reshape(depth, 1, D_MODEL)
    gf = norm_ffn_g.reshape(depth, 1, D_MODEL)
    fg = final_norm_g.reshape(1, D_MODEL)

    rope_meta = _rope_tables(np.arange(N_META))
    rope_main = _rope_tables(N_META + np.arange(seq))
    rope_samp = _rope_tables(PAST_LEN + np.arange(ls))
    dec_meta, dec_main, dec_samp = _decay_tables(N_META), _decay_tables(CHUNK), _decay_tables(ls)

    h_main = x_prompt
    zero_s = jnp.zeros((1, HEADS, DH, DH), F32)
    zero_c = jnp.zeros((1, CONV_W - 1, D_MODEL), F32)
    squares32 = (w_ret_o, w_conv_o, w_o)

    s_p, c_p, c_s, kds, vs = [], [], [], [], []
    y_samp = s_s = s_p_stacked = h_short = win_b = None
    for l in range(depth):
        last = l == depth - 1
        if l == 0:
            proj, win_b, wro_b, wco_b, wo_b, h_short = _proj_first(
                x_sample.reshape(n_samp, D_MODEL), meta_tokens.astype(F32), l, gm, w_in, squares32)
        else:
            proj, wro_b, wco_b, wo_b = _proj(h_short, l, gm, win_b, squares32, short_tm)
        gated_m, bgy_m, c_m, s_m = _ret_meta(proj, n_samp, zero_s, zero_c, l, conv_w, rope_meta, dec_meta, 1, N_META)

        h_main, s_l, c_l, wgu_b, wd_b = _mixer_main(h_main, s_m, c_m, l, gm, win_b, conv_w, wro_b, wco_b, wo_b,
                                                    rope_main, dec_main, w_gate_up, w_down)
        s_p.append(s_l)
        c_p.append(c_l)
        res = _ffn_main(h_main.reshape(nb * seq, D_MODEL), l, gf, wgu_b, wd_b, fg, last, MAIN_FFN_TM,
                        proj, state_ret, state_conv, conv_w, rope_samp, dec_samp, ls, kds, vs, s_p, w_in)
        h_main, gated_s, bgy_s, c_l = res[:4]
        h_main = h_main.reshape(nb, seq, D_MODEL)
        c_s.append(c_l)
        if last:
            s_s = res[4]
            s_p_stacked = res[5].reshape(depth, nb, HEADS, DH, DH)
        else:
            kds.append(res[4])
            vs.append(res[5])
            win_b = res[6]

        if last:
            y_samp = _out_ffn(h_short, gated_s, bgy_s, proj, l, wro_b, wco_b, wo_b, gf, wgu_b, wd_b, fg, True,
                              n_samp, 256)
        else:
            gated = jnp.concatenate([gated_s, gated_m], axis=0)
            bgy = jnp.concatenate([bgy_s, bgy_m], axis=0)
            h_short = _out_ffn(h_short, gated, bgy, proj, l, wro_b, wco_b, wo_b, gf, wgu_b, wd_b, fg, False,
                               n_short, short_tm)

    return (h_main, y_samp.reshape(ns, ls, D_MODEL), s_p_stacked, jnp.stack(c_p), s_s, jnp.stack(c_s))
```

```python
import functools

import numpy as np
import jax
import jax.numpy as jnp
from jax import lax
from jax.experimental import pallas as pl
from jax.experimental.pallas import tpu as pltpu

D_MODEL = 1024
N_META = 16
HEADS = 8
DH = D_MODEL // HEADS
CHUNK = 128
ROPE_BASE = 10000.0
CONV_W = 3
D_FF = ((8 * D_MODEL + 3 * 256 - 1) // (3 * 256)) * 256
EPS = 1e-6
PAST_LEN = 16384
N_IN = 9 * D_MODEL
C_Q, C_K, C_V, C_G, C_BG, C_CG, C_HC, C_GA, C_GB = (i * D_MODEL for i in range(9))

F32 = jnp.float32
BF16 = jnp.bfloat16

VMEM_LIMIT_BYTES = 58 * 1024 * 1024

MAIN_BG = 4
MAIN_FFN_TM = 512
FFN_PARTS = 2
assert (2 * D_FF // FFN_PARTS) % 256 == 0 and FFN_PARTS % 2 == 0


def _resident(shape):
    nd = len(shape)
    return pl.BlockSpec(shape, lambda *_: (0,) * nd, pipeline_mode=pl.Buffered(1))


def _resident_layer(shape, layer):
    nd = len(shape)
    return pl.BlockSpec((None,) + tuple(shape), lambda *_: (layer,) + (0,) * nd, pipeline_mode=pl.Buffered(1))


def _resident_col_parts(rows, width, n_parts):
    return [pl.BlockSpec((None, rows, width), functools.partial(lambda *_, q: (0, 0, q), q=q),
                         pipeline_mode=pl.Buffered(1)) for q in range(n_parts)]


def _params(sem):
    return pltpu.CompilerParams(dimension_semantics=sem, vmem_limit_bytes=VMEM_LIMIT_BYTES)


def _dot(a, b):
    return jnp.dot(a, b, preferred_element_type=F32)


def _dot_nt(a, b):
    return lax.dot_general(a, b, (((1,), (1,)), ((), ())), preferred_element_type=F32)


def _dot_tn(a, b):
    return lax.dot_general(a, b, (((0,), (0,)), ((), ())), preferred_element_type=F32)


def _rms_f32(x, g):
    return x * lax.rsqrt(jnp.mean(x * x, axis=-1, keepdims=True) + EPS) * g


def _sigmoid(x):
    return 1.0 / (1.0 + jnp.exp(-x))


def _silu(x):
    return x * _sigmoid(x)


def _rotary(t, cos, sin):
    return t * cos + pltpu.roll(t, DH // 2, 1) * sin


def _group_norm(o):
    mu = jnp.mean(o, axis=-1, keepdims=True)
    d = o - mu
    var = jnp.mean(d * d, axis=-1, keepdims=True)
    return d * lax.rsqrt(var + EPS)


def _retention_head_paired(q, kt, kdt, v, s, dmask, qdec, gl):
    L = q.shape[0]
    sc = _dot(q, jnp.concatenate([kt, s.astype(BF16)], axis=1))
    scores = (sc[:, :L] * dmask).astype(BF16)
    iu = _dot(jnp.concatenate([scores, kdt], axis=0), v)
    return iu[:L] + sc[:, L:] * qdec, gl * s + iu[L:]


def _short_conv(u, tail, cw, rows):
    r1 = pltpu.roll(u, 1, 0)
    r2 = pltpu.roll(u, 2, 0)
    t0, t1 = tail[0:1, :], tail[1:2, :]
    sh1 = jnp.where(rows == 0, t1, r1)
    sh2 = jnp.where(rows == 0, t0, jnp.where(rows == 1, t1, r2))
    return cw[0:1, :] * sh2 + cw[1:2, :] * sh1 + cw[2:3, :] * u


def _out_proj(x, gated, bgy, ga, gb, wro_ref, wco_ref, wo_ref):
    ret_out = _dot(gated.astype(BF16), wro_ref[...])
    conv_out = _dot(bgy.astype(BF16), wco_ref[...])
    merged = _sigmoid(ga) * ret_out + _sigmoid(gb) * conv_out
    return x + _dot(merged.astype(BF16), wo_ref[...])


def _swiglu(gates, ups):
    return jnp.concatenate([_silu(a) * b for a, b in zip(gates, ups)], axis=1).astype(BF16)


def _ffn_block(x, g_ref, wgu_refs, wd_ref):
    hn = _rms_f32(x, g_ref[...]).astype(BF16)
    n = len(wgu_refs) // 2
    gates = [_dot(hn, w[...]) for w in wgu_refs[:n]]
    ups = [_dot(hn, w[...]) for w in wgu_refs[n:]]
    return x + _dot(_swiglu(gates, ups), wd_ref[...])


def _log_gamma():
    return np.log1p(-np.exp2(-5.0 - np.arange(HEADS, dtype=np.float64)))


def _const(t):
    return jnp.asarray(np.asarray(t, dtype=np.float32))


def _rope_tables(pos):
    half = DH // 2
    inv = np.power(ROPE_BASE, -np.arange(half, dtype=np.float64) / half)
    ang = np.asarray(pos, dtype=np.float64)[:, None] * inv[None, :]
    cos, sin = np.cos(ang), np.sin(ang)
    cosf = np.concatenate([cos, cos], axis=-1)
    sinf = np.concatenate([-sin, sin], axis=-1)
    scale = DH ** -0.5
    return tuple(_const(t) for t in (cosf, sinf, cosf * scale, sinf * scale))


def _decay_tables(L):
    log_g = _log_gamma()
    idx = np.arange(L, dtype=np.float64)
    diff = idx[:, None] - idx[None, :]
    dmask = np.where(diff >= 0, np.exp(log_g[:, None, None] * np.maximum(diff, 0.0)[None]), 0.0)
    qdec = np.exp(log_g[:, None] * (idx + 1.0)[None])
    kdec = np.exp(log_g[:, None] * (L - 1.0 - idx)[None])
    gl = np.exp(log_g * L)
    lanes = lambda t: np.repeat(t.T, DH, axis=1)
    return tuple(_const(t) for t in (dmask, lanes(qdec), lanes(kdec), np.repeat(gl, DH)[None, :], kdec[:, None, :]))


def _mixer_main_kernel(x_ref, s0_ref, c0_ref, g_ref, *refs):
    n_groups = N_IN // D_MODEL
    wq_ref, wk_ref, wv_ref, wg_ref, wbg_ref, wcg_ref, whc_ref, wga_ref, wgb_ref = refs[:n_groups]
    (cw_ref, wro_ref, wco_ref, wo_ref, cq_ref, sq_ref, ck_ref, sk_ref, dmask_ref, qdec_ref, kdec_ref, gl_ref,
     wgu32_ref, wd32_ref, h_ref, s_ref, c_ref, wgub_ref, wdb_ref, q_s, kt_s, kdt_s, v_s, o_s) = refs[n_groups:]
    bg = x_ref.shape[0]
    m = bg * CHUNK

    wgub_ref[...] = wgu32_ref[...].astype(BF16)
    step = pl.program_id(0) * pl.num_programs(1) + pl.program_id(1)

    @pl.when(step < D_FF // wd32_ref.shape[0])
    def _():
        wdb_ref[...] = wd32_ref[...].astype(BF16)

    @pl.when(pl.program_id(1) == 0)
    def _():
        for b in range(bg):
            s_ref[b] = s0_ref[...]
            c_ref[b] = c0_ref[...]

    x = x_ref[...].reshape(m, D_MODEL)
    hn = _rms_f32(x, g_ref[...]).astype(BF16)

    cq, sq, ck, sk = cq_ref[...], sq_ref[...], ck_ref[...], sk_ref[...]
    q = _dot(hn, wq_ref[...])
    k = _dot(hn, wk_ref[...])
    v_s[...] = _dot(hn, wv_ref[...]).astype(BF16)
    for b in range(bg):
        r = slice(b * CHUNK, (b + 1) * CHUNK)
        for h in range(HEADS):
            cl = slice(h * DH, (h + 1) * DH)
            q_s[r, cl] = _rotary(q[r, cl], cq, sq).astype(BF16)
            krt = _rotary(k[r, cl], ck, sk).T
            kt_s[b * HEADS + h] = krt.astype(BF16)
            kdt_s[b * HEADS + h] = (krt * kdec_ref[h]).astype(BF16)

    for b in range(bg):
        r = slice(b * CHUNK, (b + 1) * CHUNK)
        for h in range(HEADS):
            cl = slice(h * DH, (h + 1) * DH)
            o, s_new = _retention_head_paired(q_s[r, cl], kt_s[b * HEADS + h], kdt_s[b * HEADS + h], v_s[r, cl],
                                              s_ref[b, h], dmask_ref[h], qdec_ref[:, cl], gl_ref[:, cl])
            s_ref[b, h] = s_new
            o_s[r, cl] = _group_norm(o)

    g = _dot(hn, wg_ref[...])
    gated = _silu(g) * o_s[...]

    bgate = _dot(hn, wbg_ref[...])
    u = _dot(hn, wcg_ref[...]) * _dot(hn, whc_ref[...])
    rows = lax.broadcasted_iota(jnp.int32, (CHUNK, D_MODEL), 0)
    cw = cw_ref[...]
    ys = []
    for b in range(bg):
        ub = u[b * CHUNK:(b + 1) * CHUNK]
        ys.append(_short_conv(ub, c_ref[b], cw, rows))
        c_ref[b] = ub[CHUNK - (CONV_W - 1):, :]
    bgy = bgate * jnp.concatenate(ys, axis=0)

    ga = _dot(hn, wga_ref[...])
    gb = _dot(hn, wgb_ref[...])
    out = _out_proj(x, gated, bgy, ga, gb, wro_ref, wco_ref, wo_ref)
    h_ref[...] = out.reshape(bg, CHUNK, D_MODEL)


def _mixer_main(x, s0, c0, layer, g, win, cw, wro, wco, wo, rope, decay, wgu32, wd32):
    nb, seq, _ = x.shape
    bg = MAIN_BG
    cq, sq, ck, sk = rope
    dmask, qdec, _, gl, kdec = decay
    m = bg * CHUNK
    n_chunks = seq // CHUNK
    steps = (nb // bg) * n_chunks
    slab = D_MODEL // steps
    wd_slab = 16 * (-(-D_FF // (16 * steps)))
    while D_FF % wd_slab:
        wd_slab += 16
    wd_last = D_FF // wd_slab - 1
    assert slab * steps == D_MODEL and slab % 16 == 0 and wd_last < steps

    def wd_block(i, c):
        return jnp.minimum(i * n_chunks + c, wd_last)

    rope_spec = pl.BlockSpec((CHUNK, DH), lambda i, c: (c, 0))
    return pl.pallas_call(
        _mixer_main_kernel,
        grid=(nb // bg, n_chunks),
        in_specs=[
            pl.BlockSpec((bg, CHUNK, D_MODEL), lambda i, c: (i, c, 0)),
            _resident_layer((HEADS, DH, DH), 0),
            _resident_layer((CONV_W - 1, D_MODEL), 0),
            _resident_layer((1, D_MODEL), layer),
            *_resident_col_parts(D_MODEL, D_MODEL, N_IN // D_MODEL),
            _resident_layer((CONV_W, D_MODEL), layer),
            _resident_layer((D_MODEL, D_MODEL), 0),
            _resident_layer((D_MODEL, D_MODEL), 0),
            _resident_layer((D_MODEL, D_MODEL), 0),
            rope_spec, rope_spec, rope_spec, rope_spec,
            _resident((HEADS, CHUNK, CHUNK)),
            _resident((CHUNK, D_MODEL)),
            _resident((HEADS, 1, CHUNK)),
            _resident((1, D_MODEL)),
            pl.BlockSpec((None, slab, 2 * D_FF), lambda i, c: (layer, i * n_chunks + c, 0)),
            pl.BlockSpec((None, wd_slab, D_MODEL), lambda i, c: (layer, wd_block(i, c), 0)),
        ],
        out_specs=[
            pl.BlockSpec((bg, CHUNK, D_MODEL), lambda i, c: (i, c, 0)),
            pl.BlockSpec((bg, HEADS, DH, DH), lambda i, c: (i, 0, 0, 0)),
            pl.BlockSpec((bg, CONV_W - 1, D_MODEL), lambda i, c: (i, 0, 0)),
            pl.BlockSpec((None, slab, 2 * D_FF), lambda i, c: (0, i * n_chunks + c, 0)),
            pl.BlockSpec((None, wd_slab, D_MODEL), lambda i, c: (0, wd_block(i, c), 0)),
        ],
        out_shape=[
            jax.ShapeDtypeStruct(x.shape, F32),
            jax.ShapeDtypeStruct((nb, HEADS, DH, DH), F32),
            jax.ShapeDtypeStruct((nb, CONV_W - 1, D_MODEL), F32),
            jax.ShapeDtypeStruct((1, D_MODEL, 2 * D_FF), BF16),
            jax.ShapeDtypeStruct((1, D_FF, D_MODEL), BF16),
        ],
        scratch_shapes=[
            pltpu.VMEM((m, D_MODEL), BF16),
            pltpu.VMEM((bg * HEADS, DH, CHUNK), BF16),
            pltpu.VMEM((bg * HEADS, DH, CHUNK), BF16),
            pltpu.VMEM((m, D_MODEL), BF16),
            pltpu.VMEM((m, D_MODEL), F32),
        ],
        compiler_params=_params(("arbitrary", "arbitrary")),
        name="mixer_main",
    )(x, s0, c0, g, *([win] * (N_IN // D_MODEL)), cw, wro, wco, wo, cq, sq, ck, sk, dmask, qdec, kdec, gl,
      wgu32, wd32)


def _round_square_slabs(step, n_slabs, sq_refs, sb_refs):
    @pl.when(step < n_slabs)
    def _():
        for src, dst in zip(sq_refs, sb_refs):
            dst[...] = src[...].astype(BF16)


def _square_slab_specs(layer, slab, n_slabs):
    slab_in = pl.BlockSpec((None, slab, D_MODEL), lambda j: (layer, jnp.minimum(j, n_slabs - 1), 0))
    slab_out = pl.BlockSpec((None, slab, D_MODEL), lambda j: (0, jnp.minimum(j, n_slabs - 1), 0))
    return [slab_in] * 3, [slab_out] * 3, [jax.ShapeDtypeStruct((1, D_MODEL, D_MODEL), BF16)] * 3


def _proj_first_kernel(xs_ref, xm_ref, g_ref, w32_ref, sq0_ref, sq1_ref, sq2_ref,
                       o_ref, wb_ref, sb0_ref, sb1_ref, sb2_ref, xcat_ref, hn_s, *, n_slabs):
    j = pl.program_id(0)
    n_samp = xs_ref.shape[0]

    @pl.when(j == 0)
    def _():
        xcat_ref[:n_samp, :] = xs_ref[...]
        xcat_ref[n_samp:, :] = xm_ref[...]
        hn_s[:n_samp, :] = _rms_f32(xs_ref[...], g_ref[...]).astype(BF16)
        hn_s[n_samp:, :] = _rms_f32(xm_ref[...], g_ref[...]).astype(BF16)

    _round_square_slabs(j, n_slabs, (sq0_ref, sq1_ref, sq2_ref), (sb0_ref, sb1_ref, sb2_ref))
    wb = w32_ref[...].astype(BF16)
    wb_ref[...] = wb
    o_ref[...] = _dot(hn_s[...], wb)


def _proj_first(xs, xm, layer, g, win32, squares32):
    rows = xs.shape[0] + xm.shape[0]
    n_col = N_IN // D_MODEL
    slab = 128
    n_slabs = D_MODEL // slab
    assert n_slabs <= n_col
    sq_in, sq_out, sq_shapes = _square_slab_specs(layer, slab, n_slabs)
    return pl.pallas_call(
        functools.partial(_proj_first_kernel, n_slabs=n_slabs),
        grid=(n_col,),
        in_specs=[_resident(xs.shape), _resident(xm.shape), _resident_layer((1, D_MODEL), layer),
                  pl.BlockSpec((None, D_MODEL, D_MODEL), lambda j: (layer, 0, j))] + sq_in,
        out_specs=[pl.BlockSpec((rows, D_MODEL), lambda j: (0, j)),
                   pl.BlockSpec((None, D_MODEL, D_MODEL), lambda j: (0, 0, j))] + sq_out
        + [pl.BlockSpec((rows, D_MODEL), lambda j: (0, 0))],
        out_shape=[jax.ShapeDtypeStruct((rows, N_IN), F32), jax.ShapeDtypeStruct((1, D_MODEL, N_IN), BF16)]
        + sq_shapes + [jax.ShapeDtypeStruct((rows, D_MODEL), F32)],
        scratch_shapes=[pltpu.VMEM((rows, D_MODEL), BF16)],
        compiler_params=_params(("arbitrary",)),
        name="proj_short_first",
    )(xs, xm, g, win32, *squares32)


def _proj_kernel(x_ref, g_ref, win_ref, sq0_ref, sq1_ref, sq2_ref, o_ref, sb0_ref, sb1_ref, sb2_ref, *, n_slabs):
    _round_square_slabs(pl.program_id(0), n_slabs, (sq0_ref, sq1_ref, sq2_ref), (sb0_ref, sb1_ref, sb2_ref))
    hn = _rms_f32(x_ref[...], g_ref[...]).astype(BF16)
    o_ref[...] = _dot(hn, win_ref[...])


def _proj(x, layer, g, win, squares32, tm):
    rows = x.shape[0]
    steps = rows // tm
    slab = 256
    n_slabs = D_MODEL // slab
    assert n_slabs <= steps and steps * tm == rows
    sq_in, sq_out, sq_shapes = _square_slab_specs(layer, slab, n_slabs)
    return pl.pallas_call(
        functools.partial(_proj_kernel, n_slabs=n_slabs),
        grid=(steps,),
        in_specs=[pl.BlockSpec((tm, D_MODEL), lambda i: (i, 0)), _resident_layer((1, D_MODEL), layer),
                  _resident_layer((D_MODEL, N_IN), 0)] + sq_in,
        out_specs=[pl.BlockSpec((tm, N_IN), lambda i: (i, 0))] + sq_out,
        out_shape=[jax.ShapeDtypeStruct((rows, N_IN), F32)] + sq_shapes,
        compiler_params=_params(("arbitrary",)),
        name="proj_short",
    )(x, g, win, *squares32)


def _ret_short_first(p_ref, s_ref, tab_refs, sb, L, sink):
    cq_ref, sq_ref, ck_ref, sk_ref, dmask_ref, qdec_ref, kdec_ref = tab_refs
    cq, sq, ck, sk = cq_ref[...], sq_ref[...], ck_ref[...], sk_ref[...]
    staged = []
    for b in range(sb):
        r = slice(b * L, (b + 1) * L)
        for h in range(HEADS):
            cl = slice(h * DH, (h + 1) * DH)
            q = _rotary(p_ref[r, C_Q + h * DH:C_Q + (h + 1) * DH], cq, sq).astype(BF16)
            k = _rotary(p_ref[r, C_K + h * DH:C_K + (h + 1) * DH], ck, sk)
            v = p_ref[r, C_V + h * DH:C_V + (h + 1) * DH]
            s = s_ref[b, h]
            scores = (_dot_nt(q, k.astype(BF16)) * dmask_ref[h]).astype(BF16)
            cross = _dot(q, s.astype(BF16)) * qdec_ref[:, cl]
            sink(b, h, r, cl, s, k * kdec_ref[:, cl], v)
            staged.append((scores, cross, v.astype(BF16)))
    return staged


def _ret_short_second(staged, p_ref, c_ref, cw_ref, gated_ref, bgy_ref, cn_ref, sb, L):
    rows = lax.broadcasted_iota(jnp.int32, (L, D_MODEL), 0)
    cw = cw_ref[...]
    for b in range(sb):
        r = slice(b * L, (b + 1) * L)
        for h in range(HEADS):
            cl = slice(h * DH, (h + 1) * DH)
            scores, cross, v = staged[b * HEADS + h]
            o = _dot(scores, v) + cross
            g = p_ref[r, C_G + h * DH:C_G + (h + 1) * DH]
            gated_ref[r, cl] = _silu(g) * _group_norm(o)
        u = p_ref[r, C_CG:C_CG + D_MODEL] * p_ref[r, C_HC:C_HC + D_MODEL]
        y = _short_conv(u, c_ref[b], cw, rows)
        bgy_ref[r, :] = p_ref[r, C_BG:C_BG + D_MODEL] * y
        cn_ref[b] = u[L - (CONV_W - 1):, :]


def _state_sink(sn_ref, gl_ref, layer=None):
    def sink(b, h, r, cl, s, kd, v):
        s_new = gl_ref[:, cl] * s + _dot_tn(kd.astype(BF16), v.astype(BF16))
        if layer is None:
            sn_ref[b, h] = s_new
        else:
            sn_ref[layer, b, h] = s_new
    return sink


def _ret_meta_kernel(p_ref, s_ref, c_ref, cw_ref, cq_ref, sq_ref, ck_ref, sk_ref, dmask_ref, qdec_ref, kdec_ref,
                     gl_ref, gated_ref, bgy_ref, cn_ref, sn_ref, *, L):
    sb = c_ref.shape[0]
    tabs = (cq_ref, sq_ref, ck_ref, sk_ref, dmask_ref, qdec_ref, kdec_ref)
    staged = _ret_short_first(p_ref, s_ref, tabs, sb, L, _state_sink(sn_ref, gl_ref))
    _ret_short_second(staged, p_ref, c_ref, cw_ref, gated_ref, bgy_ref, cn_ref, sb, L)


def _short_table_specs(L):
    return [_resident((L, DH))] * 4 + [_resident((HEADS, L, L)), _resident((L, D_MODEL)), _resident((L, D_MODEL)),
                                       _resident((1, D_MODEL))]


def _ret_meta(proj, row0, state, cprev, layer, cw, rope, decay, nseq, L):
    dmask, qdec, kdec, gl, _ = decay
    rows = nseq * L
    row_spec = pl.BlockSpec((rows, D_MODEL), lambda i: (0, 0))
    row_shape = jax.ShapeDtypeStruct((rows, D_MODEL), F32)
    return pl.pallas_call(
        functools.partial(_ret_meta_kernel, L=L),
        grid=(1,),
        in_specs=[
            pl.BlockSpec((rows, C_GA), lambda i: (row0 // rows, 0)),
            _resident((nseq, HEADS, DH, DH)),
            _resident((nseq, CONV_W - 1, D_MODEL)),
            _resident_layer((CONV_W, D_MODEL), layer),
        ] + _short_table_specs(L),
        out_specs=[row_spec, row_spec, pl.BlockSpec((nseq, CONV_W - 1, D_MODEL), lambda i: (0, 0, 0)),
                   pl.BlockSpec((nseq, HEADS, DH, DH), lambda i: (0, 0, 0, 0))],
        out_shape=[row_shape, row_shape, jax.ShapeDtypeStruct((nseq, CONV_W - 1, D_MODEL), F32),
                   jax.ShapeDtypeStruct((nseq, HEADS, DH, DH), F32)],
        compiler_params=_params(("arbitrary",)),
        name="ret_meta",
    )(proj, state, cprev, cw, *rope, dmask, qdec, kdec, gl)


def _out_ffn_kernel(x_ref, gated_ref, bgy_ref, ga_ref, gb_ref, wro_ref, wco_ref, wo_ref, g_ref, *refs, final):
    wgu_refs = refs[:FFN_PARTS]
    wd_ref, fg_ref, o_ref = refs[FFN_PARTS:]
    h = _out_proj(x_ref[...], gated_ref[...], bgy_ref[...], ga_ref[...], gb_ref[...], wro_ref, wco_ref, wo_ref)
    y = _ffn_block(h, g_ref, wgu_refs, wd_ref)
    if final:
        y = _rms_f32(y, fg_ref[...])
    o_ref[...] = y


def _out_ffn(x, gated, bgy, proj, layer, wro, wco, wo, g, wgu, wd, fg, final, rows, tm):
    row_spec = pl.BlockSpec((tm, D_MODEL), lambda i: (i, 0))
    w_spec = _resident_layer((D_MODEL, D_MODEL), 0)
    return pl.pallas_call(
        functools.partial(_out_ffn_kernel, final=final),
        grid=(rows // tm,),
        in_specs=[row_spec, row_spec, row_spec,
                  pl.BlockSpec((tm, D_MODEL), lambda i: (i, C_GA // D_MODEL)),
                  pl.BlockSpec((tm, D_MODEL), lambda i: (i, C_GB // D_MODEL)),
                  w_spec, w_spec, w_spec,
                  _resident_layer((1, D_MODEL), layer),
                  *_resident_col_parts(D_MODEL, 2 * D_FF // FFN_PARTS, FFN_PARTS),
                  _resident_layer((D_FF, D_MODEL), 0), _resident((1, D_MODEL))],
        out_specs=row_spec,
        out_shape=jax.ShapeDtypeStruct((rows, D_MODEL), F32),
        compiler_params=_params(("arbitrary",)),
        name="out_ffn_short",
    )(x, gated, bgy, proj, proj, wro, wco, wo, g, *([wgu] * FFN_PARTS), wd, fg)


def _ffn_main_kernel(x_ref, g_ref, *refs, final, n_prev, L):
    wgu_refs = refs[:FFN_PARTS]
    (wd_ref, fg_ref, p_ref, s_ref, c_ref, cw_ref, cq_ref, sq_ref, ck_ref, sk_ref, dmask_ref, qdec_ref, kdec_ref,
     gl_ref) = refs[FFN_PARTS:FFN_PARTS + 14]
    rest = refs[FFN_PARTS + 14:]
    sb = c_ref.shape[0]
    tabs = (cq_ref, sq_ref, ck_ref, sk_ref, dmask_ref, qdec_ref, kdec_ref)

    if final:
        n_main = n_prev + 1
        prev, mains, outs = rest[:3 * n_prev], rest[3 * n_prev:3 * n_prev + n_main], rest[3 * n_prev + n_main:]
        y_ref, gated_ref, bgy_ref, cn_ref, sn_ref, mstack_ref = outs
        for l in range(n_main):
            mstack_ref[l] = mains[l][...]
        sink = _state_sink(sn_ref, gl_ref, n_prev)
    else:
        wn32_ref, y_ref, gated_ref, bgy_ref, cn_ref, kd_ref, v_ref, wnb_ref = rest
        wnb_ref[...] = wn32_ref[...].astype(BF16)

        def sink(b, h, r, cl, s, kd, v):
            kd_ref[r, cl] = kd
            v_ref[r, cl] = v

    x = x_ref[...]
    hn = _rms_f32(x, g_ref[...]).astype(BF16)
    gates = [_dot(hn, w[...]) for w in wgu_refs[:FFN_PARTS // 2]]
    if final:
        for l in range(n_prev):
            so_ref, kdo_ref, vo_ref = prev[3 * l:3 * l + 3]
            upd = _state_sink(sn_ref, gl_ref, l)
            for b in range(sb):
                r = slice(b * L, (b + 1) * L)
                for h in range(HEADS):
                    cl = slice(h * DH, (h + 1) * DH)
                    upd(b, h, r, cl, so_ref[b, h], kdo_ref[r, cl], vo_ref[r, cl])
    staged = _ret_short_first(p_ref, s_ref, tabs, sb, L, sink)
    ups = [_dot(hn, w[...]) for w in wgu_refs[FFN_PARTS // 2:]]
    _ret_short_second(staged, p_ref, c_ref, cw_ref, gated_ref, bgy_ref, cn_ref, sb, L)
    y = x + _dot(_swiglu(gates, ups), wd_ref[...])
    if final:
        y = _rms_f32(y, fg_ref[...])
    y_ref[...] = y


def _ffn_main(x, layer, g, wgu, wd, fg, final, tm, proj, state, cconv, cw, rope, decay, L, kds, vs, main_states,
              win32):
    rows = x.shape[0]
    steps = rows // tm
    depth, ns = state.shape[:2]
    sb = ns // steps
    assert sb * steps == ns and (sb * L) % 8 == 0
    n_prev = len(kds) if final else 0
    dmask, qdec, kdec, gl, _ = decay
    n_samp = ns * L
    row_spec = pl.BlockSpec((tm, D_MODEL), lambda i: (i, 0))
    srow_spec = pl.BlockSpec((sb * L, D_MODEL), lambda i: (i, 0))
    srow_shape = jax.ShapeDtypeStruct((n_samp, D_MODEL), F32)

    def state_spec(l):
        return pl.BlockSpec((None, sb, HEADS, DH, DH), lambda i: (l, i, 0, 0, 0))

    stream_specs = [
        row_spec,
        pl.BlockSpec((sb * L, C_GA), lambda i: (i, 0)),
        state_spec(layer),
        pl.BlockSpec((None, sb, CONV_W - 1, D_MODEL), lambda i: (layer, i, 0, 0)),
    ]
    stream_args = [x, proj, state, cconv]
    for l in range(n_prev):
        stream_specs += [state_spec(l), srow_spec, srow_spec]
        stream_args += [state, kds[l], vs[l]]
    out_specs = [row_spec, srow_spec, srow_spec, pl.BlockSpec((sb, CONV_W - 1, D_MODEL), lambda i: (i, 0, 0))]
    out_shape = [jax.ShapeDtypeStruct(x.shape, F32), srow_shape, srow_shape,
                 jax.ShapeDtypeStruct((ns, CONV_W - 1, D_MODEL), F32)]
    if final:
        assert n_prev == depth - 1 and len(main_states) == depth
        out_specs.append(pl.BlockSpec((depth, sb, HEADS, DH, DH), lambda i: (0, i, 0, 0, 0)))
        out_shape.append(jax.ShapeDtypeStruct(state.shape, F32))
        n_mat = main_states[0].shape[0] * HEADS
        mb = n_mat // steps
        assert mb * steps == n_mat
        stream_specs += [pl.BlockSpec((mb, DH, DH), lambda i: (i, 0, 0))] * depth
        stream_args += [s.reshape(n_mat, DH, DH) for s in main_states]
        out_specs.append(pl.BlockSpec((depth, mb, DH, DH), lambda i: (0, i, 0, 0)))
        out_shape.append(jax.ShapeDtypeStruct((depth, n_mat, DH, DH), F32))
    else:
        slab = D_MODEL // steps
        assert slab * steps == D_MODEL and slab % 16 == 0
        stream_specs.append(pl.BlockSpec((None, slab, N_IN), lambda i: (layer + 1, i, 0)))
        stream_args.append(win32)
        out_specs += [srow_spec, srow_spec, pl.BlockSpec((None, slab, N_IN), lambda i: (0, i, 0))]
        out_shape += [srow_shape, srow_shape, jax.ShapeDtypeStruct((1, D_MODEL, N_IN), BF16)]

    resident_args = [g, wgu, wd, fg, cw, *rope, dmask, qdec, kdec, gl]
    n_res, n_in = len(resident_args), len(stream_args)
    part = 2 * D_FF // FFN_PARTS

    def outer(*refs):
        g_ref, wgu_ref, wd_ref, fg_ref, cw_ref = refs[:5]
        table_refs = refs[5:n_res]
        hbm_in, hbm_out = refs[n_res:n_res + n_in], refs[n_res + n_in:]
        wgu_parts = [wgu_ref.at[0, :, pl.ds(q * part, part)] for q in range(FFN_PARTS)]

        def step(*blocks):
            x_ref, p_ref, s_ref, c_ref = blocks[:4]
            _ffn_main_kernel(x_ref, g_ref.at[layer], *wgu_parts, wd_ref.at[0], fg_ref, p_ref, s_ref, c_ref,
                             cw_ref.at[layer], *table_refs, *blocks[4:], final=final, n_prev=n_prev, L=L)

        pltpu.emit_pipeline(step, grid=(steps,), in_specs=stream_specs, out_specs=out_specs)(*hbm_in, *hbm_out)

    vmem = pl.BlockSpec(memory_space=pltpu.VMEM)
    hbm = pl.BlockSpec(memory_space=pl.ANY)
    return pl.pallas_call(
        outer,
        in_specs=[vmem] * n_res + [hbm] * n_in,
        out_specs=[hbm] * len(out_shape),
        out_shape=out_shape,
        compiler_params=pltpu.CompilerParams(vmem_limit_bytes=VMEM_LIMIT_BYTES),
        name="ffn_main",
    )(*resident_args, *stream_args)


def kernel(x_prompt, x_sample, state_ret, state_conv, meta_tokens, norm_mix_g, w_in, conv_w, w_ret_o,
           w_conv_o, w_o, norm_ffn_g, w_gate_up, w_down, final_norm_g):
    depth = w_in.shape[0]
    nb, seq, _ = x_prompt.shape
    ns, ls, _ = x_sample.shape
    n_samp = ns * ls
    n_short = n_samp + N_META
    short_tm = n_short // 5
    assert short_tm * 5 == n_short and short_tm % 8 == 0 and n_samp % N_META == 0

    gm = norm_mix_g.reshape(depth, 1, D_MODEL)
    gf = norm_ffn_g.reshape(depth, 1, D_MODEL)
    fg = final_norm_g.reshape(1, D_MODEL)

    rope_meta = _rope_tables(np.arange(N_META))
    rope_main = _rope_tables(N_META + np.arange(seq))
    rope_samp = _rope_tables(PAST_LEN + np.arange(ls))
    dec_meta, dec_main, dec_samp = _decay_tables(N_META), _decay_tables(CHUNK), _decay_tables(ls)

    h_main = x_prompt
    zero_s = jnp.zeros((1, HEADS, DH, DH), F32)
    zero_c = jnp.zeros((1, CONV_W - 1, D_MODEL), F32)
    squares32 = (w_ret_o, w_conv_o, w_o)

    s_p, c_p, c_s, kds, vs = [], [], [], [], []
    y_samp = s_s = s_p_stacked = h_short = win_b = None
    for l in range(depth):
        last = l == depth - 1
        if l == 0:
            proj, win_b, wro_b, wco_b, wo_b, h_short = _proj_first(
                x_sample.reshape(n_samp, D_MODEL), meta_tokens.astype(F32), l, gm, w_in, squares32)
        else:
            proj, wro_b, wco_b, wo_b = _proj(h_short, l, gm, win_b, squares32, short_tm)
        gated_m, bgy_m, c_m, s_m = _ret_meta(proj, n_samp, zero_s, zero_c, l, conv_w, rope_meta, dec_meta, 1, N_META)

        h_main, s_l, c_l, wgu_b, wd_b = _mixer_main(h_main, s_m, c_m, l, gm, win_b, conv_w, wro_b, wco_b, wo_b,
                                                    rope_main, dec_main, w_gate_up, w_down)
        s_p.append(s_l)
        c_p.append(c_l)
        res = _ffn_main(h_main.reshape(nb * seq, D_MODEL), l, gf, wgu_b, wd_b, fg, last, MAIN_FFN_TM,
                        proj, state_ret, state_conv, conv_w, rope_samp, dec_samp, ls, kds, vs, s_p, w_in)
        h_main, gated_s, bgy_s, c_l = res[:4]
        h_main = h_main.reshape(nb, seq, D_MODEL)
        c_s.append(c_l)
        if last:
            s_s = res[4]
            s_p_stacked = res[5].reshape(depth, nb, HEADS, DH, DH)
        else:
            kds.append(res[4])
            vs.append(res[5])
            win_b = res[6]

        if last:
            y_samp = _out_ffn(h_short, gated_s, bgy_s, proj, l, wro_b, wco_b, wo_b, gf, wgu_b, wd_b, fg, True,
                              n_samp, 256)
        else:
            gated = jnp.concatenate([gated_s, gated_m], axis=0)
            bgy = jnp.concatenate([bgy_s, bgy_m], axis=0)
            h_short = _out_ffn(h_short, gated, bgy, proj, l, wro_b, wco_b, wo_b, gf, wgu_b, wd_b, fg, False,
                               n_short, short_tm)

    return (h_main, y_samp.reshape(ns, ls, D_MODEL), s_p_stacked, jnp.stack(c_p), s_s, jnp.stack(c_s))
```

```python
import functools

import numpy as np
import jax
import jax.numpy as jnp
from jax import lax
from jax.experimental import pallas as pl
from jax.experimental.pallas import tpu as pltpu

D_MODEL = 1024
N_META = 16
HEADS = 8
DH = D_MODEL // HEADS
CHUNK = 128
ROPE_BASE = 10000.0
CONV_W = 3
D_FF = ((8 * D_MODEL + 3 * 256 - 1) // (3 * 256)) * 256
EPS = 1e-6
PAST_LEN = 16384
N_IN = 9 * D_MODEL
C_Q, C_K, C_V, C_G, C_BG, C_CG, C_HC, C_GA, C_GB = (i * D_MODEL for i in range(9))

F32 = jnp.float32
BF16 = jnp.bfloat16

VMEM_LIMIT_BYTES = 58 * 1024 * 1024

MAIN_BG = 4
MAIN_FFN_TM = 512
SHORT_TM = 256
FFN_PARTS = 2
assert (2 * D_FF // FFN_PARTS) % 256 == 0 and FFN_PARTS % 2 == 0


def _resident(shape):
    nd = len(shape)
    return pl.BlockSpec(shape, lambda *_: (0,) * nd, pipeline_mode=pl.Buffered(1))


def _resident_layer(shape, layer):
    nd = len(shape)
    return pl.BlockSpec((None,) + tuple(shape), lambda *_: (layer,) + (0,) * nd, pipeline_mode=pl.Buffered(1))


def _resident_col_parts(rows, width, n_parts):
    return [pl.BlockSpec((None, rows, width), functools.partial(lambda *_, q: (0, 0, q), q=q),
                         pipeline_mode=pl.Buffered(1)) for q in range(n_parts)]


def _params(sem):
    return pltpu.CompilerParams(dimension_semantics=sem, vmem_limit_bytes=VMEM_LIMIT_BYTES)


def _dot(a, b):
    return jnp.dot(a, b, preferred_element_type=F32)


def _dot_nt(a, b):
    return lax.dot_general(a, b, (((1,), (1,)), ((), ())), preferred_element_type=F32)


def _dot_tn(a, b):
    return lax.dot_general(a, b, (((0,), (0,)), ((), ())), preferred_element_type=F32)


def _rms_f32(x, g):
    return x * lax.rsqrt(jnp.mean(x * x, axis=-1, keepdims=True) + EPS) * g


def _sigmoid(x):
    return 1.0 / (1.0 + jnp.exp(-x))


def _silu(x):
    return x * _sigmoid(x)


def _rotary(t, cos, sin):
    return t * cos + pltpu.roll(t, DH // 2, 1) * sin


def _group_norm(o):
    mu = jnp.mean(o, axis=-1, keepdims=True)
    d = o - mu
    var = jnp.mean(d * d, axis=-1, keepdims=True)
    return d * lax.rsqrt(var + EPS)


def _retention_head_paired(q, kt, kdt, v, s, dmask, qdec, gl):
    L = q.shape[0]
    sc = _dot(q, jnp.concatenate([kt, s.astype(BF16)], axis=1))
    scores = (sc[:, :L] * dmask).astype(BF16)
    iu = _dot(jnp.concatenate([scores, kdt], axis=0), v)
    return iu[:L] + sc[:, L:] * qdec, gl * s + iu[L:]


def _short_conv(u, tail, cw, rows):
    r1 = pltpu.roll(u, 1, 0)
    r2 = pltpu.roll(u, 2, 0)
    t0, t1 = tail[0:1, :], tail[1:2, :]
    sh1 = jnp.where(rows == 0, t1, r1)
    sh2 = jnp.where(rows == 0, t0, jnp.where(rows == 1, t1, r2))
    return cw[0:1, :] * sh2 + cw[1:2, :] * sh1 + cw[2:3, :] * u


def _out_proj(x, gated, bgy, ga, gb, wro_ref, wco_ref, wo_ref):
    ret_out = _dot(gated.astype(BF16), wro_ref[...])
    conv_out = _dot(bgy.astype(BF16), wco_ref[...])
    merged = _sigmoid(ga) * ret_out + _sigmoid(gb) * conv_out
    return x + _dot(merged.astype(BF16), wo_ref[...])


def _swiglu(gates, ups):
    return jnp.concatenate([_silu(a) * b for a, b in zip(gates, ups)], axis=1).astype(BF16)


def _ffn_block(x, g_ref, wgu_refs, wd_ref):
    hn = _rms_f32(x, g_ref[...]).astype(BF16)
    n = len(wgu_refs) // 2
    gates = [_dot(hn, w[...]) for w in wgu_refs[:n]]
    ups = [_dot(hn, w[...]) for w in wgu_refs[n:]]
    return x + _dot(_swiglu(gates, ups), wd_ref[...])


def _log_gamma():
    return np.log1p(-np.exp2(-5.0 - np.arange(HEADS, dtype=np.float64)))


def _const(t):
    return jnp.asarray(np.asarray(t, dtype=np.float32))


def _rope_tables(pos):
    half = DH // 2
    inv = np.power(ROPE_BASE, -np.arange(half, dtype=np.float64) / half)
    ang = np.asarray(pos, dtype=np.float64)[:, None] * inv[None, :]
    cos, sin = np.cos(ang), np.sin(ang)
    cosf = np.concatenate([cos, cos], axis=-1)
    sinf = np.concatenate([-sin, sin], axis=-1)
    scale = DH ** -0.5
    return tuple(_const(t) for t in (cosf, sinf, cosf * scale, sinf * scale))


def _decay_tables(L):
    log_g = _log_gamma()
    idx = np.arange(L, dtype=np.float64)
    diff = idx[:, None] - idx[None, :]
    dmask = np.where(diff >= 0, np.exp(log_g[:, None, None] * np.maximum(diff, 0.0)[None]), 0.0)
    qdec = np.exp(log_g[:, None] * (idx + 1.0)[None])
    kdec = np.exp(log_g[:, None] * (L - 1.0 - idx)[None])
    gl = np.exp(log_g * L)
    lanes = lambda t: np.repeat(t.T, DH, axis=1)
    return tuple(_const(t) for t in (dmask, lanes(qdec), lanes(kdec), np.repeat(gl, DH)[None, :], kdec[:, None, :]))


def _mixer_main_kernel(x_ref, s0_ref, c0_ref, g_ref, *refs):
    n_groups = N_IN // D_MODEL
    wq_ref, wk_ref, wv_ref, wg_ref, wbg_ref, wcg_ref, whc_ref, wga_ref, wgb_ref = refs[:n_groups]
    (cw_ref, wro_ref, wco_ref, wo_ref, cq_ref, sq_ref, ck_ref, sk_ref, dmask_ref, qdec_ref, kdec_ref, gl_ref,
     wgu32_ref, wd32_ref, h_ref, s_ref, c_ref, wgub_ref, wdb_ref, q_s, kt_s, kdt_s, v_s, o_s) = refs[n_groups:]
    bg = x_ref.shape[0]
    m = bg * CHUNK

    wgub_ref[...] = wgu32_ref[...].astype(BF16)
    step = pl.program_id(0) * pl.num_programs(1) + pl.program_id(1)

    @pl.when(step < D_FF // wd32_ref.shape[0])
    def _():
        wdb_ref[...] = wd32_ref[...].astype(BF16)

    @pl.when(pl.program_id(1) == 0)
    def _():
        for b in range(bg):
            s_ref[b] = s0_ref[...]
            c_ref[b] = c0_ref[...]

    x = x_ref[...].reshape(m, D_MODEL)
    hn = _rms_f32(x, g_ref[...]).astype(BF16)

    cq, sq, ck, sk = cq_ref[...], sq_ref[...], ck_ref[...], sk_ref[...]
    q = _dot(hn, wq_ref[...])
    k = _dot(hn, wk_ref[...])
    v_s[...] = _dot(hn, wv_ref[...]).astype(BF16)
    for b in range(bg):
        r = slice(b * CHUNK, (b + 1) * CHUNK)
        for h in range(HEADS):
            cl = slice(h * DH, (h + 1) * DH)
            q_s[r, cl] = _rotary(q[r, cl], cq, sq).astype(BF16)
            krt = _rotary(k[r, cl], ck, sk).T
            kt_s[b * HEADS + h] = krt.astype(BF16)
            kdt_s[b * HEADS + h] = (krt * kdec_ref[h]).astype(BF16)

    for b in range(bg):
        r = slice(b * CHUNK, (b + 1) * CHUNK)
        for h in range(HEADS):
            cl = slice(h * DH, (h + 1) * DH)
            o, s_new = _retention_head_paired(q_s[r, cl], kt_s[b * HEADS + h], kdt_s[b * HEADS + h], v_s[r, cl],
                                              s_ref[b, h], dmask_ref[h], qdec_ref[:, cl], gl_ref[:, cl])
            s_ref[b, h] = s_new
            o_s[r, cl] = _group_norm(o)

    g = _dot(hn, wg_ref[...])
    gated = _silu(g) * o_s[...]

    bgate = _dot(hn, wbg_ref[...])
    u = _dot(hn, wcg_ref[...]) * _dot(hn, whc_ref[...])
    rows = lax.broadcasted_iota(jnp.int32, (CHUNK, D_MODEL), 0)
    cw = cw_ref[...]
    ys = []
    for b in range(bg):
        ub = u[b * CHUNK:(b + 1) * CHUNK]
        ys.append(_short_conv(ub, c_ref[b], cw, rows))
        c_ref[b] = ub[CHUNK - (CONV_W - 1):, :]
    bgy = bgate * jnp.concatenate(ys, axis=0)

    ga = _dot(hn, wga_ref[...])
    gb = _dot(hn, wgb_ref[...])
    out = _out_proj(x, gated, bgy, ga, gb, wro_ref, wco_ref, wo_ref)
    h_ref[...] = out.reshape(bg, CHUNK, D_MODEL)


def _mixer_main(x, s0, c0, layer, g, win, cw, wro, wco, wo, rope, decay, wgu32, wd32):
    nb, seq, _ = x.shape
    bg = MAIN_BG
    cq, sq, ck, sk = rope
    dmask, qdec, _, gl, kdec = decay
    m = bg * CHUNK
    n_chunks = seq // CHUNK
    steps = (nb // bg) * n_chunks
    slab = D_MODEL // steps
    wd_slab = 16 * (-(-D_FF // (16 * steps)))
    while D_FF % wd_slab:
        wd_slab += 16
    wd_last = D_FF // wd_slab - 1
    assert slab * steps == D_MODEL and slab % 16 == 0 and wd_last < steps

    def wd_block(i, c):
        return jnp.minimum(i * n_chunks + c, wd_last)

    rope_spec = pl.BlockSpec((CHUNK, DH), lambda i, c: (c, 0))
    return pl.pallas_call(
        _mixer_main_kernel,
        grid=(nb // bg, n_chunks),
        in_specs=[
            pl.BlockSpec((bg, CHUNK, D_MODEL), lambda i, c: (i, c, 0)),
            _resident_layer((HEADS, DH, DH), 0),
            _resident_layer((CONV_W - 1, D_MODEL), 0),
            _resident_layer((1, D_MODEL), layer),
            *_resident_col_parts(D_MODEL, D_MODEL, N_IN // D_MODEL),
            _resident_layer((CONV_W, D_MODEL), layer),
            _resident_layer((D_MODEL, D_MODEL), 0),
            _resident_layer((D_MODEL, D_MODEL), 0),
            _resident_layer((D_MODEL, D_MODEL), 0),
            rope_spec, rope_spec, rope_spec, rope_spec,
            _resident((HEADS, CHUNK, CHUNK)),
            _resident((CHUNK, D_MODEL)),
            _resident((HEADS, 1, CHUNK)),
            _resident((1, D_MODEL)),
            pl.BlockSpec((None, slab, 2 * D_FF), lambda i, c: (layer, i * n_chunks + c, 0)),
            pl.BlockSpec((None, wd_slab, D_MODEL), lambda i, c: (layer, wd_block(i, c), 0)),
        ],
        out_specs=[
            pl.BlockSpec((bg, CHUNK, D_MODEL), lambda i, c: (i, c, 0)),
            pl.BlockSpec((bg, HEADS, DH, DH), lambda i, c: (i, 0, 0, 0)),
            pl.BlockSpec((bg, CONV_W - 1, D_MODEL), lambda i, c: (i, 0, 0)),
            pl.BlockSpec((None, slab, 2 * D_FF), lambda i, c: (0, i * n_chunks + c, 0)),
            pl.BlockSpec((None, wd_slab, D_MODEL), lambda i, c: (0, wd_block(i, c), 0)),
        ],
        out_shape=[
            jax.ShapeDtypeStruct(x.shape, F32),
            jax.ShapeDtypeStruct((nb, HEADS, DH, DH), F32),
            jax.ShapeDtypeStruct((nb, CONV_W - 1, D_MODEL), F32),
            jax.ShapeDtypeStruct((1, D_MODEL, 2 * D_FF), BF16),
            jax.ShapeDtypeStruct((1, D_FF, D_MODEL), BF16),
        ],
        scratch_shapes=[
            pltpu.VMEM((m, D_MODEL), BF16),
            pltpu.VMEM((bg * HEADS, DH, CHUNK), BF16),
            pltpu.VMEM((bg * HEADS, DH, CHUNK), BF16),
            pltpu.VMEM((m, D_MODEL), BF16),
            pltpu.VMEM((m, D_MODEL), F32),
        ],
        compiler_params=_params(("arbitrary", "arbitrary")),
        name="mixer_main",
    )(x, s0, c0, g, *([win] * (N_IN // D_MODEL)), cw, wro, wco, wo, cq, sq, ck, sk, dmask, qdec, kdec, gl,
      wgu32, wd32)


def _round_square_slabs(step, n_slabs, sq_refs, sb_refs):
    @pl.when(step < n_slabs)
    def _():
        for src, dst in zip(sq_refs, sb_refs):
            dst[...] = src[...].astype(BF16)


def _square_slab_specs(layer, slab, n_slabs):
    slab_in = pl.BlockSpec((None, slab, D_MODEL), lambda j: (layer, jnp.minimum(j, n_slabs - 1), 0))
    slab_out = pl.BlockSpec((None, slab, D_MODEL), lambda j: (0, jnp.minimum(j, n_slabs - 1), 0))
    return [slab_in] * 3, [slab_out] * 3, [jax.ShapeDtypeStruct((1, D_MODEL, D_MODEL), BF16)] * 3


def _proj_first_kernel(xs_ref, xm_ref, g_ref, w32_ref, sq0_ref, sq1_ref, sq2_ref,
                       o_ref, wb_ref, sb0_ref, sb1_ref, sb2_ref, xcat_ref, hn_s, *, n_slabs):
    j = pl.program_id(0)
    n_samp = xs_ref.shape[0]

    @pl.when(j == 0)
    def _():
        xcat_ref[:n_samp, :] = xs_ref[...]
        xcat_ref[n_samp:, :] = xm_ref[...]
        hn_s[:n_samp, :] = _rms_f32(xs_ref[...], g_ref[...]).astype(BF16)
        hn_s[n_samp:, :] = _rms_f32(xm_ref[...], g_ref[...]).astype(BF16)

    _round_square_slabs(j, n_slabs, (sq0_ref, sq1_ref, sq2_ref), (sb0_ref, sb1_ref, sb2_ref))
    wb = w32_ref[...].astype(BF16)
    wb_ref[...] = wb
    o_ref[...] = _dot(hn_s[...], wb)


def _proj_first(xs, xm, layer, g, win32, squares32):
    rows = xs.shape[0] + xm.shape[0]
    n_col = N_IN // D_MODEL
    slab = 128
    n_slabs = D_MODEL // slab
    assert n_slabs <= n_col
    sq_in, sq_out, sq_shapes = _square_slab_specs(layer, slab, n_slabs)
    return pl.pallas_call(
        functools.partial(_proj_first_kernel, n_slabs=n_slabs),
        grid=(n_col,),
        in_specs=[_resident(xs.shape), _resident(xm.shape), _resident_layer((1, D_MODEL), layer),
                  pl.BlockSpec((None, D_MODEL, D_MODEL), lambda j: (layer, 0, j))] + sq_in,
        out_specs=[pl.BlockSpec((rows, D_MODEL), lambda j: (0, j)),
                   pl.BlockSpec((None, D_MODEL, D_MODEL), lambda j: (0, 0, j))] + sq_out
        + [pl.BlockSpec((rows, D_MODEL), lambda j: (0, 0))],
        out_shape=[jax.ShapeDtypeStruct((rows, N_IN), F32), jax.ShapeDtypeStruct((1, D_MODEL, N_IN), BF16)]
        + sq_shapes + [jax.ShapeDtypeStruct((rows, D_MODEL), F32)],
        scratch_shapes=[pltpu.VMEM((rows, D_MODEL), BF16)],
        compiler_params=_params(("arbitrary",)),
        name="proj_short_first",
    )(xs, xm, g, win32, *squares32)


def _proj_kernel(x_ref, g_ref, win_ref, sq0_ref, sq1_ref, sq2_ref, o_ref, sb0_ref, sb1_ref, sb2_ref, *, n_slabs):
    _round_square_slabs(pl.program_id(0), n_slabs, (sq0_ref, sq1_ref, sq2_ref), (sb0_ref, sb1_ref, sb2_ref))
    hn = _rms_f32(x_ref[...], g_ref[...]).astype(BF16)
    o_ref[...] = _dot(hn, win_ref[...])


def _proj(x, layer, g, win, squares32, tm):
    rows = x.shape[0]
    steps = rows // tm
    slab = 256
    n_slabs = D_MODEL // slab
    assert n_slabs <= steps and steps * tm == rows
    sq_in, sq_out, sq_shapes = _square_slab_specs(layer, slab, n_slabs)
    return pl.pallas_call(
        functools.partial(_proj_kernel, n_slabs=n_slabs),
        grid=(steps,),
        in_specs=[pl.BlockSpec((tm, D_MODEL), lambda i: (i, 0)), _resident_layer((1, D_MODEL), layer),
                  _resident_layer((D_MODEL, N_IN), 0)] + sq_in,
        out_specs=[pl.BlockSpec((tm, N_IN), lambda i: (i, 0))] + sq_out,
        out_shape=[jax.ShapeDtypeStruct((rows, N_IN), F32)] + sq_shapes,
        compiler_params=_params(("arbitrary",)),
        name="proj_short",
    )(x, g, win, *squares32)


def _ret_short_first(p_ref, s_ref, tab_refs, sb, L, sink):
    cq_ref, sq_ref, ck_ref, sk_ref, dmask_ref, qdec_ref, kdec_ref = tab_refs
    cq, sq, ck, sk = cq_ref[...], sq_ref[...], ck_ref[...], sk_ref[...]
    staged = []
    for b in range(sb):
        r = slice(b * L, (b + 1) * L)
        for h in range(HEADS):
            cl = slice(h * DH, (h + 1) * DH)
            q = _rotary(p_ref[r, C_Q + h * DH:C_Q + (h + 1) * DH], cq, sq).astype(BF16)
            k = _rotary(p_ref[r, C_K + h * DH:C_K + (h + 1) * DH], ck, sk)
            v = p_ref[r, C_V + h * DH:C_V + (h + 1) * DH]
            s = s_ref[b, h]
            scores = (_dot_nt(q, k.astype(BF16)) * dmask_ref[h]).astype(BF16)
            cross = _dot(q, s.astype(BF16)) * qdec_ref[:, cl]
            sink(b, h, r, cl, s, k * kdec_ref[:, cl], v)
            staged.append((scores, cross, v.astype(BF16)))
    return staged


def _ret_short_second(staged, p_ref, c_ref, cw_ref, gated_ref, bgy_ref, cn_ref, sb, L):
    rows = lax.broadcasted_iota(jnp.int32, (L, D_MODEL), 0)
    cw = cw_ref[...]
    for b in range(sb):
        r = slice(b * L, (b + 1) * L)
        for h in range(HEADS):
            cl = slice(h * DH, (h + 1) * DH)
            scores, cross, v = staged[b * HEADS + h]
            o = _dot(scores, v) + cross
            g = p_ref[r, C_G + h * DH:C_G + (h + 1) * DH]
            gated_ref[r, cl] = _silu(g) * _group_norm(o)
        u = p_ref[r, C_CG:C_CG + D_MODEL] * p_ref[r, C_HC:C_HC + D_MODEL]
        y = _short_conv(u, c_ref[b], cw, rows)
        bgy_ref[r, :] = p_ref[r, C_BG:C_BG + D_MODEL] * y
        cn_ref[b] = u[L - (CONV_W - 1):, :]


def _state_sink(sn_ref, gl_ref, layer=None):
    def sink(b, h, r, cl, s, kd, v):
        s_new = gl_ref[:, cl] * s + _dot_tn(kd.astype(BF16), v.astype(BF16))
        if layer is None:
            sn_ref[b, h] = s_new
        else:
            sn_ref[layer, b, h] = s_new
    return sink


def _ret_meta_kernel(p_ref, s_ref, c_ref, cw_ref, cq_ref, sq_ref, ck_ref, sk_ref, dmask_ref, qdec_ref, kdec_ref,
                     gl_ref, gated_ref, bgy_ref, cn_ref, sn_ref, *, L):
    sb = c_ref.shape[0]
    tabs = (cq_ref, sq_ref, ck_ref, sk_ref, dmask_ref, qdec_ref, kdec_ref)
    staged = _ret_short_first(p_ref, s_ref, tabs, sb, L, _state_sink(sn_ref, gl_ref))
    _ret_short_second(staged, p_ref, c_ref, cw_ref, gated_ref, bgy_ref, cn_ref, sb, L)


def _short_table_specs(L):
    return [_resident((L, DH))] * 4 + [_resident((HEADS, L, L)), _resident((L, D_MODEL)), _resident((L, D_MODEL)),
                                       _resident((1, D_MODEL))]


def _ret_meta(proj, row0, state, cprev, layer, cw, rope, decay, nseq, L):
    dmask, qdec, kdec, gl, _ = decay
    rows = nseq * L
    row_spec = pl.BlockSpec((rows, D_MODEL), lambda i: (0, 0))
    row_shape = jax.ShapeDtypeStruct((rows, D_MODEL), F32)
    return pl.pallas_call(
        functools.partial(_ret_meta_kernel, L=L),
        grid=(1,),
        in_specs=[
            pl.BlockSpec((rows, C_GA), lambda i: (row0 // rows, 0)),
            _resident((nseq, HEADS, DH, DH)),
            _resident((nseq, CONV_W - 1, D_MODEL)),
            _resident_layer((CONV_W, D_MODEL), layer),
        ] + _short_table_specs(L),
        out_specs=[row_spec, row_spec, pl.BlockSpec((nseq, CONV_W - 1, D_MODEL), lambda i: (0, 0, 0)),
                   pl.BlockSpec((nseq, HEADS, DH, DH), lambda i: (0, 0, 0, 0))],
        out_shape=[row_shape, row_shape, jax.ShapeDtypeStruct((nseq, CONV_W - 1, D_MODEL), F32),
                   jax.ShapeDtypeStruct((nseq, HEADS, DH, DH), F32)],
        compiler_params=_params(("arbitrary",)),
        name="ret_meta",
    )(proj, state, cprev, cw, *rope, dmask, qdec, kdec, gl)


def _out_ffn_kernel(x_ref, gated_ref, bgy_ref, ga_ref, gb_ref, wro_ref, wco_ref, wo_ref, g_ref, *refs, final):
    wgu_refs = refs[:FFN_PARTS]
    wd_ref, fg_ref, o_ref = refs[FFN_PARTS:]
    h = _out_proj(x_ref[...], gated_ref[...], bgy_ref[...], ga_ref[...], gb_ref[...], wro_ref, wco_ref, wo_ref)
    y = _ffn_block(h, g_ref, wgu_refs, wd_ref)
    if final:
        y = _rms_f32(y, fg_ref[...])
    o_ref[...] = y


def _out_ffn(x, gated, bgy, proj, layer, wro, wco, wo, g, wgu, wd, fg, final, rows, tm):
    row_spec = pl.BlockSpec((tm, D_MODEL), lambda i: (i, 0))
    w_spec = _resident_layer((D_MODEL, D_MODEL), 0)
    return pl.pallas_call(
        functools.partial(_out_ffn_kernel, final=final),
        grid=(rows // tm,),
        in_specs=[row_spec, row_spec, row_spec,
                  pl.BlockSpec((tm, D_MODEL), lambda i: (i, C_GA // D_MODEL)),
                  pl.BlockSpec((tm, D_MODEL), lambda i: (i, C_GB // D_MODEL)),
                  w_spec, w_spec, w_spec,
                  _resident_layer((1, D_MODEL), layer),
                  *_resident_col_parts(D_MODEL, 2 * D_FF // FFN_PARTS, FFN_PARTS),
                  _resident_layer((D_FF, D_MODEL), 0), _resident((1, D_MODEL))],
        out_specs=row_spec,
        out_shape=jax.ShapeDtypeStruct((rows, D_MODEL), F32),
        compiler_params=_params(("arbitrary",)),
        name="out_ffn_short",
    )(x, gated, bgy, proj, proj, wro, wco, wo, g, *([wgu] * FFN_PARTS), wd, fg)


def _ffn_main_kernel(x_ref, g_ref, *refs, final, n_prev, L):
    wgu_refs = refs[:FFN_PARTS]
    (wd_ref, fg_ref, p_ref, s_ref, c_ref, cw_ref, cq_ref, sq_ref, ck_ref, sk_ref, dmask_ref, qdec_ref, kdec_ref,
     gl_ref) = refs[FFN_PARTS:FFN_PARTS + 14]
    rest = refs[FFN_PARTS + 14:]
    sb = c_ref.shape[0]
    tabs = (cq_ref, sq_ref, ck_ref, sk_ref, dmask_ref, qdec_ref, kdec_ref)

    if final:
        n_main = n_prev + 1
        prev, mains, outs = rest[:3 * n_prev], rest[3 * n_prev:3 * n_prev + n_main], rest[3 * n_prev + n_main:]
        y_ref, gated_ref, bgy_ref, cn_ref, sn_ref, mstack_ref = outs
        for l in range(n_main):
            mstack_ref[l] = mains[l][...]
        sink = _state_sink(sn_ref, gl_ref, n_prev)
    else:
        wn32_ref, y_ref, gated_ref, bgy_ref, cn_ref, kd_ref, v_ref, wnb_ref = rest
        wnb_ref[...] = wn32_ref[...].astype(BF16)

        def sink(b, h, r, cl, s, kd, v):
            kd_ref[r, cl] = kd
            v_ref[r, cl] = v

    x = x_ref[...]
    hn = _rms_f32(x, g_ref[...]).astype(BF16)
    gates = [_dot(hn, w[...]) for w in wgu_refs[:FFN_PARTS // 2]]
    if final:
        for l in range(n_prev):
            so_ref, kdo_ref, vo_ref = prev[3 * l:3 * l + 3]
            upd = _state_sink(sn_ref, gl_ref, l)
            for b in range(sb):
                r = slice(b * L, (b + 1) * L)
                for h in range(HEADS):
                    cl = slice(h * DH, (h + 1) * DH)
                    upd(b, h, r, cl, so_ref[b, h], kdo_ref[r, cl], vo_ref[r, cl])
    staged = _ret_short_first(p_ref, s_ref, tabs, sb, L, sink)
    ups = [_dot(hn, w[...]) for w in wgu_refs[FFN_PARTS // 2:]]
    _ret_short_second(staged, p_ref, c_ref, cw_ref, gated_ref, bgy_ref, cn_ref, sb, L)
    y = x + _dot(_swiglu(gates, ups), wd_ref[...])
    if final:
        y = _rms_f32(y, fg_ref[...])
    y_ref[...] = y


def _ffn_main(x, layer, g, wgu, wd, fg, final, tm, proj, state, cconv, cw, rope, decay, L, kds, vs, main_states,
              win32, h_short, gated_m, bgy_m, squares):
    rows = x.shape[0]
    steps = rows // tm
    depth, ns = state.shape[:2]
    sb = ns // steps
    assert sb * steps == ns and (sb * L) % 8 == 0
    n_prev = len(kds) if final else 0
    dmask, qdec, kdec, gl, _ = decay
    n_samp = ns * L
    row_spec = pl.BlockSpec((tm, D_MODEL), lambda i: (i, 0))
    srow_spec = pl.BlockSpec((sb * L, D_MODEL), lambda i: (i, 0))
    srow_shape = jax.ShapeDtypeStruct((n_samp, D_MODEL), F32)

    def state_spec(l):
        return pl.BlockSpec((None, sb, HEADS, DH, DH), lambda i: (l, i, 0, 0, 0))

    stream_specs = [
        row_spec,
        pl.BlockSpec((sb * L, C_GA), lambda i: (i, 0)),
        state_spec(layer),
        pl.BlockSpec((None, sb, CONV_W - 1, D_MODEL), lambda i: (layer, i, 0, 0)),
    ]
    stream_args = [x, proj, state, cconv]
    for l in range(n_prev):
        stream_specs += [state_spec(l), srow_spec, srow_spec]
        stream_args += [state, kds[l], vs[l]]
    out_specs = [row_spec, srow_spec, srow_spec, pl.BlockSpec((sb, CONV_W - 1, D_MODEL), lambda i: (i, 0, 0))]
    out_shape = [jax.ShapeDtypeStruct(x.shape, F32), srow_shape, srow_shape,
                 jax.ShapeDtypeStruct((ns, CONV_W - 1, D_MODEL), F32)]
    if final:
        assert n_prev == depth - 1 and len(main_states) == depth
        out_specs.append(pl.BlockSpec((depth, sb, HEADS, DH, DH), lambda i: (0, i, 0, 0, 0)))
        out_shape.append(jax.ShapeDtypeStruct(state.shape, F32))
        n_mat = main_states[0].shape[0] * HEADS
        mb = n_mat // steps
        assert mb * steps == n_mat
        stream_specs += [pl.BlockSpec((mb, DH, DH), lambda i: (i, 0, 0))] * depth
        stream_args += [s.reshape(n_mat, DH, DH) for s in main_states]
        out_specs.append(pl.BlockSpec((depth, mb, DH, DH), lambda i: (0, i, 0, 0)))
        out_shape.append(jax.ShapeDtypeStruct((depth, n_mat, DH, DH), F32))
    else:
        slab = D_MODEL // steps
        assert slab * steps == D_MODEL and slab % 16 == 0
        stream_specs.append(pl.BlockSpec((None, slab, N_IN), lambda i: (layer + 1, i, 0)))
        stream_args.append(win32)
        out_specs += [srow_spec, srow_spec, pl.BlockSpec((None, slab, N_IN), lambda i: (0, i, 0))]
        out_shape += [srow_shape, srow_shape, jax.ShapeDtypeStruct((1, D_MODEL, N_IN), BF16)]

    resident_args = [g, wgu, wd, fg, cw, *rope, dmask, qdec, kdec, gl]
    n_res, n_in = len(resident_args), len(stream_args)
    part = 2 * D_FF // FFN_PARTS

    short_tm = SHORT_TM
    n_meta = 0 if final else h_short.shape[0] - n_samp
    n_full = n_samp // short_tm - (1 if n_meta else 0)
    assert n_samp % short_tm == 0 and n_samp % max(n_meta, 1) == 0
    short_rows = n_samp + n_meta
    short_args = [h_short, *squares] + ([] if final else [gated_m, bgy_m])
    out_shape.append(jax.ShapeDtypeStruct((short_rows, D_MODEL), F32))
    tile = pl.BlockSpec((short_tm, D_MODEL), lambda i: (i, 0))

    def outer(*refs):
        g_ref, wgu_ref, wd_ref, fg_ref, cw_ref = refs[:5]
        table_refs = refs[5:n_res]
        hbm_in = refs[n_res:n_res + n_in]
        hbm_short = refs[n_res + n_in:n_res + n_in + len(short_args)]
        hbm_out = refs[n_res + n_in + len(short_args):]
        wgu_parts = [wgu_ref.at[0, :, pl.ds(q * part, part)] for q in range(FFN_PARTS)]

        def step(*blocks):
            x_ref, p_ref, s_ref, c_ref = blocks[:4]
            _ffn_main_kernel(x_ref, g_ref.at[layer], *wgu_parts, wd_ref.at[0], fg_ref, p_ref, s_ref, c_ref,
                             cw_ref.at[layer], *table_refs, *blocks[4:], final=final, n_prev=n_prev, L=L)

        pltpu.emit_pipeline(step, grid=(steps,), in_specs=stream_specs, out_specs=out_specs)(*hbm_in, *hbm_out[:-1])

        p_hbm = hbm_in[1]
        gated_hbm, bgy_hbm, ys_hbm = hbm_out[1], hbm_out[2], hbm_out[-1]
        hs_hbm, square_hbm = hbm_short[0], hbm_short[1:4]

        def short_phase(wro_v, wco_v, wo_v, sems):
            copies = [pltpu.make_async_copy(src.at[0], dst, sems.at[n])
                      for n, (src, dst) in enumerate(zip(square_hbm, (wro_v, wco_v, wo_v)))]
            for cp in copies:
                cp.start()
            for cp in copies:
                cp.wait()

            def out_ffn(x, gated, bgy, ga, gb):
                h = _out_proj(x, gated, bgy, ga, gb, wro_v, wco_v, wo_v)
                y = _ffn_block(h, g_ref.at[layer], wgu_parts, wd_ref.at[0])
                return _rms_f32(y, fg_ref[...]) if final else y

            def full_tile(x_ref, gated_ref, bgy_ref, ga_ref, gb_ref, o_ref):
                o_ref[...] = out_ffn(x_ref[...], gated_ref[...], bgy_ref[...], ga_ref[...], gb_ref[...])

            ga_tile = pl.BlockSpec((short_tm, D_MODEL), lambda i: (i, C_GA // D_MODEL))
            gb_tile = pl.BlockSpec((short_tm, D_MODEL), lambda i: (i, C_GB // D_MODEL))
            pltpu.emit_pipeline(full_tile, grid=(n_full,), in_specs=[tile, tile, tile, ga_tile, gb_tile],
                                out_specs=[tile])(hs_hbm, gated_hbm, bgy_hbm, p_hbm, p_hbm, ys_hbm)

            if n_meta:
                gm_hbm, bm_hbm = hbm_short[4], hbm_short[5]
                mblk = n_samp // n_meta

                def last_tile(xa, xm, ga_a, ga_m, gb_a, gb_m, gt_a, gt_m, bg_a, bg_m, oa, om):
                    cat = lambda a, b: jnp.concatenate([a[...], b[...]], axis=0)
                    y = out_ffn(cat(xa, xm), cat(gt_a, gt_m), cat(bg_a, bg_m), cat(ga_a, ga_m), cat(gb_a, gb_m))
                    oa[...] = y[:short_tm]
                    om[...] = y[short_tm:]

                def a_spec(col):
                    return pl.BlockSpec((short_tm, D_MODEL), lambda i: (n_full, col))

                def m_spec(col, blk):
                    return pl.BlockSpec((n_meta, D_MODEL), lambda i: (blk, col))

                pltpu.emit_pipeline(
                    last_tile, grid=(1,),
                    in_specs=[a_spec(0), m_spec(0, mblk), a_spec(C_GA // D_MODEL), m_spec(C_GA // D_MODEL, mblk),
                              a_spec(C_GB // D_MODEL), m_spec(C_GB // D_MODEL, mblk), a_spec(0), m_spec(0, 0),
                              a_spec(0), m_spec(0, 0)],
                    out_specs=[a_spec(0), m_spec(0, mblk)],
                )(hs_hbm, hs_hbm, p_hbm, p_hbm, p_hbm, p_hbm, gated_hbm, gm_hbm, bgy_hbm, bm_hbm, ys_hbm, ys_hbm)

        pl.run_scoped(short_phase, pltpu.VMEM((D_MODEL, D_MODEL), BF16), pltpu.VMEM((D_MODEL, D_MODEL), BF16),
                      pltpu.VMEM((D_MODEL, D_MODEL), BF16), pltpu.SemaphoreType.DMA((3,)))

    vmem = pl.BlockSpec(memory_space=pltpu.VMEM)
    hbm = pl.BlockSpec(memory_space=pl.ANY)
    return pl.pallas_call(
        outer,
        in_specs=[vmem] * n_res + [hbm] * (n_in + len(short_args)),
        out_specs=[hbm] * len(out_shape),
        out_shape=out_shape,
        compiler_params=pltpu.CompilerParams(vmem_limit_bytes=VMEM_LIMIT_BYTES),
        name="ffn_main",
    )(*resident_args, *stream_args, *short_args)


def kernel(x_prompt, x_sample, state_ret, state_conv, meta_tokens, norm_mix_g, w_in, conv_w, w_ret_o,
           w_conv_o, w_o, norm_ffn_g, w_gate_up, w_down, final_norm_g):
    depth = w_in.shape[0]
    nb, seq, _ = x_prompt.shape
    ns, ls, _ = x_sample.shape
    n_samp = ns * ls
    n_short = n_samp + N_META
    short_tm = n_short // 5
    assert short_tm * 5 == n_short and short_tm % 8 == 0 and n_samp % N_META == 0

    gm = norm_mix_g.reshape(depth, 1, D_MODEL)
    gf = norm_ffn_g.reshape(depth, 1, D_MODEL)
    fg = final_norm_g.reshape(1, D_MODEL)

    rope_meta = _rope_tables(np.arange(N_META))
    rope_main = _rope_tables(N_META + np.arange(seq))
    rope_samp = _rope_tables(PAST_LEN + np.arange(ls))
    dec_meta, dec_main, dec_samp = _decay_tables(N_META), _decay_tables(CHUNK), _decay_tables(ls)

    h_main = x_prompt
    zero_s = jnp.zeros((1, HEADS, DH, DH), F32)
    zero_c = jnp.zeros((1, CONV_W - 1, D_MODEL), F32)
    squares32 = (w_ret_o, w_conv_o, w_o)

    s_p, c_p, c_s, kds, vs = [], [], [], [], []
    y_samp = s_s = s_p_stacked = h_short = win_b = None
    for l in range(depth):
        last = l == depth - 1
        if l == 0:
            proj, win_b, wro_b, wco_b, wo_b, h_short = _proj_first(
                x_sample.reshape(n_samp, D_MODEL), meta_tokens.astype(F32), l, gm, w_in, squares32)
        else:
            proj, wro_b, wco_b, wo_b = _proj(h_short, l, gm, win_b, squares32, short_tm)
        gated_m, bgy_m, c_m, s_m = _ret_meta(proj, n_samp, zero_s, zero_c, l, conv_w, rope_meta, dec_meta, 1, N_META)

        h_main, s_l, c_l, wgu_b, wd_b = _mixer_main(h_main, s_m, c_m, l, gm, win_b, conv_w, wro_b, wco_b, wo_b,
                                                    rope_main, dec_main, w_gate_up, w_down)
        s_p.append(s_l)
        c_p.append(c_l)
        res = _ffn_main(h_main.reshape(nb * seq, D_MODEL), l, gf, wgu_b, wd_b, fg, last, MAIN_FFN_TM,
                        proj, state_ret, state_conv, conv_w, rope_samp, dec_samp, ls, kds, vs, s_p, w_in,
                        h_short, gated_m, bgy_m, (wro_b, wco_b, wo_b))
        h_main, c_l = res[0], res[3]
        h_main = h_main.reshape(nb, seq, D_MODEL)
        c_s.append(c_l)
        if last:
            s_s = res[4]
            s_p_stacked = res[5].reshape(depth, nb, HEADS, DH, DH)
            y_samp = res[-1]
        else:
            kds.append(res[4])
            vs.append(res[5])
            win_b = res[6]
            h_short = res[-1]

    return (h_main, y_samp.reshape(ns, ls, D_MODEL), s_p_stacked, jnp.stack(c_p), s_s, jnp.stack(c_s))
```

```python
import functools

import numpy as np
import jax
import jax.numpy as jnp
from jax import lax
from jax.experimental import pallas as pl
from jax.experimental.pallas import tpu as pltpu

D_MODEL = 1024
N_META = 16
HEADS = 8
DH = D_MODEL // HEADS
CHUNK = 128
ROPE_BASE = 10000.0
CONV_W = 3
D_FF = ((8 * D_MODEL + 3 * 256 - 1) // (3 * 256)) * 256
EPS = 1e-6
PAST_LEN = 16384
N_IN = 9 * D_MODEL
C_Q, C_K, C_V, C_G, C_BG, C_CG, C_HC, C_GA, C_GB = (i * D_MODEL for i in range(9))

F32 = jnp.float32
BF16 = jnp.bfloat16

VMEM_LIMIT_BYTES = 58 * 1024 * 1024

MAIN_BG = 4
MAIN_FFN_TM = 512
SHORT_TM = 256
SHORT_STEPS_WITH_META = 5
FFN_PARTS = 2
assert (2 * D_FF // FFN_PARTS) % 256 == 0 and FFN_PARTS % 2 == 0


def _resident(shape):
    nd = len(shape)
    return pl.BlockSpec(shape, lambda *_: (0,) * nd, pipeline_mode=pl.Buffered(1))


def _resident_layer(shape, layer):
    nd = len(shape)
    return pl.BlockSpec((None,) + tuple(shape), lambda *_: (layer,) + (0,) * nd, pipeline_mode=pl.Buffered(1))


def _resident_col_parts(rows, width, n_parts):
    return [pl.BlockSpec((None, rows, width), functools.partial(lambda *_, q: (0, 0, q), q=q),
                         pipeline_mode=pl.Buffered(1)) for q in range(n_parts)]


def _params(sem):
    return pltpu.CompilerParams(dimension_semantics=sem, vmem_limit_bytes=VMEM_LIMIT_BYTES)


def _dot(a, b):
    return jnp.dot(a, b, preferred_element_type=F32)


def _dot_nt(a, b):
    return lax.dot_general(a, b, (((1,), (1,)), ((), ())), preferred_element_type=F32)


def _dot_tn(a, b):
    return lax.dot_general(a, b, (((0,), (0,)), ((), ())), preferred_element_type=F32)


def _rms_f32(x, g):
    return x * lax.rsqrt(jnp.mean(x * x, axis=-1, keepdims=True) + EPS) * g


def _sigmoid(x):
    return 1.0 / (1.0 + jnp.exp(-x))


def _silu(x):
    return x * _sigmoid(x)


def _rotary(t, cos, sin):
    return t * cos + pltpu.roll(t, DH // 2, 1) * sin


def _group_norm(o):
    mu = jnp.mean(o, axis=-1, keepdims=True)
    d = o - mu
    var = jnp.mean(d * d, axis=-1, keepdims=True)
    return d * lax.rsqrt(var + EPS)


def _retention_head_paired(q, kt, kdt, v, s, dmask, qdec, gl):
    L = q.shape[0]
    sc = _dot(q, jnp.concatenate([kt, s.astype(BF16)], axis=1))
    scores = (sc[:, :L] * dmask).astype(BF16)
    iu = _dot(jnp.concatenate([scores, kdt], axis=0), v)
    return iu[:L] + sc[:, L:] * qdec, gl * s + iu[L:]


def _short_conv(u, tail, cw, rows):
    r1 = pltpu.roll(u, 1, 0)
    r2 = pltpu.roll(u, 2, 0)
    t0, t1 = tail[0:1, :], tail[1:2, :]
    sh1 = jnp.where(rows == 0, t1, r1)
    sh2 = jnp.where(rows == 0, t0, jnp.where(rows == 1, t1, r2))
    return cw[0:1, :] * sh2 + cw[1:2, :] * sh1 + cw[2:3, :] * u


def _out_proj(x, gated, bgy, ga, gb, wro_ref, wco_ref, wo_ref):
    ret_out = _dot(gated.astype(BF16), wro_ref[...])
    conv_out = _dot(bgy.astype(BF16), wco_ref[...])
    merged = _sigmoid(ga) * ret_out + _sigmoid(gb) * conv_out
    return x + _dot(merged.astype(BF16), wo_ref[...])


def _swiglu(gates, ups):
    return jnp.concatenate([_silu(a) * b for a, b in zip(gates, ups)], axis=1).astype(BF16)


def _ffn_block(x, g_ref, wgu_refs, wd_ref):
    hn = _rms_f32(x, g_ref[...]).astype(BF16)
    n = len(wgu_refs) // 2
    gates = [_dot(hn, w[...]) for w in wgu_refs[:n]]
    ups = [_dot(hn, w[...]) for w in wgu_refs[n:]]
    return x + _dot(_swiglu(gates, ups), wd_ref[...])


def _log_gamma():
    return np.log1p(-np.exp2(-5.0 - np.arange(HEADS, dtype=np.float64)))


def _const(t):
    return jnp.asarray(np.asarray(t, dtype=np.float32))


def _rope_tables(pos):
    half = DH // 2
    inv = np.power(ROPE_BASE, -np.arange(half, dtype=np.float64) / half)
    ang = np.asarray(pos, dtype=np.float64)[:, None] * inv[None, :]
    cos, sin = np.cos(ang), np.sin(ang)
    cosf = np.concatenate([cos, cos], axis=-1)
    sinf = np.concatenate([-sin, sin], axis=-1)
    scale = DH ** -0.5
    return tuple(_const(t) for t in (cosf, sinf, cosf * scale, sinf * scale))


def _decay_tables(L):
    log_g = _log_gamma()
    idx = np.arange(L, dtype=np.float64)
    diff = idx[:, None] - idx[None, :]
    dmask = np.where(diff >= 0, np.exp(log_g[:, None, None] * np.maximum(diff, 0.0)[None]), 0.0)
    qdec = np.exp(log_g[:, None] * (idx + 1.0)[None])
    kdec = np.exp(log_g[:, None] * (L - 1.0 - idx)[None])
    gl = np.exp(log_g * L)
    lanes = lambda t: np.repeat(t.T, DH, axis=1)
    return tuple(_const(t) for t in (dmask, lanes(qdec), lanes(kdec), np.repeat(gl, DH)[None, :], kdec[:, None, :]))


def _mixer_main_kernel(x_ref, s0_ref, c0_ref, g_ref, *refs):
    n_groups = N_IN // D_MODEL
    wq_ref, wk_ref, wv_ref, wg_ref, wbg_ref, wcg_ref, whc_ref, wga_ref, wgb_ref = refs[:n_groups]
    (cw_ref, wro_ref, wco_ref, wo_ref, cq_ref, sq_ref, ck_ref, sk_ref, dmask_ref, qdec_ref, kdec_ref, gl_ref,
     wgu32_ref, wd32_ref, h_ref, s_ref, c_ref, wgub_ref, wdb_ref, q_s, kt_s, kdt_s, v_s, o_s) = refs[n_groups:]
    bg = x_ref.shape[0]
    m = bg * CHUNK

    wgub_ref[...] = wgu32_ref[...].astype(BF16)
    step = pl.program_id(0) * pl.num_programs(1) + pl.program_id(1)

    @pl.when(step < D_FF // wd32_ref.shape[0])
    def _():
        wdb_ref[...] = wd32_ref[...].astype(BF16)

    @pl.when(pl.program_id(1) == 0)
    def _():
        for b in range(bg):
            s_ref[b] = s0_ref[...]
            c_ref[b] = c0_ref[...]

    x = x_ref[...].reshape(m, D_MODEL)
    hn = _rms_f32(x, g_ref[...]).astype(BF16)

    cq, sq, ck, sk = cq_ref[...], sq_ref[...], ck_ref[...], sk_ref[...]
    q = _dot(hn, wq_ref[...])
    k = _dot(hn, wk_ref[...])
    v_s[...] = _dot(hn, wv_ref[...]).astype(BF16)
    for b in range(bg):
        r = slice(b * CHUNK, (b + 1) * CHUNK)
        for h in range(HEADS):
            cl = slice(h * DH, (h + 1) * DH)
            q_s[r, cl] = _rotary(q[r, cl], cq, sq).astype(BF16)
            krt = _rotary(k[r, cl], ck, sk).T
            kt_s[b * HEADS + h] = krt.astype(BF16)
            kdt_s[b * HEADS + h] = (krt * kdec_ref[h]).astype(BF16)

    for b in range(bg):
        r = slice(b * CHUNK, (b + 1) * CHUNK)
        for h in range(HEADS):
            cl = slice(h * DH, (h + 1) * DH)
            o, s_new = _retention_head_paired(q_s[r, cl], kt_s[b * HEADS + h], kdt_s[b * HEADS + h], v_s[r, cl],
                                              s_ref[b, h], dmask_ref[h], qdec_ref[:, cl], gl_ref[:, cl])
            s_ref[b, h] = s_new
            o_s[r, cl] = _group_norm(o)

    g = _dot(hn, wg_ref[...])
    gated = _silu(g) * o_s[...]

    bgate = _dot(hn, wbg_ref[...])
    u = _dot(hn, wcg_ref[...]) * _dot(hn, whc_ref[...])
    rows = lax.broadcasted_iota(jnp.int32, (CHUNK, D_MODEL), 0)
    cw = cw_ref[...]
    ys = []
    for b in range(bg):
        ub = u[b * CHUNK:(b + 1) * CHUNK]
        ys.append(_short_conv(ub, c_ref[b], cw, rows))
        c_ref[b] = ub[CHUNK - (CONV_W - 1):, :]
    bgy = bgate * jnp.concatenate(ys, axis=0)

    ga = _dot(hn, wga_ref[...])
    gb = _dot(hn, wgb_ref[...])
    out = _out_proj(x, gated, bgy, ga, gb, wro_ref, wco_ref, wo_ref)
    h_ref[...] = out.reshape(bg, CHUNK, D_MODEL)


def _mixer_main(x, s0, c0, layer, g, win, cw, wro, wco, wo, rope, decay, wgu32, wd32):
    nb, seq, _ = x.shape
    bg = MAIN_BG
    cq, sq, ck, sk = rope
    dmask, qdec, _, gl, kdec = decay
    m = bg * CHUNK
    n_chunks = seq // CHUNK
    steps = (nb // bg) * n_chunks
    slab = D_MODEL // steps
    wd_slab = 16 * (-(-D_FF // (16 * steps)))
    while D_FF % wd_slab:
        wd_slab += 16
    wd_last = D_FF // wd_slab - 1
    assert slab * steps == D_MODEL and slab % 16 == 0 and wd_last < steps

    def wd_block(i, c):
        return jnp.minimum(i * n_chunks + c, wd_last)

    rope_spec = pl.BlockSpec((CHUNK, DH), lambda i, c: (c, 0))
    return pl.pallas_call(
        _mixer_main_kernel,
        grid=(nb // bg, n_chunks),
        in_specs=[
            pl.BlockSpec((bg, CHUNK, D_MODEL), lambda i, c: (i, c, 0)),
            _resident_layer((HEADS, DH, DH), 0),
            _resident_layer((CONV_W - 1, D_MODEL), 0),
            _resident_layer((1, D_MODEL), layer),
            *_resident_col_parts(D_MODEL, D_MODEL, N_IN // D_MODEL),
            _resident_layer((CONV_W, D_MODEL), layer),
            _resident_layer((D_MODEL, D_MODEL), 0),
            _resident_layer((D_MODEL, D_MODEL), 0),
            _resident_layer((D_MODEL, D_MODEL), 0),
            rope_spec, rope_spec, rope_spec, rope_spec,
            _resident((HEADS, CHUNK, CHUNK)),
            _resident((CHUNK, D_MODEL)),
            _resident((HEADS, 1, CHUNK)),
            _resident((1, D_MODEL)),
            pl.BlockSpec((None, slab, 2 * D_FF), lambda i, c: (layer, i * n_chunks + c, 0)),
            pl.BlockSpec((None, wd_slab, D_MODEL), lambda i, c: (layer, wd_block(i, c), 0)),
        ],
        out_specs=[
            pl.BlockSpec((bg, CHUNK, D_MODEL), lambda i, c: (i, c, 0)),
            pl.BlockSpec((bg, HEADS, DH, DH), lambda i, c: (i, 0, 0, 0)),
            pl.BlockSpec((bg, CONV_W - 1, D_MODEL), lambda i, c: (i, 0, 0)),
            pl.BlockSpec((None, slab, 2 * D_FF), lambda i, c: (0, i * n_chunks + c, 0)),
            pl.BlockSpec((None, wd_slab, D_MODEL), lambda i, c: (0, wd_block(i, c), 0)),
        ],
        out_shape=[
            jax.ShapeDtypeStruct(x.shape, F32),
            jax.ShapeDtypeStruct((nb, HEADS, DH, DH), F32),
            jax.ShapeDtypeStruct((nb, CONV_W - 1, D_MODEL), F32),
            jax.ShapeDtypeStruct((1, D_MODEL, 2 * D_FF), BF16),
            jax.ShapeDtypeStruct((1, D_FF, D_MODEL), BF16),
        ],
        scratch_shapes=[
            pltpu.VMEM((m, D_MODEL), BF16),
            pltpu.VMEM((bg * HEADS, DH, CHUNK), BF16),
            pltpu.VMEM((bg * HEADS, DH, CHUNK), BF16),
            pltpu.VMEM((m, D_MODEL), BF16),
            pltpu.VMEM((m, D_MODEL), F32),
        ],
        compiler_params=_params(("arbitrary", "arbitrary")),
        name="mixer_main",
    )(x, s0, c0, g, *([win] * (N_IN // D_MODEL)), cw, wro, wco, wo, cq, sq, ck, sk, dmask, qdec, kdec, gl,
      wgu32, wd32)


def _round_square_slabs(step, n_slabs, sq_refs, sb_refs):
    @pl.when(step < n_slabs)
    def _():
        for src, dst in zip(sq_refs, sb_refs):
            dst[...] = src[...].astype(BF16)


def _square_slab_specs(layer, slab, n_slabs):
    slab_in = pl.BlockSpec((None, slab, D_MODEL), lambda j: (layer, jnp.minimum(j, n_slabs - 1), 0))
    slab_out = pl.BlockSpec((None, slab, D_MODEL), lambda j: (0, jnp.minimum(j, n_slabs - 1), 0))
    return [slab_in] * 3, [slab_out] * 3, [jax.ShapeDtypeStruct((1, D_MODEL, D_MODEL), BF16)] * 3


def _proj_first_kernel(xs_ref, xm_ref, g_ref, w32_ref, sq0_ref, sq1_ref, sq2_ref,
                       o_ref, wb_ref, sb0_ref, sb1_ref, sb2_ref, xcat_ref, hn_s, *, n_slabs):
    j = pl.program_id(0)
    n_samp = xs_ref.shape[0]

    @pl.when(j == 0)
    def _():
        xcat_ref[:n_samp, :] = xs_ref[...]
        xcat_ref[n_samp:, :] = xm_ref[...]
        hn_s[:n_samp, :] = _rms_f32(xs_ref[...], g_ref[...]).astype(BF16)
        hn_s[n_samp:, :] = _rms_f32(xm_ref[...], g_ref[...]).astype(BF16)

    _round_square_slabs(j, n_slabs, (sq0_ref, sq1_ref, sq2_ref), (sb0_ref, sb1_ref, sb2_ref))
    wb = w32_ref[...].astype(BF16)
    wb_ref[...] = wb
    o_ref[...] = _dot(hn_s[...], wb)


def _proj_first(xs, xm, layer, g, win32, squares32):
    rows = xs.shape[0] + xm.shape[0]
    n_col = N_IN // D_MODEL
    slab = 128
    n_slabs = D_MODEL // slab
    assert n_slabs <= n_col
    sq_in, sq_out, sq_shapes = _square_slab_specs(layer, slab, n_slabs)
    return pl.pallas_call(
        functools.partial(_proj_first_kernel, n_slabs=n_slabs),
        grid=(n_col,),
        in_specs=[_resident(xs.shape), _resident(xm.shape), _resident_layer((1, D_MODEL), layer),
                  pl.BlockSpec((None, D_MODEL, D_MODEL), lambda j: (layer, 0, j))] + sq_in,
        out_specs=[pl.BlockSpec((rows, D_MODEL), lambda j: (0, j)),
                   pl.BlockSpec((None, D_MODEL, D_MODEL), lambda j: (0, 0, j))] + sq_out
        + [pl.BlockSpec((rows, D_MODEL), lambda j: (0, 0))],
        out_shape=[jax.ShapeDtypeStruct((rows, N_IN), F32), jax.ShapeDtypeStruct((1, D_MODEL, N_IN), BF16)]
        + sq_shapes + [jax.ShapeDtypeStruct((rows, D_MODEL), F32)],
        scratch_shapes=[pltpu.VMEM((rows, D_MODEL), BF16)],
        compiler_params=_params(("arbitrary",)),
        name="proj_short_first",
    )(xs, xm, g, win32, *squares32)


def _proj_kernel(x_ref, g_ref, win_ref, sq0_ref, sq1_ref, sq2_ref, o_ref, sb0_ref, sb1_ref, sb2_ref, *, n_slabs):
    _round_square_slabs(pl.program_id(0), n_slabs, (sq0_ref, sq1_ref, sq2_ref), (sb0_ref, sb1_ref, sb2_ref))
    hn = _rms_f32(x_ref[...], g_ref[...]).astype(BF16)
    o_ref[...] = _dot(hn, win_ref[...])


def _proj(x, layer, g, win, squares32, tm):
    rows = x.shape[0]
    steps = rows // tm
    slab = 256
    n_slabs = D_MODEL // slab
    assert n_slabs <= steps and steps * tm == rows
    sq_in, sq_out, sq_shapes = _square_slab_specs(layer, slab, n_slabs)
    return pl.pallas_call(
        functools.partial(_proj_kernel, n_slabs=n_slabs),
        grid=(steps,),
        in_specs=[pl.BlockSpec((tm, D_MODEL), lambda i: (i, 0)), _resident_layer((1, D_MODEL), layer),
                  _resident_layer((D_MODEL, N_IN), 0)] + sq_in,
        out_specs=[pl.BlockSpec((tm, N_IN), lambda i: (i, 0))] + sq_out,
        out_shape=[jax.ShapeDtypeStruct((rows, N_IN), F32)] + sq_shapes,
        compiler_params=_params(("arbitrary",)),
        name="proj_short",
    )(x, g, win, *squares32)


def _ret_short_first(p_ref, s_ref, tab_refs, sb, L, sink):
    cq_ref, sq_ref, ck_ref, sk_ref, dmask_ref, qdec_ref, kdec_ref = tab_refs
    cq, sq, ck, sk = cq_ref[...], sq_ref[...], ck_ref[...], sk_ref[...]
    staged = []
    for b in range(sb):
        r = slice(b * L, (b + 1) * L)
        for h in range(HEADS):
            cl = slice(h * DH, (h + 1) * DH)
            q = _rotary(p_ref[r, C_Q + h * DH:C_Q + (h + 1) * DH], cq, sq).astype(BF16)
            k = _rotary(p_ref[r, C_K + h * DH:C_K + (h + 1) * DH], ck, sk)
            v = p_ref[r, C_V + h * DH:C_V + (h + 1) * DH]
            s = s_ref[b, h]
            scores = (_dot_nt(q, k.astype(BF16)) * dmask_ref[h]).astype(BF16)
            cross = _dot(q, s.astype(BF16)) * qdec_ref[:, cl]
            sink(b, h, r, cl, s, k * kdec_ref[:, cl], v)
            staged.append((scores, cross, v.astype(BF16)))
    return staged


def _ret_short_second(staged, p_ref, c_ref, cw_ref, gated_ref, bgy_ref, cn_ref, sb, L):
    rows = lax.broadcasted_iota(jnp.int32, (L, D_MODEL), 0)
    cw = cw_ref[...]
    for b in range(sb):
        r = slice(b * L, (b + 1) * L)
        for h in range(HEADS):
            cl = slice(h * DH, (h + 1) * DH)
            scores, cross, v = staged[b * HEADS + h]
            o = _dot(scores, v) + cross
            g = p_ref[r, C_G + h * DH:C_G + (h + 1) * DH]
            gated_ref[r, cl] = _silu(g) * _group_norm(o)
        u = p_ref[r, C_CG:C_CG + D_MODEL] * p_ref[r, C_HC:C_HC + D_MODEL]
        y = _short_conv(u, c_ref[b], cw, rows)
        bgy_ref[r, :] = p_ref[r, C_BG:C_BG + D_MODEL] * y
        cn_ref[b] = u[L - (CONV_W - 1):, :]


def _state_sink(sn_ref, gl_ref, layer=None):
    def sink(b, h, r, cl, s, kd, v):
        s_new = gl_ref[:, cl] * s + _dot_tn(kd.astype(BF16), v.astype(BF16))
        if layer is None:
            sn_ref[b, h] = s_new
        else:
            sn_ref[layer, b, h] = s_new
    return sink


def _ret_meta_kernel(p_ref, s_ref, c_ref, cw_ref, cq_ref, sq_ref, ck_ref, sk_ref, dmask_ref, qdec_ref, kdec_ref,
                     gl_ref, gated_ref, bgy_ref, cn_ref, sn_ref, *, L):
    sb = c_ref.shape[0]
    tabs = (cq_ref, sq_ref, ck_ref, sk_ref, dmask_ref, qdec_ref, kdec_ref)
    staged = _ret_short_first(p_ref, s_ref, tabs, sb, L, _state_sink(sn_ref, gl_ref))
    _ret_short_second(staged, p_ref, c_ref, cw_ref, gated_ref, bgy_ref, cn_ref, sb, L)


def _short_table_specs(L):
    return [_resident((L, DH))] * 4 + [_resident((HEADS, L, L)), _resident((L, D_MODEL)), _resident((L, D_MODEL)),
                                       _resident((1, D_MODEL))]


def _ret_meta(proj, row0, state, cprev, layer, cw, rope, decay, nseq, L):
    dmask, qdec, kdec, gl, _ = decay
    rows = nseq * L
    row_spec = pl.BlockSpec((rows, D_MODEL), lambda i: (0, 0))
    row_shape = jax.ShapeDtypeStruct((rows, D_MODEL), F32)
    return pl.pallas_call(
        functools.partial(_ret_meta_kernel, L=L),
        grid=(1,),
        in_specs=[
            pl.BlockSpec((rows, C_GA), lambda i: (row0 // rows, 0)),
            _resident((nseq, HEADS, DH, DH)),
            _resident((nseq, CONV_W - 1, D_MODEL)),
            _resident_layer((CONV_W, D_MODEL), layer),
        ] + _short_table_specs(L),
        out_specs=[row_spec, row_spec, pl.BlockSpec((nseq, CONV_W - 1, D_MODEL), lambda i: (0, 0, 0)),
                   pl.BlockSpec((nseq, HEADS, DH, DH), lambda i: (0, 0, 0, 0))],
        out_shape=[row_shape, row_shape, jax.ShapeDtypeStruct((nseq, CONV_W - 1, D_MODEL), F32),
                   jax.ShapeDtypeStruct((nseq, HEADS, DH, DH), F32)],
        compiler_params=_params(("arbitrary",)),
        name="ret_meta",
    )(proj, state, cprev, cw, *rope, dmask, qdec, kdec, gl)


def _ffn_main_kernel(x_ref, g_ref, *refs, final, n_prev, L):
    wgu_refs = refs[:FFN_PARTS]
    (wd_ref, fg_ref, p_ref, s_ref, c_ref, cw_ref, cq_ref, sq_ref, ck_ref, sk_ref, dmask_ref, qdec_ref, kdec_ref,
     gl_ref) = refs[FFN_PARTS:FFN_PARTS + 14]
    rest = refs[FFN_PARTS + 14:]
    sb = c_ref.shape[0]
    tabs = (cq_ref, sq_ref, ck_ref, sk_ref, dmask_ref, qdec_ref, kdec_ref)

    if final:
        n_main = n_prev + 1
        prev, mains, outs = rest[:3 * n_prev], rest[3 * n_prev:3 * n_prev + n_main], rest[3 * n_prev + n_main:]
        y_ref, gated_ref, bgy_ref, cn_ref, sn_ref, mstack_ref = outs
        for l in range(n_main):
            mstack_ref[l] = mains[l][...]
        sink = _state_sink(sn_ref, gl_ref, n_prev)
    else:
        wn32_ref, y_ref, gated_ref, bgy_ref, cn_ref, kd_ref, v_ref, wnb_ref = rest
        wnb_ref[...] = wn32_ref[...].astype(BF16)

        def sink(b, h, r, cl, s, kd, v):
            kd_ref[r, cl] = kd
            v_ref[r, cl] = v

    x = x_ref[...]
    hn = _rms_f32(x, g_ref[...]).astype(BF16)
    gates = [_dot(hn, w[...]) for w in wgu_refs[:FFN_PARTS // 2]]
    if final:
        for l in range(n_prev):
            so_ref, kdo_ref, vo_ref = prev[3 * l:3 * l + 3]
            upd = _state_sink(sn_ref, gl_ref, l)
            for b in range(sb):
                r = slice(b * L, (b + 1) * L)
                for h in range(HEADS):
                    cl = slice(h * DH, (h + 1) * DH)
                    upd(b, h, r, cl, so_ref[b, h], kdo_ref[r, cl], vo_ref[r, cl])
    staged = _ret_short_first(p_ref, s_ref, tabs, sb, L, sink)
    ups = [_dot(hn, w[...]) for w in wgu_refs[FFN_PARTS // 2:]]
    _ret_short_second(staged, p_ref, c_ref, cw_ref, gated_ref, bgy_ref, cn_ref, sb, L)
    y = x + _dot(_swiglu(gates, ups), wd_ref[...])
    if final:
        y = _rms_f32(y, fg_ref[...])
    y_ref[...] = y


def _ffn_main(x, layer, g, wgu, wd, fg, final, tm, proj, state, cconv, cw, rope, decay, L, kds, vs, main_states,
              win32, h_short, gated_m, bgy_m, squares):
    rows = x.shape[0]
    steps = rows // tm
    depth, ns = state.shape[:2]
    sb = ns // steps
    assert sb * steps == ns and (sb * L) % 8 == 0
    n_prev = len(kds) if final else 0
    dmask, qdec, kdec, gl, _ = decay
    n_samp = ns * L
    row_spec = pl.BlockSpec((tm, D_MODEL), lambda i: (i, 0))
    srow_spec = pl.BlockSpec((sb * L, D_MODEL), lambda i: (i, 0))
    srow_shape = jax.ShapeDtypeStruct((n_samp, D_MODEL), F32)

    def state_spec(l):
        return pl.BlockSpec((None, sb, HEADS, DH, DH), lambda i: (l, i, 0, 0, 0))

    stream_specs = [
        row_spec,
        pl.BlockSpec((sb * L, C_GA), lambda i: (i, 0)),
        state_spec(layer),
        pl.BlockSpec((None, sb, CONV_W - 1, D_MODEL), lambda i: (layer, i, 0, 0)),
    ]
    stream_args = [x, proj, state, cconv]
    for l in range(n_prev):
        stream_specs += [state_spec(l), srow_spec, srow_spec]
        stream_args += [state, kds[l], vs[l]]
    n_meta = 0 if final else h_short.shape[0] - n_samp
    short_rows = n_samp + n_meta
    branch_shape = jax.ShapeDtypeStruct((short_rows, D_MODEL), F32)
    out_specs = [row_spec, srow_spec, srow_spec, pl.BlockSpec((sb, CONV_W - 1, D_MODEL), lambda i: (i, 0, 0))]
    out_shape = [jax.ShapeDtypeStruct(x.shape, F32), branch_shape, branch_shape,
                 jax.ShapeDtypeStruct((ns, CONV_W - 1, D_MODEL), F32)]
    if final:
        assert n_prev == depth - 1 and len(main_states) == depth
        out_specs.append(pl.BlockSpec((depth, sb, HEADS, DH, DH), lambda i: (0, i, 0, 0, 0)))
        out_shape.append(jax.ShapeDtypeStruct(state.shape, F32))
        n_mat = main_states[0].shape[0] * HEADS
        mb = n_mat // steps
        assert mb * steps == n_mat
        stream_specs += [pl.BlockSpec((mb, DH, DH), lambda i: (i, 0, 0))] * depth
        stream_args += [s.reshape(n_mat, DH, DH) for s in main_states]
        out_specs.append(pl.BlockSpec((depth, mb, DH, DH), lambda i: (0, i, 0, 0)))
        out_shape.append(jax.ShapeDtypeStruct((depth, n_mat, DH, DH), F32))
    else:
        slab = D_MODEL // steps
        assert slab * steps == D_MODEL and slab % 16 == 0
        stream_specs.append(pl.BlockSpec((None, slab, N_IN), lambda i: (layer + 1, i, 0)))
        stream_args.append(win32)
        out_specs += [srow_spec, srow_spec, pl.BlockSpec((None, slab, N_IN), lambda i: (0, i, 0))]
        out_shape += [srow_shape, srow_shape, jax.ShapeDtypeStruct((1, D_MODEL, N_IN), BF16)]

    resident_args = [g, wgu, wd, fg, cw, *rope, dmask, qdec, kdec, gl]
    n_res, n_in = len(resident_args), len(stream_args)
    part = 2 * D_FF // FFN_PARTS

    short_tm = SHORT_TM if final else short_rows // SHORT_STEPS_WITH_META
    assert short_rows % short_tm == 0 and short_tm % 8 == 0
    short_args = [h_short, *squares] + ([] if final else [gated_m, bgy_m])
    out_shape.append(jax.ShapeDtypeStruct((short_rows, D_MODEL), F32))
    tile = pl.BlockSpec((short_tm, D_MODEL), lambda i: (i, 0))

    def outer(*refs):
        g_ref, wgu_ref, wd_ref, fg_ref, cw_ref = refs[:5]
        table_refs = refs[5:n_res]
        hbm_in = refs[n_res:n_res + n_in]
        hbm_short = refs[n_res + n_in:n_res + n_in + len(short_args)]
        hbm_out = refs[n_res + n_in + len(short_args):]
        wgu_parts = [wgu_ref.at[0, :, pl.ds(q * part, part)] for q in range(FFN_PARTS)]

        def step(*blocks):
            x_ref, p_ref, s_ref, c_ref = blocks[:4]
            _ffn_main_kernel(x_ref, g_ref.at[layer], *wgu_parts, wd_ref.at[0], fg_ref, p_ref, s_ref, c_ref,
                             cw_ref.at[layer], *table_refs, *blocks[4:], final=final, n_prev=n_prev, L=L)

        pltpu.emit_pipeline(step, grid=(steps,), in_specs=stream_specs, out_specs=out_specs)(*hbm_in, *hbm_out[:-1])

        p_hbm = hbm_in[1]
        gated_hbm, bgy_hbm, ys_hbm = hbm_out[1], hbm_out[2], hbm_out[-1]
        hs_hbm, square_hbm = hbm_short[0], hbm_short[1:4]

        def short_phase(wro_v, wco_v, wo_v, sems):
            copies = [pltpu.make_async_copy(src.at[0], dst, sems.at[n])
                      for n, (src, dst) in enumerate(zip(square_hbm, (wro_v, wco_v, wo_v)))]
            if n_meta:
                meta_rows = pl.ds(n_samp, n_meta)
                copies += [pltpu.make_async_copy(hbm_short[4], gated_hbm.at[meta_rows], sems.at[3]),
                           pltpu.make_async_copy(hbm_short[5], bgy_hbm.at[meta_rows], sems.at[4])]
            for cp in copies:
                cp.start()
            for cp in copies:
                cp.wait()

            def tile_body(x_ref, gated_ref, bgy_ref, ga_ref, gb_ref, o_ref):
                h = _out_proj(x_ref[...], gated_ref[...], bgy_ref[...], ga_ref[...], gb_ref[...], wro_v, wco_v, wo_v)
                y = _ffn_block(h, g_ref.at[layer], wgu_parts, wd_ref.at[0])
                o_ref[...] = _rms_f32(y, fg_ref[...]) if final else y

            ga_tile = pl.BlockSpec((short_tm, D_MODEL), lambda i: (i, C_GA // D_MODEL))
            gb_tile = pl.BlockSpec((short_tm, D_MODEL), lambda i: (i, C_GB // D_MODEL))
            pltpu.emit_pipeline(tile_body, grid=(short_rows // short_tm,), in_specs=[tile, tile, tile, ga_tile, gb_tile],
                                out_specs=[tile])(hs_hbm, gated_hbm, bgy_hbm, p_hbm, p_hbm, ys_hbm)

        pl.run_scoped(short_phase, pltpu.VMEM((D_MODEL, D_MODEL), BF16), pltpu.VMEM((D_MODEL, D_MODEL), BF16),
                      pltpu.VMEM((D_MODEL, D_MODEL), BF16), pltpu.SemaphoreType.DMA((5,)))

    vmem = pl.BlockSpec(memory_space=pltpu.VMEM)
    hbm = pl.BlockSpec(memory_space=pl.ANY)
    return pl.pallas_call(
        outer,
        in_specs=[vmem] * n_res + [hbm] * (n_in + len(short_args)),
        out_specs=[hbm] * len(out_shape),
        out_shape=out_shape,
        compiler_params=pltpu.CompilerParams(vmem_limit_bytes=VMEM_LIMIT_BYTES),
        name="ffn_main",
    )(*resident_args, *stream_args, *short_args)


def kernel(x_prompt, x_sample, state_ret, state_conv, meta_tokens, norm_mix_g, w_in, conv_w, w_ret_o,
           w_conv_o, w_o, norm_ffn_g, w_gate_up, w_down, final_norm_g):
    depth = w_in.shape[0]
    nb, seq, _ = x_prompt.shape
    ns, ls, _ = x_sample.shape
    n_samp = ns * ls
    n_short = n_samp + N_META
    short_tm = n_short // 5
    assert short_tm * 5 == n_short and short_tm % 8 == 0 and n_samp % N_META == 0

    gm = norm_mix_g.reshape(depth, 1, D_MODEL)
    gf = norm_ffn_g.reshape(depth, 1, D_MODEL)
    fg = final_norm_g.reshape(1, D_MODEL)

    rope_meta = _rope_tables(np.arange(N_META))
    rope_main = _rope_tables(N_META + np.arange(seq))
    rope_samp = _rope_tables(PAST_LEN + np.arange(ls))
    dec_meta, dec_main, dec_samp = _decay_tables(N_META), _decay_tables(CHUNK), _decay_tables(ls)

    h_main = x_prompt
    zero_s = jnp.zeros((1, HEADS, DH, DH), F32)
    zero_c = jnp.zeros((1, CONV_W - 1, D_MODEL), F32)
    squares32 = (w_ret_o, w_conv_o, w_o)

    s_p, c_p, c_s, kds, vs = [], [], [], [], []
    y_samp = s_s = s_p_stacked = h_short = win_b = None
    for l in range(depth):
        last = l == depth - 1
        if l == 0:
            proj, win_b, wro_b, wco_b, wo_b, h_short = _proj_first(
                x_sample.reshape(n_samp, D_MODEL), meta_tokens.astype(F32), l, gm, w_in, squares32)
        else:
            proj, wro_b, wco_b, wo_b = _proj(h_short, l, gm, win_b, squares32, short_tm)
        gated_m, bgy_m, c_m, s_m = _ret_meta(proj, n_samp, zero_s, zero_c, l, conv_w, rope_meta, dec_meta, 1, N_META)

        h_main, s_l, c_l, wgu_b, wd_b = _mixer_main(h_main, s_m, c_m, l, gm, win_b, conv_w, wro_b, wco_b, wo_b,
                                                    rope_main, dec_main, w_gate_up, w_down)
        s_p.append(s_l)
        c_p.append(c_l)
        res = _ffn_main(h_main.reshape(nb * seq, D_MODEL), l, gf, wgu_b, wd_b, fg, last, MAIN_FFN_TM,
                        proj, state_ret, state_conv, conv_w, rope_samp, dec_samp, ls, kds, vs, s_p, w_in,
                        h_short, gated_m, bgy_m, (wro_b, wco_b, wo_b))
        h_main, c_l = res[0], res[3]
        h_main = h_main.reshape(nb, seq, D_MODEL)
        c_s.append(c_l)
        if last:
            s_s = res[4]
            s_p_stacked = res[5].reshape(depth, nb, HEADS, DH, DH)
            y_samp = res[-1]
        else:
            kds.append(res[4])
            vs.append(res[5])
            win_b = res[6]
            h_short = res[-1]

    return (h_main, y_samp.reshape(ns, ls, D_MODEL), s_p_stacked, jnp.stack(c_p), s_s, jnp.stack(c_s))
```

```python
import functools

import numpy as np
import jax
import jax.numpy as jnp
from jax import lax
from jax.experimental import pallas as pl
from jax.experimental.pallas import tpu as pltpu

D_MODEL = 1024
N_META = 16
HEADS = 8
DH = D_MODEL // HEADS
CHUNK = 128
ROPE_BASE = 10000.0
CONV_W = 3
D_FF = ((8 * D_MODEL + 3 * 256 - 1) // (3 * 256)) * 256
EPS = 1e-6
PAST_LEN = 16384
N_IN = 9 * D_MODEL
C_Q, C_K, C_V, C_G, C_BG, C_CG, C_HC, C_GA, C_GB = (i * D_MODEL for i in range(9))

F32 = jnp.float32
BF16 = jnp.bfloat16

VMEM_LIMIT_BYTES = 58 * 1024 * 1024

MAIN_BG = 4
MAIN_FFN_TM = 512
SHORT_TM = 256
SHORT_STEPS_WITH_META = 5
FFN_PARTS = 2
assert (2 * D_FF // FFN_PARTS) % 256 == 0 and FFN_PARTS % 2 == 0


def _resident(shape):
    nd = len(shape)
    return pl.BlockSpec(shape, lambda *_: (0,) * nd, pipeline_mode=pl.Buffered(1))


def _resident_layer(shape, layer):
    nd = len(shape)
    return pl.BlockSpec((None,) + tuple(shape), lambda *_: (layer,) + (0,) * nd, pipeline_mode=pl.Buffered(1))


def _resident_col_parts(rows, width, n_parts):
    return [pl.BlockSpec((None, rows, width), functools.partial(lambda *_, q: (0, 0, q), q=q),
                         pipeline_mode=pl.Buffered(1)) for q in range(n_parts)]


def _params(sem):
    return pltpu.CompilerParams(dimension_semantics=sem, vmem_limit_bytes=VMEM_LIMIT_BYTES)


def _dot(a, b):
    return jnp.dot(a, b, preferred_element_type=F32)


def _dot_nt(a, b):
    return lax.dot_general(a, b, (((1,), (1,)), ((), ())), preferred_element_type=F32)


def _dot_tn(a, b):
    return lax.dot_general(a, b, (((0,), (0,)), ((), ())), preferred_element_type=F32)


def _rms_f32(x, g):
    return x * lax.rsqrt(jnp.mean(x * x, axis=-1, keepdims=True) + EPS) * g


def _sigmoid(x):
    return 1.0 / (1.0 + jnp.exp(-x))


def _silu(x):
    return x * _sigmoid(x)


def _rotary(t, cos, sin):
    return t * cos + pltpu.roll(t, DH // 2, 1) * sin


def _group_norm(o):
    mu = jnp.mean(o, axis=-1, keepdims=True)
    d = o - mu
    var = jnp.mean(d * d, axis=-1, keepdims=True)
    return d * lax.rsqrt(var + EPS)


def _retention_head_paired(q, kt, kdt, v, s, dmask, qdec, gl):
    L = q.shape[0]
    sc = _dot(q, jnp.concatenate([kt, s.astype(BF16)], axis=1))
    scores = (sc[:, :L] * dmask).astype(BF16)
    iu = _dot(jnp.concatenate([scores, kdt], axis=0), v)
    return iu[:L] + sc[:, L:] * qdec, gl * s + iu[L:]


def _short_conv(u, tail, cw, rows):
    r1 = pltpu.roll(u, 1, 0)
    r2 = pltpu.roll(u, 2, 0)
    t0, t1 = tail[0:1, :], tail[1:2, :]
    sh1 = jnp.where(rows == 0, t1, r1)
    sh2 = jnp.where(rows == 0, t0, jnp.where(rows == 1, t1, r2))
    return cw[0:1, :] * sh2 + cw[1:2, :] * sh1 + cw[2:3, :] * u


def _out_proj(x, gated, bgy, ga, gb, wro_ref, wco_ref, wo_ref):
    ret_out = _dot(gated.astype(BF16), wro_ref[...])
    conv_out = _dot(bgy.astype(BF16), wco_ref[...])
    merged = _sigmoid(ga) * ret_out + _sigmoid(gb) * conv_out
    return x + _dot(merged.astype(BF16), wo_ref[...])


def _swiglu(gates, ups):
    return jnp.concatenate([_silu(a) * b for a, b in zip(gates, ups)], axis=1).astype(BF16)


def _ffn_block(x, g_ref, wgu_refs, wd_ref):
    hn = _rms_f32(x, g_ref[...]).astype(BF16)
    n = len(wgu_refs) // 2
    gates = [_dot(hn, w[...]) for w in wgu_refs[:n]]
    ups = [_dot(hn, w[...]) for w in wgu_refs[n:]]
    return x + _dot(_swiglu(gates, ups), wd_ref[...])


def _log_gamma():
    return np.log1p(-np.exp2(-5.0 - np.arange(HEADS, dtype=np.float64)))


def _const(t):
    return jnp.asarray(np.asarray(t, dtype=np.float32))


def _rope_tables(pos):
    half = DH // 2
    inv = np.power(ROPE_BASE, -np.arange(half, dtype=np.float64) / half)
    ang = np.asarray(pos, dtype=np.float64)[:, None] * inv[None, :]
    cos, sin = np.cos(ang), np.sin(ang)
    cosf = np.concatenate([cos, cos], axis=-1)
    sinf = np.concatenate([-sin, sin], axis=-1)
    scale = DH ** -0.5
    return tuple(_const(t) for t in (cosf, sinf, cosf * scale, sinf * scale))


def _decay_tables(L):
    log_g = _log_gamma()
    idx = np.arange(L, dtype=np.float64)
    diff = idx[:, None] - idx[None, :]
    dmask = np.where(diff >= 0, np.exp(log_g[:, None, None] * np.maximum(diff, 0.0)[None]), 0.0)
    qdec = np.exp(log_g[:, None] * (idx + 1.0)[None])
    kdec = np.exp(log_g[:, None] * (L - 1.0 - idx)[None])
    gl = np.exp(log_g * L)
    lanes = lambda t: np.repeat(t.T, DH, axis=1)
    return tuple(_const(t) for t in (dmask, lanes(qdec), lanes(kdec), np.repeat(gl, DH)[None, :], kdec[:, None, :]))


def _mixer_main_kernel(x_ref, s0_ref, c0_ref, g_ref, *refs):
    n_groups = N_IN // D_MODEL
    wq_ref, wk_ref, wv_ref, wg_ref, wbg_ref, wcg_ref, whc_ref, wga_ref, wgb_ref = refs[:n_groups]
    (cw_ref, wro_ref, wco_ref, wo_ref, cq_ref, sq_ref, ck_ref, sk_ref, dmask_ref, qdec_ref, kdec_ref, gl_ref,
     wgu32_ref, wd32_ref, h_ref, s_ref, c_ref, wgub_ref, wdb_ref, q_s, kt_s, kdt_s, v_s, o_s) = refs[n_groups:]
    bg = x_ref.shape[0]
    m = bg * CHUNK

    wgub_ref[...] = wgu32_ref[...].astype(BF16)
    step = pl.program_id(0) * pl.num_programs(1) + pl.program_id(1)

    @pl.when(step < D_FF // wd32_ref.shape[0])
    def _():
        wdb_ref[...] = wd32_ref[...].astype(BF16)

    @pl.when(pl.program_id(1) == 0)
    def _():
        for b in range(bg):
            s_ref[b] = s0_ref[...]
            c_ref[b] = c0_ref[...]

    x = x_ref[...].reshape(m, D_MODEL)
    hn = _rms_f32(x, g_ref[...]).astype(BF16)

    cq, sq, ck, sk = cq_ref[...], sq_ref[...], ck_ref[...], sk_ref[...]
    q = _dot(hn, wq_ref[...])
    k = _dot(hn, wk_ref[...])
    v_s[...] = _dot(hn, wv_ref[...]).astype(BF16)
    for b in range(bg):
        r = slice(b * CHUNK, (b + 1) * CHUNK)
        for h in range(HEADS):
            cl = slice(h * DH, (h + 1) * DH)
            q_s[r, cl] = _rotary(q[r, cl], cq, sq).astype(BF16)
            krt = _rotary(k[r, cl], ck, sk).T
            kt_s[b * HEADS + h] = krt.astype(BF16)
            kdt_s[b * HEADS + h] = (krt * kdec_ref[h]).astype(BF16)

    for b in range(bg):
        r = slice(b * CHUNK, (b + 1) * CHUNK)
        for h in range(HEADS):
            cl = slice(h * DH, (h + 1) * DH)
            o, s_new = _retention_head_paired(q_s[r, cl], kt_s[b * HEADS + h], kdt_s[b * HEADS + h], v_s[r, cl],
                                              s_ref[b, h], dmask_ref[h], qdec_ref[:, cl], gl_ref[:, cl])
            s_ref[b, h] = s_new
            o_s[r, cl] = _group_norm(o)

    g = _dot(hn, wg_ref[...])
    gated = _silu(g) * o_s[...]

    bgate = _dot(hn, wbg_ref[...])
    u = _dot(hn, wcg_ref[...]) * _dot(hn, whc_ref[...])
    rows = lax.broadcasted_iota(jnp.int32, (CHUNK, D_MODEL), 0)
    cw = cw_ref[...]
    ys = []
    for b in range(bg):
        ub = u[b * CHUNK:(b + 1) * CHUNK]
        ys.append(_short_conv(ub, c_ref[b], cw, rows))
        c_ref[b] = ub[CHUNK - (CONV_W - 1):, :]
    bgy = bgate * jnp.concatenate(ys, axis=0)

    ga = _dot(hn, wga_ref[...])
    gb = _dot(hn, wgb_ref[...])
    out = _out_proj(x, gated, bgy, ga, gb, wro_ref, wco_ref, wo_ref)
    h_ref[...] = out.reshape(bg, CHUNK, D_MODEL)


def _mixer_main(x, s0, c0, layer, g, win, cw, wro, wco, wo, rope, decay, wgu32, wd32):
    nb, seq, _ = x.shape
    bg = MAIN_BG
    cq, sq, ck, sk = rope
    dmask, qdec, _, gl, kdec = decay
    m = bg * CHUNK
    n_chunks = seq // CHUNK
    steps = (nb // bg) * n_chunks
    slab = D_MODEL // steps
    wd_slab = 16 * (-(-D_FF // (16 * steps)))
    while D_FF % wd_slab:
        wd_slab += 16
    wd_last = D_FF // wd_slab - 1
    assert slab * steps == D_MODEL and slab % 16 == 0 and wd_last < steps

    def wd_block(i, c):
        return jnp.minimum(i * n_chunks + c, wd_last)

    rope_spec = pl.BlockSpec((CHUNK, DH), lambda i, c: (c, 0))
    stream_specs = [
        pl.BlockSpec((bg, CHUNK, D_MODEL), lambda i, c: (i, c, 0)),
        rope_spec, rope_spec, rope_spec, rope_spec,
        pl.BlockSpec((None, slab, 2 * D_FF), lambda i, c: (layer, i * n_chunks + c, 0)),
        pl.BlockSpec((None, wd_slab, D_MODEL), lambda i, c: (layer, wd_block(i, c), 0)),
    ]
    out_specs = [
        pl.BlockSpec((bg, CHUNK, D_MODEL), lambda i, c: (i, c, 0)),
        pl.BlockSpec((bg, HEADS, DH, DH), lambda i, c: (i, 0, 0, 0)),
        pl.BlockSpec((bg, CONV_W - 1, D_MODEL), lambda i, c: (i, 0, 0)),
        pl.BlockSpec((None, slab, 2 * D_FF), lambda i, c: (0, i * n_chunks + c, 0)),
        pl.BlockSpec((None, wd_slab, D_MODEL), lambda i, c: (0, wd_block(i, c), 0)),
    ]
    stream_args = [x, cq, sq, ck, sk, wgu32, wd32]
    resident_args = [s0, c0, g, win, cw, wro, wco, wo, dmask, qdec, kdec, gl]
    n_res, n_in, n_out = len(resident_args), len(stream_args), len(out_specs)
    n_groups = N_IN // D_MODEL

    def outer(*refs):
        s0_ref, c0_ref, g_ref, win_ref, cw_ref, wro_ref, wco_ref, wo_ref = refs[:8]
        table_refs = refs[8:n_res]
        hbm_in, hbm_out = refs[n_res:n_res + n_in], refs[n_res + n_in:n_res + n_in + n_out]
        scratch = refs[n_res + n_in + n_out:]
        win_parts = [win_ref.at[0, :, pl.ds(j * D_MODEL, D_MODEL)] for j in range(n_groups)]

        def step(x_b, cq_b, sq_b, ck_b, sk_b, wgu32_b, wd32_b, *out_blocks):
            _mixer_main_kernel(x_b, s0_ref.at[0], c0_ref.at[0], g_ref.at[layer], *win_parts, cw_ref.at[layer],
                               wro_ref.at[0], wco_ref.at[0], wo_ref.at[0], cq_b, sq_b, ck_b, sk_b, *table_refs,
                               wgu32_b, wd32_b, *out_blocks, *scratch)

        pltpu.emit_pipeline(step, grid=(nb // bg, n_chunks), in_specs=stream_specs,
                            out_specs=out_specs)(*hbm_in, *hbm_out)

    vmem = pl.BlockSpec(memory_space=pltpu.VMEM)
    hbm = pl.BlockSpec(memory_space=pl.ANY)
    return pl.pallas_call(
        outer,
        in_specs=[vmem] * n_res + [hbm] * n_in,
        out_specs=[hbm] * n_out,
        out_shape=[
            jax.ShapeDtypeStruct(x.shape, F32),
            jax.ShapeDtypeStruct((nb, HEADS, DH, DH), F32),
            jax.ShapeDtypeStruct((nb, CONV_W - 1, D_MODEL), F32),
            jax.ShapeDtypeStruct((1, D_MODEL, 2 * D_FF), BF16),
            jax.ShapeDtypeStruct((1, D_FF, D_MODEL), BF16),
        ],
        scratch_shapes=[
            pltpu.VMEM((m, D_MODEL), BF16),
            pltpu.VMEM((bg * HEADS, DH, CHUNK), BF16),
            pltpu.VMEM((bg * HEADS, DH, CHUNK), BF16),
            pltpu.VMEM((m, D_MODEL), BF16),
            pltpu.VMEM((m, D_MODEL), F32),
        ],
        compiler_params=pltpu.CompilerParams(vmem_limit_bytes=VMEM_LIMIT_BYTES),
        name="mixer_main",
    )(*resident_args, *stream_args)


def _round_square_slabs(step, n_slabs, sq_refs, sb_refs):
    @pl.when(step < n_slabs)
    def _():
        for src, dst in zip(sq_refs, sb_refs):
            dst[...] = src[...].astype(BF16)


def _square_slab_specs(layer, slab, n_slabs):
    slab_in = pl.BlockSpec((None, slab, D_MODEL), lambda j: (layer, jnp.minimum(j, n_slabs - 1), 0))
    slab_out = pl.BlockSpec((None, slab, D_MODEL), lambda j: (0, jnp.minimum(j, n_slabs - 1), 0))
    return [slab_in] * 3, [slab_out] * 3, [jax.ShapeDtypeStruct((1, D_MODEL, D_MODEL), BF16)] * 3


def _proj_first_kernel(xs_ref, xm_ref, g_ref, w32_ref, sq0_ref, sq1_ref, sq2_ref,
                       o_ref, wb_ref, sb0_ref, sb1_ref, sb2_ref, xcat_ref, hn_s, *, n_slabs):
    j = pl.program_id(0)
    n_samp = xs_ref.shape[0]

    @pl.when(j == 0)
    def _():
        xcat_ref[:n_samp, :] = xs_ref[...]
        xcat_ref[n_samp:, :] = xm_ref[...]
        hn_s[:n_samp, :] = _rms_f32(xs_ref[...], g_ref[...]).astype(BF16)
        hn_s[n_samp:, :] = _rms_f32(xm_ref[...], g_ref[...]).astype(BF16)

    _round_square_slabs(j, n_slabs, (sq0_ref, sq1_ref, sq2_ref), (sb0_ref, sb1_ref, sb2_ref))
    wb = w32_ref[...].astype(BF16)
    wb_ref[...] = wb
    o_ref[...] = _dot(hn_s[...], wb)


def _proj_first(xs, xm, layer, g, win32, squares32):
    rows = xs.shape[0] + xm.shape[0]
    n_col = N_IN // D_MODEL
    slab = 128
    n_slabs = D_MODEL // slab
    assert n_slabs <= n_col
    sq_in, sq_out, sq_shapes = _square_slab_specs(layer, slab, n_slabs)
    return pl.pallas_call(
        functools.partial(_proj_first_kernel, n_slabs=n_slabs),
        grid=(n_col,),
        in_specs=[_resident(xs.shape), _resident(xm.shape), _resident_layer((1, D_MODEL), layer),
                  pl.BlockSpec((None, D_MODEL, D_MODEL), lambda j: (layer, 0, j))] + sq_in,
        out_specs=[pl.BlockSpec((rows, D_MODEL), lambda j: (0, j)),
                   pl.BlockSpec((None, D_MODEL, D_MODEL), lambda j: (0, 0, j))] + sq_out
        + [pl.BlockSpec((rows, D_MODEL), lambda j: (0, 0))],
        out_shape=[jax.ShapeDtypeStruct((rows, N_IN), F32), jax.ShapeDtypeStruct((1, D_MODEL, N_IN), BF16)]
        + sq_shapes + [jax.ShapeDtypeStruct((rows, D_MODEL), F32)],
        scratch_shapes=[pltpu.VMEM((rows, D_MODEL), BF16)],
        compiler_params=_params(("arbitrary",)),
        name="proj_short_first",
    )(xs, xm, g, win32, *squares32)


def _proj_kernel(x_ref, g_ref, win_ref, sq0_ref, sq1_ref, sq2_ref, o_ref, sb0_ref, sb1_ref, sb2_ref, *, n_slabs):
    _round_square_slabs(pl.program_id(0), n_slabs, (sq0_ref, sq1_ref, sq2_ref), (sb0_ref, sb1_ref, sb2_ref))
    hn = _rms_f32(x_ref[...], g_ref[...]).astype(BF16)
    o_ref[...] = _dot(hn, win_ref[...])


def _proj(x, layer, g, win, squares32, tm):
    rows = x.shape[0]
    steps = rows // tm
    slab = 256
    n_slabs = D_MODEL // slab
    assert n_slabs <= steps and steps * tm == rows
    sq_in, sq_out, sq_shapes = _square_slab_specs(layer, slab, n_slabs)
    return pl.pallas_call(
        functools.partial(_proj_kernel, n_slabs=n_slabs),
        grid=(steps,),
        in_specs=[pl.BlockSpec((tm, D_MODEL), lambda i: (i, 0)), _resident_layer((1, D_MODEL), layer),
                  _resident_layer((D_MODEL, N_IN), 0)] + sq_in,
        out_specs=[pl.BlockSpec((tm, N_IN), lambda i: (i, 0))] + sq_out,
        out_shape=[jax.ShapeDtypeStruct((rows, N_IN), F32)] + sq_shapes,
        compiler_params=_params(("arbitrary",)),
        name="proj_short",
    )(x, g, win, *squares32)


def _ret_short_first(p_ref, s_ref, tab_refs, sb, L, sink):
    cq_ref, sq_ref, ck_ref, sk_ref, dmask_ref, qdec_ref, kdec_ref = tab_refs
    cq, sq, ck, sk = cq_ref[...], sq_ref[...], ck_ref[...], sk_ref[...]
    staged = []
    for b in range(sb):
        r = slice(b * L, (b + 1) * L)
        for h in range(HEADS):
            cl = slice(h * DH, (h + 1) * DH)
            q = _rotary(p_ref[r, C_Q + h * DH:C_Q + (h + 1) * DH], cq, sq).astype(BF16)
            k = _rotary(p_ref[r, C_K + h * DH:C_K + (h + 1) * DH], ck, sk)
            v = p_ref[r, C_V + h * DH:C_V + (h + 1) * DH]
            s = s_ref[b, h]
            scores = (_dot_nt(q, k.astype(BF16)) * dmask_ref[h]).astype(BF16)
            cross = _dot(q, s.astype(BF16)) * qdec_ref[:, cl]
            sink(b, h, r, cl, s, k * kdec_ref[:, cl], v)
            staged.append((scores, cross, v.astype(BF16)))
    return staged


def _ret_short_second(staged, p_ref, c_ref, cw_ref, gated_ref, bgy_ref, cn_ref, sb, L):
    rows = lax.broadcasted_iota(jnp.int32, (L, D_MODEL), 0)
    cw = cw_ref[...]
    for b in range(sb):
        r = slice(b * L, (b + 1) * L)
        for h in range(HEADS):
            cl = slice(h * DH, (h + 1) * DH)
            scores, cross, v = staged[b * HEADS + h]
            o = _dot(scores, v) + cross
            g = p_ref[r, C_G + h * DH:C_G + (h + 1) * DH]
            gated_ref[r, cl] = _silu(g) * _group_norm(o)
        u = p_ref[r, C_CG:C_CG + D_MODEL] * p_ref[r, C_HC:C_HC + D_MODEL]
        y = _short_conv(u, c_ref[b], cw, rows)
        bgy_ref[r, :] = p_ref[r, C_BG:C_BG + D_MODEL] * y
        cn_ref[b] = u[L - (CONV_W - 1):, :]


def _state_sink(sn_ref, gl_ref, layer=None):
    def sink(b, h, r, cl, s, kd, v):
        s_new = gl_ref[:, cl] * s + _dot_tn(kd.astype(BF16), v.astype(BF16))
        if layer is None:
            sn_ref[b, h] = s_new
        else:
            sn_ref[layer, b, h] = s_new
    return sink


def _ret_meta_kernel(p_ref, s_ref, c_ref, cw_ref, cq_ref, sq_ref, ck_ref, sk_ref, dmask_ref, qdec_ref, kdec_ref,
                     gl_ref, gated_ref, bgy_ref, cn_ref, sn_ref, *, L):
    sb = c_ref.shape[0]
    tabs = (cq_ref, sq_ref, ck_ref, sk_ref, dmask_ref, qdec_ref, kdec_ref)
    staged = _ret_short_first(p_ref, s_ref, tabs, sb, L, _state_sink(sn_ref, gl_ref))
    _ret_short_second(staged, p_ref, c_ref, cw_ref, gated_ref, bgy_ref, cn_ref, sb, L)


def _short_table_specs(L):
    return [_resident((L, DH))] * 4 + [_resident((HEADS, L, L)), _resident((L, D_MODEL)), _resident((L, D_MODEL)),
                                       _resident((1, D_MODEL))]


def _ret_meta(proj, row0, state, cprev, layer, cw, rope, decay, nseq, L):
    dmask, qdec, kdec, gl, _ = decay
    rows = nseq * L
    row_spec = pl.BlockSpec((rows, D_MODEL), lambda i: (0, 0))
    row_shape = jax.ShapeDtypeStruct((rows, D_MODEL), F32)
    return pl.pallas_call(
        functools.partial(_ret_meta_kernel, L=L),
        grid=(1,),
        in_specs=[
            pl.BlockSpec((rows, C_GA), lambda i: (row0 // rows, 0)),
            _resident((nseq, HEADS, DH, DH)),
            _resident((nseq, CONV_W - 1, D_MODEL)),
            _resident_layer((CONV_W, D_MODEL), layer),
        ] + _short_table_specs(L),
        out_specs=[row_spec, row_spec, pl.BlockSpec((nseq, CONV_W - 1, D_MODEL), lambda i: (0, 0, 0)),
                   pl.BlockSpec((nseq, HEADS, DH, DH), lambda i: (0, 0, 0, 0))],
        out_shape=[row_shape, row_shape, jax.ShapeDtypeStruct((nseq, CONV_W - 1, D_MODEL), F32),
                   jax.ShapeDtypeStruct((nseq, HEADS, DH, DH), F32)],
        compiler_params=_params(("arbitrary",)),
        name="ret_meta",
    )(proj, state, cprev, cw, *rope, dmask, qdec, kdec, gl)


def _ffn_main_kernel(x_ref, g_ref, *refs, final, n_prev, L):
    wgu_refs = refs[:FFN_PARTS]
    (wd_ref, fg_ref, p_ref, s_ref, c_ref, cw_ref, cq_ref, sq_ref, ck_ref, sk_ref, dmask_ref, qdec_ref, kdec_ref,
     gl_ref) = refs[FFN_PARTS:FFN_PARTS + 14]
    rest = refs[FFN_PARTS + 14:]
    sb = c_ref.shape[0]
    tabs = (cq_ref, sq_ref, ck_ref, sk_ref, dmask_ref, qdec_ref, kdec_ref)

    if final:
        n_main = n_prev + 1
        prev, mains, outs = rest[:3 * n_prev], rest[3 * n_prev:3 * n_prev + n_main], rest[3 * n_prev + n_main:]
        y_ref, gated_ref, bgy_ref, cn_ref, sn_ref, mstack_ref = outs
        for l in range(n_main):
            mstack_ref[l] = mains[l][...]
        sink = _state_sink(sn_ref, gl_ref, n_prev)
    else:
        wn32_ref, y_ref, gated_ref, bgy_ref, cn_ref, kd_ref, v_ref, wnb_ref = rest
        wnb_ref[...] = wn32_ref[...].astype(BF16)

        def sink(b, h, r, cl, s, kd, v):
            kd_ref[r, cl] = kd
            v_ref[r, cl] = v

    x = x_ref[...]
    hn = _rms_f32(x, g_ref[...]).astype(BF16)
    gates = [_dot(hn, w[...]) for w in wgu_refs[:FFN_PARTS // 2]]
    if final:
        for l in range(n_prev):
            so_ref, kdo_ref, vo_ref = prev[3 * l:3 * l + 3]
            upd = _state_sink(sn_ref, gl_ref, l)
            for b in range(sb):
                r = slice(b * L, (b + 1) * L)
                for h in range(HEADS):
                    cl = slice(h * DH, (h + 1) * DH)
                    upd(b, h, r, cl, so_ref[b, h], kdo_ref[r, cl], vo_ref[r, cl])
    staged = _ret_short_first(p_ref, s_ref, tabs, sb, L, sink)
    ups = [_dot(hn, w[...]) for w in wgu_refs[FFN_PARTS // 2:]]
    _ret_short_second(staged, p_ref, c_ref, cw_ref, gated_ref, bgy_ref, cn_ref, sb, L)
    y = x + _dot(_swiglu(gates, ups), wd_ref[...])
    if final:
        y = _rms_f32(y, fg_ref[...])
    y_ref[...] = y


def _ffn_main(x, layer, g, wgu, wd, fg, final, tm, proj, state, cconv, cw, rope, decay, L, kds, vs, main_states,
              win32, h_short, gated_m, bgy_m, squares):
    rows = x.shape[0]
    steps = rows // tm
    depth, ns = state.shape[:2]
    sb = ns // steps
    assert sb * steps == ns and (sb * L) % 8 == 0
    n_prev = len(kds) if final else 0
    dmask, qdec, kdec, gl, _ = decay
    n_samp = ns * L
    row_spec = pl.BlockSpec((tm, D_MODEL), lambda i: (i, 0))
    srow_spec = pl.BlockSpec((sb * L, D_MODEL), lambda i: (i, 0))
    srow_shape = jax.ShapeDtypeStruct((n_samp, D_MODEL), F32)

    def state_spec(l):
        return pl.BlockSpec((None, sb, HEADS, DH, DH), lambda i: (l, i, 0, 0, 0))

    stream_specs = [
        row_spec,
        pl.BlockSpec((sb * L, C_GA), lambda i: (i, 0)),
        state_spec(layer),
        pl.BlockSpec((None, sb, CONV_W - 1, D_MODEL), lambda i: (layer, i, 0, 0)),
    ]
    stream_args = [x, proj, state, cconv]
    for l in range(n_prev):
        stream_specs += [state_spec(l), srow_spec, srow_spec]
        stream_args += [state, kds[l], vs[l]]
    n_meta = 0 if final else h_short.shape[0] - n_samp
    short_rows = n_samp + n_meta
    branch_shape = jax.ShapeDtypeStruct((short_rows, D_MODEL), F32)
    out_specs = [row_spec, srow_spec, srow_spec, pl.BlockSpec((sb, CONV_W - 1, D_MODEL), lambda i: (i, 0, 0))]
    out_shape = [jax.ShapeDtypeStruct(x.shape, F32), branch_shape, branch_shape,
                 jax.ShapeDtypeStruct((ns, CONV_W - 1, D_MODEL), F32)]
    if final:
        assert n_prev == depth - 1 and len(main_states) == depth
        out_specs.append(pl.BlockSpec((depth, sb, HEADS, DH, DH), lambda i: (0, i, 0, 0, 0)))
        out_shape.append(jax.ShapeDtypeStruct(state.shape, F32))
        n_mat = main_states[0].shape[0] * HEADS
        mb = n_mat // steps
        assert mb * steps == n_mat
        stream_specs += [pl.BlockSpec((mb, DH, DH), lambda i: (i, 0, 0))] * depth
        stream_args += [s.reshape(n_mat, DH, DH) for s in main_states]
        out_specs.append(pl.BlockSpec((depth, mb, DH, DH), lambda i: (0, i, 0, 0)))
        out_shape.append(jax.ShapeDtypeStruct((depth, n_mat, DH, DH), F32))
    else:
        slab = D_MODEL // steps
        assert slab * steps == D_MODEL and slab % 16 == 0
        stream_specs.append(pl.BlockSpec((None, slab, N_IN), lambda i: (layer + 1, i, 0)))
        stream_args.append(win32)
        out_specs += [srow_spec, srow_spec, pl.BlockSpec((None, slab, N_IN), lambda i: (0, i, 0))]
        out_shape += [srow_shape, srow_shape, jax.ShapeDtypeStruct((1, D_MODEL, N_IN), BF16)]

    resident_args = [g, wgu, wd, fg, cw, *rope, dmask, qdec, kdec, gl]
    n_res, n_in = len(resident_args), len(stream_args)
    part = 2 * D_FF // FFN_PARTS

    short_tm = SHORT_TM if final else short_rows // SHORT_STEPS_WITH_META
    assert short_rows % short_tm == 0 and short_tm % 8 == 0
    short_args = [h_short, *squares] + ([] if final else [gated_m, bgy_m])
    out_shape.append(jax.ShapeDtypeStruct((short_rows, D_MODEL), F32))
    tile = pl.BlockSpec((short_tm, D_MODEL), lambda i: (i, 0))

    def outer(*refs):
        g_ref, wgu_ref, wd_ref, fg_ref, cw_ref = refs[:5]
        table_refs = refs[5:n_res]
        hbm_in = refs[n_res:n_res + n_in]
        hbm_short = refs[n_res + n_in:n_res + n_in + len(short_args)]
        hbm_out = refs[n_res + n_in + len(short_args):]
        wgu_parts = [wgu_ref.at[0, :, pl.ds(q * part, part)] for q in range(FFN_PARTS)]

        def step(*blocks):
            x_ref, p_ref, s_ref, c_ref = blocks[:4]
            _ffn_main_kernel(x_ref, g_ref.at[layer], *wgu_parts, wd_ref.at[0], fg_ref, p_ref, s_ref, c_ref,
                             cw_ref.at[layer], *table_refs, *blocks[4:], final=final, n_prev=n_prev, L=L)

        pltpu.emit_pipeline(step, grid=(steps,), in_specs=stream_specs, out_specs=out_specs)(*hbm_in, *hbm_out[:-1])

        p_hbm = hbm_in[1]
        gated_hbm, bgy_hbm, ys_hbm = hbm_out[1], hbm_out[2], hbm_out[-1]
        hs_hbm, square_hbm = hbm_short[0], hbm_short[1:4]

        def short_phase(wro_v, wco_v, wo_v, sems):
            copies = [pltpu.make_async_copy(src.at[0], dst, sems.at[n])
                      for n, (src, dst) in enumerate(zip(square_hbm, (wro_v, wco_v, wo_v)))]
            if n_meta:
                meta_rows = pl.ds(n_samp, n_meta)
                copies += [pltpu.make_async_copy(hbm_short[4], gated_hbm.at[meta_rows], sems.at[3]),
                           pltpu.make_async_copy(hbm_short[5], bgy_hbm.at[meta_rows], sems.at[4])]
            for cp in copies:
                cp.start()
            for cp in copies:
                cp.wait()

            def tile_body(x_ref, gated_ref, bgy_ref, ga_ref, gb_ref, o_ref):
                h = _out_proj(x_ref[...], gated_ref[...], bgy_ref[...], ga_ref[...], gb_ref[...], wro_v, wco_v, wo_v)
                y = _ffn_block(h, g_ref.at[layer], wgu_parts, wd_ref.at[0])
                o_ref[...] = _rms_f32(y, fg_ref[...]) if final else y

            ga_tile = pl.BlockSpec((short_tm, D_MODEL), lambda i: (i, C_GA // D_MODEL))
            gb_tile = pl.BlockSpec((short_tm, D_MODEL), lambda i: (i, C_GB // D_MODEL))
            pltpu.emit_pipeline(tile_body, grid=(short_rows // short_tm,), in_specs=[tile, tile, tile, ga_tile, gb_tile],
                                out_specs=[tile])(hs_hbm, gated_hbm, bgy_hbm, p_hbm, p_hbm, ys_hbm)

        pl.run_scoped(short_phase, pltpu.VMEM((D_MODEL, D_MODEL), BF16), pltpu.VMEM((D_MODEL, D_MODEL), BF16),
                      pltpu.VMEM((D_MODEL, D_MODEL), BF16), pltpu.SemaphoreType.DMA((5,)))

    vmem = pl.BlockSpec(memory_space=pltpu.VMEM)
    hbm = pl.BlockSpec(memory_space=pl.ANY)
    return pl.pallas_call(
        outer,
        in_specs=[vmem] * n_res + [hbm] * (n_in + len(short_args)),
        out_specs=[hbm] * len(out_shape),
        out_shape=out_shape,
        compiler_params=pltpu.CompilerParams(vmem_limit_bytes=VMEM_LIMIT_BYTES),
        name="ffn_main",
    )(*resident_args, *stream_args, *short_args)


def kernel(x_prompt, x_sample, state_ret, state_conv, meta_tokens, norm_mix_g, w_in, conv_w, w_ret_o,
           w_conv_o, w_o, norm_ffn_g, w_gate_up, w_down, final_norm_g):
    depth = w_in.shape[0]
    nb, seq, _ = x_prompt.shape
    ns, ls, _ = x_sample.shape
    n_samp = ns * ls
    n_short = n_samp + N_META
    short_tm = n_short // 5
    assert short_tm * 5 == n_short and short_tm % 8 == 0 and n_samp % N_META == 0

    gm = norm_mix_g.reshape(depth, 1, D_MODEL)
    gf = norm_ffn_g.reshape(depth, 1, D_MODEL)
    fg = final_norm_g.reshape(1, D_MODEL)

    rope_meta = _rope_tables(np.arange(N_META))
    rope_main = _rope_tables(N_META + np.arange(seq))
    rope_samp = _rope_tables(PAST_LEN + np.arange(ls))
    dec_meta, dec_main, dec_samp = _decay_tables(N_META), _decay_tables(CHUNK), _decay_tables(ls)

    h_main = x_prompt
    zero_s = jnp.zeros((1, HEADS, DH, DH), F32)
    zero_c = jnp.zeros((1, CONV_W - 1, D_MODEL), F32)
    squares32 = (w_ret_o, w_conv_o, w_o)

    s_p, c_p, c_s, kds, vs = [], [], [], [], []
    y_samp = s_s = s_p_stacked = h_short = win_b = None
    for l in range(depth):
        last = l == depth - 1
        if l == 0:
            proj, win_b, wro_b, wco_b, wo_b, h_short = _proj_first(
                x_sample.reshape(n_samp, D_MODEL), meta_tokens.astype(F32), l, gm, w_in, squares32)
        else:
            proj, wro_b, wco_b, wo_b = _proj(h_short, l, gm, win_b, squares32, short_tm)
        gated_m, bgy_m, c_m, s_m = _ret_meta(proj, n_samp, zero_s, zero_c, l, conv_w, rope_meta, dec_meta, 1, N_META)

        h_main, s_l, c_l, wgu_b, wd_b = _mixer_main(h_main, s_m, c_m, l, gm, win_b, conv_w, wro_b, wco_b, wo_b,
                                                    rope_main, dec_main, w_gate_up, w_down)
        s_p.append(s_l)
        c_p.append(c_l)
        res = _ffn_main(h_main.reshape(nb * seq, D_MODEL), l, gf, wgu_b, wd_b, fg, last, MAIN_FFN_TM,
                        proj, state_ret, state_conv, conv_w, rope_samp, dec_samp, ls, kds, vs, s_p, w_in,
                        h_short, gated_m, bgy_m, (wro_b, wco_b, wo_b))
        h_main, c_l = res[0], res[3]
        h_main = h_main.reshape(nb, seq, D_MODEL)
        c_s.append(c_l)
        if last:
            s_s = res[4]
            s_p_stacked = res[5].reshape(depth, nb, HEADS, DH, DH)
            y_samp = res[-1]
        else:
            kds.append(res[4])
            vs.append(res[5])
            win_b = res[6]
            h_short = res[-1]

    return (h_main, y_samp.reshape(ns, ls, D_MODEL), s_p_stacked, jnp.stack(c_p), s_s, jnp.stack(c_s))
```

```python
import functools

import numpy as np
import jax
import jax.numpy as jnp
from jax import lax
from jax.experimental import pallas as pl
from jax.experimental.pallas import tpu as pltpu

D_MODEL = 1024
N_META = 16
HEADS = 8
DH = D_MODEL // HEADS
CHUNK = 128
ROPE_BASE = 10000.0
CONV_W = 3
D_FF = ((8 * D_MODEL + 3 * 256 - 1) // (3 * 256)) * 256
EPS = 1e-6
PAST_LEN = 16384
N_IN = 9 * D_MODEL
C_Q, C_K, C_V, C_G, C_BG, C_CG, C_HC, C_GA, C_GB = (i * D_MODEL for i in range(9))

F32 = jnp.float32
BF16 = jnp.bfloat16

VMEM_LIMIT_BYTES = 58 * 1024 * 1024

MAIN_BG = 4
MAIN_FFN_TM = 512
SHORT_TM = 256
SHORT_STEPS_WITH_META = 5
FFN_PARTS = 2
assert (2 * D_FF // FFN_PARTS) % 256 == 0 and FFN_PARTS % 2 == 0


def _resident(shape):
    nd = len(shape)
    return pl.BlockSpec(shape, lambda *_: (0,) * nd, pipeline_mode=pl.Buffered(1))


def _resident_layer(shape, layer):
    nd = len(shape)
    return pl.BlockSpec((None,) + tuple(shape), lambda *_: (layer,) + (0,) * nd, pipeline_mode=pl.Buffered(1))


def _resident_col_parts(rows, width, n_parts):
    return [pl.BlockSpec((None, rows, width), functools.partial(lambda *_, q: (0, 0, q), q=q),
                         pipeline_mode=pl.Buffered(1)) for q in range(n_parts)]


def _params(sem):
    return pltpu.CompilerParams(dimension_semantics=sem, vmem_limit_bytes=VMEM_LIMIT_BYTES)


def _dot(a, b):
    return jnp.dot(a, b, preferred_element_type=F32)


def _dot_nt(a, b):
    return lax.dot_general(a, b, (((1,), (1,)), ((), ())), preferred_element_type=F32)


def _dot_tn(a, b):
    return lax.dot_general(a, b, (((0,), (0,)), ((), ())), preferred_element_type=F32)


def _rms_f32(x, g):
    return x * lax.rsqrt(jnp.mean(x * x, axis=-1, keepdims=True) + EPS) * g


def _sigmoid(x):
    return 1.0 / (1.0 + jnp.exp(-x))


def _silu(x):
    return x * _sigmoid(x)


def _rotary(t, cos, sin):
    return t * cos + pltpu.roll(t, DH // 2, 1) * sin


def _group_norm(o):
    mu = jnp.mean(o, axis=-1, keepdims=True)
    d = o - mu
    var = jnp.mean(d * d, axis=-1, keepdims=True)
    return d * lax.rsqrt(var + EPS)


def _retention_head_paired(q, kt, kdt, v, s, dmask, qdec, gl):
    L = q.shape[0]
    sc = _dot(q, jnp.concatenate([kt, s.astype(BF16)], axis=1))
    scores = (sc[:, :L] * dmask).astype(BF16)
    iu = _dot(jnp.concatenate([scores, kdt], axis=0), v)
    return iu[:L] + sc[:, L:] * qdec, gl * s + iu[L:]


def _short_conv(u, tail, cw, rows):
    r1 = pltpu.roll(u, 1, 0)
    r2 = pltpu.roll(u, 2, 0)
    t0, t1 = tail[0:1, :], tail[1:2, :]
    sh1 = jnp.where(rows == 0, t1, r1)
    sh2 = jnp.where(rows == 0, t0, jnp.where(rows == 1, t1, r2))
    return cw[0:1, :] * sh2 + cw[1:2, :] * sh1 + cw[2:3, :] * u


def _out_proj(x, gated, bgy, ga, gb, wro_ref, wco_ref, wo_ref):
    ret_out = _dot(gated.astype(BF16), wro_ref[...])
    conv_out = _dot(bgy.astype(BF16), wco_ref[...])
    merged = _sigmoid(ga) * ret_out + _sigmoid(gb) * conv_out
    return x + _dot(merged.astype(BF16), wo_ref[...])


def _swiglu(gates, ups):
    return jnp.concatenate([_silu(a) * b for a, b in zip(gates, ups)], axis=1).astype(BF16)


def _ffn_block(x, g_ref, wgu_refs, wd_ref):
    hn = _rms_f32(x, g_ref[...]).astype(BF16)
    n = len(wgu_refs) // 2
    gates = [_dot(hn, w[...]) for w in wgu_refs[:n]]
    ups = [_dot(hn, w[...]) for w in wgu_refs[n:]]
    return x + _dot(_swiglu(gates, ups), wd_ref[...])


def _log_gamma():
    return np.log1p(-np.exp2(-5.0 - np.arange(HEADS, dtype=np.float64)))


def _const(t):
    return jnp.asarray(np.asarray(t, dtype=np.float32))


def _rope_tables(pos):
    half = DH // 2
    inv = np.power(ROPE_BASE, -np.arange(half, dtype=np.float64) / half)
    ang = np.asarray(pos, dtype=np.float64)[:, None] * inv[None, :]
    cos, sin = np.cos(ang), np.sin(ang)
    cosf = np.concatenate([cos, cos], axis=-1)
    sinf = np.concatenate([-sin, sin], axis=-1)
    scale = DH ** -0.5
    return tuple(_const(t) for t in (cosf, sinf, cosf * scale, sinf * scale))


def _decay_tables(L):
    log_g = _log_gamma()
    idx = np.arange(L, dtype=np.float64)
    diff = idx[:, None] - idx[None, :]
    dmask = np.where(diff >= 0, np.exp(log_g[:, None, None] * np.maximum(diff, 0.0)[None]), 0.0)
    qdec = np.exp(log_g[:, None] * (idx + 1.0)[None])
    kdec = np.exp(log_g[:, None] * (L - 1.0 - idx)[None])
    gl = np.exp(log_g * L)
    lanes = lambda t: np.repeat(t.T, DH, axis=1)
    return tuple(_const(t) for t in (dmask, lanes(qdec), lanes(kdec), np.repeat(gl, DH)[None, :], kdec[:, None, :]))


def _mixer_main_kernel(x_ref, s0_ref, c0_ref, g_ref, *refs):
    n_groups = N_IN // D_MODEL
    wq_ref, wk_ref, wv_ref, wg_ref, wbg_ref, wcg_ref, whc_ref, wga_ref, wgb_ref = refs[:n_groups]
    (cw_ref, wro_ref, wco_ref, wo_ref, cq_ref, sq_ref, ck_ref, sk_ref, dmask_ref, qdec_ref, kdec_ref, gl_ref,
     wgu32_ref, wd32_ref, h_ref, s_ref, c_ref, wgub_ref, wdb_ref, q_s, kt_s, kdt_s, v_s, o_s) = refs[n_groups:]
    bg = x_ref.shape[0]
    m = bg * CHUNK

    wgub_ref[...] = wgu32_ref[...].astype(BF16)
    step = pl.program_id(0) * pl.num_programs(1) + pl.program_id(1)

    @pl.when(step < D_FF // wd32_ref.shape[0])
    def _():
        wdb_ref[...] = wd32_ref[...].astype(BF16)

    @pl.when(pl.program_id(1) == 0)
    def _():
        for b in range(bg):
            s_ref[b] = s0_ref[...]
            c_ref[b] = c0_ref[...]

    x = x_ref[...].reshape(m, D_MODEL)
    hn = _rms_f32(x, g_ref[...]).astype(BF16)

    cq, sq, ck, sk = cq_ref[...], sq_ref[...], ck_ref[...], sk_ref[...]
    q = _dot(hn, wq_ref[...])
    k = _dot(hn, wk_ref[...])
    v_s[...] = _dot(hn, wv_ref[...]).astype(BF16)
    for b in range(bg):
        r = slice(b * CHUNK, (b + 1) * CHUNK)
        for h in range(HEADS):
            cl = slice(h * DH, (h + 1) * DH)
            q_s[r, cl] = _rotary(q[r, cl], cq, sq).astype(BF16)
            krt = _rotary(k[r, cl], ck, sk).T
            kt_s[b * HEADS + h] = krt.astype(BF16)
            kdt_s[b * HEADS + h] = (krt * kdec_ref[h]).astype(BF16)

    for b in range(bg):
        r = slice(b * CHUNK, (b + 1) * CHUNK)
        for h in range(HEADS):
            cl = slice(h * DH, (h + 1) * DH)
            o, s_new = _retention_head_paired(q_s[r, cl], kt_s[b * HEADS + h], kdt_s[b * HEADS + h], v_s[r, cl],
                                              s_ref[b, h], dmask_ref[h], qdec_ref[:, cl], gl_ref[:, cl])
            s_ref[b, h] = s_new
            o_s[r, cl] = _group_norm(o)

    g = _dot(hn, wg_ref[...])
    gated = _silu(g) * o_s[...]

    bgate = _dot(hn, wbg_ref[...])
    u = _dot(hn, wcg_ref[...]) * _dot(hn, whc_ref[...])
    rows = lax.broadcasted_iota(jnp.int32, (CHUNK, D_MODEL), 0)
    cw = cw_ref[...]
    ys = []
    for b in range(bg):
        ub = u[b * CHUNK:(b + 1) * CHUNK]
        ys.append(_short_conv(ub, c_ref[b], cw, rows))
        c_ref[b] = ub[CHUNK - (CONV_W - 1):, :]
    bgy = bgate * jnp.concatenate(ys, axis=0)

    ga = _dot(hn, wga_ref[...])
    gb = _dot(hn, wgb_ref[...])
    out = _out_proj(x, gated, bgy, ga, gb, wro_ref, wco_ref, wo_ref)
    h_ref[...] = out.reshape(bg, CHUNK, D_MODEL)


def _mixer_main(x, s0, c0, layer, g, win, cw, wro, wco, wo, rope, decay, wgu32, wd32):
    nb, seq, _ = x.shape
    bg = MAIN_BG
    cq, sq, ck, sk = rope
    dmask, qdec, _, gl, kdec = decay
    m = bg * CHUNK
    n_chunks = seq // CHUNK
    steps = (nb // bg) * n_chunks
    slab = D_MODEL // steps
    wd_slab = 16 * (-(-D_FF // (16 * steps)))
    while D_FF % wd_slab:
        wd_slab += 16
    wd_last = D_FF // wd_slab - 1
    assert slab * steps == D_MODEL and slab % 16 == 0 and wd_last < steps

    def wd_block(i, c):
        return jnp.minimum(i * n_chunks + c, wd_last)

    rope_spec = pl.BlockSpec((CHUNK, DH), lambda i, c: (c, 0))
    stream_specs = [
        pl.BlockSpec((bg, CHUNK, D_MODEL), lambda i, c: (i, c, 0)),
        rope_spec, rope_spec, rope_spec, rope_spec,
        pl.BlockSpec((None, slab, 2 * D_FF), lambda i, c: (layer, i * n_chunks + c, 0)),
        pl.BlockSpec((None, wd_slab, D_MODEL), lambda i, c: (layer, wd_block(i, c), 0)),
    ]
    out_specs = [
        pl.BlockSpec((bg, CHUNK, D_MODEL), lambda i, c: (i, c, 0)),
        pl.BlockSpec((bg, HEADS, DH, DH), lambda i, c: (i, 0, 0, 0)),
        pl.BlockSpec((bg, CONV_W - 1, D_MODEL), lambda i, c: (i, 0, 0)),
        pl.BlockSpec((None, slab, 2 * D_FF), lambda i, c: (0, i * n_chunks + c, 0)),
        pl.BlockSpec((None, wd_slab, D_MODEL), lambda i, c: (0, wd_block(i, c), 0)),
    ]
    stream_args = [x, cq, sq, ck, sk, wgu32, wd32]
    resident_args = [s0, c0, g, win, cw, wro, wco, wo, dmask, qdec, kdec, gl]
    n_res, n_in, n_out = len(resident_args), len(stream_args), len(out_specs)
    n_groups = N_IN // D_MODEL

    def outer(*refs):
        s0_ref, c0_ref, g_ref, win_ref, cw_ref, wro_ref, wco_ref, wo_ref = refs[:8]
        table_refs = refs[8:n_res]
        hbm_in, hbm_out = refs[n_res:n_res + n_in], refs[n_res + n_in:n_res + n_in + n_out]
        scratch = refs[n_res + n_in + n_out:]
        win_parts = [win_ref.at[0, :, pl.ds(j * D_MODEL, D_MODEL)] for j in range(n_groups)]

        def step(x_b, cq_b, sq_b, ck_b, sk_b, wgu32_b, wd32_b, *out_blocks):
            _mixer_main_kernel(x_b, s0_ref.at[0], c0_ref.at[0], g_ref.at[layer], *win_parts, cw_ref.at[layer],
                               wro_ref.at[0], wco_ref.at[0], wo_ref.at[0], cq_b, sq_b, ck_b, sk_b, *table_refs,
                               wgu32_b, wd32_b, *out_blocks, *scratch)

        pltpu.emit_pipeline(step, grid=(nb // bg, n_chunks), in_specs=stream_specs,
                            out_specs=out_specs)(*hbm_in, *hbm_out)

    vmem = pl.BlockSpec(memory_space=pltpu.VMEM)
    hbm = pl.BlockSpec(memory_space=pl.ANY)
    return pl.pallas_call(
        outer,
        in_specs=[vmem] * n_res + [hbm] * n_in,
        out_specs=[hbm] * n_out,
        out_shape=[
            jax.ShapeDtypeStruct(x.shape, F32),
            jax.ShapeDtypeStruct((nb, HEADS, DH, DH), F32),
            jax.ShapeDtypeStruct((nb, CONV_W - 1, D_MODEL), F32),
            jax.ShapeDtypeStruct((1, D_MODEL, 2 * D_FF), BF16),
            jax.ShapeDtypeStruct((1, D_FF, D_MODEL), BF16),
        ],
        scratch_shapes=[
            pltpu.VMEM((m, D_MODEL), BF16),
            pltpu.VMEM((bg * HEADS, DH, CHUNK), BF16),
            pltpu.VMEM((bg * HEADS, DH, CHUNK), BF16),
            pltpu.VMEM((m, D_MODEL), BF16),
            pltpu.VMEM((m, D_MODEL), F32),
        ],
        compiler_params=pltpu.CompilerParams(vmem_limit_bytes=VMEM_LIMIT_BYTES),
        name="mixer_main",
    )(*resident_args, *stream_args)


def _round_square_slabs(step, n_slabs, sq_refs, sb_refs):
    @pl.when(step < n_slabs)
    def _():
        for src, dst in zip(sq_refs, sb_refs):
            dst[...] = src[...].astype(BF16)


def _square_slab_specs(layer, slab, n_slabs):
    slab_in = pl.BlockSpec((None, slab, D_MODEL), lambda j: (layer, jnp.minimum(j, n_slabs - 1), 0))
    slab_out = pl.BlockSpec((None, slab, D_MODEL), lambda j: (0, jnp.minimum(j, n_slabs - 1), 0))
    return [slab_in] * 3, [slab_out] * 3, [jax.ShapeDtypeStruct((1, D_MODEL, D_MODEL), BF16)] * 3


def _proj_first_kernel(xs_ref, xm_ref, g_ref, w32_ref, sq0_ref, sq1_ref, sq2_ref,
                       o_ref, wb_ref, sb0_ref, sb1_ref, sb2_ref, xcat_ref, hn_s, *, n_slabs):
    j = pl.program_id(0)
    n_samp = xs_ref.shape[0]

    @pl.when(j == 0)
    def _():
        xcat_ref[:n_samp, :] = xs_ref[...]
        xcat_ref[n_samp:, :] = xm_ref[...]
        hn_s[:n_samp, :] = _rms_f32(xs_ref[...], g_ref[...]).astype(BF16)
        hn_s[n_samp:, :] = _rms_f32(xm_ref[...], g_ref[...]).astype(BF16)

    _round_square_slabs(j, n_slabs, (sq0_ref, sq1_ref, sq2_ref), (sb0_ref, sb1_ref, sb2_ref))
    wb = w32_ref[...].astype(BF16)
    wb_ref[...] = wb
    o_ref[...] = _dot(hn_s[...], wb)


def _proj_first(xs, xm, layer, g, win32, squares32):
    rows = xs.shape[0] + xm.shape[0]
    n_col = N_IN // D_MODEL
    slab = 128
    n_slabs = D_MODEL // slab
    assert n_slabs <= n_col
    sq_in, sq_out, sq_shapes = _square_slab_specs(layer, slab, n_slabs)
    return pl.pallas_call(
        functools.partial(_proj_first_kernel, n_slabs=n_slabs),
        grid=(n_col,),
        in_specs=[_resident(xs.shape), _resident(xm.shape), _resident_layer((1, D_MODEL), layer),
                  pl.BlockSpec((None, D_MODEL, D_MODEL), lambda j: (layer, 0, j))] + sq_in,
        out_specs=[pl.BlockSpec((rows, D_MODEL), lambda j: (0, j)),
                   pl.BlockSpec((None, D_MODEL, D_MODEL), lambda j: (0, 0, j))] + sq_out
        + [pl.BlockSpec((rows, D_MODEL), lambda j: (0, 0))],
        out_shape=[jax.ShapeDtypeStruct((rows, N_IN), F32), jax.ShapeDtypeStruct((1, D_MODEL, N_IN), BF16)]
        + sq_shapes + [jax.ShapeDtypeStruct((rows, D_MODEL), F32)],
        scratch_shapes=[pltpu.VMEM((rows, D_MODEL), BF16)],
        compiler_params=_params(("arbitrary",)),
        name="proj_short_first",
    )(xs, xm, g, win32, *squares32)


def _proj_kernel(x_ref, g_ref, win_ref, o_ref):
    hn = _rms_f32(x_ref[...], g_ref[...]).astype(BF16)
    o_ref[...] = _dot(hn, win_ref[...])


def _proj(x, layer, g, win, tm):
    rows = x.shape[0]
    assert rows % tm == 0
    return pl.pallas_call(
        _proj_kernel,
        grid=(rows // tm,),
        in_specs=[pl.BlockSpec((tm, D_MODEL), lambda i: (i, 0)), _resident_layer((1, D_MODEL), layer),
                  _resident_layer((D_MODEL, N_IN), 0)],
        out_specs=pl.BlockSpec((tm, N_IN), lambda i: (i, 0)),
        out_shape=jax.ShapeDtypeStruct((rows, N_IN), F32),
        compiler_params=_params(("arbitrary",)),
        name="proj_short",
    )(x, g, win)


def _ret_short_first(p_ref, s_ref, tab_refs, sb, L, sink):
    cq_ref, sq_ref, ck_ref, sk_ref, dmask_ref, qdec_ref, kdec_ref = tab_refs
    cq, sq, ck, sk = cq_ref[...], sq_ref[...], ck_ref[...], sk_ref[...]
    staged = []
    for b in range(sb):
        r = slice(b * L, (b + 1) * L)
        for h in range(HEADS):
            cl = slice(h * DH, (h + 1) * DH)
            q = _rotary(p_ref[r, C_Q + h * DH:C_Q + (h + 1) * DH], cq, sq).astype(BF16)
            k = _rotary(p_ref[r, C_K + h * DH:C_K + (h + 1) * DH], ck, sk)
            v = p_ref[r, C_V + h * DH:C_V + (h + 1) * DH]
            s = s_ref[b, h]
            scores = (_dot_nt(q, k.astype(BF16)) * dmask_ref[h]).astype(BF16)
            cross = _dot(q, s.astype(BF16)) * qdec_ref[:, cl]
            sink(b, h, r, cl, s, k * kdec_ref[:, cl], v)
            staged.append((scores, cross, v.astype(BF16)))
    return staged


def _ret_short_second(staged, p_ref, c_ref, cw_ref, gated_ref, bgy_ref, cn_ref, sb, L):
    rows = lax.broadcasted_iota(jnp.int32, (L, D_MODEL), 0)
    cw = cw_ref[...]
    for b in range(sb):
        r = slice(b * L, (b + 1) * L)
        for h in range(HEADS):
            cl = slice(h * DH, (h + 1) * DH)
            scores, cross, v = staged[b * HEADS + h]
            o = _dot(scores, v) + cross
            g = p_ref[r, C_G + h * DH:C_G + (h + 1) * DH]
            gated_ref[r, cl] = _silu(g) * _group_norm(o)
        u = p_ref[r, C_CG:C_CG + D_MODEL] * p_ref[r, C_HC:C_HC + D_MODEL]
        y = _short_conv(u, c_ref[b], cw, rows)
        bgy_ref[r, :] = p_ref[r, C_BG:C_BG + D_MODEL] * y
        cn_ref[b] = u[L - (CONV_W - 1):, :]


def _state_sink(sn_ref, gl_ref, layer=None):
    def sink(b, h, r, cl, s, kd, v):
        s_new = gl_ref[:, cl] * s + _dot_tn(kd.astype(BF16), v.astype(BF16))
        if layer is None:
            sn_ref[b, h] = s_new
        else:
            sn_ref[layer, b, h] = s_new
    return sink


def _ret_meta_kernel(p_ref, s_ref, c_ref, cw_ref, cq_ref, sq_ref, ck_ref, sk_ref, dmask_ref, qdec_ref, kdec_ref,
                     gl_ref, gated_ref, bgy_ref, cn_ref, sn_ref, *, L):
    sb = c_ref.shape[0]
    tabs = (cq_ref, sq_ref, ck_ref, sk_ref, dmask_ref, qdec_ref, kdec_ref)
    staged = _ret_short_first(p_ref, s_ref, tabs, sb, L, _state_sink(sn_ref, gl_ref))
    _ret_short_second(staged, p_ref, c_ref, cw_ref, gated_ref, bgy_ref, cn_ref, sb, L)


def _short_table_specs(L):
    return [_resident((L, DH))] * 4 + [_resident((HEADS, L, L)), _resident((L, D_MODEL)), _resident((L, D_MODEL)),
                                       _resident((1, D_MODEL))]


def _ret_meta(proj, row0, state, cprev, layer, cw, rope, decay, nseq, L):
    dmask, qdec, kdec, gl, _ = decay
    rows = nseq * L
    row_spec = pl.BlockSpec((rows, D_MODEL), lambda i: (0, 0))
    row_shape = jax.ShapeDtypeStruct((rows, D_MODEL), F32)
    return pl.pallas_call(
        functools.partial(_ret_meta_kernel, L=L),
        grid=(1,),
        in_specs=[
            pl.BlockSpec((rows, C_GA), lambda i: (row0 // rows, 0)),
            _resident((nseq, HEADS, DH, DH)),
            _resident((nseq, CONV_W - 1, D_MODEL)),
            _resident_layer((CONV_W, D_MODEL), layer),
        ] + _short_table_specs(L),
        out_specs=[row_spec, row_spec, pl.BlockSpec((nseq, CONV_W - 1, D_MODEL), lambda i: (0, 0, 0)),
                   pl.BlockSpec((nseq, HEADS, DH, DH), lambda i: (0, 0, 0, 0))],
        out_shape=[row_shape, row_shape, jax.ShapeDtypeStruct((nseq, CONV_W - 1, D_MODEL), F32),
                   jax.ShapeDtypeStruct((nseq, HEADS, DH, DH), F32)],
        compiler_params=_params(("arbitrary",)),
        name="ret_meta",
    )(proj, state, cprev, cw, *rope, dmask, qdec, kdec, gl)


def _ffn_main_kernel(x_ref, g_ref, *refs, final, n_prev, L):
    wgu_refs = refs[:FFN_PARTS]
    (wd_ref, fg_ref, p_ref, s_ref, c_ref, cw_ref, cq_ref, sq_ref, ck_ref, sk_ref, dmask_ref, qdec_ref, kdec_ref,
     gl_ref) = refs[FFN_PARTS:FFN_PARTS + 14]
    rest = refs[FFN_PARTS + 14:]
    sb = c_ref.shape[0]
    tabs = (cq_ref, sq_ref, ck_ref, sk_ref, dmask_ref, qdec_ref, kdec_ref)

    if final:
        n_main = n_prev + 1
        prev, mains, outs = rest[:3 * n_prev], rest[3 * n_prev:3 * n_prev + n_main], rest[3 * n_prev + n_main:]
        y_ref, gated_ref, bgy_ref, cn_ref, sn_ref, mstack_ref = outs
        for l in range(n_main):
            mstack_ref[l] = mains[l][...]
        sink = _state_sink(sn_ref, gl_ref, n_prev)
    else:
        next32, (y_ref, gated_ref, bgy_ref, cn_ref, kd_ref, v_ref), nextb = rest[:4], rest[4:10], rest[10:]
        for src, dst in zip(next32, nextb):
            dst[...] = src[...].astype(BF16)

        def sink(b, h, r, cl, s, kd, v):
            kd_ref[r, cl] = kd
            v_ref[r, cl] = v

    x = x_ref[...]
    hn = _rms_f32(x, g_ref[...]).astype(BF16)
    gates = [_dot(hn, w[...]) for w in wgu_refs[:FFN_PARTS // 2]]
    if final:
        for l in range(n_prev):
            so_ref, kdo_ref, vo_ref = prev[3 * l:3 * l + 3]
            upd = _state_sink(sn_ref, gl_ref, l)
            for b in range(sb):
                r = slice(b * L, (b + 1) * L)
                for h in range(HEADS):
                    cl = slice(h * DH, (h + 1) * DH)
                    upd(b, h, r, cl, so_ref[b, h], kdo_ref[r, cl], vo_ref[r, cl])
    staged = _ret_short_first(p_ref, s_ref, tabs, sb, L, sink)
    ups = [_dot(hn, w[...]) for w in wgu_refs[FFN_PARTS // 2:]]
    _ret_short_second(staged, p_ref, c_ref, cw_ref, gated_ref, bgy_ref, cn_ref, sb, L)
    y = x + _dot(_swiglu(gates, ups), wd_ref[...])
    if final:
        y = _rms_f32(y, fg_ref[...])
    y_ref[...] = y


def _ffn_main(x, layer, g, wgu, wd, fg, final, tm, proj, state, cconv, cw, rope, decay, L, kds, vs, main_states,
              win32, squares32, h_short, gated_m, bgy_m, squares):
    rows = x.shape[0]
    steps = rows // tm
    depth, ns = state.shape[:2]
    sb = ns // steps
    assert sb * steps == ns and (sb * L) % 8 == 0
    n_prev = len(kds) if final else 0
    dmask, qdec, kdec, gl, _ = decay
    n_samp = ns * L
    row_spec = pl.BlockSpec((tm, D_MODEL), lambda i: (i, 0))
    srow_spec = pl.BlockSpec((sb * L, D_MODEL), lambda i: (i, 0))
    srow_shape = jax.ShapeDtypeStruct((n_samp, D_MODEL), F32)

    def state_spec(l):
        return pl.BlockSpec((None, sb, HEADS, DH, DH), lambda i: (l, i, 0, 0, 0))

    stream_specs = [
        row_spec,
        pl.BlockSpec((sb * L, C_GA), lambda i: (i, 0)),
        state_spec(layer),
        pl.BlockSpec((None, sb, CONV_W - 1, D_MODEL), lambda i: (layer, i, 0, 0)),
    ]
    stream_args = [x, proj, state, cconv]
    for l in range(n_prev):
        stream_specs += [state_spec(l), srow_spec, srow_spec]
        stream_args += [state, kds[l], vs[l]]
    n_meta = 0 if final else h_short.shape[0] - n_samp
    short_rows = n_samp + n_meta
    branch_shape = jax.ShapeDtypeStruct((short_rows, D_MODEL), F32)
    out_specs = [row_spec, srow_spec, srow_spec, pl.BlockSpec((sb, CONV_W - 1, D_MODEL), lambda i: (i, 0, 0))]
    out_shape = [jax.ShapeDtypeStruct(x.shape, F32), branch_shape, branch_shape,
                 jax.ShapeDtypeStruct((ns, CONV_W - 1, D_MODEL), F32)]
    if final:
        assert n_prev == depth - 1 and len(main_states) == depth
        out_specs.append(pl.BlockSpec((depth, sb, HEADS, DH, DH), lambda i: (0, i, 0, 0, 0)))
        out_shape.append(jax.ShapeDtypeStruct(state.shape, F32))
        n_mat = main_states[0].shape[0] * HEADS
        mb = n_mat // steps
        assert mb * steps == n_mat
        stream_specs += [pl.BlockSpec((mb, DH, DH), lambda i: (i, 0, 0))] * depth
        stream_args += [s.reshape(n_mat, DH, DH) for s in main_states]
        out_specs.append(pl.BlockSpec((depth, mb, DH, DH), lambda i: (0, i, 0, 0)))
        out_shape.append(jax.ShapeDtypeStruct((depth, n_mat, DH, DH), F32))
    else:
        slab = D_MODEL // steps
        assert slab * steps == D_MODEL and slab % 16 == 0
        out_specs += [srow_spec, srow_spec]
        out_shape += [srow_shape, srow_shape]
        for w32 in (win32, *squares32):
            width = w32.shape[-1]
            stream_specs.append(pl.BlockSpec((None, slab, width), lambda i: (layer + 1, i, 0)))
            stream_args.append(w32)
            out_specs.append(pl.BlockSpec((None, slab, width), lambda i: (0, i, 0)))
            out_shape.append(jax.ShapeDtypeStruct((1, D_MODEL, width), BF16))

    resident_args = [g, wgu, wd, fg, cw, *rope, dmask, qdec, kdec, gl]
    n_res, n_in = len(resident_args), len(stream_args)
    part = 2 * D_FF // FFN_PARTS

    short_tm = SHORT_TM if final else short_rows // SHORT_STEPS_WITH_META
    assert short_rows % short_tm == 0 and short_tm % 8 == 0
    short_args = [h_short, *squares] + ([] if final else [gated_m, bgy_m])
    out_shape.append(jax.ShapeDtypeStruct((short_rows, D_MODEL), F32))
    tile = pl.BlockSpec((short_tm, D_MODEL), lambda i: (i, 0))

    def outer(*refs):
        g_ref, wgu_ref, wd_ref, fg_ref, cw_ref = refs[:5]
        table_refs = refs[5:n_res]
        hbm_in = refs[n_res:n_res + n_in]
        hbm_short = refs[n_res + n_in:n_res + n_in + len(short_args)]
        hbm_out = refs[n_res + n_in + len(short_args):]
        wgu_parts = [wgu_ref.at[0, :, pl.ds(q * part, part)] for q in range(FFN_PARTS)]

        def step(*blocks):
            x_ref, p_ref, s_ref, c_ref = blocks[:4]
            _ffn_main_kernel(x_ref, g_ref.at[layer], *wgu_parts, wd_ref.at[0], fg_ref, p_ref, s_ref, c_ref,
                             cw_ref.at[layer], *table_refs, *blocks[4:], final=final, n_prev=n_prev, L=L)

        pltpu.emit_pipeline(step, grid=(steps,), in_specs=stream_specs, out_specs=out_specs)(*hbm_in, *hbm_out[:-1])

        p_hbm = hbm_in[1]
        gated_hbm, bgy_hbm, ys_hbm = hbm_out[1], hbm_out[2], hbm_out[-1]
        hs_hbm, square_hbm = hbm_short[0], hbm_short[1:4]

        def short_phase(wro_v, wco_v, wo_v, sems):
            copies = [pltpu.make_async_copy(src.at[0], dst, sems.at[n])
                      for n, (src, dst) in enumerate(zip(square_hbm, (wro_v, wco_v, wo_v)))]
            if n_meta:
                meta_rows = pl.ds(n_samp, n_meta)
                copies += [pltpu.make_async_copy(hbm_short[4], gated_hbm.at[meta_rows], sems.at[3]),
                           pltpu.make_async_copy(hbm_short[5], bgy_hbm.at[meta_rows], sems.at[4])]
            for cp in copies:
                cp.start()
            for cp in copies:
                cp.wait()

            def tile_body(x_ref, gated_ref, bgy_ref, ga_ref, gb_ref, o_ref):
                h = _out_proj(x_ref[...], gated_ref[...], bgy_ref[...], ga_ref[...], gb_ref[...], wro_v, wco_v, wo_v)
                y = _ffn_block(h, g_ref.at[layer], wgu_parts, wd_ref.at[0])
                o_ref[...] = _rms_f32(y, fg_ref[...]) if final else y

            ga_tile = pl.BlockSpec((short_tm, D_MODEL), lambda i: (i, C_GA // D_MODEL))
            gb_tile = pl.BlockSpec((short_tm, D_MODEL), lambda i: (i, C_GB // D_MODEL))
            pltpu.emit_pipeline(tile_body, grid=(short_rows // short_tm,), in_specs=[tile, tile, tile, ga_tile, gb_tile],
                                out_specs=[tile])(hs_hbm, gated_hbm, bgy_hbm, p_hbm, p_hbm, ys_hbm)

        pl.run_scoped(short_phase, pltpu.VMEM((D_MODEL, D_MODEL), BF16), pltpu.VMEM((D_MODEL, D_MODEL), BF16),
                      pltpu.VMEM((D_MODEL, D_MODEL), BF16), pltpu.SemaphoreType.DMA((5,)))

    vmem = pl.BlockSpec(memory_space=pltpu.VMEM)
    hbm = pl.BlockSpec(memory_space=pl.ANY)
    return pl.pallas_call(
        outer,
        in_specs=[vmem] * n_res + [hbm] * (n_in + len(short_args)),
        out_specs=[hbm] * len(out_shape),
        out_shape=out_shape,
        compiler_params=pltpu.CompilerParams(vmem_limit_bytes=VMEM_LIMIT_BYTES),
        name="ffn_main",
    )(*resident_args, *stream_args, *short_args)


def kernel(x_prompt, x_sample, state_ret, state_conv, meta_tokens, norm_mix_g, w_in, conv_w, w_ret_o,
           w_conv_o, w_o, norm_ffn_g, w_gate_up, w_down, final_norm_g):
    depth = w_in.shape[0]
    nb, seq, _ = x_prompt.shape
    ns, ls, _ = x_sample.shape
    n_samp = ns * ls
    n_short = n_samp + N_META
    short_tm = n_short // 5
    assert short_tm * 5 == n_short and short_tm % 8 == 0 and n_samp % N_META == 0

    gm = norm_mix_g.reshape(depth, 1, D_MODEL)
    gf = norm_ffn_g.reshape(depth, 1, D_MODEL)
    fg = final_norm_g.reshape(1, D_MODEL)

    rope_meta = _rope_tables(np.arange(N_META))
    rope_main = _rope_tables(N_META + np.arange(seq))
    rope_samp = _rope_tables(PAST_LEN + np.arange(ls))
    dec_meta, dec_main, dec_samp = _decay_tables(N_META), _decay_tables(CHUNK), _decay_tables(ls)

    h_main = x_prompt
    zero_s = jnp.zeros((1, HEADS, DH, DH), F32)
    zero_c = jnp.zeros((1, CONV_W - 1, D_MODEL), F32)
    squares32 = (w_ret_o, w_conv_o, w_o)

    s_p, c_p, c_s, kds, vs = [], [], [], [], []
    y_samp = s_s = s_p_stacked = h_short = next_weights = None
    for l in range(depth):
        last = l == depth - 1
        if l == 0:
            proj, win_b, wro_b, wco_b, wo_b, h_short = _proj_first(
                x_sample.reshape(n_samp, D_MODEL), meta_tokens.astype(F32), l, gm, w_in, squares32)
        else:
            win_b, wro_b, wco_b, wo_b = next_weights
            proj = _proj(h_short, l, gm, win_b, short_tm)
        gated_m, bgy_m, c_m, s_m = _ret_meta(proj, n_samp, zero_s, zero_c, l, conv_w, rope_meta, dec_meta, 1, N_META)

        h_main, s_l, c_l, wgu_b, wd_b = _mixer_main(h_main, s_m, c_m, l, gm, win_b, conv_w, wro_b, wco_b, wo_b,
                                                    rope_main, dec_main, w_gate_up, w_down)
        s_p.append(s_l)
        c_p.append(c_l)
        res = _ffn_main(h_main.reshape(nb * seq, D_MODEL), l, gf, wgu_b, wd_b, fg, last, MAIN_FFN_TM,
                        proj, state_ret, state_conv, conv_w, rope_samp, dec_samp, ls, kds, vs, s_p, w_in, squares32,
                        h_short, gated_m, bgy_m, (wro_b, wco_b, wo_b))
        h_main, c_l = res[0], res[3]
        h_main = h_main.reshape(nb, seq, D_MODEL)
        c_s.append(c_l)
        if last:
            s_s = res[4]
            s_p_stacked = res[5].reshape(depth, nb, HEADS, DH, DH)
            y_samp = res[-1]
        else:
            kds.append(res[4])
            vs.append(res[5])
            next_weights = res[6:10]
            h_short = res[-1]

    return (h_main, y_samp.reshape(ns, ls, D_MODEL), s_p_stacked, jnp.stack(c_p), s_s, jnp.stack(c_s))
```

```python
import functools

import numpy as np
import jax
import jax.numpy as jnp
from jax import lax
from jax.experimental import pallas as pl
from jax.experimental.pallas import tpu as pltpu

D_MODEL = 1024
N_META = 16
HEADS = 8
DH = D_MODEL // HEADS
CHUNK = 128
ROPE_BASE = 10000.0
CONV_W = 3
D_FF = ((8 * D_MODEL + 3 * 256 - 1) // (3 * 256)) * 256
EPS = 1e-6
PAST_LEN = 16384
N_IN = 9 * D_MODEL
C_Q, C_K, C_V, C_G, C_BG, C_CG, C_HC, C_GA, C_GB = (i * D_MODEL for i in range(9))

F32 = jnp.float32
BF16 = jnp.bfloat16

VMEM_LIMIT_BYTES = 58 * 1024 * 1024

MAIN_BG = 4
MAIN_FFN_TM = 512
SHORT_TM = 256
SHORT_STEPS_WITH_META = 5
FFN_PARTS = 2
assert (2 * D_FF // FFN_PARTS) % 256 == 0 and FFN_PARTS % 2 == 0


def _resident(shape):
    nd = len(shape)
    return pl.BlockSpec(shape, lambda *_: (0,) * nd, pipeline_mode=pl.Buffered(1))


def _resident_layer(shape, layer):
    nd = len(shape)
    return pl.BlockSpec((None,) + tuple(shape), lambda *_: (layer,) + (0,) * nd, pipeline_mode=pl.Buffered(1))


def _resident_col_parts(rows, width, n_parts):
    return [pl.BlockSpec((None, rows, width), functools.partial(lambda *_, q: (0, 0, q), q=q),
                         pipeline_mode=pl.Buffered(1)) for q in range(n_parts)]


def _params(sem):
    return pltpu.CompilerParams(dimension_semantics=sem, vmem_limit_bytes=VMEM_LIMIT_BYTES)


def _dot(a, b):
    return jnp.dot(a, b, preferred_element_type=F32)


def _dot_nt(a, b):
    return lax.dot_general(a, b, (((1,), (1,)), ((), ())), preferred_element_type=F32)


def _dot_tn(a, b):
    return lax.dot_general(a, b, (((0,), (0,)), ((), ())), preferred_element_type=F32)


def _rms_f32(x, g):
    return x * lax.rsqrt(jnp.mean(x * x, axis=-1, keepdims=True) + EPS) * g


def _sigmoid(x):
    return 1.0 / (1.0 + jnp.exp(-x))


def _silu(x):
    return x * _sigmoid(x)


def _rotary(t, cos, sin):
    return t * cos + pltpu.roll(t, DH // 2, 1) * sin


def _group_norm(o):
    mu = jnp.mean(o, axis=-1, keepdims=True)
    d = o - mu
    var = jnp.mean(d * d, axis=-1, keepdims=True)
    return d * lax.rsqrt(var + EPS)


def _retention_head_paired(q, kt, kdt, v, s, dmask, qdec, gl):
    L = q.shape[0]
    sc = _dot(q, jnp.concatenate([kt, s.astype(BF16)], axis=1))
    scores = (sc[:, :L] * dmask).astype(BF16)
    iu = _dot(jnp.concatenate([scores, kdt], axis=0), v)
    return iu[:L] + sc[:, L:] * qdec, gl * s + iu[L:]


def _short_conv(u, tail, cw, rows):
    r1 = pltpu.roll(u, 1, 0)
    r2 = pltpu.roll(u, 2, 0)
    t0, t1 = tail[0:1, :], tail[1:2, :]
    sh1 = jnp.where(rows == 0, t1, r1)
    sh2 = jnp.where(rows == 0, t0, jnp.where(rows == 1, t1, r2))
    return cw[0:1, :] * sh2 + cw[1:2, :] * sh1 + cw[2:3, :] * u


def _out_proj(x, gated, bgy, ga, gb, wro_ref, wco_ref, wo_ref):
    ret_out = _dot(gated.astype(BF16), wro_ref[...])
    conv_out = _dot(bgy.astype(BF16), wco_ref[...])
    merged = _sigmoid(ga) * ret_out + _sigmoid(gb) * conv_out
    return x + _dot(merged.astype(BF16), wo_ref[...])


def _swiglu(gates, ups):
    return jnp.concatenate([_silu(a) * b for a, b in zip(gates, ups)], axis=1).astype(BF16)


def _ffn_block(x, g_ref, wgu_refs, wd_ref):
    hn = _rms_f32(x, g_ref[...]).astype(BF16)
    n = len(wgu_refs) // 2
    gates = [_dot(hn, w[...]) for w in wgu_refs[:n]]
    ups = [_dot(hn, w[...]) for w in wgu_refs[n:]]
    return x + _dot(_swiglu(gates, ups), wd_ref[...])


def _log_gamma():
    return np.log1p(-np.exp2(-5.0 - np.arange(HEADS, dtype=np.float64)))


def _const(t):
    return jnp.asarray(np.asarray(t, dtype=np.float32))


def _rope_tables(pos):
    half = DH // 2
    inv = np.power(ROPE_BASE, -np.arange(half, dtype=np.float64) / half)
    ang = np.asarray(pos, dtype=np.float64)[:, None] * inv[None, :]
    cos, sin = np.cos(ang), np.sin(ang)
    cosf = np.concatenate([cos, cos], axis=-1)
    sinf = np.concatenate([-sin, sin], axis=-1)
    scale = DH ** -0.5
    return tuple(_const(t) for t in (cosf, sinf, cosf * scale, sinf * scale))


def _decay_tables(L):
    log_g = _log_gamma()
    idx = np.arange(L, dtype=np.float64)
    diff = idx[:, None] - idx[None, :]
    dmask = np.where(diff >= 0, np.exp(log_g[:, None, None] * np.maximum(diff, 0.0)[None]), 0.0)
    qdec = np.exp(log_g[:, None] * (idx + 1.0)[None])
    kdec = np.exp(log_g[:, None] * (L - 1.0 - idx)[None])
    gl = np.exp(log_g * L)
    lanes = lambda t: np.repeat(t.T, DH, axis=1)
    return tuple(_const(t) for t in (dmask, lanes(qdec), lanes(kdec), np.repeat(gl, DH)[None, :], kdec[:, None, :]))


def _mixer_main_kernel(x_ref, s0_ref, c0_ref, g_ref, *refs):
    n_groups = N_IN // D_MODEL
    wq_ref, wk_ref, wv_ref, wg_ref, wbg_ref, wcg_ref, whc_ref, wga_ref, wgb_ref = refs[:n_groups]
    (cw_ref, wro_ref, wco_ref, wo_ref, cq_ref, sq_ref, ck_ref, sk_ref, dmask_ref, qdec_ref, kdec_ref, gl_ref,
     wgu32_ref, wd32_ref, h_ref, s_ref, c_ref, wgub_ref, wdb_ref, q_s, kt_s, kdt_s, v_s, o_s) = refs[n_groups:]
    bg = x_ref.shape[0]
    m = bg * CHUNK

    wgub_ref[...] = wgu32_ref[...].astype(BF16)
    step = pl.program_id(0) * pl.num_programs(1) + pl.program_id(1)

    @pl.when(step < D_FF // wd32_ref.shape[0])
    def _():
        wdb_ref[...] = wd32_ref[...].astype(BF16)

    @pl.when(pl.program_id(1) == 0)
    def _():
        for b in range(bg):
            s_ref[b] = s0_ref[...]
            c_ref[b] = c0_ref[...]

    x = x_ref[...].reshape(m, D_MODEL)
    hn = _rms_f32(x, g_ref[...]).astype(BF16)

    cq, sq, ck, sk = cq_ref[...], sq_ref[...], ck_ref[...], sk_ref[...]
    q = _dot(hn, wq_ref[...])
    k = _dot(hn, wk_ref[...])
    v_s[...] = _dot(hn, wv_ref[...]).astype(BF16)
    for b in range(bg):
        r = slice(b * CHUNK, (b + 1) * CHUNK)
        for h in range(HEADS):
            cl = slice(h * DH, (h + 1) * DH)
            q_s[r, cl] = _rotary(q[r, cl], cq, sq).astype(BF16)
            krt = _rotary(k[r, cl], ck, sk).T
            kt_s[b * HEADS + h] = krt.astype(BF16)
            kdt_s[b * HEADS + h] = (krt * kdec_ref[h]).astype(BF16)

    for b in range(bg):
        r = slice(b * CHUNK, (b + 1) * CHUNK)
        for h in range(HEADS):
            cl = slice(h * DH, (h + 1) * DH)
            o, s_new = _retention_head_paired(q_s[r, cl], kt_s[b * HEADS + h], kdt_s[b * HEADS + h], v_s[r, cl],
                                              s_ref[b, h], dmask_ref[h], qdec_ref[:, cl], gl_ref[:, cl])
            s_ref[b, h] = s_new
            o_s[r, cl] = _group_norm(o)

    g = _dot(hn, wg_ref[...])
    gated = _silu(g) * o_s[...]

    bgate = _dot(hn, wbg_ref[...])
    u = _dot(hn, wcg_ref[...]) * _dot(hn, whc_ref[...])
    rows = lax.broadcasted_iota(jnp.int32, (CHUNK, D_MODEL), 0)
    cw = cw_ref[...]
    ys = []
    for b in range(bg):
        ub = u[b * CHUNK:(b + 1) * CHUNK]
        ys.append(_short_conv(ub, c_ref[b], cw, rows))
        c_ref[b] = ub[CHUNK - (CONV_W - 1):, :]
    bgy = bgate * jnp.concatenate(ys, axis=0)

    ga = _dot(hn, wga_ref[...])
    gb = _dot(hn, wgb_ref[...])
    out = _out_proj(x, gated, bgy, ga, gb, wro_ref, wco_ref, wo_ref)
    h_ref[...] = out.reshape(bg, CHUNK, D_MODEL)


def _mixer_main(x, s0, c0, layer, g, win, cw, wro, wco, wo, rope, decay, wgu32, wd32):
    nb, seq, _ = x.shape
    bg = MAIN_BG
    cq, sq, ck, sk = rope
    dmask, qdec, _, gl, kdec = decay
    m = bg * CHUNK
    n_chunks = seq // CHUNK
    steps = (nb // bg) * n_chunks
    slab = D_MODEL // steps
    wd_slab = 16 * (-(-D_FF // (16 * steps)))
    while D_FF % wd_slab:
        wd_slab += 16
    wd_last = D_FF // wd_slab - 1
    assert slab * steps == D_MODEL and slab % 16 == 0 and wd_last < steps

    def wd_block(i, c):
        return jnp.minimum(i * n_chunks + c, wd_last)

    rope_spec = pl.BlockSpec((CHUNK, DH), lambda i, c: (c, 0))
    stream_specs = [
        pl.BlockSpec((bg, CHUNK, D_MODEL), lambda i, c: (i, c, 0)),
        rope_spec, rope_spec, rope_spec, rope_spec,
        pl.BlockSpec((None, slab, 2 * D_FF), lambda i, c: (layer, i * n_chunks + c, 0)),
        pl.BlockSpec((None, wd_slab, D_MODEL), lambda i, c: (layer, wd_block(i, c), 0)),
    ]
    out_specs = [
        pl.BlockSpec((bg, CHUNK, D_MODEL), lambda i, c: (i, c, 0)),
        pl.BlockSpec((bg, HEADS, DH, DH), lambda i, c: (i, 0, 0, 0)),
        pl.BlockSpec((bg, CONV_W - 1, D_MODEL), lambda i, c: (i, 0, 0)),
        pl.BlockSpec((None, slab, 2 * D_FF), lambda i, c: (0, i * n_chunks + c, 0)),
        pl.BlockSpec((None, wd_slab, D_MODEL), lambda i, c: (0, wd_block(i, c), 0)),
    ]
    stream_args = [x, cq, sq, ck, sk, wgu32, wd32]
    resident_args = [s0, c0, g, win, cw, wro, wco, wo, dmask, qdec, kdec, gl]
    n_res, n_in, n_out = len(resident_args), len(stream_args), len(out_specs)
    n_groups = N_IN // D_MODEL

    def outer(*refs):
        s0_ref, c0_ref, g_ref, win_ref, cw_ref, wro_ref, wco_ref, wo_ref = refs[:8]
        table_refs = refs[8:n_res]
        hbm_in, hbm_out = refs[n_res:n_res + n_in], refs[n_res + n_in:n_res + n_in + n_out]
        scratch = refs[n_res + n_in + n_out:]
        win_parts = [win_ref.at[0, :, pl.ds(j * D_MODEL, D_MODEL)] for j in range(n_groups)]

        def step(x_b, cq_b, sq_b, ck_b, sk_b, wgu32_b, wd32_b, *out_blocks):
            _mixer_main_kernel(x_b, s0_ref.at[0], c0_ref.at[0], g_ref.at[layer], *win_parts, cw_ref.at[layer],
                               wro_ref.at[0], wco_ref.at[0], wo_ref.at[0], cq_b, sq_b, ck_b, sk_b, *table_refs,
                               wgu32_b, wd32_b, *out_blocks, *scratch)

        pltpu.emit_pipeline(step, grid=(nb // bg, n_chunks), in_specs=stream_specs,
                            out_specs=out_specs)(*hbm_in, *hbm_out)

    vmem = pl.BlockSpec(memory_space=pltpu.VMEM)
    hbm = pl.BlockSpec(memory_space=pl.ANY)
    return pl.pallas_call(
        outer,
        in_specs=[vmem] * n_res + [hbm] * n_in,
        out_specs=[hbm] * n_out,
        out_shape=[
            jax.ShapeDtypeStruct(x.shape, F32),
            jax.ShapeDtypeStruct((nb, HEADS, DH, DH), F32),
            jax.ShapeDtypeStruct((nb, CONV_W - 1, D_MODEL), F32),
            jax.ShapeDtypeStruct((1, D_MODEL, 2 * D_FF), BF16),
            jax.ShapeDtypeStruct((1, D_FF, D_MODEL), BF16),
        ],
        scratch_shapes=[
            pltpu.VMEM((m, D_MODEL), BF16),
            pltpu.VMEM((bg * HEADS, DH, CHUNK), BF16),
            pltpu.VMEM((bg * HEADS, DH, CHUNK), BF16),
            pltpu.VMEM((m, D_MODEL), BF16),
            pltpu.VMEM((m, D_MODEL), F32),
        ],
        compiler_params=pltpu.CompilerParams(vmem_limit_bytes=VMEM_LIMIT_BYTES),
        name="mixer_main",
    )(*resident_args, *stream_args)


def _round_square_slabs(step, n_slabs, sq_refs, sb_refs):
    @pl.when(step < n_slabs)
    def _():
        for src, dst in zip(sq_refs, sb_refs):
            dst[...] = src[...].astype(BF16)


def _square_slab_specs(layer, slab, n_slabs):
    slab_in = pl.BlockSpec((None, slab, D_MODEL), lambda j: (layer, jnp.minimum(j, n_slabs - 1), 0))
    slab_out = pl.BlockSpec((None, slab, D_MODEL), lambda j: (0, jnp.minimum(j, n_slabs - 1), 0))
    return [slab_in] * 3, [slab_out] * 3, [jax.ShapeDtypeStruct((1, D_MODEL, D_MODEL), BF16)] * 3


def _proj_first_kernel(xs_ref, xm_ref, g_ref, w32_ref, sq0_ref, sq1_ref, sq2_ref,
                       o_ref, wb_ref, sb0_ref, sb1_ref, sb2_ref, xcat_ref, hn_s, *, n_slabs):
    j = pl.program_id(0)
    n_samp = xs_ref.shape[0]

    @pl.when(j == 0)
    def _():
        xcat_ref[:n_samp, :] = xs_ref[...]
        xcat_ref[n_samp:, :] = xm_ref[...]
        hn_s[:n_samp, :] = _rms_f32(xs_ref[...], g_ref[...]).astype(BF16)
        hn_s[n_samp:, :] = _rms_f32(xm_ref[...], g_ref[...]).astype(BF16)

    _round_square_slabs(j, n_slabs, (sq0_ref, sq1_ref, sq2_ref), (sb0_ref, sb1_ref, sb2_ref))
    wb = w32_ref[...].astype(BF16)
    wb_ref[...] = wb
    o_ref[...] = _dot(hn_s[...], wb)


def _proj_first(xs, xm, layer, g, win32, squares32):
    rows = xs.shape[0] + xm.shape[0]
    n_col = N_IN // D_MODEL
    slab = 128
    n_slabs = D_MODEL // slab
    assert n_slabs <= n_col
    sq_in, sq_out, sq_shapes = _square_slab_specs(layer, slab, n_slabs)
    return pl.pallas_call(
        functools.partial(_proj_first_kernel, n_slabs=n_slabs),
        grid=(n_col,),
        in_specs=[_resident(xs.shape), _resident(xm.shape), _resident_layer((1, D_MODEL), layer),
                  pl.BlockSpec((None, D_MODEL, D_MODEL), lambda j: (layer, 0, j))] + sq_in,
        out_specs=[pl.BlockSpec((rows, D_MODEL), lambda j: (0, j)),
                   pl.BlockSpec((None, D_MODEL, D_MODEL), lambda j: (0, 0, j))] + sq_out
        + [pl.BlockSpec((rows, D_MODEL), lambda j: (0, 0))],
        out_shape=[jax.ShapeDtypeStruct((rows, N_IN), F32), jax.ShapeDtypeStruct((1, D_MODEL, N_IN), BF16)]
        + sq_shapes + [jax.ShapeDtypeStruct((rows, D_MODEL), F32)],
        scratch_shapes=[pltpu.VMEM((rows, D_MODEL), BF16)],
        compiler_params=_params(("arbitrary",)),
        name="proj_short_first",
    )(xs, xm, g, win32, *squares32)


def _proj_kernel(x_ref, g_ref, win_ref, o_ref):
    hn = _rms_f32(x_ref[...], g_ref[...]).astype(BF16)
    o_ref[...] = _dot(hn, win_ref[...])


def _proj(x, layer, g, win, tm):
    rows = x.shape[0]
    assert rows % tm == 0
    return pl.pallas_call(
        _proj_kernel,
        grid=(rows // tm,),
        in_specs=[pl.BlockSpec((tm, D_MODEL), lambda i: (i, 0)), _resident_layer((1, D_MODEL), layer),
                  _resident_layer((D_MODEL, N_IN), 0)],
        out_specs=pl.BlockSpec((tm, N_IN), lambda i: (i, 0)),
        out_shape=jax.ShapeDtypeStruct((rows, N_IN), F32),
        compiler_params=_params(("arbitrary",)),
        name="proj_short",
    )(x, g, win)


def _ret_short_first(p_ref, s_ref, tab_refs, sb, L, sink):
    cq_ref, sq_ref, ck_ref, sk_ref, dmask_ref, qdec_ref, kdec_ref = tab_refs
    cq, sq, ck, sk = cq_ref[...], sq_ref[...], ck_ref[...], sk_ref[...]
    staged = []
    for b in range(sb):
        r = slice(b * L, (b + 1) * L)
        for h in range(HEADS):
            cl = slice(h * DH, (h + 1) * DH)
            q = _rotary(p_ref[r, C_Q + h * DH:C_Q + (h + 1) * DH], cq, sq).astype(BF16)
            k = _rotary(p_ref[r, C_K + h * DH:C_K + (h + 1) * DH], ck, sk)
            v = p_ref[r, C_V + h * DH:C_V + (h + 1) * DH]
            s = s_ref[b, h]
            scores = (_dot_nt(q, k.astype(BF16)) * dmask_ref[h]).astype(BF16)
            cross = _dot(q, s.astype(BF16)) * qdec_ref[:, cl]
            sink(b, h, r, cl, s, k * kdec_ref[:, cl], v)
            staged.append((scores, cross, v.astype(BF16)))
    return staged


def _ret_short_second(staged, p_ref, c_ref, cw_ref, gated_ref, bgy_ref, cn_ref, sb, L):
    rows = lax.broadcasted_iota(jnp.int32, (L, D_MODEL), 0)
    cw = cw_ref[...]
    for b in range(sb):
        r = slice(b * L, (b + 1) * L)
        for h in range(HEADS):
            cl = slice(h * DH, (h + 1) * DH)
            scores, cross, v = staged[b * HEADS + h]
            o = _dot(scores, v) + cross
            g = p_ref[r, C_G + h * DH:C_G + (h + 1) * DH]
            gated_ref[r, cl] = _silu(g) * _group_norm(o)
        u = p_ref[r, C_CG:C_CG + D_MODEL] * p_ref[r, C_HC:C_HC + D_MODEL]
        y = _short_conv(u, c_ref[b], cw, rows)
        bgy_ref[r, :] = p_ref[r, C_BG:C_BG + D_MODEL] * y
        cn_ref[b] = u[L - (CONV_W - 1):, :]


def _state_sink(sn_ref, gl_ref, layer=None):
    def sink(b, h, r, cl, s, kd, v):
        s_new = gl_ref[:, cl] * s + _dot_tn(kd.astype(BF16), v.astype(BF16))
        if layer is None:
            sn_ref[b, h] = s_new
        else:
            sn_ref[layer, b, h] = s_new
    return sink


def _ret_meta_kernel(p_ref, s_ref, c_ref, cw_ref, cq_ref, sq_ref, ck_ref, sk_ref, dmask_ref, qdec_ref, kdec_ref,
                     gl_ref, gated_ref, bgy_ref, cn_ref, sn_ref, *, L):
    sb = c_ref.shape[0]
    tabs = (cq_ref, sq_ref, ck_ref, sk_ref, dmask_ref, qdec_ref, kdec_ref)
    staged = _ret_short_first(p_ref, s_ref, tabs, sb, L, _state_sink(sn_ref, gl_ref))
    _ret_short_second(staged, p_ref, c_ref, cw_ref, gated_ref, bgy_ref, cn_ref, sb, L)


def _short_table_specs(L):
    return [_resident((L, DH))] * 4 + [_resident((HEADS, L, L)), _resident((L, D_MODEL)), _resident((L, D_MODEL)),
                                       _resident((1, D_MODEL))]


def _ret_meta(proj, row0, state, cprev, layer, cw, rope, decay, nseq, L):
    dmask, qdec, kdec, gl, _ = decay
    rows = nseq * L
    row_spec = pl.BlockSpec((rows, D_MODEL), lambda i: (0, 0))
    row_shape = jax.ShapeDtypeStruct((rows, D_MODEL), F32)
    return pl.pallas_call(
        functools.partial(_ret_meta_kernel, L=L),
        grid=(1,),
        in_specs=[
            pl.BlockSpec((rows, C_GA), lambda i: (row0 // rows, 0)),
            _resident((nseq, HEADS, DH, DH)),
            _resident((nseq, CONV_W - 1, D_MODEL)),
            _resident_layer((CONV_W, D_MODEL), layer),
        ] + _short_table_specs(L),
        out_specs=[row_spec, row_spec, pl.BlockSpec((nseq, CONV_W - 1, D_MODEL), lambda i: (0, 0, 0)),
                   pl.BlockSpec((nseq, HEADS, DH, DH), lambda i: (0, 0, 0, 0))],
        out_shape=[row_shape, row_shape, jax.ShapeDtypeStruct((nseq, CONV_W - 1, D_MODEL), F32),
                   jax.ShapeDtypeStruct((nseq, HEADS, DH, DH), F32)],
        compiler_params=_params(("arbitrary",)),
        name="ret_meta",
    )(proj, state, cprev, cw, *rope, dmask, qdec, kdec, gl)


def _ffn_main_kernel(x_ref, g_ref, *refs, final, n_prev, L):
    wgu_refs = refs[:FFN_PARTS]
    (wd_ref, fg_ref, p_ref, s_ref, c_ref, cw_ref, cq_ref, sq_ref, ck_ref, sk_ref, dmask_ref, qdec_ref, kdec_ref,
     gl_ref) = refs[FFN_PARTS:FFN_PARTS + 14]
    rest = refs[FFN_PARTS + 14:]
    sb = c_ref.shape[0]
    tabs = (cq_ref, sq_ref, ck_ref, sk_ref, dmask_ref, qdec_ref, kdec_ref)

    if final:
        n_main = n_prev + 1
        prev, mains = rest[:3 * n_prev], rest[3 * n_prev:3 * n_prev + n_main]
        prev_cn, outs = rest[3 * n_prev + n_main:4 * n_prev + n_main], rest[4 * n_prev + n_main:]
        y_ref, gated_ref, bgy_ref, cn_stack_ref, sn_ref, mstack_ref = outs
        for l in range(n_main):
            mstack_ref[l] = mains[l][...]
        for l in range(n_prev):
            cn_stack_ref[l] = prev_cn[l][...]
        cn_ref = cn_stack_ref.at[n_prev]
        sink = _state_sink(sn_ref, gl_ref, n_prev)
    else:
        next32, (y_ref, gated_ref, bgy_ref, cn_ref, kd_ref, v_ref), nextb = rest[:4], rest[4:10], rest[10:]
        for src, dst in zip(next32, nextb):
            dst[...] = src[...].astype(BF16)

        def sink(b, h, r, cl, s, kd, v):
            kd_ref[r, cl] = kd
            v_ref[r, cl] = v

    x = x_ref[...]
    hn = _rms_f32(x, g_ref[...]).astype(BF16)
    gates = [_dot(hn, w[...]) for w in wgu_refs[:FFN_PARTS // 2]]
    if final:
        for l in range(n_prev):
            so_ref, kdo_ref, vo_ref = prev[3 * l:3 * l + 3]
            upd = _state_sink(sn_ref, gl_ref, l)
            for b in range(sb):
                r = slice(b * L, (b + 1) * L)
                for h in range(HEADS):
                    cl = slice(h * DH, (h + 1) * DH)
                    upd(b, h, r, cl, so_ref[b, h], kdo_ref[r, cl], vo_ref[r, cl])
    staged = _ret_short_first(p_ref, s_ref, tabs, sb, L, sink)
    ups = [_dot(hn, w[...]) for w in wgu_refs[FFN_PARTS // 2:]]
    _ret_short_second(staged, p_ref, c_ref, cw_ref, gated_ref, bgy_ref, cn_ref, sb, L)
    y = x + _dot(_swiglu(gates, ups), wd_ref[...])
    if final:
        y = _rms_f32(y, fg_ref[...])
    y_ref[...] = y


def _ffn_main(x, layer, g, wgu, wd, fg, final, tm, proj, state, cconv, cw, rope, decay, L, kds, vs, main_states,
              win32, squares32, h_short, gated_m, bgy_m, squares, conv_states):
    rows = x.shape[0]
    steps = rows // tm
    depth, ns = state.shape[:2]
    sb = ns // steps
    assert sb * steps == ns and (sb * L) % 8 == 0
    n_prev = len(kds) if final else 0
    dmask, qdec, kdec, gl, _ = decay
    n_samp = ns * L
    row_spec = pl.BlockSpec((tm, D_MODEL), lambda i: (i, 0))
    srow_spec = pl.BlockSpec((sb * L, D_MODEL), lambda i: (i, 0))
    srow_shape = jax.ShapeDtypeStruct((n_samp, D_MODEL), F32)

    def state_spec(l):
        return pl.BlockSpec((None, sb, HEADS, DH, DH), lambda i: (l, i, 0, 0, 0))

    stream_specs = [
        row_spec,
        pl.BlockSpec((sb * L, C_GA), lambda i: (i, 0)),
        state_spec(layer),
        pl.BlockSpec((None, sb, CONV_W - 1, D_MODEL), lambda i: (layer, i, 0, 0)),
    ]
    stream_args = [x, proj, state, cconv]
    for l in range(n_prev):
        stream_specs += [state_spec(l), srow_spec, srow_spec]
        stream_args += [state, kds[l], vs[l]]
    n_meta = 0 if final else h_short.shape[0] - n_samp
    short_rows = n_samp + n_meta
    branch_shape = jax.ShapeDtypeStruct((short_rows, D_MODEL), F32)
    out_specs = [row_spec, srow_spec, srow_spec, pl.BlockSpec((sb, CONV_W - 1, D_MODEL), lambda i: (i, 0, 0))]
    out_shape = [jax.ShapeDtypeStruct(x.shape, F32), branch_shape, branch_shape,
                 jax.ShapeDtypeStruct((ns, CONV_W - 1, D_MODEL), F32)]
    if final:
        assert n_prev == depth - 1 and len(main_states) == depth
        out_specs.append(pl.BlockSpec((depth, sb, HEADS, DH, DH), lambda i: (0, i, 0, 0, 0)))
        out_shape.append(jax.ShapeDtypeStruct(state.shape, F32))
        n_mat = main_states[0].shape[0] * HEADS
        mb = n_mat // steps
        assert mb * steps == n_mat
        stream_specs += [pl.BlockSpec((mb, DH, DH), lambda i: (i, 0, 0))] * depth
        stream_args += [s.reshape(n_mat, DH, DH) for s in main_states]
        out_specs.append(pl.BlockSpec((depth, mb, DH, DH), lambda i: (0, i, 0, 0)))
        out_shape.append(jax.ShapeDtypeStruct((depth, n_mat, DH, DH), F32))
        assert len(conv_states) == n_prev
        stream_specs += [pl.BlockSpec((sb, CONV_W - 1, D_MODEL), lambda i: (i, 0, 0))] * n_prev
        stream_args += list(conv_states)
        out_specs[3] = pl.BlockSpec((depth, sb, CONV_W - 1, D_MODEL), lambda i: (0, i, 0, 0))
        out_shape[3] = jax.ShapeDtypeStruct((depth, ns, CONV_W - 1, D_MODEL), F32)
    else:
        slab = D_MODEL // steps
        assert slab * steps == D_MODEL and slab % 16 == 0
        out_specs += [srow_spec, srow_spec]
        out_shape += [srow_shape, srow_shape]
        for w32 in (win32, *squares32):
            width = w32.shape[-1]
            stream_specs.append(pl.BlockSpec((None, slab, width), lambda i: (layer + 1, i, 0)))
            stream_args.append(w32)
            out_specs.append(pl.BlockSpec((None, slab, width), lambda i: (0, i, 0)))
            out_shape.append(jax.ShapeDtypeStruct((1, D_MODEL, width), BF16))

    resident_args = [g, wgu, wd, fg, cw, *rope, dmask, qdec, kdec, gl]
    n_res, n_in = len(resident_args), len(stream_args)
    part = 2 * D_FF // FFN_PARTS

    short_tm = SHORT_TM if final else short_rows // SHORT_STEPS_WITH_META
    assert short_rows % short_tm == 0 and short_tm % 8 == 0
    short_args = [h_short, *squares] + ([] if final else [gated_m, bgy_m])
    out_shape.append(jax.ShapeDtypeStruct((short_rows, D_MODEL), F32))
    tile = pl.BlockSpec((short_tm, D_MODEL), lambda i: (i, 0))

    def outer(*refs):
        g_ref, wgu_ref, wd_ref, fg_ref, cw_ref = refs[:5]
        table_refs = refs[5:n_res]
        hbm_in = refs[n_res:n_res + n_in]
        hbm_short = refs[n_res + n_in:n_res + n_in + len(short_args)]
        hbm_out = refs[n_res + n_in + len(short_args):]
        wgu_parts = [wgu_ref.at[0, :, pl.ds(q * part, part)] for q in range(FFN_PARTS)]

        def step(*blocks):
            x_ref, p_ref, s_ref, c_ref = blocks[:4]
            _ffn_main_kernel(x_ref, g_ref.at[layer], *wgu_parts, wd_ref.at[0], fg_ref, p_ref, s_ref, c_ref,
                             cw_ref.at[layer], *table_refs, *blocks[4:], final=final, n_prev=n_prev, L=L)

        pltpu.emit_pipeline(step, grid=(steps,), in_specs=stream_specs, out_specs=out_specs)(*hbm_in, *hbm_out[:-1])

        p_hbm = hbm_in[1]
        gated_hbm, bgy_hbm, ys_hbm = hbm_out[1], hbm_out[2], hbm_out[-1]
        hs_hbm, square_hbm = hbm_short[0], hbm_short[1:4]

        def short_phase(wro_v, wco_v, wo_v, sems):
            copies = [pltpu.make_async_copy(src.at[0], dst, sems.at[n])
                      for n, (src, dst) in enumerate(zip(square_hbm, (wro_v, wco_v, wo_v)))]
            if n_meta:
                meta_rows = pl.ds(n_samp, n_meta)
                copies += [pltpu.make_async_copy(hbm_short[4], gated_hbm.at[meta_rows], sems.at[3]),
                           pltpu.make_async_copy(hbm_short[5], bgy_hbm.at[meta_rows], sems.at[4])]
            for cp in copies:
                cp.start()
            for cp in copies:
                cp.wait()

            def tile_body(x_ref, gated_ref, bgy_ref, ga_ref, gb_ref, o_ref):
                h = _out_proj(x_ref[...], gated_ref[...], bgy_ref[...], ga_ref[...], gb_ref[...], wro_v, wco_v, wo_v)
                y = _ffn_block(h, g_ref.at[layer], wgu_parts, wd_ref.at[0])
                o_ref[...] = _rms_f32(y, fg_ref[...]) if final else y

            ga_tile = pl.BlockSpec((short_tm, D_MODEL), lambda i: (i, C_GA // D_MODEL))
            gb_tile = pl.BlockSpec((short_tm, D_MODEL), lambda i: (i, C_GB // D_MODEL))
            pltpu.emit_pipeline(tile_body, grid=(short_rows // short_tm,), in_specs=[tile, tile, tile, ga_tile, gb_tile],
                                out_specs=[tile])(hs_hbm, gated_hbm, bgy_hbm, p_hbm, p_hbm, ys_hbm)

        pl.run_scoped(short_phase, pltpu.VMEM((D_MODEL, D_MODEL), BF16), pltpu.VMEM((D_MODEL, D_MODEL), BF16),
                      pltpu.VMEM((D_MODEL, D_MODEL), BF16), pltpu.SemaphoreType.DMA((5,)))

    vmem = pl.BlockSpec(memory_space=pltpu.VMEM)
    hbm = pl.BlockSpec(memory_space=pl.ANY)
    return pl.pallas_call(
        outer,
        in_specs=[vmem] * n_res + [hbm] * (n_in + len(short_args)),
        out_specs=[hbm] * len(out_shape),
        out_shape=out_shape,
        compiler_params=pltpu.CompilerParams(vmem_limit_bytes=VMEM_LIMIT_BYTES),
        name="ffn_main",
    )(*resident_args, *stream_args, *short_args)


def kernel(x_prompt, x_sample, state_ret, state_conv, meta_tokens, norm_mix_g, w_in, conv_w, w_ret_o,
           w_conv_o, w_o, norm_ffn_g, w_gate_up, w_down, final_norm_g):
    depth = w_in.shape[0]
    nb, seq, _ = x_prompt.shape
    ns, ls, _ = x_sample.shape
    n_samp = ns * ls
    n_short = n_samp + N_META
    short_tm = n_short // 5
    assert short_tm * 5 == n_short and short_tm % 8 == 0 and n_samp % N_META == 0

    gm = norm_mix_g.reshape(depth, 1, D_MODEL)
    gf = norm_ffn_g.reshape(depth, 1, D_MODEL)
    fg = final_norm_g.reshape(1, D_MODEL)

    rope_meta = _rope_tables(np.arange(N_META))
    rope_main = _rope_tables(N_META + np.arange(seq))
    rope_samp = _rope_tables(PAST_LEN + np.arange(ls))
    dec_meta, dec_main, dec_samp = _decay_tables(N_META), _decay_tables(CHUNK), _decay_tables(ls)

    h_main = x_prompt
    zero_s = jnp.zeros((1, HEADS, DH, DH), F32)
    zero_c = jnp.zeros((1, CONV_W - 1, D_MODEL), F32)
    squares32 = (w_ret_o, w_conv_o, w_o)

    s_p, c_p, c_s, kds, vs = [], [], [], [], []
    y_samp = s_s = s_p_stacked = c_s_stacked = h_short = next_weights = None
    for l in range(depth):
        last = l == depth - 1
        if l == 0:
            proj, win_b, wro_b, wco_b, wo_b, h_short = _proj_first(
                x_sample.reshape(n_samp, D_MODEL), meta_tokens.astype(F32), l, gm, w_in, squares32)
        else:
            win_b, wro_b, wco_b, wo_b = next_weights
            proj = _proj(h_short, l, gm, win_b, short_tm)
        gated_m, bgy_m, c_m, s_m = _ret_meta(proj, n_samp, zero_s, zero_c, l, conv_w, rope_meta, dec_meta, 1, N_META)

        h_main, s_l, c_l, wgu_b, wd_b = _mixer_main(h_main, s_m, c_m, l, gm, win_b, conv_w, wro_b, wco_b, wo_b,
                                                    rope_main, dec_main, w_gate_up, w_down)
        s_p.append(s_l)
        c_p.append(c_l)
        res = _ffn_main(h_main.reshape(nb * seq, D_MODEL), l, gf, wgu_b, wd_b, fg, last, MAIN_FFN_TM,
                        proj, state_ret, state_conv, conv_w, rope_samp, dec_samp, ls, kds, vs, s_p, w_in, squares32,
                        h_short, gated_m, bgy_m, (wro_b, wco_b, wo_b), c_s)
        h_main = res[0].reshape(nb, seq, D_MODEL)
        if last:
            c_s_stacked = res[3]
            s_s = res[4]
            s_p_stacked = res[5].reshape(depth, nb, HEADS, DH, DH)
            y_samp = res[-1]
        else:
            c_s.append(res[3])
            kds.append(res[4])
            vs.append(res[5])
            next_weights = res[6:10]
            h_short = res[-1]

    return (h_main, y_samp.reshape(ns, ls, D_MODEL), s_p_stacked, jnp.stack(c_p), s_s, c_s_stacked)
```

```python
import functools

import numpy as np
import jax
import jax.numpy as jnp
from jax import lax
from jax.experimental import pallas as pl
from jax.experimental.pallas import tpu as pltpu

D_MODEL = 1024
N_META = 16
HEADS = 8
DH = D_MODEL // HEADS
CHUNK = 128
ROPE_BASE = 10000.0
CONV_W = 3
D_FF = ((8 * D_MODEL + 3 * 256 - 1) // (3 * 256)) * 256
EPS = 1e-6
PAST_LEN = 16384
N_IN = 9 * D_MODEL
C_Q, C_K, C_V, C_G, C_BG, C_CG, C_HC, C_GA, C_GB = (i * D_MODEL for i in range(9))

F32 = jnp.float32
BF16 = jnp.bfloat16

VMEM_LIMIT_BYTES = 58 * 1024 * 1024

MAIN_BG = 4
MAIN_FFN_TM = 512
SHORT_TM = 256
SHORT_STEPS_WITH_META = 5
FFN_PARTS = 2
assert (2 * D_FF // FFN_PARTS) % 256 == 0 and FFN_PARTS % 2 == 0


def _resident(shape):
    nd = len(shape)
    return pl.BlockSpec(shape, lambda *_: (0,) * nd, pipeline_mode=pl.Buffered(1))


def _resident_layer(shape, layer):
    nd = len(shape)
    return pl.BlockSpec((None,) + tuple(shape), lambda *_: (layer,) + (0,) * nd, pipeline_mode=pl.Buffered(1))


def _resident_col_parts(rows, width, n_parts):
    return [pl.BlockSpec((None, rows, width), functools.partial(lambda *_, q: (0, 0, q), q=q),
                         pipeline_mode=pl.Buffered(1)) for q in range(n_parts)]


def _params(sem):
    return pltpu.CompilerParams(dimension_semantics=sem, vmem_limit_bytes=VMEM_LIMIT_BYTES)


def _dot(a, b):
    return jnp.dot(a, b, preferred_element_type=F32)


def _dot_nt(a, b):
    return lax.dot_general(a, b, (((1,), (1,)), ((), ())), preferred_element_type=F32)


def _dot_tn(a, b):
    return lax.dot_general(a, b, (((0,), (0,)), ((), ())), preferred_element_type=F32)


def _rms_f32(x, g):
    return x * lax.rsqrt(jnp.mean(x * x, axis=-1, keepdims=True) + EPS) * g


def _sigmoid(x):
    return 1.0 / (1.0 + jnp.exp(-x))


def _silu(x):
    return x * _sigmoid(x)


def _rotary(t, cos, sin):
    return t * cos + pltpu.roll(t, DH // 2, 1) * sin


def _group_norm(o):
    mu = jnp.mean(o, axis=-1, keepdims=True)
    d = o - mu
    var = jnp.mean(d * d, axis=-1, keepdims=True)
    return d * lax.rsqrt(var + EPS)


def _retention_head_paired(q, kt, kdt, v, s, dmask, qdec, gl):
    L = q.shape[0]
    sc = _dot(q, jnp.concatenate([kt, s.astype(BF16)], axis=1))
    scores = (sc[:, :L] * dmask).astype(BF16)
    iu = _dot(jnp.concatenate([scores, kdt], axis=0), v)
    return iu[:L] + sc[:, L:] * qdec, gl * s + iu[L:]


def _short_conv(u, tail, cw, rows):
    r1 = pltpu.roll(u, 1, 0)
    r2 = pltpu.roll(u, 2, 0)
    t0, t1 = tail[0:1, :], tail[1:2, :]
    sh1 = jnp.where(rows == 0, t1, r1)
    sh2 = jnp.where(rows == 0, t0, jnp.where(rows == 1, t1, r2))
    return cw[0:1, :] * sh2 + cw[1:2, :] * sh1 + cw[2:3, :] * u


def _out_proj(x, gated, bgy, ga, gb, wro_ref, wco_ref, wo_ref):
    ret_out = _dot(gated.astype(BF16), wro_ref[...])
    conv_out = _dot(bgy.astype(BF16), wco_ref[...])
    merged = _sigmoid(ga) * ret_out + _sigmoid(gb) * conv_out
    return x + _dot(merged.astype(BF16), wo_ref[...])


def _swiglu(gates, ups):
    return jnp.concatenate([_silu(a) * b for a, b in zip(gates, ups)], axis=1).astype(BF16)


def _ffn_block(x, g_ref, wgu_refs, wd_ref):
    hn = _rms_f32(x, g_ref[...]).astype(BF16)
    n = len(wgu_refs) // 2
    gates = [_dot(hn, w[...]) for w in wgu_refs[:n]]
    ups = [_dot(hn, w[...]) for w in wgu_refs[n:]]
    return x + _dot(_swiglu(gates, ups), wd_ref[...])


def _log_gamma():
    return np.log1p(-np.exp2(-5.0 - np.arange(HEADS, dtype=np.float64)))


def _const(t):
    return jnp.asarray(np.asarray(t, dtype=np.float32))


def _rope_tables(pos):
    half = DH // 2
    inv = np.power(ROPE_BASE, -np.arange(half, dtype=np.float64) / half)
    ang = np.asarray(pos, dtype=np.float64)[:, None] * inv[None, :]
    cos, sin = np.cos(ang), np.sin(ang)
    cosf = np.concatenate([cos, cos], axis=-1)
    sinf = np.concatenate([-sin, sin], axis=-1)
    scale = DH ** -0.5
    return tuple(_const(t) for t in (cosf, sinf, cosf * scale, sinf * scale))


def _decay_tables(L):
    log_g = _log_gamma()
    idx = np.arange(L, dtype=np.float64)
    diff = idx[:, None] - idx[None, :]
    dmask = np.where(diff >= 0, np.exp(log_g[:, None, None] * np.maximum(diff, 0.0)[None]), 0.0)
    qdec = np.exp(log_g[:, None] * (idx + 1.0)[None])
    kdec = np.exp(log_g[:, None] * (L - 1.0 - idx)[None])
    gl = np.exp(log_g * L)
    lanes = lambda t: np.repeat(t.T, DH, axis=1)
    return tuple(_const(t) for t in (dmask, lanes(qdec), lanes(kdec), np.repeat(gl, DH)[None, :], kdec[:, None, :]))


def _mixer_main_kernel(x_ref, s0_ref, c0_ref, g_ref, *refs):
    n_groups = N_IN // D_MODEL
    wq_ref, wk_ref, wv_ref, wg_ref, wbg_ref, wcg_ref, whc_ref, wga_ref, wgb_ref = refs[:n_groups]
    (cw_ref, wro_ref, wco_ref, wo_ref, cq_ref, sq_ref, ck_ref, sk_ref, dmask_ref, qdec_ref, kdec_ref, gl_ref,
     wgu32_ref, wd32_ref, h_ref, s_ref, c_ref, wgub_ref, wdb_ref, q_s, kt_s, kdt_s, v_s, o_s) = refs[n_groups:]
    bg = x_ref.shape[0]
    m = bg * CHUNK

    wgub_ref[...] = wgu32_ref[...].astype(BF16)
    step = pl.program_id(0) * pl.num_programs(1) + pl.program_id(1)

    @pl.when(step < D_FF // wd32_ref.shape[0])
    def _():
        wdb_ref[...] = wd32_ref[...].astype(BF16)

    @pl.when(pl.program_id(1) == 0)
    def _():
        for b in range(bg):
            s_ref[b] = s0_ref[...]
            c_ref[b] = c0_ref[...]

    x = x_ref[...].reshape(m, D_MODEL)
    hn = _rms_f32(x, g_ref[...]).astype(BF16)

    cq, sq, ck, sk = cq_ref[...], sq_ref[...], ck_ref[...], sk_ref[...]
    q = _dot(hn, wq_ref[...])
    k = _dot(hn, wk_ref[...])
    v_s[...] = _dot(hn, wv_ref[...]).astype(BF16)
    for b in range(bg):
        r = slice(b * CHUNK, (b + 1) * CHUNK)
        for h in range(HEADS):
            cl = slice(h * DH, (h + 1) * DH)
            q_s[r, cl] = _rotary(q[r, cl], cq, sq).astype(BF16)
            krt = _rotary(k[r, cl], ck, sk).T
            kt_s[b * HEADS + h] = krt.astype(BF16)
            kdt_s[b * HEADS + h] = (krt * kdec_ref[h]).astype(BF16)

    for b in range(bg):
        r = slice(b * CHUNK, (b + 1) * CHUNK)
        for h in range(HEADS):
            cl = slice(h * DH, (h + 1) * DH)
            o, s_new = _retention_head_paired(q_s[r, cl], kt_s[b * HEADS + h], kdt_s[b * HEADS + h], v_s[r, cl],
                                              s_ref[b, h], dmask_ref[h], qdec_ref[:, cl], gl_ref[:, cl])
            s_ref[b, h] = s_new
            o_s[r, cl] = _group_norm(o)

    g = _dot(hn, wg_ref[...])
    gated = _silu(g) * o_s[...]

    bgate = _dot(hn, wbg_ref[...])
    u = _dot(hn, wcg_ref[...]) * _dot(hn, whc_ref[...])
    rows = lax.broadcasted_iota(jnp.int32, (CHUNK, D_MODEL), 0)
    cw = cw_ref[...]
    ys = []
    for b in range(bg):
        ub = u[b * CHUNK:(b + 1) * CHUNK]
        ys.append(_short_conv(ub, c_ref[b], cw, rows))
        c_ref[b] = ub[CHUNK - (CONV_W - 1):, :]
    bgy = bgate * jnp.concatenate(ys, axis=0)

    ga = _dot(hn, wga_ref[...])
    gb = _dot(hn, wgb_ref[...])
    out = _out_proj(x, gated, bgy, ga, gb, wro_ref, wco_ref, wo_ref)
    h_ref[...] = out.reshape(bg, CHUNK, D_MODEL)


def _mixer_main(x, s0, c0, layer, g, win, cw, wro, wco, wo, rope, decay, wgu32, wd32):
    nb, seq, _ = x.shape
    bg = MAIN_BG
    cq, sq, ck, sk = rope
    dmask, qdec, _, gl, kdec = decay
    m = bg * CHUNK
    n_chunks = seq // CHUNK
    steps = (nb // bg) * n_chunks
    slab = D_MODEL // steps
    wd_slab = 16 * (-(-D_FF // (16 * steps)))
    while D_FF % wd_slab:
        wd_slab += 16
    wd_last = D_FF // wd_slab - 1
    assert slab * steps == D_MODEL and slab % 16 == 0 and wd_last < steps

    def wd_block(i, c):
        return jnp.minimum(i * n_chunks + c, wd_last)

    rope_spec = pl.BlockSpec((CHUNK, DH), lambda i, c: (c, 0))
    stream_specs = [
        pl.BlockSpec((bg, CHUNK, D_MODEL), lambda i, c: (i, c, 0)),
        rope_spec, rope_spec, rope_spec, rope_spec,
        pl.BlockSpec((None, slab, 2 * D_FF), lambda i, c: (layer, i * n_chunks + c, 0)),
        pl.BlockSpec((None, wd_slab, D_MODEL), lambda i, c: (layer, wd_block(i, c), 0)),
    ]
    out_specs = [
        pl.BlockSpec((bg, CHUNK, D_MODEL), lambda i, c: (i, c, 0)),
        pl.BlockSpec((bg, HEADS, DH, DH), lambda i, c: (i, 0, 0, 0)),
        pl.BlockSpec((bg, CONV_W - 1, D_MODEL), lambda i, c: (i, 0, 0)),
        pl.BlockSpec((None, slab, 2 * D_FF), lambda i, c: (0, i * n_chunks + c, 0)),
        pl.BlockSpec((None, wd_slab, D_MODEL), lambda i, c: (0, wd_block(i, c), 0)),
    ]
    stream_args = [x, cq, sq, ck, sk, wgu32, wd32]
    resident_args = [s0, c0, g, win, cw, wro, wco, wo, dmask, qdec, kdec, gl]
    n_res, n_in, n_out = len(resident_args), len(stream_args), len(out_specs)
    n_groups = N_IN // D_MODEL

    def outer(*refs):
        s0_ref, c0_ref, g_ref, win_ref, cw_ref, wro_ref, wco_ref, wo_ref = refs[:8]
        table_refs = refs[8:n_res]
        hbm_in, hbm_out = refs[n_res:n_res + n_in], refs[n_res + n_in:n_res + n_in + n_out]
        scratch = refs[n_res + n_in + n_out:]
        win_parts = [win_ref.at[0, :, pl.ds(j * D_MODEL, D_MODEL)] for j in range(n_groups)]

        def step(x_b, cq_b, sq_b, ck_b, sk_b, wgu32_b, wd32_b, *out_blocks):
            _mixer_main_kernel(x_b, s0_ref.at[0], c0_ref.at[0], g_ref.at[layer], *win_parts, cw_ref.at[layer],
                               wro_ref.at[0], wco_ref.at[0], wo_ref.at[0], cq_b, sq_b, ck_b, sk_b, *table_refs,
                               wgu32_b, wd32_b, *out_blocks, *scratch)

        pltpu.emit_pipeline(step, grid=(nb // bg, n_chunks), in_specs=stream_specs,
                            out_specs=out_specs)(*hbm_in, *hbm_out)

    vmem = pl.BlockSpec(memory_space=pltpu.VMEM)
    hbm = pl.BlockSpec(memory_space=pl.ANY)
    return pl.pallas_call(
        outer,
        in_specs=[vmem] * n_res + [hbm] * n_in,
        out_specs=[hbm] * n_out,
        out_shape=[
            jax.ShapeDtypeStruct(x.shape, F32),
            jax.ShapeDtypeStruct((nb, HEADS, DH, DH), F32),
            jax.ShapeDtypeStruct((nb, CONV_W - 1, D_MODEL), F32),
            jax.ShapeDtypeStruct((1, D_MODEL, 2 * D_FF), BF16),
            jax.ShapeDtypeStruct((1, D_FF, D_MODEL), BF16),
        ],
        scratch_shapes=[
            pltpu.VMEM((m, D_MODEL), BF16),
            pltpu.VMEM((bg * HEADS, DH, CHUNK), BF16),
            pltpu.VMEM((bg * HEADS, DH, CHUNK), BF16),
            pltpu.VMEM((m, D_MODEL), BF16),
            pltpu.VMEM((m, D_MODEL), F32),
        ],
        compiler_params=pltpu.CompilerParams(vmem_limit_bytes=VMEM_LIMIT_BYTES),
        name="mixer_main",
    )(*resident_args, *stream_args)


def _round_square_slabs(step, n_slabs, sq_refs, sb_refs):
    @pl.when(step < n_slabs)
    def _():
        for src, dst in zip(sq_refs, sb_refs):
            dst[...] = src[...].astype(BF16)


def _square_slab_specs(layer, slab, n_slabs):
    slab_in = pl.BlockSpec((None, slab, D_MODEL), lambda j: (layer, jnp.minimum(j, n_slabs - 1), 0))
    slab_out = pl.BlockSpec((None, slab, D_MODEL), lambda j: (0, jnp.minimum(j, n_slabs - 1), 0))
    return [slab_in] * 3, [slab_out] * 3, [jax.ShapeDtypeStruct((1, D_MODEL, D_MODEL), BF16)] * 3


def _proj_first_kernel(xs_ref, xm_ref, g_ref, w32_ref, sq0_ref, sq1_ref, sq2_ref,
                       o_ref, wb_ref, sb0_ref, sb1_ref, sb2_ref, xcat_ref, hn_s, *, n_slabs):
    j = pl.program_id(0)
    n_samp = xs_ref.shape[0]

    @pl.when(j == 0)
    def _():
        xcat_ref[:n_samp, :] = xs_ref[...]
        xcat_ref[n_samp:, :] = xm_ref[...]
        hn_s[:n_samp, :] = _rms_f32(xs_ref[...], g_ref[...]).astype(BF16)
        hn_s[n_samp:, :] = _rms_f32(xm_ref[...], g_ref[...]).astype(BF16)

    _round_square_slabs(j, n_slabs, (sq0_ref, sq1_ref, sq2_ref), (sb0_ref, sb1_ref, sb2_ref))
    wb = w32_ref[...].astype(BF16)
    wb_ref[...] = wb
    o_ref[...] = _dot(hn_s[...], wb)


def _proj_first(xs, xm, layer, g, win32, squares32):
    rows = xs.shape[0] + xm.shape[0]
    n_col = N_IN // D_MODEL
    slab = 128
    n_slabs = D_MODEL // slab
    assert n_slabs <= n_col
    sq_in, sq_out, sq_shapes = _square_slab_specs(layer, slab, n_slabs)
    return pl.pallas_call(
        functools.partial(_proj_first_kernel, n_slabs=n_slabs),
        grid=(n_col,),
        in_specs=[_resident(xs.shape), _resident(xm.shape), _resident_layer((1, D_MODEL), layer),
                  pl.BlockSpec((None, D_MODEL, D_MODEL), lambda j: (layer, 0, j))] + sq_in,
        out_specs=[pl.BlockSpec((rows, D_MODEL), lambda j: (0, j)),
                   pl.BlockSpec((None, D_MODEL, D_MODEL), lambda j: (0, 0, j))] + sq_out
        + [pl.BlockSpec((rows, D_MODEL), lambda j: (0, 0))],
        out_shape=[jax.ShapeDtypeStruct((rows, N_IN), F32), jax.ShapeDtypeStruct((1, D_MODEL, N_IN), BF16)]
        + sq_shapes + [jax.ShapeDtypeStruct((rows, D_MODEL), F32)],
        scratch_shapes=[pltpu.VMEM((rows, D_MODEL), BF16)],
        compiler_params=_params(("arbitrary",)),
        name="proj_short_first",
    )(xs, xm, g, win32, *squares32)


def _proj_kernel(x_ref, g_ref, win_ref, o_ref):
    hn = _rms_f32(x_ref[...], g_ref[...]).astype(BF16)
    o_ref[...] = _dot(hn, win_ref[...])


def _proj(x, layer, g, win, tm):
    rows = x.shape[0]
    assert rows % tm == 0
    return pl.pallas_call(
        _proj_kernel,
        grid=(rows // tm,),
        in_specs=[pl.BlockSpec((tm, D_MODEL), lambda i: (i, 0)), _resident_layer((1, D_MODEL), layer),
                  _resident_layer((D_MODEL, N_IN), 0)],
        out_specs=pl.BlockSpec((tm, N_IN), lambda i: (i, 0)),
        out_shape=jax.ShapeDtypeStruct((rows, N_IN), F32),
        compiler_params=_params(("arbitrary",)),
        name="proj_short",
    )(x, g, win)


def _ret_short_first(p_ref, s_ref, tab_refs, sb, L, sink):
    cq_ref, sq_ref, ck_ref, sk_ref, dmask_ref, qdec_ref, kdec_ref = tab_refs
    cq, sq, ck, sk = cq_ref[...], sq_ref[...], ck_ref[...], sk_ref[...]
    staged = []
    for b in range(sb):
        r = slice(b * L, (b + 1) * L)
        for h in range(HEADS):
            cl = slice(h * DH, (h + 1) * DH)
            q = _rotary(p_ref[r, C_Q + h * DH:C_Q + (h + 1) * DH], cq, sq).astype(BF16)
            k = _rotary(p_ref[r, C_K + h * DH:C_K + (h + 1) * DH], ck, sk)
            v = p_ref[r, C_V + h * DH:C_V + (h + 1) * DH]
            s = s_ref[b, h]
            scores = (_dot_nt(q, k.astype(BF16)) * dmask_ref[h]).astype(BF16)
            cross = _dot(q, s.astype(BF16)) * qdec_ref[:, cl]
            sink(b, h, r, cl, s, k * kdec_ref[:, cl], v)
            staged.append((scores, cross, v.astype(BF16)))
    return staged


def _ret_short_second(staged, p_ref, c_ref, cw_ref, gated_ref, bgy_ref, cn_ref, sb, L):
    rows = lax.broadcasted_iota(jnp.int32, (L, D_MODEL), 0)
    cw = cw_ref[...]
    for b in range(sb):
        r = slice(b * L, (b + 1) * L)
        for h in range(HEADS):
            cl = slice(h * DH, (h + 1) * DH)
            scores, cross, v = staged[b * HEADS + h]
            o = _dot(scores, v) + cross
            g = p_ref[r, C_G + h * DH:C_G + (h + 1) * DH]
            gated_ref[r, cl] = _silu(g) * _group_norm(o)
        u = p_ref[r, C_CG:C_CG + D_MODEL] * p_ref[r, C_HC:C_HC + D_MODEL]
        y = _short_conv(u, c_ref[b], cw, rows)
        bgy_ref[r, :] = p_ref[r, C_BG:C_BG + D_MODEL] * y
        cn_ref[b] = u[L - (CONV_W - 1):, :]


def _state_sink(sn_ref, gl_ref, layer=None):
    def sink(b, h, r, cl, s, kd, v):
        s_new = gl_ref[:, cl] * s + _dot_tn(kd.astype(BF16), v.astype(BF16))
        if layer is None:
            sn_ref[b, h] = s_new
        else:
            sn_ref[layer, b, h] = s_new
    return sink


def _ret_meta_kernel(p_ref, s_ref, c_ref, cw_ref, cq_ref, sq_ref, ck_ref, sk_ref, dmask_ref, qdec_ref, kdec_ref,
                     gl_ref, gated_ref, bgy_ref, cn_ref, sn_ref, *, L):
    sb = c_ref.shape[0]
    tabs = (cq_ref, sq_ref, ck_ref, sk_ref, dmask_ref, qdec_ref, kdec_ref)
    staged = _ret_short_first(p_ref, s_ref, tabs, sb, L, _state_sink(sn_ref, gl_ref))
    _ret_short_second(staged, p_ref, c_ref, cw_ref, gated_ref, bgy_ref, cn_ref, sb, L)


def _short_table_specs(L):
    return [_resident((L, DH))] * 4 + [_resident((HEADS, L, L)), _resident((L, D_MODEL)), _resident((L, D_MODEL)),
                                       _resident((1, D_MODEL))]


def _ret_meta(proj, row0, state, cprev, layer, cw, rope, decay, nseq, L):
    dmask, qdec, kdec, gl, _ = decay
    rows = nseq * L
    row_spec = pl.BlockSpec((rows, D_MODEL), lambda i: (0, 0))
    row_shape = jax.ShapeDtypeStruct((rows, D_MODEL), F32)
    return pl.pallas_call(
        functools.partial(_ret_meta_kernel, L=L),
        grid=(1,),
        in_specs=[
            pl.BlockSpec((rows, C_GA), lambda i: (row0 // rows, 0)),
            _resident((nseq, HEADS, DH, DH)),
            _resident((nseq, CONV_W - 1, D_MODEL)),
            _resident_layer((CONV_W, D_MODEL), layer),
        ] + _short_table_specs(L),
        out_specs=[row_spec, row_spec, pl.BlockSpec((nseq, CONV_W - 1, D_MODEL), lambda i: (0, 0, 0)),
                   pl.BlockSpec((nseq, HEADS, DH, DH), lambda i: (0, 0, 0, 0))],
        out_shape=[row_shape, row_shape, jax.ShapeDtypeStruct((nseq, CONV_W - 1, D_MODEL), F32),
                   jax.ShapeDtypeStruct((nseq, HEADS, DH, DH), F32)],
        compiler_params=_params(("arbitrary",)),
        name="ret_meta",
    )(proj, state, cprev, cw, *rope, dmask, qdec, kdec, gl)


def _ffn_main_kernel(x_ref, g_ref, *refs, final, n_prev, L):
    wgu_refs = refs[:FFN_PARTS]
    (wd_ref, fg_ref, p_ref, s_ref, c_ref, cw_ref, cq_ref, sq_ref, ck_ref, sk_ref, dmask_ref, qdec_ref, kdec_ref,
     gl_ref) = refs[FFN_PARTS:FFN_PARTS + 14]
    rest = refs[FFN_PARTS + 14:]
    sb = c_ref.shape[0]
    tabs = (cq_ref, sq_ref, ck_ref, sk_ref, dmask_ref, qdec_ref, kdec_ref)

    if final:
        n_main = n_prev + 1
        prev, mains = rest[:3 * n_prev], rest[3 * n_prev:3 * n_prev + n_main]
        prev_cn, outs = rest[3 * n_prev + n_main:4 * n_prev + n_main], rest[4 * n_prev + n_main:]
        y_ref, gated_ref, bgy_ref, cn_stack_ref, sn_ref, mstack_ref = outs
        for l in range(n_main):
            mstack_ref[l] = mains[l][...]
        for l in range(n_prev):
            cn_stack_ref[l] = prev_cn[l][...]
        cn_ref = cn_stack_ref.at[n_prev]
        sink = _state_sink(sn_ref, gl_ref, n_prev)
    else:
        next32, (y_ref, gated_ref, bgy_ref, cn_ref, kd_ref, v_ref), nextb = rest[:4], rest[4:10], rest[10:]
        for src, dst in zip(next32, nextb):
            dst[...] = src[...].astype(BF16)

        def sink(b, h, r, cl, s, kd, v):
            kd_ref[r, cl] = kd
            v_ref[r, cl] = v

    x = x_ref[...]
    hn = _rms_f32(x, g_ref[...]).astype(BF16)
    gates = [_dot(hn, w[...]) for w in wgu_refs[:FFN_PARTS // 2]]
    if final:
        for l in range(n_prev):
            so_ref, kdo_ref, vo_ref = prev[3 * l:3 * l + 3]
            upd = _state_sink(sn_ref, gl_ref, l)
            for b in range(sb):
                r = slice(b * L, (b + 1) * L)
                for h in range(HEADS):
                    cl = slice(h * DH, (h + 1) * DH)
                    upd(b, h, r, cl, so_ref[b, h], kdo_ref[r, cl], vo_ref[r, cl])
    staged = _ret_short_first(p_ref, s_ref, tabs, sb, L, sink)
    ups = [_dot(hn, w[...]) for w in wgu_refs[FFN_PARTS // 2:]]
    _ret_short_second(staged, p_ref, c_ref, cw_ref, gated_ref, bgy_ref, cn_ref, sb, L)
    y = x + _dot(_swiglu(gates, ups), wd_ref[...])
    if final:
        y = _rms_f32(y, fg_ref[...])
    y_ref[...] = y


def _ffn_main(x, layer, g, wgu, wd, fg, final, tm, proj, state, cconv, cw, rope, decay, L, kds, vs, main_states,
              win32, squares32, h_short, gated_m, bgy_m, squares, conv_states):
    rows = x.shape[0]
    steps = rows // tm
    depth, ns = state.shape[:2]
    sb = ns // steps
    assert sb * steps == ns and (sb * L) % 8 == 0
    n_prev = len(kds) if final else 0
    dmask, qdec, kdec, gl, _ = decay
    n_samp = ns * L
    row_spec = pl.BlockSpec((tm, D_MODEL), lambda i: (i, 0))
    srow_spec = pl.BlockSpec((sb * L, D_MODEL), lambda i: (i, 0))
    srow_shape = jax.ShapeDtypeStruct((n_samp, D_MODEL), F32)

    def state_spec(l):
        return pl.BlockSpec((None, sb, HEADS, DH, DH), lambda i: (l, i, 0, 0, 0))

    stream_specs = [
        row_spec,
        pl.BlockSpec((sb * L, C_GA), lambda i: (i, 0)),
        state_spec(layer),
        pl.BlockSpec((None, sb, CONV_W - 1, D_MODEL), lambda i: (layer, i, 0, 0)),
    ]
    stream_args = [x, proj, state, cconv]
    for l in range(n_prev):
        stream_specs += [state_spec(l), srow_spec, srow_spec]
        stream_args += [state, kds[l], vs[l]]
    n_meta = 0 if final else h_short.shape[0] - n_samp
    short_rows = n_samp + n_meta
    branch_shape = jax.ShapeDtypeStruct((short_rows, D_MODEL), F32)
    out_specs = [row_spec, srow_spec, srow_spec, pl.BlockSpec((sb, CONV_W - 1, D_MODEL), lambda i: (i, 0, 0))]
    out_shape = [jax.ShapeDtypeStruct(x.shape, F32), branch_shape, branch_shape,
                 jax.ShapeDtypeStruct((ns, CONV_W - 1, D_MODEL), F32)]
    if final:
        assert n_prev == depth - 1 and len(main_states) == depth
        out_specs.append(pl.BlockSpec((depth, sb, HEADS, DH, DH), lambda i: (0, i, 0, 0, 0)))
        out_shape.append(jax.ShapeDtypeStruct(state.shape, F32))
        n_mat = main_states[0].shape[0] * HEADS
        mb = n_mat // steps
        assert mb * steps == n_mat
        stream_specs += [pl.BlockSpec((mb, DH, DH), lambda i: (i, 0, 0))] * depth
        stream_args += [s.reshape(n_mat, DH, DH) for s in main_states]
        out_specs.append(pl.BlockSpec((depth, mb, DH, DH), lambda i: (0, i, 0, 0)))
        out_shape.append(jax.ShapeDtypeStruct((depth, n_mat, DH, DH), F32))
        assert len(conv_states) == n_prev
        stream_specs += [pl.BlockSpec((sb, CONV_W - 1, D_MODEL), lambda i: (i, 0, 0))] * n_prev
        stream_args += list(conv_states)
        out_specs[3] = pl.BlockSpec((depth, sb, CONV_W - 1, D_MODEL), lambda i: (0, i, 0, 0))
        out_shape[3] = jax.ShapeDtypeStruct((depth, ns, CONV_W - 1, D_MODEL), F32)
    else:
        slab = D_MODEL // steps
        assert slab * steps == D_MODEL and slab % 16 == 0
        out_specs += [srow_spec, srow_spec]
        out_shape += [srow_shape, srow_shape]
        for w32 in (win32, *squares32):
            width = w32.shape[-1]
            stream_specs.append(pl.BlockSpec((None, slab, width), lambda i: (layer + 1, i, 0)))
            stream_args.append(w32)
            out_specs.append(pl.BlockSpec((None, slab, width), lambda i: (0, i, 0)))
            out_shape.append(jax.ShapeDtypeStruct((1, D_MODEL, width), BF16))

    resident_args = [g, wgu, wd, fg, cw, *rope, dmask, qdec, kdec, gl]
    n_res, n_in = len(resident_args), len(stream_args)
    part = 2 * D_FF // FFN_PARTS

    short_tm = SHORT_TM if final else short_rows // SHORT_STEPS_WITH_META
    assert short_rows % short_tm == 0 and short_tm % 8 == 0
    short_args = [h_short, *squares] + ([] if final else [gated_m, bgy_m])
    out_shape.append(jax.ShapeDtypeStruct((short_rows, D_MODEL), F32))
    tile = pl.BlockSpec((short_tm, D_MODEL), lambda i: (i, 0))

    def outer(*refs):
        g_ref, wgu_ref, wd_ref, fg_ref, cw_ref = refs[:5]
        table_refs = refs[5:n_res]
        hbm_in = refs[n_res:n_res + n_in]
        hbm_short = refs[n_res + n_in:n_res + n_in + len(short_args)]
        hbm_out = refs[n_res + n_in + len(short_args):]
        wgu_parts = [wgu_ref.at[0, :, pl.ds(q * part, part)] for q in range(FFN_PARTS)]

        def step(*blocks):
            x_ref, p_ref, s_ref, c_ref = blocks[:4]
            _ffn_main_kernel(x_ref, g_ref.at[layer], *wgu_parts, wd_ref.at[0], fg_ref, p_ref, s_ref, c_ref,
                             cw_ref.at[layer], *table_refs, *blocks[4:], final=final, n_prev=n_prev, L=L)

        def main_pipeline():
            pltpu.emit_pipeline(step, grid=(steps,), in_specs=stream_specs,
                                out_specs=out_specs)(*hbm_in, *hbm_out[:-1])

        p_hbm = hbm_in[1]
        gated_hbm, bgy_hbm, ys_hbm = hbm_out[1], hbm_out[2], hbm_out[-1]
        hs_hbm, square_hbm = hbm_short[0], hbm_short[1:4]
        early_weights = not final

        def short_phase(wro_v, wco_v, wo_v, sems):
            copies = [pltpu.make_async_copy(src.at[0], dst, sems.at[n])
                      for n, (src, dst) in enumerate(zip(square_hbm, (wro_v, wco_v, wo_v)))]
            if early_weights:
                for cp in copies:
                    cp.start()
                main_pipeline()
            late = [] if early_weights else list(copies)
            if n_meta:
                meta_rows = pl.ds(n_samp, n_meta)
                late += [pltpu.make_async_copy(hbm_short[4], gated_hbm.at[meta_rows], sems.at[3]),
                         pltpu.make_async_copy(hbm_short[5], bgy_hbm.at[meta_rows], sems.at[4])]
                copies += late[-2:]
            for cp in late:
                cp.start()
            for cp in copies:
                cp.wait()

            def tile_body(x_ref, gated_ref, bgy_ref, ga_ref, gb_ref, o_ref):
                h = _out_proj(x_ref[...], gated_ref[...], bgy_ref[...], ga_ref[...], gb_ref[...], wro_v, wco_v, wo_v)
                y = _ffn_block(h, g_ref.at[layer], wgu_parts, wd_ref.at[0])
                o_ref[...] = _rms_f32(y, fg_ref[...]) if final else y

            ga_tile = pl.BlockSpec((short_tm, D_MODEL), lambda i: (i, C_GA // D_MODEL))
            gb_tile = pl.BlockSpec((short_tm, D_MODEL), lambda i: (i, C_GB // D_MODEL))
            pltpu.emit_pipeline(tile_body, grid=(short_rows // short_tm,), in_specs=[tile, tile, tile, ga_tile, gb_tile],
                                out_specs=[tile])(hs_hbm, gated_hbm, bgy_hbm, p_hbm, p_hbm, ys_hbm)

        if not early_weights:
            main_pipeline()
        pl.run_scoped(short_phase, pltpu.VMEM((D_MODEL, D_MODEL), BF16), pltpu.VMEM((D_MODEL, D_MODEL), BF16),
                      pltpu.VMEM((D_MODEL, D_MODEL), BF16), pltpu.SemaphoreType.DMA((5,)))

    vmem = pl.BlockSpec(memory_space=pltpu.VMEM)
    hbm = pl.BlockSpec(memory_space=pl.ANY)
    return pl.pallas_call(
        outer,
        in_specs=[vmem] * n_res + [hbm] * (n_in + len(short_args)),
        out_specs=[hbm] * len(out_shape),
        out_shape=out_shape,
        compiler_params=pltpu.CompilerParams(vmem_limit_bytes=VMEM_LIMIT_BYTES),
        name="ffn_main",
    )(*resident_args, *stream_args, *short_args)


def kernel(x_prompt, x_sample, state_ret, state_conv, meta_tokens, norm_mix_g, w_in, conv_w, w_ret_o,
           w_conv_o, w_o, norm_ffn_g, w_gate_up, w_down, final_norm_g):
    depth = w_in.shape[0]
    nb, seq, _ = x_prompt.shape
    ns, ls, _ = x_sample.shape
    n_samp = ns * ls
    n_short = n_samp + N_META
    short_tm = n_short // 5
    assert short_tm * 5 == n_short and short_tm % 8 == 0 and n_samp % N_META == 0

    gm = norm_mix_g.reshape(depth, 1, D_MODEL)
    gf = norm_ffn_g.reshape(depth, 1, D_MODEL)
    fg = final_norm_g.reshape(1, D_MODEL)

    rope_meta = _rope_tables(np.arange(N_META))
    rope_main = _rope_tables(N_META + np.arange(seq))
    rope_samp = _rope_tables(PAST_LEN + np.arange(ls))
    dec_meta, dec_main, dec_samp = _decay_tables(N_META), _decay_tables(CHUNK), _decay_tables(ls)

    h_main = x_prompt
    zero_s = jnp.zeros((1, HEADS, DH, DH), F32)
    zero_c = jnp.zeros((1, CONV_W - 1, D_MODEL), F32)
    squares32 = (w_ret_o, w_conv_o, w_o)

    s_p, c_p, c_s, kds, vs = [], [], [], [], []
    y_samp = s_s = s_p_stacked = c_s_stacked = h_short = next_weights = None
    for l in range(depth):
        last = l == depth - 1
        if l == 0:
            proj, win_b, wro_b, wco_b, wo_b, h_short = _proj_first(
                x_sample.reshape(n_samp, D_MODEL), meta_tokens.astype(F32), l, gm, w_in, squares32)
        else:
            win_b, wro_b, wco_b, wo_b = next_weights
            proj = _proj(h_short, l, gm, win_b, short_tm)
        gated_m, bgy_m, c_m, s_m = _ret_meta(proj, n_samp, zero_s, zero_c, l, conv_w, rope_meta, dec_meta, 1, N_META)

        h_main, s_l, c_l, wgu_b, wd_b = _mixer_main(h_main, s_m, c_m, l, gm, win_b, conv_w, wro_b, wco_b, wo_b,
                                                    rope_main, dec_main, w_gate_up, w_down)
        s_p.append(s_l)
        c_p.append(c_l)
        res = _ffn_main(h_main.reshape(nb * seq, D_MODEL), l, gf, wgu_b, wd_b, fg, last, MAIN_FFN_TM,
                        proj, state_ret, state_conv, conv_w, rope_samp, dec_samp, ls, kds, vs, s_p, w_in, squares32,
                        h_short, gated_m, bgy_m, (wro_b, wco_b, wo_b), c_s)
        h_main = res[0].reshape(nb, seq, D_MODEL)
        if last:
            c_s_stacked = res[3]
            s_s = res[4]
            s_p_stacked = res[5].reshape(depth, nb, HEADS, DH, DH)
            y_samp = res[-1]
        else:
            c_s.append(res[3])
            kds.append(res[4])
            vs.append(res[5])
            next_weights = res[6:10]
            h_short = res[-1]

    return (h_main, y_samp.reshape(ns, ls, D_MODEL), s_p_stacked, jnp.stack(c_p), s_s, c_s_stacked)
```

```python
import functools

import numpy as np
import jax
import jax.numpy as jnp
from jax import lax
from jax.experimental import pallas as pl
from jax.experimental.pallas import tpu as pltpu

D_MODEL = 1024
N_META = 16
HEADS = 8
DH = D_MODEL // HEADS
CHUNK = 128
ROPE_BASE = 10000.0
CONV_W = 3
D_FF = ((8 * D_MODEL + 3 * 256 - 1) // (3 * 256)) * 256
EPS = 1e-6
PAST_LEN = 16384
N_IN = 9 * D_MODEL
C_Q, C_K, C_V, C_G, C_BG, C_CG, C_HC, C_GA, C_GB = (i * D_MODEL for i in range(9))

F32 = jnp.float32
BF16 = jnp.bfloat16

VMEM_LIMIT_BYTES = 58 * 1024 * 1024

MAIN_BG = 4
MAIN_FFN_TM = 512
SHORT_TM = 256
SHORT_STEPS_WITH_META = 5
FFN_PARTS = 2
assert (2 * D_FF // FFN_PARTS) % 256 == 0 and FFN_PARTS % 2 == 0


def _resident(shape):
    nd = len(shape)
    return pl.BlockSpec(shape, lambda *_: (0,) * nd, pipeline_mode=pl.Buffered(1))


def _resident_layer(shape, layer):
    nd = len(shape)
    return pl.BlockSpec((None,) + tuple(shape), lambda *_: (layer,) + (0,) * nd, pipeline_mode=pl.Buffered(1))


def _resident_col_parts(rows, width, n_parts):
    return [pl.BlockSpec((None, rows, width), functools.partial(lambda *_, q: (0, 0, q), q=q),
                         pipeline_mode=pl.Buffered(1)) for q in range(n_parts)]


def _params(sem):
    return pltpu.CompilerParams(dimension_semantics=sem, vmem_limit_bytes=VMEM_LIMIT_BYTES)


def _dot(a, b):
    return jnp.dot(a, b, preferred_element_type=F32)


def _dot_nt(a, b):
    return lax.dot_general(a, b, (((1,), (1,)), ((), ())), preferred_element_type=F32)


def _dot_tn(a, b):
    return lax.dot_general(a, b, (((0,), (0,)), ((), ())), preferred_element_type=F32)


def _rms_f32(x, g):
    return x * lax.rsqrt(jnp.mean(x * x, axis=-1, keepdims=True) + EPS) * g


def _sigmoid(x):
    return 1.0 / (1.0 + jnp.exp(-x))


def _silu(x):
    return x * _sigmoid(x)


def _rotary(t, cos, sin):
    return t * cos + pltpu.roll(t, DH // 2, 1) * sin


def _group_norm(o):
    mu = jnp.mean(o, axis=-1, keepdims=True)
    d = o - mu
    var = jnp.mean(d * d, axis=-1, keepdims=True)
    return d * lax.rsqrt(var + EPS)


def _retention_head_paired(q, kt, kdt, v, s, dmask, qdec, gl):
    L = q.shape[0]
    sc = _dot(q, jnp.concatenate([kt, s.astype(BF16)], axis=1))
    scores = (sc[:, :L] * dmask).astype(BF16)
    iu = _dot(jnp.concatenate([scores, kdt], axis=0), v)
    return iu[:L] + sc[:, L:] * qdec, gl * s + iu[L:]


def _short_conv(u, tail, cw, rows):
    r1 = pltpu.roll(u, 1, 0)
    r2 = pltpu.roll(u, 2, 0)
    t0, t1 = tail[0:1, :], tail[1:2, :]
    sh1 = jnp.where(rows == 0, t1, r1)
    sh2 = jnp.where(rows == 0, t0, jnp.where(rows == 1, t1, r2))
    return cw[0:1, :] * sh2 + cw[1:2, :] * sh1 + cw[2:3, :] * u


def _out_proj(x, gated, bgy, ga, gb, wro_ref, wco_ref, wo_ref):
    ret_out = _dot(gated.astype(BF16), wro_ref[...])
    conv_out = _dot(bgy.astype(BF16), wco_ref[...])
    merged = _sigmoid(ga) * ret_out + _sigmoid(gb) * conv_out
    return x + _dot(merged.astype(BF16), wo_ref[...])


def _swiglu(gates, ups):
    return jnp.concatenate([_silu(a) * b for a, b in zip(gates, ups)], axis=1).astype(BF16)


def _ffn_block(x, g_ref, wgu_refs, wd_ref):
    hn = _rms_f32(x, g_ref[...]).astype(BF16)
    n = len(wgu_refs) // 2
    gates = [_dot(hn, w[...]) for w in wgu_refs[:n]]
    ups = [_dot(hn, w[...]) for w in wgu_refs[n:]]
    return x + _dot(_swiglu(gates, ups), wd_ref[...])


def _log_gamma():
    return np.log1p(-np.exp2(-5.0 - np.arange(HEADS, dtype=np.float64)))


def _const(t):
    return jnp.asarray(np.asarray(t, dtype=np.float32))


def _rope_tables(pos):
    half = DH // 2
    inv = np.power(ROPE_BASE, -np.arange(half, dtype=np.float64) / half)
    ang = np.asarray(pos, dtype=np.float64)[:, None] * inv[None, :]
    cos, sin = np.cos(ang), np.sin(ang)
    cosf = np.concatenate([cos, cos], axis=-1)
    sinf = np.concatenate([-sin, sin], axis=-1)
    scale = DH ** -0.5
    return tuple(_const(t) for t in (cosf, sinf, cosf * scale, sinf * scale))


def _decay_tables(L):
    log_g = _log_gamma()
    idx = np.arange(L, dtype=np.float64)
    diff = idx[:, None] - idx[None, :]
    dmask = np.where(diff >= 0, np.exp(log_g[:, None, None] * np.maximum(diff, 0.0)[None]), 0.0)
    qdec = np.exp(log_g[:, None] * (idx + 1.0)[None])
    kdec = np.exp(log_g[:, None] * (L - 1.0 - idx)[None])
    gl = np.exp(log_g * L)
    lanes = lambda t: np.repeat(t.T, DH, axis=1)
    return tuple(_const(t) for t in (dmask, lanes(qdec), lanes(kdec), np.repeat(gl, DH)[None, :], kdec[:, None, :]))


def _mixer_main_kernel(x_ref, s0_ref, c0_ref, g_ref, *refs):
    n_groups = N_IN // D_MODEL
    wq_ref, wk_ref, wv_ref, wg_ref, wbg_ref, wcg_ref, whc_ref, wga_ref, wgb_ref = refs[:n_groups]
    (cw_ref, wro_ref, wco_ref, wo_ref, cq_ref, sq_ref, ck_ref, sk_ref, dmask_ref, qdec_ref, kdec_ref, gl_ref,
     wgu32_ref, wd32_ref, h_ref, s_ref, c_ref, wgub_ref, wdb_ref, q_s, kt_s, kdt_s, v_s, o_s) = refs[n_groups:]
    bg = x_ref.shape[0]
    m = bg * CHUNK

    wgub_ref[...] = wgu32_ref[...].astype(BF16)
    step = pl.program_id(0) * pl.num_programs(1) + pl.program_id(1)

    @pl.when(step < D_FF // wd32_ref.shape[0])
    def _():
        wdb_ref[...] = wd32_ref[...].astype(BF16)

    @pl.when(pl.program_id(1) == 0)
    def _():
        for b in range(bg):
            s_ref[b] = s0_ref[...]
            c_ref[b] = c0_ref[...]

    x = x_ref[...].reshape(m, D_MODEL)
    hn = _rms_f32(x, g_ref[...]).astype(BF16)

    cq, sq, ck, sk = cq_ref[...], sq_ref[...], ck_ref[...], sk_ref[...]
    q = _dot(hn, wq_ref[...])
    k = _dot(hn, wk_ref[...])
    v_s[...] = _dot(hn, wv_ref[...]).astype(BF16)
    for b in range(bg):
        r = slice(b * CHUNK, (b + 1) * CHUNK)
        for h in range(HEADS):
            cl = slice(h * DH, (h + 1) * DH)
            q_s[r, cl] = _rotary(q[r, cl], cq, sq).astype(BF16)
            krt = _rotary(k[r, cl], ck, sk).T
            kt_s[b * HEADS + h] = krt.astype(BF16)
            kdt_s[b * HEADS + h] = (krt * kdec_ref[h]).astype(BF16)

    for b in range(bg):
        r = slice(b * CHUNK, (b + 1) * CHUNK)
        for h in range(HEADS):
            cl = slice(h * DH, (h + 1) * DH)
            o, s_new = _retention_head_paired(q_s[r, cl], kt_s[b * HEADS + h], kdt_s[b * HEADS + h], v_s[r, cl],
                                              s_ref[b, h], dmask_ref[h], qdec_ref[:, cl], gl_ref[:, cl])
            s_ref[b, h] = s_new
            o_s[r, cl] = _group_norm(o)

    g = _dot(hn, wg_ref[...])
    gated = _silu(g) * o_s[...]

    bgate = _dot(hn, wbg_ref[...])
    u = _dot(hn, wcg_ref[...]) * _dot(hn, whc_ref[...])
    rows = lax.broadcasted_iota(jnp.int32, (CHUNK, D_MODEL), 0)
    cw = cw_ref[...]
    ys = []
    for b in range(bg):
        ub = u[b * CHUNK:(b + 1) * CHUNK]
        ys.append(_short_conv(ub, c_ref[b], cw, rows))
        c_ref[b] = ub[CHUNK - (CONV_W - 1):, :]
    bgy = bgate * jnp.concatenate(ys, axis=0)

    ga = _dot(hn, wga_ref[...])
    gb = _dot(hn, wgb_ref[...])
    out = _out_proj(x, gated, bgy, ga, gb, wro_ref, wco_ref, wo_ref)
    h_ref[...] = out.reshape(bg, CHUNK, D_MODEL)


def _mixer_main(x, s0, c0, layer, g, win, cw, wro, wco, wo, rope, decay, wgu32, wd32):
    nb, seq, _ = x.shape
    bg = MAIN_BG
    cq, sq, ck, sk = rope
    dmask, qdec, _, gl, kdec = decay
    m = bg * CHUNK
    n_chunks = seq // CHUNK
    steps = (nb // bg) * n_chunks
    slab = D_MODEL // steps
    wd_slab = 16 * (-(-D_FF // (16 * steps)))
    while D_FF % wd_slab:
        wd_slab += 16
    wd_last = D_FF // wd_slab - 1
    assert slab * steps == D_MODEL and slab % 16 == 0 and wd_last < steps

    def wd_block(i, c):
        return jnp.minimum(i * n_chunks + c, wd_last)

    rope_spec = pl.BlockSpec((CHUNK, DH), lambda i, c: (c, 0))
    stream_specs = [
        pl.BlockSpec((bg, CHUNK, D_MODEL), lambda i, c: (i, c, 0)),
        rope_spec, rope_spec, rope_spec, rope_spec,
        pl.BlockSpec((None, slab, 2 * D_FF), lambda i, c: (layer, i * n_chunks + c, 0)),
        pl.BlockSpec((None, wd_slab, D_MODEL), lambda i, c: (layer, wd_block(i, c), 0)),
    ]
    out_specs = [
        pl.BlockSpec((bg, CHUNK, D_MODEL), lambda i, c: (i, c, 0)),
        pl.BlockSpec((bg, HEADS, DH, DH), lambda i, c: (i, 0, 0, 0)),
        pl.BlockSpec((bg, CONV_W - 1, D_MODEL), lambda i, c: (i, 0, 0)),
        pl.BlockSpec((None, slab, 2 * D_FF), lambda i, c: (0, i * n_chunks + c, 0)),
        pl.BlockSpec((None, wd_slab, D_MODEL), lambda i, c: (0, wd_block(i, c), 0)),
    ]
    stream_args = [x, cq, sq, ck, sk, wgu32, wd32]
    resident_args = [s0, c0, g, win, cw, wro, wco, wo, dmask, qdec, kdec, gl]
    n_res, n_in, n_out = len(resident_args), len(stream_args), len(out_specs)
    n_groups = N_IN // D_MODEL

    def outer(*refs):
        s0_ref, c0_ref, g_ref, win_ref, cw_ref, wro_ref, wco_ref, wo_ref = refs[:8]
        table_refs = refs[8:n_res]
        hbm_in, hbm_out = refs[n_res:n_res + n_in], refs[n_res + n_in:n_res + n_in + n_out]
        scratch = refs[n_res + n_in + n_out:]
        win_parts = [win_ref.at[0, :, pl.ds(j * D_MODEL, D_MODEL)] for j in range(n_groups)]

        def step(x_b, cq_b, sq_b, ck_b, sk_b, wgu32_b, wd32_b, *out_blocks):
            _mixer_main_kernel(x_b, s0_ref.at[0], c0_ref.at[0], g_ref.at[layer], *win_parts, cw_ref.at[layer],
                               wro_ref.at[0], wco_ref.at[0], wo_ref.at[0], cq_b, sq_b, ck_b, sk_b, *table_refs,
                               wgu32_b, wd32_b, *out_blocks, *scratch)

        pltpu.emit_pipeline(step, grid=(nb // bg, n_chunks), in_specs=stream_specs,
                            out_specs=out_specs)(*hbm_in, *hbm_out)

    vmem = pl.BlockSpec(memory_space=pltpu.VMEM)
    hbm = pl.BlockSpec(memory_space=pl.ANY)
    return pl.pallas_call(
        outer,
        in_specs=[vmem] * n_res + [hbm] * n_in,
        out_specs=[hbm] * n_out,
        out_shape=[
            jax.ShapeDtypeStruct(x.shape, F32),
            jax.ShapeDtypeStruct((nb, HEADS, DH, DH), F32),
            jax.ShapeDtypeStruct((nb, CONV_W - 1, D_MODEL), F32),
            jax.ShapeDtypeStruct((1, D_MODEL, 2 * D_FF), BF16),
            jax.ShapeDtypeStruct((1, D_FF, D_MODEL), BF16),
        ],
        scratch_shapes=[
            pltpu.VMEM((m, D_MODEL), BF16),
            pltpu.VMEM((bg * HEADS, DH, CHUNK), BF16),
            pltpu.VMEM((bg * HEADS, DH, CHUNK), BF16),
            pltpu.VMEM((m, D_MODEL), BF16),
            pltpu.VMEM((m, D_MODEL), F32),
        ],
        compiler_params=pltpu.CompilerParams(vmem_limit_bytes=VMEM_LIMIT_BYTES),
        name="mixer_main",
    )(*resident_args, *stream_args)


def _round_square_slabs(step, n_slabs, sq_refs, sb_refs):
    @pl.when(step < n_slabs)
    def _():
        for src, dst in zip(sq_refs, sb_refs):
            dst[...] = src[...].astype(BF16)


def _square_slab_specs(layer, slab, n_slabs):
    slab_in = pl.BlockSpec((None, slab, D_MODEL), lambda j: (layer, jnp.minimum(j, n_slabs - 1), 0))
    slab_out = pl.BlockSpec((None, slab, D_MODEL), lambda j: (0, jnp.minimum(j, n_slabs - 1), 0))
    return [slab_in] * 3, [slab_out] * 3, [jax.ShapeDtypeStruct((1, D_MODEL, D_MODEL), BF16)] * 3


def _proj_first_kernel(xs_ref, xm_ref, g_ref, w32_ref, sq0_ref, sq1_ref, sq2_ref,
                       o_ref, wb_ref, sb0_ref, sb1_ref, sb2_ref, xcat_ref, hn_s, *, n_slabs):
    j = pl.program_id(0)
    n_samp = xs_ref.shape[0]

    @pl.when(j == 0)
    def _():
        xcat_ref[:n_samp, :] = xs_ref[...]
        xcat_ref[n_samp:, :] = xm_ref[...]
        hn_s[:n_samp, :] = _rms_f32(xs_ref[...], g_ref[...]).astype(BF16)
        hn_s[n_samp:, :] = _rms_f32(xm_ref[...], g_ref[...]).astype(BF16)

    _round_square_slabs(j, n_slabs, (sq0_ref, sq1_ref, sq2_ref), (sb0_ref, sb1_ref, sb2_ref))
    wb = w32_ref[...].astype(BF16)
    wb_ref[...] = wb
    o_ref[...] = _dot(hn_s[...], wb)


def _proj_first(xs, xm, layer, g, win32, squares32):
    rows = xs.shape[0] + xm.shape[0]
    n_col = N_IN // D_MODEL
    slab = 128
    n_slabs = D_MODEL // slab
    assert n_slabs <= n_col
    sq_in, sq_out, sq_shapes = _square_slab_specs(layer, slab, n_slabs)
    return pl.pallas_call(
        functools.partial(_proj_first_kernel, n_slabs=n_slabs),
        grid=(n_col,),
        in_specs=[_resident(xs.shape), _resident(xm.shape), _resident_layer((1, D_MODEL), layer),
                  pl.BlockSpec((None, D_MODEL, D_MODEL), lambda j: (layer, 0, j))] + sq_in,
        out_specs=[pl.BlockSpec((rows, D_MODEL), lambda j: (0, j)),
                   pl.BlockSpec((None, D_MODEL, D_MODEL), lambda j: (0, 0, j))] + sq_out
        + [pl.BlockSpec((rows, D_MODEL), lambda j: (0, 0))],
        out_shape=[jax.ShapeDtypeStruct((rows, N_IN), F32), jax.ShapeDtypeStruct((1, D_MODEL, N_IN), BF16)]
        + sq_shapes + [jax.ShapeDtypeStruct((rows, D_MODEL), F32)],
        scratch_shapes=[pltpu.VMEM((rows, D_MODEL), BF16)],
        compiler_params=_params(("arbitrary",)),
        name="proj_short_first",
    )(xs, xm, g, win32, *squares32)


def _proj_kernel(x_ref, g_ref, win_ref, o_ref):
    hn = _rms_f32(x_ref[...], g_ref[...]).astype(BF16)
    o_ref[...] = _dot(hn, win_ref[...])


def _proj(x, layer, g, win, tm):
    rows = x.shape[0]
    assert rows % tm == 0
    return pl.pallas_call(
        _proj_kernel,
        grid=(rows // tm,),
        in_specs=[pl.BlockSpec((tm, D_MODEL), lambda i: (i, 0)), _resident_layer((1, D_MODEL), layer),
                  _resident_layer((D_MODEL, N_IN), 0)],
        out_specs=pl.BlockSpec((tm, N_IN), lambda i: (i, 0)),
        out_shape=jax.ShapeDtypeStruct((rows, N_IN), F32),
        compiler_params=_params(("arbitrary",)),
        name="proj_short",
    )(x, g, win)


def _ret_short_first(p_ref, s_ref, tab_refs, sb, L, sink):
    cq_ref, sq_ref, ck_ref, sk_ref, dmask_ref, qdec_ref, kdec_ref = tab_refs
    cq, sq, ck, sk = cq_ref[...], sq_ref[...], ck_ref[...], sk_ref[...]
    staged = []
    for b in range(sb):
        r = slice(b * L, (b + 1) * L)
        for h in range(HEADS):
            cl = slice(h * DH, (h + 1) * DH)
            q = _rotary(p_ref[r, C_Q + h * DH:C_Q + (h + 1) * DH], cq, sq).astype(BF16)
            k = _rotary(p_ref[r, C_K + h * DH:C_K + (h + 1) * DH], ck, sk)
            v = p_ref[r, C_V + h * DH:C_V + (h + 1) * DH]
            s = s_ref[b, h]
            scores = (_dot_nt(q, k.astype(BF16)) * dmask_ref[h]).astype(BF16)
            cross = _dot(q, s.astype(BF16)) * qdec_ref[:, cl]
            sink(b, h, r, cl, s, k * kdec_ref[:, cl], v)
            staged.append((scores, cross, v.astype(BF16)))
    return staged


def _ret_short_second(staged, p_ref, c_ref, cw_ref, gated_ref, bgy_ref, cn_ref, sb, L):
    rows = lax.broadcasted_iota(jnp.int32, (L, D_MODEL), 0)
    cw = cw_ref[...]
    for b in range(sb):
        r = slice(b * L, (b + 1) * L)
        for h in range(HEADS):
            cl = slice(h * DH, (h + 1) * DH)
            scores, cross, v = staged[b * HEADS + h]
            o = _dot(scores, v) + cross
            g = p_ref[r, C_G + h * DH:C_G + (h + 1) * DH]
            gated_ref[r, cl] = _silu(g) * _group_norm(o)
        u = p_ref[r, C_CG:C_CG + D_MODEL] * p_ref[r, C_HC:C_HC + D_MODEL]
        y = _short_conv(u, c_ref[b], cw, rows)
        bgy_ref[r, :] = p_ref[r, C_BG:C_BG + D_MODEL] * y
        cn_ref[b] = u[L - (CONV_W - 1):, :]


def _state_sink(sn_ref, gl_ref, layer=None):
    def sink(b, h, r, cl, s, kd, v):
        s_new = gl_ref[:, cl] * s + _dot_tn(kd.astype(BF16), v.astype(BF16))
        if layer is None:
            sn_ref[b, h] = s_new
        else:
            sn_ref[layer, b, h] = s_new
    return sink


def _ret_meta_kernel(p_ref, s_ref, c_ref, cw_ref, cq_ref, sq_ref, ck_ref, sk_ref, dmask_ref, qdec_ref, kdec_ref,
                     gl_ref, gated_ref, bgy_ref, cn_ref, sn_ref, *, L):
    sb = c_ref.shape[0]
    tabs = (cq_ref, sq_ref, ck_ref, sk_ref, dmask_ref, qdec_ref, kdec_ref)
    staged = _ret_short_first(p_ref, s_ref, tabs, sb, L, _state_sink(sn_ref, gl_ref))
    _ret_short_second(staged, p_ref, c_ref, cw_ref, gated_ref, bgy_ref, cn_ref, sb, L)


def _short_table_specs(L):
    return [_resident((L, DH))] * 4 + [_resident((HEADS, L, L)), _resident((L, D_MODEL)), _resident((L, D_MODEL)),
                                       _resident((1, D_MODEL))]


def _ret_meta(proj, row0, state, cprev, layer, cw, rope, decay, nseq, L):
    dmask, qdec, kdec, gl, _ = decay
    rows = nseq * L
    row_spec = pl.BlockSpec((rows, D_MODEL), lambda i: (0, 0))
    row_shape = jax.ShapeDtypeStruct((rows, D_MODEL), F32)
    return pl.pallas_call(
        functools.partial(_ret_meta_kernel, L=L),
        grid=(1,),
        in_specs=[
            pl.BlockSpec((rows, C_GA), lambda i: (row0 // rows, 0)),
            _resident((nseq, HEADS, DH, DH)),
            _resident((nseq, CONV_W - 1, D_MODEL)),
            _resident_layer((CONV_W, D_MODEL), layer),
        ] + _short_table_specs(L),
        out_specs=[row_spec, row_spec, pl.BlockSpec((nseq, CONV_W - 1, D_MODEL), lambda i: (0, 0, 0)),
                   pl.BlockSpec((nseq, HEADS, DH, DH), lambda i: (0, 0, 0, 0))],
        out_shape=[row_shape, row_shape, jax.ShapeDtypeStruct((nseq, CONV_W - 1, D_MODEL), F32),
                   jax.ShapeDtypeStruct((nseq, HEADS, DH, DH), F32)],
        compiler_params=_params(("arbitrary",)),
        name="ret_meta",
    )(proj, state, cprev, cw, *rope, dmask, qdec, kdec, gl)


def _ffn_main_kernel(x_ref, g_ref, *refs, final, n_prev, L):
    wgu_refs = refs[:FFN_PARTS]
    (wd_ref, fg_ref, p_ref, s_ref, c_ref, cw_ref, cq_ref, sq_ref, ck_ref, sk_ref, dmask_ref, qdec_ref, kdec_ref,
     gl_ref) = refs[FFN_PARTS:FFN_PARTS + 14]
    rest = refs[FFN_PARTS + 14:]
    sb = c_ref.shape[0]
    tabs = (cq_ref, sq_ref, ck_ref, sk_ref, dmask_ref, qdec_ref, kdec_ref)

    if final:
        n_main = n_prev + 1
        prev, mains = rest[:3 * n_prev], rest[3 * n_prev:3 * n_prev + n_main]
        prev_cn, outs = rest[3 * n_prev + n_main:4 * n_prev + n_main], rest[4 * n_prev + n_main:]
        y_ref, gated_ref, bgy_ref, cn_stack_ref, sn_ref, mstack_ref = outs
        for l in range(n_main):
            mstack_ref[l] = mains[l][...]
        for l in range(n_prev):
            cn_stack_ref[l] = prev_cn[l][...]
        cn_ref = cn_stack_ref.at[n_prev]
        sink = _state_sink(sn_ref, gl_ref, n_prev)
    else:
        next32, (y_ref, gated_ref, bgy_ref, cn_ref, kd_ref, v_ref), nextb = rest[:4], rest[4:10], rest[10:]
        for src, dst in zip(next32, nextb):
            dst[...] = src[...].astype(BF16)

        def sink(b, h, r, cl, s, kd, v):
            kd_ref[r, cl] = kd
            v_ref[r, cl] = v

    x = x_ref[...]
    hn = _rms_f32(x, g_ref[...]).astype(BF16)
    gates = [_dot(hn, w[...]) for w in wgu_refs[:FFN_PARTS // 2]]
    if final:
        for l in range(n_prev):
            so_ref, kdo_ref, vo_ref = prev[3 * l:3 * l + 3]
            upd = _state_sink(sn_ref, gl_ref, l)
            for b in range(sb):
                r = slice(b * L, (b + 1) * L)
                for h in range(HEADS):
                    cl = slice(h * DH, (h + 1) * DH)
                    upd(b, h, r, cl, so_ref[b, h], kdo_ref[r, cl], vo_ref[r, cl])
    staged = _ret_short_first(p_ref, s_ref, tabs, sb, L, sink)
    ups = [_dot(hn, w[...]) for w in wgu_refs[FFN_PARTS // 2:]]
    _ret_short_second(staged, p_ref, c_ref, cw_ref, gated_ref, bgy_ref, cn_ref, sb, L)
    y = x + _dot(_swiglu(gates, ups), wd_ref[...])
    if final:
        y = _rms_f32(y, fg_ref[...])
    y_ref[...] = y


def _ffn_main(x, layer, g, wgu, wd, fg, final, tm, proj, state, cconv, cw, rope, decay, L, kds, vs, main_states,
              win32, squares32, h_short, gated_m, bgy_m, squares, conv_states):
    rows = x.shape[0]
    steps = rows // tm
    depth, ns = state.shape[:2]
    sb = ns // steps
    assert sb * steps == ns and (sb * L) % 8 == 0
    n_prev = len(kds) if final else 0
    dmask, qdec, kdec, gl, _ = decay
    n_samp = ns * L
    row_spec = pl.BlockSpec((tm, D_MODEL), lambda i: (i, 0))
    srow_spec = pl.BlockSpec((sb * L, D_MODEL), lambda i: (i, 0))
    srow_shape = jax.ShapeDtypeStruct((n_samp, D_MODEL), F32)

    def state_spec(l):
        return pl.BlockSpec((None, sb, HEADS, DH, DH), lambda i: (l, i, 0, 0, 0))

    stream_specs = [
        row_spec,
        pl.BlockSpec((sb * L, C_GA), lambda i: (i, 0)),
        state_spec(layer),
        pl.BlockSpec((None, sb, CONV_W - 1, D_MODEL), lambda i: (layer, i, 0, 0)),
    ]
    stream_args = [x, proj, state, cconv]
    for l in range(n_prev):
        stream_specs += [state_spec(l), srow_spec, srow_spec]
        stream_args += [state, kds[l], vs[l]]
    n_meta = 0 if final else h_short.shape[0] - n_samp
    short_rows = n_samp + n_meta
    branch_shape = jax.ShapeDtypeStruct((short_rows, D_MODEL), F32)
    out_specs = [row_spec, srow_spec, srow_spec, pl.BlockSpec((sb, CONV_W - 1, D_MODEL), lambda i: (i, 0, 0))]
    out_shape = [jax.ShapeDtypeStruct(x.shape, F32), branch_shape, branch_shape,
                 jax.ShapeDtypeStruct((ns, CONV_W - 1, D_MODEL), F32)]
    if final:
        assert n_prev == depth - 1 and len(main_states) == depth
        out_specs.append(pl.BlockSpec((depth, sb, HEADS, DH, DH), lambda i: (0, i, 0, 0, 0)))
        out_shape.append(jax.ShapeDtypeStruct(state.shape, F32))
        n_mat = main_states[0].shape[0] * HEADS
        mb = n_mat // steps
        assert mb * steps == n_mat
        stream_specs += [pl.BlockSpec((mb, DH, DH), lambda i: (i, 0, 0))] * depth
        stream_args += [s.reshape(n_mat, DH, DH) for s in main_states]
        out_specs.append(pl.BlockSpec((depth, mb, DH, DH), lambda i: (0, i, 0, 0)))
        out_shape.append(jax.ShapeDtypeStruct((depth, n_mat, DH, DH), F32))
        assert len(conv_states) == n_prev
        stream_specs += [pl.BlockSpec((sb, CONV_W - 1, D_MODEL), lambda i: (i, 0, 0))] * n_prev
        stream_args += list(conv_states)
        out_specs[3] = pl.BlockSpec((depth, sb, CONV_W - 1, D_MODEL), lambda i: (0, i, 0, 0))
        out_shape[3] = jax.ShapeDtypeStruct((depth, ns, CONV_W - 1, D_MODEL), F32)
    else:
        slab = D_MODEL // steps
        assert slab * steps == D_MODEL and slab % 16 == 0
        out_specs += [srow_spec, srow_spec]
        out_shape += [srow_shape, srow_shape]
        for w32 in (win32, *squares32):
            width = w32.shape[-1]
            stream_specs.append(pl.BlockSpec((None, slab, width), lambda i: (layer + 1, i, 0)))
            stream_args.append(w32)
            out_specs.append(pl.BlockSpec((None, slab, width), lambda i: (0, i, 0)))
            out_shape.append(jax.ShapeDtypeStruct((1, D_MODEL, width), BF16))

    resident_args = [g, wgu, wd, fg, cw, *rope, dmask, qdec, kdec, gl]
    n_res, n_in = len(resident_args), len(stream_args)
    part = 2 * D_FF // FFN_PARTS

    short_tm = SHORT_TM if final else short_rows // SHORT_STEPS_WITH_META
    assert short_rows % short_tm == 0 and short_tm % 8 == 0
    short_args = [h_short, *squares] + ([] if final else [gated_m, bgy_m])
    out_shape.append(jax.ShapeDtypeStruct((short_rows, D_MODEL), F32))
    tile = pl.BlockSpec((short_tm, D_MODEL), lambda i: (i, 0))

    def outer(*refs):
        g_ref, wgu_ref, wd_ref, fg_ref, cw_ref = refs[:5]
        table_refs = refs[5:n_res]
        hbm_in = refs[n_res:n_res + n_in]
        hbm_short = refs[n_res + n_in:n_res + n_in + len(short_args)]
        hbm_out = refs[n_res + n_in + len(short_args):]
        wgu_parts = [wgu_ref.at[0, :, pl.ds(q * part, part)] for q in range(FFN_PARTS)]

        def step(*blocks):
            x_ref, p_ref, s_ref, c_ref = blocks[:4]
            _ffn_main_kernel(x_ref, g_ref.at[layer], *wgu_parts, wd_ref.at[0], fg_ref, p_ref, s_ref, c_ref,
                             cw_ref.at[layer], *table_refs, *blocks[4:], final=final, n_prev=n_prev, L=L)

        pltpu.emit_pipeline(step, grid=(steps,), in_specs=stream_specs, out_specs=out_specs)(*hbm_in, *hbm_out[:-1])

        p_hbm = hbm_in[1]
        gated_hbm, bgy_hbm, ys_hbm = hbm_out[1], hbm_out[2], hbm_out[-1]
        hs_hbm, square_hbm = hbm_short[0], hbm_short[1:4]

        def short_phase(wro_v, wco_v, wo_v, sems):
            copies = [pltpu.make_async_copy(src.at[0], dst, sems.at[n])
                      for n, (src, dst) in enumerate(zip(square_hbm, (wro_v, wco_v, wo_v)))]
            if n_meta:
                meta_rows = pl.ds(n_samp, n_meta)
                copies += [pltpu.make_async_copy(hbm_short[4], gated_hbm.at[meta_rows], sems.at[3]),
                           pltpu.make_async_copy(hbm_short[5], bgy_hbm.at[meta_rows], sems.at[4])]
            for cp in copies:
                cp.start()
            for cp in copies:
                cp.wait()

            def tile_body(x_ref, gated_ref, bgy_ref, ga_ref, gb_ref, o_ref):
                h = _out_proj(x_ref[...], gated_ref[...], bgy_ref[...], ga_ref[...], gb_ref[...], wro_v, wco_v, wo_v)
                y = _ffn_block(h, g_ref.at[layer], wgu_parts, wd_ref.at[0])
                o_ref[...] = _rms_f32(y, fg_ref[...]) if final else y

            def in_tile(col):
                return pl.BlockSpec((short_tm, D_MODEL), lambda i: (i, col), pipeline_mode=pl.Buffered(3))

            pltpu.emit_pipeline(tile_body, grid=(short_rows // short_tm,),
                                in_specs=[in_tile(0), in_tile(0), in_tile(0), in_tile(C_GA // D_MODEL),
                                          in_tile(C_GB // D_MODEL)],
                                out_specs=[tile])(hs_hbm, gated_hbm, bgy_hbm, p_hbm, p_hbm, ys_hbm)

        pl.run_scoped(short_phase, pltpu.VMEM((D_MODEL, D_MODEL), BF16), pltpu.VMEM((D_MODEL, D_MODEL), BF16),
                      pltpu.VMEM((D_MODEL, D_MODEL), BF16), pltpu.SemaphoreType.DMA((5,)))

    vmem = pl.BlockSpec(memory_space=pltpu.VMEM)
    hbm = pl.BlockSpec(memory_space=pl.ANY)
    return pl.pallas_call(
        outer,
        in_specs=[vmem] * n_res + [hbm] * (n_in + len(short_args)),
        out_specs=[hbm] * len(out_shape),
        out_shape=out_shape,
        compiler_params=pltpu.CompilerParams(vmem_limit_bytes=VMEM_LIMIT_BYTES),
        name="ffn_main",
    )(*resident_args, *stream_args, *short_args)


def kernel(x_prompt, x_sample, state_ret, state_conv, meta_tokens, norm_mix_g, w_in, conv_w, w_ret_o,
           w_conv_o, w_o, norm_ffn_g, w_gate_up, w_down, final_norm_g):
    depth = w_in.shape[0]
    nb, seq, _ = x_prompt.shape
    ns, ls, _ = x_sample.shape
    n_samp = ns * ls
    n_short = n_samp + N_META
    short_tm = n_short // 5
    assert short_tm * 5 == n_short and short_tm % 8 == 0 and n_samp % N_META == 0

    gm = norm_mix_g.reshape(depth, 1, D_MODEL)
    gf = norm_ffn_g.reshape(depth, 1, D_MODEL)
    fg = final_norm_g.reshape(1, D_MODEL)

    rope_meta = _rope_tables(np.arange(N_META))
    rope_main = _rope_tables(N_META + np.arange(seq))
    rope_samp = _rope_tables(PAST_LEN + np.arange(ls))
    dec_meta, dec_main, dec_samp = _decay_tables(N_META), _decay_tables(CHUNK), _decay_tables(ls)

    h_main = x_prompt
    zero_s = jnp.zeros((1, HEADS, DH, DH), F32)
    zero_c = jnp.zeros((1, CONV_W - 1, D_MODEL), F32)
    squares32 = (w_ret_o, w_conv_o, w_o)

    s_p, c_p, c_s, kds, vs = [], [], [], [], []
    y_samp = s_s = s_p_stacked = c_s_stacked = h_short = next_weights = None
    for l in range(depth):
        last = l == depth - 1
        if l == 0:
            proj, win_b, wro_b, wco_b, wo_b, h_short = _proj_first(
                x_sample.reshape(n_samp, D_MODEL), meta_tokens.astype(F32), l, gm, w_in, squares32)
        else:
            win_b, wro_b, wco_b, wo_b = next_weights
            proj = _proj(h_short, l, gm, win_b, short_tm)
        gated_m, bgy_m, c_m, s_m = _ret_meta(proj, n_samp, zero_s, zero_c, l, conv_w, rope_meta, dec_meta, 1, N_META)

        h_main, s_l, c_l, wgu_b, wd_b = _mixer_main(h_main, s_m, c_m, l, gm, win_b, conv_w, wro_b, wco_b, wo_b,
                                                    rope_main, dec_main, w_gate_up, w_down)
        s_p.append(s_l)
        c_p.append(c_l)
        res = _ffn_main(h_main.reshape(nb * seq, D_MODEL), l, gf, wgu_b, wd_b, fg, last, MAIN_FFN_TM,
                        proj, state_ret, state_conv, conv_w, rope_samp, dec_samp, ls, kds, vs, s_p, w_in, squares32,
                        h_short, gated_m, bgy_m, (wro_b, wco_b, wo_b), c_s)
        h_main = res[0].reshape(nb, seq, D_MODEL)
        if last:
            c_s_stacked = res[3]
            s_s = res[4]
            s_p_stacked = res[5].reshape(depth, nb, HEADS, DH, DH)
            y_samp = res[-1]
        else:
            c_s.append(res[3])
            kds.append(res[4])
            vs.append(res[5])
            next_weights = res[6:10]
            h_short = res[-1]

    return (h_main, y_samp.reshape(ns, ls, D_MODEL), s_p_stacked, jnp.stack(c_p), s_s, c_s_stacked)
```
